```python
import math
import jax
import jax.numpy as jnp
from jax import lax
import numpy as np

D_MODEL = 1024
BATCH = 4
SEQ = 4096
DEPTH = 1
DEC_BATCH = 128
DEC_SEQ = 4
PAST_LEN = 2048
PAGE_SIZE = 128

N_HEADS = 4
HEAD_DIM = 64
QK_DIM = 2 * HEAD_DIM
V_DIM = 2 * HEAD_DIM
ATT_WIDTH = N_HEADS * V_DIM
ATTN_SCALE = HEAD_DIM ** -0.5
Q_BLOCK = 128
NEG_INF = -1e30
NUM_BUCKETS = 32
MAX_EXACT = NUM_BUCKETS // 2
MAX_DISTANCE = 128
CONV_DIM = D_MODEL // 2
CONV_WIDTH = 31
Q_COLS = N_HEADS * QK_DIM
K_COLS = N_HEADS * QK_DIM
V_COLS = N_HEADS * V_DIM
GLU_COLS = 2 * CONV_DIM
GATE_COLS = 2 * D_MODEL
IN_COLS = Q_COLS + K_COLS + V_COLS + GLU_COLS + GATE_COLS
SPLITS = [Q_COLS, Q_COLS + K_COLS, Q_COLS + K_COLS + V_COLS, Q_COLS + K_COLS + V_COLS + GLU_COLS]
N_EXPERTS = 32
TOP_K = 4
D_FF = D_MODEL
SWIGLU_LIMIT = 7.0
SWIGLU_ALPHA = 1.702
MOE_BLOCK = 256
DEEPNORM_ALPHA = (2 * DEPTH) ** 0.25
DEEPNORM_BETA = (8 * DEPTH) ** -0.25
LN_EPS = 1e-5

kernel_name = 'hybrid_diffattn_conformer_moe_step'


def layer_norm(x, g, b):
    xf = x.astype(jnp.float32)
    mu = jnp.mean(xf, -1, keepdims=True)
    var = jnp.mean(jnp.square(xf - mu), -1, keepdims=True)
    y = (xf - mu) * lax.rsqrt(var + LN_EPS) * g.astype(jnp.float32) + b.astype(jnp.float32)
    return y.astype(x.dtype)


def lambda_init(layer):
    return 0.8 - 0.6 * math.exp(-0.3 * layer)


def t5_bucket(dist):
    n = jnp.maximum(dist, 0)
    nf = jnp.maximum(n, 1).astype(jnp.float32)
    large = MAX_EXACT + (jnp.log(nf / MAX_EXACT) / math.log(MAX_DISTANCE / MAX_EXACT)
                         * (NUM_BUCKETS - MAX_EXACT)).astype(jnp.int32)
    large = jnp.minimum(large, NUM_BUCKETS - 1)
    return jnp.where(n < MAX_EXACT, n, large)


def diff_attn_block(q, k, v, q_pos, k_pos, rel_bias, lam):
    s = jnp.einsum('bqhcd,bkhcd->bchqk', q, k).astype(jnp.float32) * ATTN_SCALE
    bias = rel_bias[t5_bucket(q_pos[:, None] - k_pos[None, :])].astype(jnp.float32)
    s = s + jnp.transpose(bias, (2, 0, 1))
    s = jnp.where(k_pos[None, :] <= q_pos[:, None], s, NEG_INF)
    p = jax.nn.softmax(s, axis=-1)
    w = p[:, 0] - lam * p[:, 1]
    return jnp.einsum('bhqk,bkhv->bqhv', w.astype(v.dtype), v)


def diff_attention(q, k, v, q_pos, k_pos, rel_bias, lam):
    B, T = q.shape[0], q.shape[1]
    if T > Q_BLOCK and T % Q_BLOCK == 0:
        nb = T // Q_BLOCK
        qb = jnp.swapaxes(q.reshape(B, nb, Q_BLOCK, N_HEADS, 2, HEAD_DIM), 0, 1)
        pb = q_pos.reshape(nb, Q_BLOCK)
        ob = lax.map(lambda a: diff_attn_block(a[0], k, v, a[1], k_pos, rel_bias, lam), (qb, pb))
        return jnp.swapaxes(ob, 0, 1).reshape(B, T, N_HEADS, V_DIM)
    return diff_attn_block(q, k, v, q_pos, k_pos, rel_bias, lam)


def conformer_conv(c, conv_prev, conv_w, conv_b, ln_g, ln_b, w_proj, b_proj):
    u = c[..., :CONV_DIM] * jax.nn.sigmoid(c[..., CONV_DIM:])
    buf = jnp.concatenate([conv_prev.astype(u.dtype), u], axis=1)
    y = lax.conv_general_dilated(buf, conv_w[:, None, :].astype(u.dtype), window_strides=(1,),
                                 padding='VALID', dimension_numbers=('NWC', 'WIO', 'NWC'),
                                 feature_group_count=CONV_DIM) + conv_b
    y = jax.nn.silu(layer_norm(y, ln_g, ln_b))
    return y @ w_proj + b_proj, buf[:, buf.shape[1] - (CONV_WIDTH - 1):]


def token_mixers(h, k_past, v_past, conv_prev, p, lam_init):
    B, T, _ = h.shape
    P = k_past.shape[1]
    z = h @ p['w_in'] + p['b_in']
    q, k_new, v_new, c, gates = jnp.split(z, SPLITS, axis=-1)
    q = q.reshape(B, T, N_HEADS, 2, HEAD_DIM)
    k_new = k_new.reshape(B, T, N_HEADS, QK_DIM)
    v_new = v_new.reshape(B, T, N_HEADS, V_DIM)
    k_all = jnp.concatenate([k_past.astype(h.dtype), k_new], axis=1).reshape(B, P + T, N_HEADS, 2, HEAD_DIM)
    v_all = jnp.concatenate([v_past.astype(h.dtype), v_new], axis=1)
    f32 = jnp.float32
    lam = (jnp.exp(jnp.sum(p['lambda_q1'].astype(f32) * p['lambda_k1'].astype(f32)))
           - jnp.exp(jnp.sum(p['lambda_q2'].astype(f32) * p['lambda_k2'].astype(f32))) + lam_init)
    q_pos = P + jnp.arange(T, dtype=jnp.int32)
    k_pos = jnp.arange(P + T, dtype=jnp.int32)
    o = diff_attention(q, k_all, v_all, q_pos, k_pos, p['rel_bias'], lam).astype(f32)
    o = o * lax.rsqrt(jnp.mean(jnp.square(o), -1, keepdims=True) + LN_EPS) * p['subln_g'].astype(f32)
    o = (o * (1.0 - lam_init)).astype(h.dtype).reshape(B, T, ATT_WIDTH)
    a = o @ p['w_attn_proj']
    b, conv_new = conformer_conv(c, conv_prev, p['conv_w'], p['conv_b'], p['conv_ln_g'], p['conv_ln_b'],
                                 p['w_conv_proj'], p['b_conv_proj'])
    g_a, g_b = jnp.split(jax.nn.sigmoid(gates), 2, axis=-1)
    mix = (g_a * a + g_b * b) @ p['w_out']
    return mix, k_new, v_new, conv_new


def moe(x, router_w, router_b, w1, b1, w2, b2):
    n, d = x.shape
    logits = (x @ router_w + router_b).astype(jnp.float32)
    top_v, top_i = lax.top_k(logits, TOP_K)
    top_w = jax.nn.softmax(top_v, axis=-1)
    gates = jnp.einsum('tk,tke->te', top_w, jax.nn.one_hot(top_i, N_EXPERTS, dtype=jnp.float32)).astype(x.dtype)
    blk = min(MOE_BLOCK, n)
    nb = -(-n // blk)
    pad = nb * blk - n
    xp = jnp.pad(x, ((0, pad), (0, 0))).reshape(nb, blk, d)
    gp = jnp.pad(gates, ((0, pad), (0, 0))).reshape(nb, blk, N_EXPERTS)

    def expert_block(args):
        xb, gb = args
        hh = jnp.einsum('td,edf->tef', xb, w1) + b1
        g = jnp.minimum(hh[..., :D_FF], SWIGLU_LIMIT)
        u = jnp.clip(hh[..., D_FF:], -SWIGLU_LIMIT, SWIGLU_LIMIT)
        act = (u + 1.0) * g * jax.nn.sigmoid(SWIGLU_ALPHA * g)
        return jnp.einsum('tef,efd->td', act * gb[..., None], w2) + gb @ b2

    y = lax.map(expert_block, (xp, gp))
    return y.reshape(nb * blk, d)[:n]


def decoder_layer(h, k_past, v_past, conv_prev, p, lam_init):
    mix, k_new, v_new, conv_new = token_mixers(h, k_past, v_past, conv_prev, p, lam_init)
    h1 = layer_norm(DEEPNORM_ALPHA * h + mix, p['ln1_g'], p['ln1_b'])
    B, T, D = h1.shape
    ff = moe(h1.reshape(B * T, D), p['router_w'], p['router_b'], p['expert_w1'], p['expert_b1'],
             p['expert_w2'], p['expert_b2']).reshape(B, T, D)
    h2 = layer_norm(DEEPNORM_ALPHA * h1 + ff, p['ln2_g'], p['ln2_b'])
    return h2, k_new, v_new, conv_new


def setup_inputs(seed: int = 0) -> dict:
    key = jax.random.key(seed)
    ks = jax.random.split(key, 36)
    f32 = jnp.float32
    nrm = lambda k, s: jax.random.normal(k, s, f32)
    n_pages = PAST_LEN // PAGE_SIZE
    n_used = DEC_BATCH * n_pages
    n_pool = n_used + max(1, n_used // 4)
    page_table = jax.random.permutation(ks[0], n_pool)[:n_used].reshape(DEC_BATCH, n_pages).astype(jnp.int32)
    w_in = nrm(ks[1], (DEPTH, D_MODEL, IN_COLS)) * D_MODEL ** -0.5
    w_in = w_in.at[:, :, SPLITS[1]:SPLITS[2]].multiply(DEEPNORM_BETA)
    return {
        'x_prompt': nrm(ks[2], (BATCH, SEQ, D_MODEL)),
        'x_sample': nrm(ks[3], (DEC_BATCH, DEC_SEQ, D_MODEL)),
        'cache_k': nrm(ks[4], (DEPTH, n_pool, PAGE_SIZE, N_HEADS, QK_DIM)),
        'cache_v': nrm(ks[5], (DEPTH, n_pool, PAGE_SIZE, N_HEADS, V_DIM)) * DEEPNORM_BETA,
        'page_table': page_table,
        'state_conv': nrm(ks[6], (DEPTH, DEC_BATCH, CONV_WIDTH - 1, CONV_DIM)) * 0.5,
        'w_in': w_in,
        'b_in': nrm(ks[7], (DEPTH, IN_COLS)) * 0.02,
        'lambda_q1': nrm(ks[8], (DEPTH, HEAD_DIM)) * 0.1,
        'lambda_k1': nrm(ks[9], (DEPTH, HEAD_DIM)) * 0.1,
        'lambda_q2': nrm(ks[10], (DEPTH, HEAD_DIM)) * 0.1,
        'lambda_k2': nrm(ks[11], (DEPTH, HEAD_DIM)) * 0.1,
        'subln_g': 1.0 + 0.01 * nrm(ks[12], (DEPTH, V_DIM)),
        'rel_bias': nrm(ks[13], (NUM_BUCKETS, N_HEADS)) * 0.5,
        'w_attn_proj': nrm(ks[14], (DEPTH, ATT_WIDTH, D_MODEL)) * ATT_WIDTH ** -0.5 * DEEPNORM_BETA,
        'conv_w': nrm(ks[15], (DEPTH, CONV_WIDTH, CONV_DIM)) * CONV_WIDTH ** -0.5,
        'conv_b': nrm(ks[16], (DEPTH, CONV_DIM)) * 0.02,
        'conv_ln_g': 1.0 + 0.01 * nrm(ks[17], (DEPTH, CONV_DIM)),
        'conv_ln_b': nrm(ks[18], (DEPTH, CONV_DIM)) * 0.01,
        'w_conv_proj': nrm(ks[19], (DEPTH, CONV_DIM, D_MODEL)) * CONV_DIM ** -0.5 * DEEPNORM_BETA,
        'b_conv_proj': nrm(ks[20], (DEPTH, D_MODEL)) * 0.02,
        'w_out': nrm(ks[21], (DEPTH, D_MODEL, D_MODEL)) * D_MODEL ** -0.5 * DEEPNORM_BETA,
        'ln1_g': 1.0 + 0.01 * nrm(ks[22], (DEPTH, D_MODEL)),
        'ln1_b': nrm(ks[23], (DEPTH, D_MODEL)) * 0.01,
        'router_w': nrm(ks[24], (DEPTH, D_MODEL, N_EXPERTS)) * D_MODEL ** -0.5,
        'router_b': nrm(ks[25], (DEPTH, N_EXPERTS)) * 0.01,
        'expert_w1': nrm(ks[26], (DEPTH, N_EXPERTS, D_MODEL, 2 * D_FF)) * D_MODEL ** -0.5,
        'expert_b1': nrm(ks[27], (DEPTH, N_EXPERTS, 2 * D_FF)) * 0.01,
        'expert_w2': nrm(ks[28], (DEPTH, N_EXPERTS, D_FF, D_MODEL)) * D_FF ** -0.5 * DEEPNORM_BETA,
        'expert_b2': nrm(ks[29], (DEPTH, N_EXPERTS, D_MODEL)) * 0.01,
        'ln2_g': 1.0 + 0.01 * nrm(ks[30], (DEPTH, D_MODEL)),
        'ln2_b': nrm(ks[31], (DEPTH, D_MODEL)) * 0.01,
    }


def reference(x_prompt, x_sample, cache_k, cache_v, page_table, state_conv, w_in, b_in, lambda_q1, lambda_k1,
              lambda_q2, lambda_k2, subln_g, rel_bias, w_attn_proj, conv_w, conv_b, conv_ln_g, conv_ln_b,
              w_conv_proj, b_conv_proj, w_out, ln1_g, ln1_b, router_w, router_b, expert_w1, expert_b1,
              expert_w2, expert_b2, ln2_g, ln2_b):
    n_pages = page_table.shape[1]
    past = n_pages * PAGE_SIZE
    hp, hs = x_prompt, x_sample
    bp, bs = hp.shape[0], hs.shape[0]
    kp_l, vp_l, cp_l, ks_l, vs_l, cs_l = [], [], [], [], [], []
    for l in range(DEPTH):
        p = dict(w_in=w_in[l], b_in=b_in[l], lambda_q1=lambda_q1[l], lambda_k1=lambda_k1[l],
                 lambda_q2=lambda_q2[l], lambda_k2=lambda_k2[l], subln_g=subln_g[l], rel_bias=rel_bias,
                 w_attn_proj=w_attn_proj[l], conv_w=conv_w[l], conv_b=conv_b[l], conv_ln_g=conv_ln_g[l],
                 conv_ln_b=conv_ln_b[l], w_conv_proj=w_conv_proj[l], b_conv_proj=b_conv_proj[l], w_out=w_out[l],
                 ln1_g=ln1_g[l], ln1_b=ln1_b[l], router_w=router_w[l], router_b=router_b[l],
                 expert_w1=expert_w1[l], expert_b1=expert_b1[l], expert_w2=expert_w2[l], expert_b2=expert_b2[l],
                 ln2_g=ln2_g[l], ln2_b=ln2_b[l])
        lam0 = lambda_init(l)
        hp, kp, vp, cp = decoder_layer(hp, jnp.zeros((bp, 0, N_HEADS, QK_DIM), hp.dtype),
                                       jnp.zeros((bp, 0, N_HEADS, V_DIM), hp.dtype),
                                       jnp.zeros((bp, CONV_WIDTH - 1, CONV_DIM), hp.dtype), p, lam0)
        k_past = cache_k[l][page_table].reshape(bs, past, N_HEADS, QK_DIM)
        v_past = cache_v[l][page_table].reshape(bs, past, N_HEADS, V_DIM)
        hs, ksn, vsn, csn = decoder_layer(hs, k_past, v_past, state_conv[l], p, lam0)
        kp_l.append(kp); vp_l.append(vp); cp_l.append(cp)
        ks_l.append(ksn); vs_l.append(vsn); cs_l.append(csn)
    return (hp, hs, jnp.stack(kp_l), jnp.stack(vp_l), jnp.stack(cp_l), jnp.stack(ks_l), jnp.stack(vs_l), jnp.stack(cs_l))
```

```python
import functools
import math

import jax
import jax.numpy as jnp
from jax import lax
from jax.experimental import pallas as pl
from jax.experimental.pallas import tpu as pltpu

F32 = jnp.float32
BF16 = jnp.bfloat16
I32 = jnp.int32

N_HEADS = 4
HEAD_DIM = 64
QK_DIM = 2 * HEAD_DIM
V_DIM = 2 * HEAD_DIM
ATTN_SCALE = HEAD_DIM ** -0.5
NEG_INF = -1e30
NUM_BUCKETS = 32
MAX_EXACT = NUM_BUCKETS // 2
MAX_DISTANCE = 128
CONV_WIDTH = 31
TOP_K = 4
SWIGLU_LIMIT = 7.0
SWIGLU_ALPHA = 1.702
LN_EPS = 1e-5
PAGE_SIZE = 128

LANES = 128
SUBLANES = 8
VMEM_LIMIT = 56 * 1024 * 1024

TOKEN_TILE = 512
ATTN_TILE = 256
MOE_TILE = 256
CONV_HALO = 32
CONV_CHUNK = 64


def _cparams(*sem):
    return pltpu.CompilerParams(dimension_semantics=sem, vmem_limit_bytes=VMEM_LIMIT)


def _layer_norm(x, g, b):
    mu = jnp.mean(x, -1, keepdims=True)
    xc = x - mu
    var = jnp.mean(xc * xc, -1, keepdims=True)
    return xc * lax.rsqrt(var + LN_EPS) * g + b


def _in_proj_body(x_ref, w_ref, b_ref, q_ref, k_ref, v_ref, kb_ref, vb_ref, u_ref, sg_ref, *, d_att, d_conv):
    x = x_ref[...].astype(BF16)

    def seg(lo, hi):
        return jnp.dot(x, w_ref[:, lo:hi], preferred_element_type=F32) + b_ref[:, lo:hi]

    o = 0
    q_ref[...] = (seg(o, o + d_att) * ATTN_SCALE).astype(BF16)
    o += d_att
    k = seg(o, o + d_att)
    k_ref[...] = k
    kb_ref[...] = k.astype(BF16)
    o += d_att
    v = seg(o, o + d_att)
    v_ref[...] = v
    vb_ref[...] = v.astype(BF16)
    o += d_att
    c = seg(o, o + 2 * d_conv)
    u_ref[...] = c[:, :d_conv] * jax.nn.sigmoid(c[:, d_conv:])
    o += 2 * d_conv
    sg_ref[...] = jax.nn.sigmoid(seg(o, w_ref.shape[1])).astype(BF16)


def _in_proj(x, w_bf, b, d_att, d_conv):
    n, d = x.shape
    cols = w_bf.shape[1]
    tm = min(TOKEN_TILE, n)
    row = lambda width: pl.BlockSpec((tm, width), lambda i: (i, 0))
    full = lambda a: pl.BlockSpec(a.shape, lambda i: (0, 0))
    out_shape = (
        jax.ShapeDtypeStruct((n, d_att), BF16),
        jax.ShapeDtypeStruct((n, d_att), F32),
        jax.ShapeDtypeStruct((n, d_att), F32),
        jax.ShapeDtypeStruct((n, d_att), BF16),
        jax.ShapeDtypeStruct((n, d_att), BF16),
        jax.ShapeDtypeStruct((n, d_conv), F32),
        jax.ShapeDtypeStruct((n, 2 * d), BF16),
    )
    return pl.pallas_call(
        functools.partial(_in_proj_body, d_att=d_att, d_conv=d_conv),
        grid=(n // tm,),
        in_specs=[row(d), full(w_bf), full(b)],
        out_specs=(row(d_att), row(d_att), row(d_att), row(d_att), row(d_att), row(d_conv), row(2 * d)),
        out_shape=out_shape,
        compiler_params=_cparams("parallel"),
        name="in_proj",
    )(x, w_bf, b)


def _t5_bucket(dist):
    n = jnp.maximum(dist, 0)
    nf = jnp.maximum(n, 1).astype(F32)
    large = MAX_EXACT + (jnp.log(nf / MAX_EXACT) / math.log(MAX_DISTANCE / MAX_EXACT)
                         * (NUM_BUCKETS - MAX_EXACT)).astype(I32)
    large = jnp.minimum(large, NUM_BUCKETS - 1)
    return jnp.where(n < MAX_EXACT, n, large)


def _bias_of(dist, rel_bias):
    b = rel_bias[_t5_bucket(dist)].astype(F32)
    b = jnp.where((dist >= 0)[..., None], b, NEG_INF)
    return jnp.moveaxis(b, -1, 0)


def _flash_body(lam_ref, far_ref, q_ref, k_ref, v_ref, bias_ref, g_ref, o_ref, m_s, l_s, acc_s, *, t, lam_init):
    h = pl.program_id(1)
    i = pl.program_id(2)
    q = q_ref[0]
    lane = lax.broadcasted_iota(I32, (1, QK_DIM), 1)
    zero = jnp.zeros_like(q)
    qs = (jnp.where(lane < HEAD_DIM, q, zero), jnp.where(lane >= HEAD_DIM, q, zero))
    m_s[...] = jnp.full(m_s.shape, NEG_INF, F32)
    l_s[...] = jnp.zeros(l_s.shape, F32)
    acc_s[...] = jnp.zeros(acc_s.shape, F32)

    def step(j, bias):
        start = pl.multiple_of(j * t, t)
        k = k_ref[0, pl.ds(start, t), :]
        v = v_ref[0, pl.ds(start, t), :]
        for c in range(2):
            s = lax.dot_general(qs[c], k, (((1,), (1,)), ((), ())), preferred_element_type=F32) + bias
            m_old = m_s[c]
            m_new = jnp.maximum(m_old, jnp.max(s, -1, keepdims=True))
            alpha = jnp.exp(m_old - m_new)
            p = jnp.exp(s - m_new)
            l_s[c] = alpha * l_s[c] + jnp.sum(p, -1, keepdims=True)
            acc_s[c] = alpha * acc_s[c] + jnp.dot(p.astype(BF16), v, preferred_element_type=F32)
            m_s[c] = m_new

    def far_step(j, carry):
        step(j, far_ref[h])
        return carry

    lax.fori_loop(0, jnp.maximum(i - 1, 0), far_step, 0)

    @pl.when(i >= 1)
    def _():
        step(i - 1, bias_ref[0, 1])

    step(i, bias_ref[0, 0])

    o = acc_s[0] / l_s[0] - lam_ref[0] * (acc_s[1] / l_s[1])
    o = o * lax.rsqrt(jnp.mean(o * o, -1, keepdims=True) + LN_EPS) * g_ref[...]
    o_ref[0] = (o * (1.0 - lam_init)).astype(BF16)


def _prompt_attention(q, kb, vb, rel_bias, lam, subln_g, lam_init):
    bsz, seq, width = q.shape
    t = min(ATTN_TILE, seq)
    assert t >= MAX_DISTANCE and seq % t == 0
    r = jnp.arange(t, dtype=I32)
    d0 = r[:, None] - r[None, :]
    bias = jnp.stack([_bias_of(d0, rel_bias), _bias_of(d0 + t, rel_bias)], 1)
    far = rel_bias[NUM_BUCKETS - 1].astype(F32)
    smem = pl.BlockSpec(memory_space=pltpu.SMEM)
    return pl.pallas_call(
        functools.partial(_flash_body, t=t, lam_init=lam_init),
        grid=(bsz, N_HEADS, seq // t),
        in_specs=[
            smem, smem,
            pl.BlockSpec((1, t, QK_DIM), lambda b, h, i: (b, i, h)),
            pl.BlockSpec((1, seq, QK_DIM), lambda b, h, i: (b, 0, h)),
            pl.BlockSpec((1, seq, V_DIM), lambda b, h, i: (b, 0, h)),
            pl.BlockSpec((1, 2, t, t), lambda b, h, i: (h, 0, 0, 0)),
            pl.BlockSpec((1, V_DIM), lambda b, h, i: (0, 0)),
        ],
        out_specs=pl.BlockSpec((1, t, V_DIM), lambda b, h, i: (b, i, h)),
        out_shape=jax.ShapeDtypeStruct((bsz, seq, width), BF16),
        scratch_shapes=[pltpu.VMEM((2, t, 1), F32), pltpu.VMEM((2, t, 1), F32), pltpu.VMEM((2, t, V_DIM), F32)],
        compiler_params=_cparams("parallel", "parallel", "arbitrary"),
        name="prompt_attention",
    )(lam, far, q, kb, vb, bias, subln_g)


def _paged_body(pt_ref, lam_ref, q_ref, kn_ref, vn_ref, bias_ref, g_ref, *refs, n_pages, t_new, lam_init):
    k_refs = refs[:n_pages]
    v_refs = refs[n_pages:2 * n_pages]
    o_ref = refs[2 * n_pages]
    lam = lam_ref[0]
    rows = 2 * t_new
    outs = []
    for h in range(N_HEADS):
        q = q_ref[0, h]
        parts = []
        for p in range(n_pages):
            k = k_refs[p][:, h, :].astype(BF16)
            parts.append(lax.dot_general(q, k, (((1,), (1,)), ((), ())), preferred_element_type=F32))
        kn = kn_ref[0, h]
        parts.append(lax.dot_general(q, kn, (((1,), (1,)), ((), ())), preferred_element_type=F32))
        s = jnp.concatenate(parts, axis=1) + bias_ref[h]
        m = jnp.max(s, -1, keepdims=True)
        p_un = jnp.exp(s - m)
        pn = p_un / jnp.sum(p_un, -1, keepdims=True)
        w = (pn - lam * pltpu.roll(pn, rows - t_new, axis=0)).astype(BF16)
        o = jnp.dot(w[:, n_pages * PAGE_SIZE:], vn_ref[0, h], preferred_element_type=F32)
        for p in range(n_pages):
            v = v_refs[p][:, h, :].astype(BF16)
            o = o + jnp.dot(w[:, p * PAGE_SIZE:(p + 1) * PAGE_SIZE], v, preferred_element_type=F32)
        o = o * lax.rsqrt(jnp.mean(o * o, -1, keepdims=True) + LN_EPS) * g_ref[...]
        outs.append((o * (1.0 - lam_init)).astype(BF16))
    o_ref[0] = jnp.concatenate(outs, axis=1)


def _sample_attention(q, kb, vb, cache_k, cache_v, page_table, rel_bias, lam, subln_g, lam_init):
    bs, t_new, width = q.shape
    n_pages = page_table.shape[1]
    past = n_pages * PAGE_SIZE
    rows = 2 * t_new
    assert rows % SUBLANES == 0 and t_new <= PAGE_SIZE
    qh = q.reshape(bs, t_new, N_HEADS, QK_DIM).transpose(0, 2, 1, 3)
    lane_map = (jnp.arange(QK_DIM) // HEAD_DIM)[None, :] == jnp.arange(2)[:, None]
    q_rows = (qh[:, :, None] * lane_map[None, None, :, None, :].astype(BF16)).reshape(bs, N_HEADS, rows, QK_DIM)
    pad = ((0, 0), (0, 0), (0, PAGE_SIZE - t_new), (0, 0))
    kn = jnp.pad(kb.reshape(bs, t_new, N_HEADS, QK_DIM).transpose(0, 2, 1, 3), pad)
    vn = jnp.pad(vb.reshape(bs, t_new, N_HEADS, V_DIM).transpose(0, 2, 1, 3), pad)
    qpos = past + jnp.arange(t_new, dtype=I32)
    kpos = jnp.concatenate([jnp.arange(past + t_new, dtype=I32),
                            jnp.full((PAGE_SIZE - t_new,), past + t_new + PAGE_SIZE, I32)])
    bias = _bias_of(qpos[:, None] - kpos[None, :], rel_bias)
    bias = jnp.concatenate([bias, bias], axis=1)

    page_spec = lambda j: pl.BlockSpec((None, PAGE_SIZE, N_HEADS, QK_DIM),
                                       lambda b, pt, j=j: (pt[b * n_pages + j], 0, 0, 0))
    per_seq = lambda shape: pl.BlockSpec((1,) + shape, lambda b, pt: (b, 0, 0, 0))
    grid_spec = pltpu.PrefetchScalarGridSpec(
        num_scalar_prefetch=1,
        grid=(bs,),
        in_specs=[
            pl.BlockSpec(memory_space=pltpu.SMEM),
            per_seq((N_HEADS, rows, QK_DIM)),
            per_seq((N_HEADS, PAGE_SIZE, QK_DIM)),
            per_seq((N_HEADS, PAGE_SIZE, V_DIM)),
            pl.BlockSpec(bias.shape, lambda b, pt: (0, 0, 0)),
            pl.BlockSpec((1, V_DIM), lambda b, pt: (0, 0)),
        ] + [page_spec(j) for j in range(n_pages)] * 2,
        out_specs=pl.BlockSpec((1, rows, width), lambda b, pt: (b, 0, 0)),
    )
    return pl.pallas_call(
        functools.partial(_paged_body, n_pages=n_pages, t_new=t_new, lam_init=lam_init),
        grid_spec=grid_spec,
        out_shape=jax.ShapeDtypeStruct((bs, rows, width), BF16),
        compiler_params=_cparams("parallel"),
        name="sample_attention",
    )(page_table.reshape(-1), lam, q_rows, kn, vn, bias, subln_g, *([cache_k] * n_pages), *([cache_v] * n_pages))


def _conv_prompt_body(u_ref, halo_ref, w_ref, cb_ref, g_ref, b_ref, o_ref, buf, *, tc):
    i = pl.program_id(1)
    halo = halo_ref[0]
    buf[0:CONV_HALO, :] = jnp.where(i > 0, halo, jnp.zeros_like(halo))
    buf[CONV_HALO:, :] = u_ref[0]
    first = CONV_HALO - (CONV_WIDTH - 1)
    for r0 in range(0, tc, CONV_CHUNK):
        acc = jnp.zeros((CONV_CHUNK, u_ref.shape[2]), F32) + cb_ref[...]
        for j in range(CONV_WIDTH):
            acc = acc + w_ref[j:j + 1, :] * buf[first + r0 + j:first + r0 + j + CONV_CHUNK, :]
        y = _layer_norm(acc, g_ref[...], b_ref[...])
        o_ref[0, r0:r0 + CONV_CHUNK, :] = (y * jax.nn.sigmoid(y)).astype(BF16)


def _conv_prompt(u, conv_w, conv_b, ln_g, ln_b):
    bsz, seq, ch = u.shape
    tc = min(TOKEN_TILE, seq)
    assert seq % tc == 0 and tc % CONV_CHUNK == 0 and tc % CONV_HALO == 0
    vec = pl.BlockSpec((1, ch), lambda b, i: (0, 0))
    return pl.pallas_call(
        functools.partial(_conv_prompt_body, tc=tc),
        grid=(bsz, seq // tc),
        in_specs=[
            pl.BlockSpec((1, tc, ch), lambda b, i: (b, i, 0)),
            pl.BlockSpec((1, CONV_HALO, ch), lambda b, i: (b, jnp.maximum(i * (tc // CONV_HALO) - 1, 0), 0)),
            pl.BlockSpec((CONV_WIDTH, ch), lambda b, i: (0, 0)),
            vec, vec, vec,
        ],
        out_specs=pl.BlockSpec((1, tc, ch), lambda b, i: (b, i, 0)),
        out_shape=jax.ShapeDtypeStruct((bsz, seq, ch), BF16),
        scratch_shapes=[pltpu.VMEM((CONV_HALO + tc, ch), F32)],
        compiler_params=_cparams("parallel", "arbitrary"),
        name="conv_prompt",
    )(u, u, conv_w, conv_b, ln_g, ln_b)


def _conv_sample_body(buf_ref, w_ref, cb_ref, g_ref, b_ref, o_ref, *, t_new):
    for t in range(t_new):
        acc = jnp.zeros(buf_ref.shape[1:], F32) + cb_ref[...]
        for j in range(CONV_WIDTH):
            acc = acc + w_ref[j:j + 1, :] * buf_ref[t + j]
        y = _layer_norm(acc, g_ref[...], b_ref[...])
        o_ref[t] = (y * jax.nn.sigmoid(y)).astype(BF16)


def _conv_sample(buf_t, conv_w, conv_b, ln_g, ln_b):
    rows, bs, ch = buf_t.shape
    t_new = rows - (CONV_WIDTH - 1)
    gb = min(32, bs)
    assert bs % gb == 0
    vec = pl.BlockSpec((1, ch), lambda i: (0, 0))
    return pl.pallas_call(
        functools.partial(_conv_sample_body, t_new=t_new),
        grid=(bs // gb,),
        in_specs=[pl.BlockSpec((rows, gb, ch), lambda i: (0, i, 0)),
                  pl.BlockSpec((CONV_WIDTH, ch), lambda i: (0, 0)), vec, vec, vec],
        out_specs=pl.BlockSpec((t_new, gb, ch), lambda i: (0, i, 0)),
        out_shape=jax.ShapeDtypeStruct((t_new, bs, ch), BF16),
        compiler_params=_cparams("parallel"),
        name="conv_sample",
    )(buf_t, conv_w, conv_b, ln_g, ln_b)


def _merge_body(*refs, alpha, n_alias):
    (o_ref, y_ref, sg_ref, x_ref, wa_ref, wc_ref, bc_ref, wo_ref, g1_ref, b1_ref, rw_ref, rb_ref) = refs[:12]
    h1_ref, ti_ref, tw_ref = refs[12 + n_alias:]
    d = x_ref.shape[1]
    a = jnp.dot(o_ref[...], wa_ref[...], preferred_element_type=F32)
    b = jnp.dot(y_ref[...], wc_ref[...], preferred_element_type=F32) + bc_ref[...]
    mix_in = sg_ref[:, :d].astype(F32) * a + sg_ref[:, d:].astype(F32) * b
    mix = jnp.dot(mix_in.astype(BF16), wo_ref[...], preferred_element_type=F32)
    h1 = _layer_norm(alpha * x_ref[...] + mix, g1_ref[...], b1_ref[...])
    h1_ref[...] = h1
    logits = lax.dot_general(rw_ref[...], h1, (((1,), (1,)), ((), ())), preferred_element_type=F32,
                             precision=lax.Precision.HIGHEST) + rb_ref[...]
    n_e = logits.shape[0]
    eid = lax.broadcasted_iota(I32, logits.shape, 0)
    vals, idxs = [], []
    for _ in range(TOP_K):
        m = jnp.max(logits, 0, keepdims=True)
        idx = jnp.min(jnp.where(logits == m, eid, n_e), 0, keepdims=True)
        vals.append(m)
        idxs.append(idx)
        logits = jnp.where(eid == idx, -jnp.inf, logits)
    ex = [jnp.exp(v - vals[0]) for v in vals]
    den = ex[0]
    for e in ex[1:]:
        den = den + e
    ti_ref[...] = jnp.concatenate(idxs, 0)
    tw_ref[...] = jnp.concatenate([e / den for e in ex], 0)


def _merge(o, yact, sg, x, wts, alpha, n_total, row_off, tm, prev):
    n, d = x.shape
    wa, wc, bc, wo, g1, b1, rwt, rb = wts
    off = row_off // tm
    last = n // tm - 1
    steps = n // tm if prev is not None else n_total // tm
    row = lambda width: pl.BlockSpec((tm, width), lambda i: (jnp.minimum(i, last), 0))
    full = lambda a: pl.BlockSpec(a.shape, lambda i: (0, 0))
    in_specs = [row(o.shape[1]), row(yact.shape[1]), row(2 * d), row(d),
                full(wa), full(wc), full(bc), full(wo), full(g1), full(b1), full(rwt), full(rb)]
    args = [o, yact, sg, x, wa, wc, bc, wo, g1, b1, rwt, rb]
    aliases = {}
    if prev is not None:
        in_specs += [pl.BlockSpec(memory_space=pl.ANY)] * 3
        aliases = {len(args) + j: j for j in range(3)}
        args += list(prev)
    return pl.pallas_call(
        functools.partial(_merge_body, alpha=alpha, n_alias=0 if prev is None else 3),
        grid=(steps,),
        in_specs=in_specs,
        out_specs=(pl.BlockSpec((tm, d), lambda i: (off + i, 0)),
                   pl.BlockSpec((TOP_K, tm), lambda i: (0, off + i)),
                   pl.BlockSpec((TOP_K, tm), lambda i: (0, off + i))),
        out_shape=(jax.ShapeDtypeStruct((n_total, d), F32),
                   jax.ShapeDtypeStruct((TOP_K, n_total), I32),
                   jax.ShapeDtypeStruct((TOP_K, n_total), F32)),
        input_output_aliases=aliases,
        compiler_params=_cparams("parallel"),
        name="merge_ln1_router",
    )(*args)


def _route(topi, topw, n_experts, tm):
    k, n = topi.shape
    e_f = topi.reshape(-1)
    order = jnp.argsort(e_f, stable=True).astype(I32)
    counts = jnp.sum((e_f[:, None] == jnp.arange(n_experts, dtype=I32)[None, :]).astype(I32), 0)
    cstart = jnp.cumsum(counts) - counts
    ntile_e = (counts + tm - 1) // tm
    tend = jnp.cumsum(ntile_e)
    tstart = tend - ntile_e
    n_tiles = (k * n + tm - 1) // tm + n_experts
    tid = jnp.arange(n_tiles, dtype=I32)
    used = tid < tend[-1]
    te = jnp.minimum(jnp.searchsorted(tend, tid, side="right").astype(I32), n_experts - 1)
    local = tid - tstart[te]
    nvalid = jnp.where(used, jnp.clip(counts[te] - local * tm, 0, tm), 0).astype(I32)
    srow = (cstart[te] + local * tm)[:, None] + jnp.arange(tm, dtype=I32)[None, :]
    fidx = order[jnp.clip(srow, 0, k * n - 1)]
    tok = fidx % n
    dest = (fidx // n) * n + tok
    gate = topw.reshape(-1)[fidx]
    idx = jnp.stack([tok, dest], 1).astype(I32)
    return te, nvalid, idx, gate.reshape(n_tiles * tm, 1)


def _moe_body(te_ref, nv_ref, idx_ref, gate_ref, x_hbm, w1_ref, b1_ref, w2_ref, b2_ref, y_hbm,
              xbuf, obuf, gsem, ssem):
    i = pl.program_id(0)
    nv = nv_ref[i]
    d_ff = w2_ref.shape[1]

    @pl.when(i == 0)
    def _():
        xbuf[...] = jnp.zeros(xbuf.shape, F32)

    def gather_copy(r, tok):
        return pltpu.make_async_copy(x_hbm.at[pl.ds(tok, 1), :], xbuf.at[pl.ds(r, 1), :], gsem)

    def scatter_copy(r, dest):
        return pltpu.make_async_copy(obuf.at[pl.ds(r, 1), :], y_hbm.at[pl.ds(dest, 1), :], ssem)

    @pl.when(nv > 0)
    def _():
        def g_start(r, c):
            gather_copy(r, idx_ref[0, 0, r]).start()
            return c

        def g_wait(r, c):
            gather_copy(r, 0).wait()
            return c

        lax.fori_loop(0, nv, g_start, 0)
        lax.fori_loop(0, nv, g_wait, 0)
        x = xbuf[...].astype(BF16)
        hh = jnp.dot(x, w1_ref[0], preferred_element_type=F32) + b1_ref[0]
        g = jnp.minimum(hh[:, :d_ff], SWIGLU_LIMIT)
        u = jnp.clip(hh[:, d_ff:], -SWIGLU_LIMIT, SWIGLU_LIMIT)
        act = (u + 1.0) * g * jax.nn.sigmoid(SWIGLU_ALPHA * g)
        out = jnp.dot(act.astype(BF16), w2_ref[0], preferred_element_type=F32) + b2_ref[0]
        obuf[...] = out * gate_ref[...]

        def s_start(r, c):
            scatter_copy(r, idx_ref[0, 1, r]).start()
            return c

        def s_wait(r, c):
            scatter_copy(r, 0).wait()
            return c

        lax.fori_loop(0, nv, s_start, 0)
        lax.fori_loop(0, nv, s_wait, 0)


def _moe(h1, topi, topw, w1_bf, b1, w2_bf, b2):
    n, d = h1.shape
    n_e, _, f2 = w1_bf.shape
    tm = min(MOE_TILE, n)
    te, nvalid, idx, gate = _route(topi, topw, n_e, tm)
    n_tiles = te.shape[0]
    grid_spec = pltpu.PrefetchScalarGridSpec(
        num_scalar_prefetch=2,
        grid=(n_tiles,),
        in_specs=[
            pl.BlockSpec((1, 2, tm), lambda i, te, nv: (i, 0, 0), memory_space=pltpu.SMEM),
            pl.BlockSpec((tm, 1), lambda i, te, nv: (i, 0)),
            pl.BlockSpec(memory_space=pl.ANY),
            pl.BlockSpec((1, d, f2), lambda i, te, nv: (te[i], 0, 0)),
            pl.BlockSpec((1, 1, f2), lambda i, te, nv: (te[i], 0, 0)),
            pl.BlockSpec((1, f2 // 2, d), lambda i, te, nv: (te[i], 0, 0)),
            pl.BlockSpec((1, 1, d), lambda i, te, nv: (te[i], 0, 0)),
        ],
        out_specs=pl.BlockSpec(memory_space=pl.ANY),
        scratch_shapes=[pltpu.VMEM((tm, d), F32), pltpu.VMEM((tm, d), F32),
                        pltpu.SemaphoreType.DMA, pltpu.SemaphoreType.DMA],
    )
    y = pl.pallas_call(
        _moe_body,
        grid_spec=grid_spec,
        out_shape=jax.ShapeDtypeStruct((TOP_K * n, d), F32),
        compiler_params=_cparams("arbitrary"),
        name="moe_experts",
    )(te, nvalid, idx, gate, h1, w1_bf, b1.reshape(n_e, 1, f2), w2_bf, b2.reshape(n_e, 1, d))
    return y.reshape(TOP_K, n, d)


def _ln2_body(h1_ref, y_ref, g_ref, b_ref, o_ref, *, alpha):
    ff = y_ref[0]
    for k in range(1, y_ref.shape[0]):
        ff = ff + y_ref[k]
    o_ref[...] = _layer_norm(alpha * h1_ref[...] + ff, g_ref[...], b_ref[...])


def _ln2(h1, yk, g, b, alpha, row_off, n, tm):
    d = h1.shape[1]
    off = row_off // tm
    vec = pl.BlockSpec((1, d), lambda i: (0, 0))
    return pl.pallas_call(
        functools.partial(_ln2_body, alpha=alpha),
        grid=(n // tm,),
        in_specs=[pl.BlockSpec((tm, d), lambda i: (off + i, 0)),
                  pl.BlockSpec((yk.shape[0], tm, d), lambda i: (0, off + i, 0)), vec, vec],
        out_specs=pl.BlockSpec((tm, d), lambda i: (i, 0)),
        out_shape=jax.ShapeDtypeStruct((n, d), F32),
        compiler_params=_cparams("parallel"),
        name="combine_ln2",
    )(h1, yk, g, b)


def _row2(v):
    return v.reshape(1, -1).astype(F32)


def kernel(x_prompt, x_sample, cache_k, cache_v, page_table, state_conv, w_in, b_in, lambda_q1, lambda_k1,
           lambda_q2, lambda_k2, subln_g, rel_bias, w_attn_proj, conv_w, conv_b, conv_ln_g, conv_ln_b,
           w_conv_proj, b_conv_proj, w_out, ln1_g, ln1_b, router_w, router_b, expert_w1, expert_b1,
           expert_w2, expert_b2, ln2_g, ln2_b):
    depth = w_in.shape[0]
    bp, seq, d = x_prompt.shape
    bs, t_new, _ = x_sample.shape
    d_att = N_HEADS * QK_DIM
    d_conv = conv_w.shape[2]
    n_p, n_s = bp * seq, bs * t_new
    n_tot = n_p + n_s
    tm = min(TOKEN_TILE, math.gcd(n_p, n_s))
    assert tm % LANES == 0
    alpha = (2 * depth) ** 0.25

    hp = x_prompt.reshape(n_p, d)
    hs = x_sample.reshape(n_s, d)
    outs = [[] for _ in range(6)]
    for l in range(depth):
        lam_init = 0.8 - 0.6 * math.exp(-0.3 * l)
        lam = (jnp.exp(jnp.sum(lambda_q1[l].astype(F32) * lambda_k1[l].astype(F32)))
               - jnp.exp(jnp.sum(lambda_q2[l].astype(F32) * lambda_k2[l].astype(F32))) + lam_init).reshape(1)
        w_in_bf = w_in[l].astype(BF16)
        b_in_l = _row2(b_in[l])
        g_sub = _row2(subln_g[l])
        conv_args = (conv_w[l].astype(F32), _row2(conv_b[l]), _row2(conv_ln_g[l]), _row2(conv_ln_b[l]))
        merge_w = (w_attn_proj[l].astype(BF16), w_conv_proj[l].astype(BF16), _row2(b_conv_proj[l]),
                   w_out[l].astype(BF16), _row2(ln1_g[l]), _row2(ln1_b[l]),
                   router_w[l].astype(F32).T, router_b[l].astype(F32).reshape(-1, 1))

        qp, kp, vp, kbp, vbp, up, sgp = _in_proj(hp, w_in_bf, b_in_l, d_att, d_conv)
        op = _prompt_attention(qp.reshape(bp, seq, d_att), kbp.reshape(bp, seq, d_att), vbp.reshape(bp, seq, d_att),
                               rel_bias, lam, g_sub, lam_init)
        up3 = up.reshape(bp, seq, d_conv)
        yp = _conv_prompt(up3, *conv_args)
        merged = _merge(op.reshape(n_p, d_att), yp.reshape(n_p, d_conv), sgp, hp, merge_w, alpha, n_tot, 0, tm, None)

        qs, ks, vs, kbs, vbs, us, sgs = _in_proj(hs, w_in_bf, b_in_l, d_att, d_conv)
        osr = _sample_attention(qs.reshape(bs, t_new, d_att), kbs.reshape(bs, t_new, d_att),
                                vbs.reshape(bs, t_new, d_att), cache_k[l], cache_v[l], page_table,
                                rel_bias, lam, g_sub, lam_init)
        os_ = osr[:, :t_new].reshape(n_s, d_att)
        buf_s = jnp.concatenate([state_conv[l].astype(F32), us.reshape(bs, t_new, d_conv)], axis=1)
        ys = _conv_sample(buf_s.transpose(1, 0, 2), *conv_args).transpose(1, 0, 2).reshape(n_s, d_conv)
        h1, topi, topw = _merge(os_, ys, sgs, hs, merge_w, alpha, n_tot, n_p, tm, merged)

        yk = _moe(h1, topi, topw, expert_w1[l].astype(BF16), expert_b1[l].astype(F32),
                  expert_w2[l].astype(BF16), expert_b2[l].astype(F32))
        g2, b2 = _row2(ln2_g[l]), _row2(ln2_b[l])
        hp = _ln2(h1, yk, g2, b2, alpha, 0, n_p, tm)
        hs = _ln2(h1, yk, g2, b2, alpha, n_p, n_s, tm)

        w1 = CONV_WIDTH - 1
        cp = up3[:, seq - w1:] if seq >= w1 else jnp.concatenate(
            [jnp.zeros((bp, w1 - seq, d_conv), F32), up3], axis=1)
        for lst, val in zip(outs, (kp.reshape(bp, seq, N_HEADS, QK_DIM), vp.reshape(bp, seq, N_HEADS, V_DIM), cp,
                                   ks.reshape(bs, t_new, N_HEADS, QK_DIM), vs.reshape(bs, t_new, N_HEADS, V_DIM),
                                   buf_s[:, t_new:])):
            lst.append(val)
    return (hp.reshape(bp, seq, d), hs.reshape(bs, t_new, d)) + tuple(jnp.stack(o) for o in outs)
```

```python
import functools
import math

import jax
import jax.numpy as jnp
from jax import lax
from jax.experimental import pallas as pl
from jax.experimental.pallas import tpu as pltpu

F32 = jnp.float32
BF16 = jnp.bfloat16
I32 = jnp.int32

N_HEADS = 4
HEAD_DIM = 64
QK_DIM = 2 * HEAD_DIM
V_DIM = 2 * HEAD_DIM
ATTN_SCALE = HEAD_DIM ** -0.5
NEG_INF = -1e30
NUM_BUCKETS = 32
MAX_EXACT = NUM_BUCKETS // 2
MAX_DISTANCE = 128
CONV_WIDTH = 31
TOP_K = 4
SWIGLU_LIMIT = 7.0
SWIGLU_ALPHA = 1.702
LN_EPS = 1e-5
PAGE_SIZE = 128

LANES = 128
SUBLANES = 8
VMEM_LIMIT = 56 * 1024 * 1024

TOKEN_TILE = 512
ATTN_TILE = 256
MOE_TILE = 256
COMBINE_TILE = 256
CONV_HALO = 32
CONV_CHUNK = 64


def _cparams(*sem):
    return pltpu.CompilerParams(dimension_semantics=sem, vmem_limit_bytes=VMEM_LIMIT)


def _layer_norm(x, g, b):
    mu = jnp.mean(x, -1, keepdims=True)
    xc = x - mu
    var = jnp.mean(xc * xc, -1, keepdims=True)
    return xc * lax.rsqrt(var + LN_EPS) * g + b


def _in_proj_body(x_ref, w_ref, b_ref, q_ref, k_ref, v_ref, kb_ref, vb_ref, u_ref, sg_ref, *, d_att, d_conv):
    x = x_ref[...].astype(BF16)

    def seg(lo, hi):
        return jnp.dot(x, w_ref[:, lo:hi], preferred_element_type=F32) + b_ref[:, lo:hi]

    o = 0
    q_ref[...] = (seg(o, o + d_att) * ATTN_SCALE).astype(BF16)
    o += d_att
    k = seg(o, o + d_att)
    k_ref[...] = k
    kb_ref[...] = k.astype(BF16)
    o += d_att
    v = seg(o, o + d_att)
    v_ref[...] = v
    vb_ref[...] = v.astype(BF16)
    o += d_att
    c = seg(o, o + 2 * d_conv)
    u_ref[...] = c[:, :d_conv] * jax.nn.sigmoid(c[:, d_conv:])
    o += 2 * d_conv
    sg_ref[...] = jax.nn.sigmoid(seg(o, w_ref.shape[1])).astype(BF16)


def _in_proj(x, w_bf, b, d_att, d_conv):
    n, d = x.shape
    tm = min(TOKEN_TILE, n)
    row = lambda width: pl.BlockSpec((tm, width), lambda i: (i, 0))
    full = lambda a: pl.BlockSpec(a.shape, lambda i: (0, 0))
    out_shape = (
        jax.ShapeDtypeStruct((n, d_att), BF16),
        jax.ShapeDtypeStruct((n, d_att), F32),
        jax.ShapeDtypeStruct((n, d_att), F32),
        jax.ShapeDtypeStruct((n, d_att), BF16),
        jax.ShapeDtypeStruct((n, d_att), BF16),
        jax.ShapeDtypeStruct((n, d_conv), F32),
        jax.ShapeDtypeStruct((n, 2 * d), BF16),
    )
    return pl.pallas_call(
        functools.partial(_in_proj_body, d_att=d_att, d_conv=d_conv),
        grid=(n // tm,),
        in_specs=[row(d), full(w_bf), full(b)],
        out_specs=(row(d_att), row(d_att), row(d_att), row(d_att), row(d_att), row(d_conv), row(2 * d)),
        out_shape=out_shape,
        compiler_params=_cparams("parallel"),
        name="in_proj",
    )(x, w_bf, b)


def _t5_bucket(dist):
    n = jnp.maximum(dist, 0)
    nf = jnp.maximum(n, 1).astype(F32)
    large = MAX_EXACT + (jnp.log(nf / MAX_EXACT) / math.log(MAX_DISTANCE / MAX_EXACT)
                         * (NUM_BUCKETS - MAX_EXACT)).astype(I32)
    large = jnp.minimum(large, NUM_BUCKETS - 1)
    return jnp.where(n < MAX_EXACT, n, large)


def _bias_of(dist, rel_bias, shift=None):
    b = rel_bias[_t5_bucket(dist)].astype(F32)
    if shift is not None:
        b = b - shift
    b = jnp.where((dist >= 0)[..., None], b, NEG_INF)
    return jnp.moveaxis(b, -1, 0)


def _flash_body(lam_ref, qt_ref, k_ref, vt_ref, bias_ref, g_ref, o_ref, m_s, l_s, acc_s, *, lam_init):
    i = pl.program_id(2)
    qt = qt_ref[...]
    sub = lax.broadcasted_iota(I32, (QK_DIM, 1), 0)
    zero = jnp.zeros_like(qt)
    qs = (jnp.where(sub < HEAD_DIM, qt, zero), jnp.where(sub >= HEAD_DIM, qt, zero))
    m_s[...] = jnp.full(m_s.shape, NEG_INF, F32)
    l_s[...] = jnp.zeros(l_s.shape, F32)
    acc_s[...] = jnp.zeros(acc_s.shape, F32)

    def step(j, bias):
        k = k_ref[j]
        vt = vt_ref[j]
        for c in range(2):
            s = jnp.dot(k, qs[c], preferred_element_type=F32)
            if bias is not None:
                s = s + bias
            m_old = m_s[c]
            m_new = jnp.maximum(m_old, jnp.max(s, 0, keepdims=True))
            alpha = jnp.exp(m_old - m_new)
            p = jnp.exp(s - m_new)
            l_s[c] = alpha * l_s[c] + jnp.sum(p, 0, keepdims=True)
            acc_s[c] = alpha * acc_s[c] + jnp.dot(vt, p.astype(BF16), preferred_element_type=F32)
            m_s[c] = m_new

    def far_step(j, carry):
        step(j, None)
        return carry

    lax.fori_loop(0, jnp.maximum(i - 1, 0), far_step, 0)

    @pl.when(i >= 1)
    def _():
        step(i - 1, bias_ref[1])

    step(i, bias_ref[0])

    o = acc_s[0] / l_s[0] - lam_ref[0] * (acc_s[1] / l_s[1])
    o = o * lax.rsqrt(jnp.mean(o * o, 0, keepdims=True) + LN_EPS) * g_ref[...]
    o_ref[0] = (o * (1.0 - lam_init)).T.astype(BF16)


def _prompt_attention(q, kb, vb, rel_bias, lam, subln_g, lam_init):
    bsz, seq, width = q.shape
    t = min(ATTN_TILE, seq)
    assert t >= MAX_DISTANCE and seq % t == 0
    nt = seq // t
    far = rel_bias[NUM_BUCKETS - 1].astype(F32)
    r = jnp.arange(t, dtype=I32)
    d0 = r[None, :] - r[:, None]
    bias = jnp.stack([_bias_of(d0, rel_bias, far), _bias_of(d0 + t, rel_bias, far)], 1)
    split = lambda a: a.reshape(bsz, nt, t, N_HEADS, QK_DIM)
    qt = split(q).transpose(0, 3, 1, 4, 2)
    k4 = split(kb).transpose(0, 3, 1, 2, 4)
    vt = split(vb).transpose(0, 3, 1, 4, 2)
    return pl.pallas_call(
        functools.partial(_flash_body, lam_init=lam_init),
        grid=(bsz, N_HEADS, nt),
        in_specs=[
            pl.BlockSpec(memory_space=pltpu.SMEM),
            pl.BlockSpec((None, None, None, QK_DIM, t), lambda b, h, i: (b, h, i, 0, 0)),
            pl.BlockSpec((None, None, nt, t, QK_DIM), lambda b, h, i: (b, h, 0, 0, 0)),
            pl.BlockSpec((None, None, nt, V_DIM, t), lambda b, h, i: (b, h, 0, 0, 0)),
            pl.BlockSpec((None, 2, t, t), lambda b, h, i: (h, 0, 0, 0)),
            pl.BlockSpec((V_DIM, 1), lambda b, h, i: (0, 0)),
        ],
        out_specs=pl.BlockSpec((1, t, V_DIM), lambda b, h, i: (b, i, h)),
        out_shape=jax.ShapeDtypeStruct((bsz, seq, width), BF16),
        scratch_shapes=[pltpu.VMEM((2, 1, t), F32), pltpu.VMEM((2, 1, t), F32), pltpu.VMEM((2, V_DIM, t), F32)],
        compiler_params=_cparams("parallel", "parallel", "arbitrary"),
        name="prompt_attention",
    )(lam, qt, k4, vt, bias, subln_g.reshape(V_DIM, 1))


def _paged_body(pt_ref, lam_ref, q_ref, kn_ref, vn_ref, bias_ref, g_ref, *refs, n_pages, t_new, lam_init):
    k_refs = refs[:n_pages]
    v_refs = refs[n_pages:2 * n_pages]
    o_ref = refs[2 * n_pages]
    lam = lam_ref[0]
    rows = 2 * t_new
    cols = PAGE_SIZE * N_HEADS
    nt = (((1,), (1,)), ((), ()))
    q = q_ref[0]
    parts = [lax.dot_general(q, k_refs[p][...].astype(BF16), nt, preferred_element_type=F32)
             for p in range(n_pages)]
    parts.append(lax.dot_general(q, kn_ref[0], nt, preferred_element_type=F32))
    s = jnp.concatenate(parts, axis=1) + bias_ref[...]
    m = jnp.max(s, -1, keepdims=True)
    p_un = jnp.exp(s - m)
    pn = p_un / jnp.sum(p_un, -1, keepdims=True)
    w = (pn - lam * pltpu.roll(pn, N_HEADS * rows - t_new, axis=0)).astype(BF16)
    o = jnp.dot(w[:, n_pages * cols:], vn_ref[0], preferred_element_type=F32)
    for p in range(n_pages):
        o = o + jnp.dot(w[:, p * cols:(p + 1) * cols], v_refs[p][...].astype(BF16), preferred_element_type=F32)
    o = o * lax.rsqrt(jnp.mean(o * o, -1, keepdims=True) + LN_EPS) * g_ref[...]
    o = (o * (1.0 - lam_init)).astype(BF16)
    o_ref[0] = jnp.concatenate([o[h * rows:(h + 1) * rows] for h in range(N_HEADS)], axis=1)


def _sample_attention(q, kb, vb, cache_k, cache_v, page_table, rel_bias, lam, subln_g, lam_init):
    bs, t_new, width = q.shape
    n_pool = cache_k.shape[0]
    n_pages = page_table.shape[1]
    past = n_pages * PAGE_SIZE
    rows = 2 * t_new
    cols = PAGE_SIZE * N_HEADS
    new_pos = LANES // N_HEADS
    assert rows % SUBLANES == 0 and t_new <= new_pos
    qh = q.reshape(bs, t_new, N_HEADS, QK_DIM).transpose(0, 2, 1, 3)
    lane_map = (jnp.arange(QK_DIM) // HEAD_DIM)[None, :] == jnp.arange(2)[:, None]
    q_rows = (qh[:, :, None] * lane_map[None, None, :, None, :].astype(BF16)).reshape(bs, N_HEADS * rows, QK_DIM)
    pad = ((0, 0), (0, LANES - t_new * N_HEADS), (0, 0))
    kn = jnp.pad(kb.reshape(bs, t_new * N_HEADS, QK_DIM), pad)
    vn = jnp.pad(vb.reshape(bs, t_new * N_HEADS, V_DIM), pad)
    qpos = past + jnp.arange(t_new, dtype=I32)
    kpos = jnp.concatenate([jnp.arange(past + t_new, dtype=I32),
                            jnp.full((new_pos - t_new,), past + t_new + new_pos, I32)])
    b = _bias_of(qpos[:, None] - kpos[None, :], rel_bias)
    same = jnp.arange(N_HEADS)[:, None] == jnp.arange(N_HEADS)[None, :]
    b = jnp.where(same[:, None, None, :], b[..., None], NEG_INF)
    b = b.reshape(N_HEADS, 1, t_new, -1)
    bias = jnp.concatenate([b, b], axis=1).reshape(N_HEADS * rows, -1)

    page_spec = lambda j: pl.BlockSpec((None, cols, QK_DIM), lambda b, pt, j=j: (pt[b * n_pages + j], 0, 0))
    per_seq = lambda r, c: pl.BlockSpec((1, r, c), lambda b, pt: (b, 0, 0))
    grid_spec = pltpu.PrefetchScalarGridSpec(
        num_scalar_prefetch=1,
        grid=(bs,),
        in_specs=[
            pl.BlockSpec(memory_space=pltpu.SMEM),
            per_seq(N_HEADS * rows, QK_DIM),
            per_seq(LANES, QK_DIM),
            per_seq(LANES, V_DIM),
            pl.BlockSpec(bias.shape, lambda b, pt: (0, 0)),
            pl.BlockSpec((1, V_DIM), lambda b, pt: (0, 0)),
        ] + [page_spec(j) for j in range(n_pages)] * 2,
        out_specs=pl.BlockSpec((1, rows, width), lambda b, pt: (b, 0, 0)),
    )
    ck = cache_k.reshape(n_pool, cols, QK_DIM)
    cv = cache_v.reshape(n_pool, cols, V_DIM)
    return pl.pallas_call(
        functools.partial(_paged_body, n_pages=n_pages, t_new=t_new, lam_init=lam_init),
        grid_spec=grid_spec,
        out_shape=jax.ShapeDtypeStruct((bs, rows, width), BF16),
        compiler_params=_cparams("parallel"),
        name="sample_attention",
    )(page_table.reshape(-1), lam, q_rows, kn, vn, bias, subln_g, *([ck] * n_pages), *([cv] * n_pages))


def _conv_prompt_body(u_ref, halo_ref, w_ref, cb_ref, g_ref, b_ref, o_ref, buf, *, tc):
    i = pl.program_id(1)
    halo = halo_ref[0]
    buf[0:CONV_HALO, :] = jnp.where(i > 0, halo, jnp.zeros_like(halo))
    buf[CONV_HALO:, :] = u_ref[0]
    first = CONV_HALO - (CONV_WIDTH - 1)
    for r0 in range(0, tc, CONV_CHUNK):
        acc = jnp.zeros((CONV_CHUNK, u_ref.shape[2]), F32) + cb_ref[...]
        for j in range(CONV_WIDTH):
            acc = acc + w_ref[j:j + 1, :] * buf[first + r0 + j:first + r0 + j + CONV_CHUNK, :]
        y = _layer_norm(acc, g_ref[...], b_ref[...])
        o_ref[0, r0:r0 + CONV_CHUNK, :] = (y * jax.nn.sigmoid(y)).astype(BF16)


def _conv_prompt(u, conv_w, conv_b, ln_g, ln_b):
    bsz, seq, ch = u.shape
    tc = min(TOKEN_TILE, seq)
    assert seq % tc == 0 and tc % CONV_CHUNK == 0 and tc % CONV_HALO == 0
    vec = pl.BlockSpec((1, ch), lambda b, i: (0, 0))
    return pl.pallas_call(
        functools.partial(_conv_prompt_body, tc=tc),
        grid=(bsz, seq // tc),
        in_specs=[
            pl.BlockSpec((1, tc, ch), lambda b, i: (b, i, 0)),
            pl.BlockSpec((1, CONV_HALO, ch), lambda b, i: (b, jnp.maximum(i * (tc // CONV_HALO) - 1, 0), 0)),
            pl.BlockSpec((CONV_WIDTH, ch), lambda b, i: (0, 0)),
            vec, vec, vec,
        ],
        out_specs=pl.BlockSpec((1, tc, ch), lambda b, i: (b, i, 0)),
        out_shape=jax.ShapeDtypeStruct((bsz, seq, ch), BF16),
        scratch_shapes=[pltpu.VMEM((CONV_HALO + tc, ch), F32)],
        compiler_params=_cparams("parallel", "arbitrary"),
        name="conv_prompt",
    )(u, u, conv_w, conv_b, ln_g, ln_b)


def _conv_sample_body(buf_ref, w_ref, cb_ref, g_ref, b_ref, o_ref, *, t_new):
    for t in range(t_new):
        acc = jnp.zeros(buf_ref.shape[1:], F32) + cb_ref[...]
        for j in range(CONV_WIDTH):
            acc = acc + w_ref[j:j + 1, :] * buf_ref[t + j]
        y = _layer_norm(acc, g_ref[...], b_ref[...])
        o_ref[t] = (y * jax.nn.sigmoid(y)).astype(BF16)


def _conv_sample(buf_t, conv_w, conv_b, ln_g, ln_b):
    rows, bs, ch = buf_t.shape
    t_new = rows - (CONV_WIDTH - 1)
    gb = min(32, bs)
    assert bs % gb == 0
    vec = pl.BlockSpec((1, ch), lambda i: (0, 0))
    return pl.pallas_call(
        functools.partial(_conv_sample_body, t_new=t_new),
        grid=(bs // gb,),
        in_specs=[pl.BlockSpec((rows, gb, ch), lambda i: (0, i, 0)),
                  pl.BlockSpec((CONV_WIDTH, ch), lambda i: (0, 0)), vec, vec, vec],
        out_specs=pl.BlockSpec((t_new, gb, ch), lambda i: (0, i, 0)),
        out_shape=jax.ShapeDtypeStruct((t_new, bs, ch), BF16),
        compiler_params=_cparams("parallel"),
        name="conv_sample",
    )(buf_t, conv_w, conv_b, ln_g, ln_b)


N_MERGE_OUT = 5


def _merge_body(*refs, alpha, n_alias):
    (o_ref, y_ref, sg_ref, x_ref, wa_ref, wc_ref, bc_ref, wo_ref, g1_ref, b1_ref, rw_ref, rb_ref) = refs[:12]
    h1_ref, ti_ref, rk_ref, wcol_ref, cnt_ref = refs[12 + n_alias:]
    tm, d = x_ref.shape
    a = jnp.dot(o_ref[...], wa_ref[...], preferred_element_type=F32)
    b = jnp.dot(y_ref[...], wc_ref[...], preferred_element_type=F32) + bc_ref[...]
    mix_in = sg_ref[:, :d].astype(F32) * a + sg_ref[:, d:].astype(F32) * b
    mix = jnp.dot(mix_in.astype(BF16), wo_ref[...], preferred_element_type=F32)
    h1 = _layer_norm(alpha * x_ref[...] + mix, g1_ref[...], b1_ref[...])
    h1_ref[...] = h1
    logits = lax.dot_general(rw_ref[...], h1, (((1,), (1,)), ((), ())), preferred_element_type=F32,
                             precision=lax.Precision.HIGHEST) + rb_ref[...]
    n_e = logits.shape[0]
    eid = lax.broadcasted_iota(I32, logits.shape, 0)
    vals, idxs, hots = [], [], []
    for _ in range(TOP_K):
        m = jnp.max(logits, 0, keepdims=True)
        idx = jnp.min(jnp.where(logits == m, eid, n_e), 0, keepdims=True)
        hot = eid == idx
        vals.append(m)
        idxs.append(idx)
        hots.append(hot)
        logits = jnp.where(hot, -jnp.inf, logits)
    ex = [jnp.exp(v - vals[0]) for v in vals]
    den = ex[0]
    for e in ex[1:]:
        den = den + e
    ti_ref[...] = jnp.concatenate(idxs, 0)
    wrows = jnp.concatenate([e / den for e in ex] + [jnp.zeros((LANES - TOP_K, tm), F32)], 0)
    wcol_ref[...] = wrows.T
    sel = hots[0]
    for hot in hots[1:]:
        sel = sel | hot
    sel_f = jnp.where(sel, 1.0, 0.0)
    before = lax.broadcasted_iota(I32, (tm, tm), 0) < lax.broadcasted_iota(I32, (tm, tm), 1)
    upper = jnp.where(before, 1.0, 0.0).astype(BF16)
    ahead = jnp.dot(sel_f.astype(BF16), upper, preferred_element_type=F32)
    rk_ref[...] = jnp.concatenate(
        [jnp.sum(jnp.where(hot, ahead, 0.0), 0, keepdims=True) for hot in hots], 0).astype(I32)
    cnt = jnp.sum(sel_f, 1, keepdims=True).astype(I32)
    cnt_ref[0] = jnp.broadcast_to(cnt, (n_e, LANES))


def _merge(o, yact, sg, x, wts, alpha, n_total, row_off, tm, prev):
    n, d = x.shape
    wa, wc, bc, wo, g1, b1, rwt, rb = wts
    n_e = rwt.shape[0]
    off = row_off // tm
    last = n // tm - 1
    steps = n // tm if prev is not None else n_total // tm
    row = lambda width: pl.BlockSpec((tm, width), lambda i: (jnp.minimum(i, last), 0))
    full = lambda a: pl.BlockSpec(a.shape, lambda i: (0, 0))
    in_specs = [row(o.shape[1]), row(yact.shape[1]), row(2 * d), row(d),
                full(wa), full(wc), full(bc), full(wo), full(g1), full(b1), full(rwt), full(rb)]
    args = [o, yact, sg, x, wa, wc, bc, wo, g1, b1, rwt, rb]
    aliases = {}
    if prev is not None:
        in_specs += [pl.BlockSpec(memory_space=pl.ANY)] * N_MERGE_OUT
        aliases = {len(args) + j: j for j in range(N_MERGE_OUT)}
        args += list(prev)
    slot_major = pl.BlockSpec((TOP_K, tm), lambda i: (0, off + i))
    return pl.pallas_call(
        functools.partial(_merge_body, alpha=alpha, n_alias=0 if prev is None else N_MERGE_OUT),
        grid=(steps,),
        in_specs=in_specs,
        out_specs=(pl.BlockSpec((tm, d), lambda i: (off + i, 0)), slot_major, slot_major,
                   pl.BlockSpec((tm, LANES), lambda i: (off + i, 0)),
                   pl.BlockSpec((1, n_e, LANES), lambda i: (off + i, 0, 0))),
        out_shape=(jax.ShapeDtypeStruct((n_total, d), F32),
                   jax.ShapeDtypeStruct((TOP_K, n_total), I32),
                   jax.ShapeDtypeStruct((TOP_K, n_total), I32),
                   jax.ShapeDtypeStruct((n_total, LANES), F32),
                   jax.ShapeDtypeStruct((n_total // tm, n_e, LANES), I32)),
        input_output_aliases=aliases,
        compiler_params=_cparams("parallel"),
        name="merge_ln1_router",
    )(*args)


def _plan(topi, rank, cnt, tm_tok, tm_moe):
    k, n = topi.shape
    n_e = cnt.shape[1]
    before = jnp.cumsum(cnt, 0) - cnt
    total = jnp.sum(cnt, 0)
    ntile_e = (total + tm_moe - 1) // tm_moe
    tend = jnp.cumsum(ntile_e)
    tstart = tend - ntile_e
    base = (tstart * tm_moe)[None, :] + before
    base_tok = jnp.repeat(base, tm_tok, axis=0)
    hot = topi[:, :, None] == jnp.arange(n_e, dtype=I32)[None, None, :]
    pos = jnp.sum(jnp.where(hot, base_tok[None], 0), -1).astype(I32) + rank
    n_tiles = (k * n + tm_moe - 1) // tm_moe + n_e
    n_used = tend[-1]
    tid = jnp.arange(n_tiles, dtype=I32)
    src = jnp.minimum(tid, n_used - 1)
    te = jnp.minimum(jnp.sum((src[:, None] >= tend[None, :]).astype(I32), 1), n_e - 1)
    first = (tid == tstart[te]).astype(I32)
    last_tile = jnp.where(ntile_e > 0, tend - 1, -1).astype(I32)
    return pos, te.astype(I32), first, n_used.reshape(1).astype(I32), last_tile, n_tiles


def _row_wait(src_hbm, dst, sem, rows):
    pltpu.make_async_copy(src_hbm.at[pl.ds(0, rows), :], dst.at[pl.ds(0, rows), :], sem).wait()


def _dispatch_body(lt_ref, nu_ref, pos_ref, x_hbm, xs_hbm, zbuf, zsem, sem, *, tm, tm_moe, n_e, n_tiles):
    i = pl.program_id(0)

    def zero_copy(tile):
        start = pl.multiple_of(tile * tm_moe, tm_moe)
        return pltpu.make_async_copy(zbuf, xs_hbm.at[pl.ds(start, tm_moe), :], zsem)

    @pl.when(i == 0)
    def _():
        zbuf[...] = jnp.zeros(zbuf.shape, F32)

        def z_start(e, c):
            @pl.when(lt_ref[e] >= 0)
            def _():
                zero_copy(lt_ref[e]).start()
            return c

        def z_wait(e, c):
            @pl.when(lt_ref[e] >= 0)
            def _():
                zero_copy(lt_ref[e]).wait()
            return c

        def t_start(t, c):
            zero_copy(t).start()
            return c

        def t_wait(t, c):
            zero_copy(t).wait()
            return c

        lax.fori_loop(0, n_e, z_start, 0)
        lax.fori_loop(nu_ref[0], n_tiles, t_start, 0)
        lax.fori_loop(0, n_e, z_wait, 0)
        lax.fori_loop(nu_ref[0], n_tiles, t_wait, 0)

    def issue(r, c):
        src = x_hbm.at[pl.ds(i * tm + r, 1), :]
        for k in range(TOP_K):
            pltpu.make_async_copy(src, xs_hbm.at[pl.ds(pos_ref[k, r], 1), :], sem).start()
        return c

    lax.fori_loop(0, tm, issue, 0)
    for k in range(TOP_K):
        _row_wait(x_hbm, xs_hbm, sem, tm)


def _dispatch(h1, pos, last_tile, n_used, n_tiles, tm, tm_moe):
    n, d = h1.shape
    n_e = last_tile.shape[0]
    grid_spec = pltpu.PrefetchScalarGridSpec(
        num_scalar_prefetch=2,
        grid=(n // tm,),
        in_specs=[pl.BlockSpec((TOP_K, tm), lambda i, lt, nu: (0, i), memory_space=pltpu.SMEM),
                  pl.BlockSpec(memory_space=pl.ANY)],
        out_specs=pl.BlockSpec(memory_space=pl.ANY),
        scratch_shapes=[pltpu.VMEM((tm_moe, d), F32), pltpu.SemaphoreType.DMA, pltpu.SemaphoreType.DMA],
    )
    return pl.pallas_call(
        functools.partial(_dispatch_body, tm=tm, tm_moe=tm_moe, n_e=n_e, n_tiles=n_tiles),
        grid_spec=grid_spec,
        out_shape=jax.ShapeDtypeStruct((n_tiles * tm_moe, d), F32),
        compiler_params=_cparams("arbitrary"),
        name="moe_dispatch",
    )(last_tile, n_used, pos, h1)


def _experts_body(te_ref, first_ref, nu_ref, x_ref, w1_ref, b1_ref, w2_ref, b2_ref, y_ref, w1b, w2b):
    i = pl.program_id(0)
    d_ff = w2_ref.shape[1]

    @pl.when(i >= nu_ref[0])
    def _():
        y_ref[...] = jnp.zeros(y_ref.shape, F32)

    @pl.when(i < nu_ref[0])
    def _():
        @pl.when(first_ref[i] == 1)
        def _():
            w1b[...] = w1_ref[0].astype(BF16)
            w2b[...] = w2_ref[0].astype(BF16)

        x = x_ref[...].astype(BF16)
        hh = jnp.dot(x, w1b[...], preferred_element_type=F32) + b1_ref[0]
        g = jnp.minimum(hh[:, :d_ff], SWIGLU_LIMIT)
        u = jnp.clip(hh[:, d_ff:], -SWIGLU_LIMIT, SWIGLU_LIMIT)
        act = (u + 1.0) * g * jax.nn.sigmoid(SWIGLU_ALPHA * g)
        y_ref[...] = jnp.dot(act.astype(BF16), w2b[...], preferred_element_type=F32) + b2_ref[0]


def _experts(xs, te, first, n_used, w1, b1, w2, b2, tm):
    n_rows, d = xs.shape
    n_e, _, f2 = w1.shape
    by_tile = lambda i, te, first, nu: (i, 0)
    by_expert = lambda i, te, first, nu: (te[i], 0, 0)
    grid_spec = pltpu.PrefetchScalarGridSpec(
        num_scalar_prefetch=3,
        grid=(n_rows // tm,),
        in_specs=[
            pl.BlockSpec((tm, d), by_tile),
            pl.BlockSpec((1, d, f2), by_expert),
            pl.BlockSpec((1, 1, f2), by_expert),
            pl.BlockSpec((1, f2 // 2, d), by_expert),
            pl.BlockSpec((1, 1, d), by_expert),
        ],
        out_specs=pl.BlockSpec((tm, d), by_tile),
        scratch_shapes=[pltpu.VMEM((d, f2), BF16), pltpu.VMEM((f2 // 2, d), BF16)],
    )
    return pl.pallas_call(
        _experts_body,
        grid_spec=grid_spec,
        out_shape=jax.ShapeDtypeStruct((n_rows, d), F32),
        compiler_params=_cparams("arbitrary"),
        name="moe_experts",
    )(te, first, n_used, xs, w1, b1.reshape(n_e, 1, f2), w2, b2.reshape(n_e, 1, d))


def _combine_body(pos_ref, h1_ref, wcol_ref, ys_hbm, g_ref, b_ref, o_ref, ybuf, sem, *, alpha, tm):
    def issue(r, c):
        for k in range(TOP_K):
            pltpu.make_async_copy(ys_hbm.at[pl.ds(pos_ref[k, r], 1), :], ybuf.at[k, pl.ds(r, 1), :], sem).start()
        return c

    lax.fori_loop(0, tm, issue, 0)
    for k in range(TOP_K):
        _row_wait(ys_hbm, ybuf.at[k], sem, tm)
    ff = wcol_ref[:, 0:1] * ybuf[0]
    for k in range(1, TOP_K):
        ff = ff + wcol_ref[:, k:k + 1] * ybuf[k]
    o_ref[...] = _layer_norm(alpha * h1_ref[...] + ff, g_ref[...], b_ref[...])


def _combine(h1, wcol, pos, ys, g, b, alpha, row_off, n):
    d = h1.shape[1]
    tm = min(COMBINE_TILE, n)
    off = row_off // tm
    vec = pl.BlockSpec((1, d), lambda i: (0, 0))
    return pl.pallas_call(
        functools.partial(_combine_body, alpha=alpha, tm=tm),
        grid=(n // tm,),
        in_specs=[pl.BlockSpec((TOP_K, tm), lambda i: (0, off + i), memory_space=pltpu.SMEM),
                  pl.BlockSpec((tm, d), lambda i: (off + i, 0)),
                  pl.BlockSpec((tm, LANES), lambda i: (off + i, 0)),
                  pl.BlockSpec(memory_space=pl.ANY), vec, vec],
        out_specs=pl.BlockSpec((tm, d), lambda i: (i, 0)),
        out_shape=jax.ShapeDtypeStruct((n, d), F32),
        scratch_shapes=[pltpu.VMEM((TOP_K, tm, d), F32), pltpu.SemaphoreType.DMA],
        compiler_params=_cparams("arbitrary"),
        name="combine_ln2",
    )(pos, h1, wcol, ys, g, b)


def _row2(v):
    return v.reshape(1, -1).astype(F32)


def kernel(x_prompt, x_sample, cache_k, cache_v, page_table, state_conv, w_in, b_in, lambda_q1, lambda_k1,
           lambda_q2, lambda_k2, subln_g, rel_bias, w_attn_proj, conv_w, conv_b, conv_ln_g, conv_ln_b,
           w_conv_proj, b_conv_proj, w_out, ln1_g, ln1_b, router_w, router_b, expert_w1, expert_b1,
           expert_w2, expert_b2, ln2_g, ln2_b):
    depth = w_in.shape[0]
    bp, seq, d = x_prompt.shape
    bs, t_new, _ = x_sample.shape
    d_att = N_HEADS * QK_DIM
    d_conv = conv_w.shape[2]
    n_p, n_s = bp * seq, bs * t_new
    n_tot = n_p + n_s
    tm = min(TOKEN_TILE, math.gcd(n_p, n_s))
    assert tm % LANES == 0
    tm_moe = min(MOE_TILE, n_tot)
    alpha = (2 * depth) ** 0.25

    hp = x_prompt.reshape(n_p, d)
    hs = x_sample.reshape(n_s, d)
    outs = [[] for _ in range(6)]
    for l in range(depth):
        lam_init = 0.8 - 0.6 * math.exp(-0.3 * l)
        lam = (jnp.exp(jnp.sum(lambda_q1[l].astype(F32) * lambda_k1[l].astype(F32)))
               - jnp.exp(jnp.sum(lambda_q2[l].astype(F32) * lambda_k2[l].astype(F32))) + lam_init).reshape(1)
        w_in_bf = w_in[l].astype(BF16)
        b_in_l = _row2(b_in[l])
        g_sub = _row2(subln_g[l])
        conv_args = (conv_w[l].astype(F32), _row2(conv_b[l]), _row2(conv_ln_g[l]), _row2(conv_ln_b[l]))
        merge_w = (w_attn_proj[l].astype(BF16), w_conv_proj[l].astype(BF16), _row2(b_conv_proj[l]),
                   w_out[l].astype(BF16), _row2(ln1_g[l]), _row2(ln1_b[l]),
                   router_w[l].astype(F32).T, router_b[l].astype(F32).reshape(-1, 1))

        qp, kp, vp, kbp, vbp, up, sgp = _in_proj(hp, w_in_bf, b_in_l, d_att, d_conv)
        op = _prompt_attention(qp.reshape(bp, seq, d_att), kbp.reshape(bp, seq, d_att), vbp.reshape(bp, seq, d_att),
                               rel_bias, lam, g_sub, lam_init)
        up3 = up.reshape(bp, seq, d_conv)
        yp = _conv_prompt(up3, *conv_args)
        merged = _merge(op.reshape(n_p, d_att), yp.reshape(n_p, d_conv), sgp, hp, merge_w, alpha, n_tot, 0, tm, None)

        qs, ks, vs, kbs, vbs, us, sgs = _in_proj(hs, w_in_bf, b_in_l, d_att, d_conv)
        osr = _sample_attention(qs.reshape(bs, t_new, d_att), kbs.reshape(bs, t_new, d_att),
                                vbs.reshape(bs, t_new, d_att), cache_k[l], cache_v[l], page_table,
                                rel_bias, lam, g_sub, lam_init)
        os_ = osr[:, :t_new].reshape(n_s, d_att)
        buf_s = jnp.concatenate([state_conv[l].astype(F32), us.reshape(bs, t_new, d_conv)], axis=1)
        ys = _conv_sample(buf_s.transpose(1, 0, 2), *conv_args).transpose(1, 0, 2).reshape(n_s, d_conv)
        h1, topi, rank, wcol, cnt = _merge(os_, ys, sgs, hs, merge_w, alpha, n_tot, n_p, tm, merged)

        pos, te, first, n_used, last_tile, n_tiles = _plan(topi, rank, cnt[:, :, 0], tm, tm_moe)
        xs = _dispatch(h1, pos, last_tile, n_used, n_tiles, tm, tm_moe)
        ysort = _experts(xs, te, first, n_used, expert_w1[l], expert_b1[l].astype(F32),
                         expert_w2[l], expert_b2[l].astype(F32), tm_moe)
        g2, b2 = _row2(ln2_g[l]), _row2(ln2_b[l])
        hp = _combine(h1, wcol, pos, ysort, g2, b2, alpha, 0, n_p)
        hs = _combine(h1, wcol, pos, ysort, g2, b2, alpha, n_p, n_s)

        w1 = CONV_WIDTH - 1
        cp = up3[:, seq - w1:] if seq >= w1 else jnp.concatenate(
            [jnp.zeros((bp, w1 - seq, d_conv), F32), up3], axis=1)
        for lst, val in zip(outs, (kp.reshape(bp, seq, N_HEADS, QK_DIM), vp.reshape(bp, seq, N_HEADS, V_DIM), cp,
                                   ks.reshape(bs, t_new, N_HEADS, QK_DIM), vs.reshape(bs, t_new, N_HEADS, V_DIM),
                                   buf_s[:, t_new:])):
            lst.append(val)
    return (hp.reshape(bp, seq, d), hs.reshape(bs, t_new, d)) + tuple(jnp.stack(o) for o in outs)
```

```python
import functools
import math

import jax
import jax.numpy as jnp
from jax import lax
from jax.experimental import pallas as pl
from jax.experimental.pallas import tpu as pltpu

F32 = jnp.float32
BF16 = jnp.bfloat16
I32 = jnp.int32

N_HEADS = 4
HEAD_DIM = 64
QK_DIM = 2 * HEAD_DIM
V_DIM = 2 * HEAD_DIM
ATTN_SCALE = HEAD_DIM ** -0.5
NEG_INF = -1e30
NUM_BUCKETS = 32
MAX_EXACT = NUM_BUCKETS // 2
MAX_DISTANCE = 128
CONV_WIDTH = 31
TOP_K = 4
SWIGLU_LIMIT = 7.0
SWIGLU_ALPHA = 1.702
LN_EPS = 1e-5
PAGE_SIZE = 128

LANES = 128
SUBLANES = 8
VMEM_LIMIT = 56 * 1024 * 1024

TOKEN_TILE = 512
ATTN_TQ = 1024
ATTN_TK = 512
MOE_TILE = 256
COMBINE_TILE = 256
CONV_HALO = 32
CONV_CHUNK = 64


def _cparams(*sem):
    return pltpu.CompilerParams(dimension_semantics=sem, vmem_limit_bytes=VMEM_LIMIT)


def _layer_norm(x, g, b):
    mu = jnp.mean(x, -1, keepdims=True)
    xc = x - mu
    var = jnp.mean(xc * xc, -1, keepdims=True)
    return xc * lax.rsqrt(var + LN_EPS) * g + b


def _in_proj_body(x_ref, w_ref, b_ref, q_ref, k_ref, v_ref, kb_ref, vb_ref, u_ref, sg_ref, *, d_att, d_conv):
    x = x_ref[...].astype(BF16)

    def seg(lo, hi):
        return jnp.dot(x, w_ref[:, lo:hi], preferred_element_type=F32) + b_ref[:, lo:hi]

    o = 0
    q_ref[...] = (seg(o, o + d_att) * ATTN_SCALE).astype(BF16)
    o += d_att
    k = seg(o, o + d_att)
    k_ref[...] = k
    kb_ref[...] = k.astype(BF16)
    o += d_att
    v = seg(o, o + d_att)
    v_ref[...] = v
    vb_ref[...] = v.astype(BF16)
    o += d_att
    c = seg(o, o + 2 * d_conv)
    u_ref[...] = c[:, :d_conv] * jax.nn.sigmoid(c[:, d_conv:])
    o += 2 * d_conv
    sg_ref[...] = jax.nn.sigmoid(seg(o, w_ref.shape[1])).astype(BF16)


def _in_proj(x, w_bf, b, d_att, d_conv):
    n, d = x.shape
    tm = min(TOKEN_TILE, n)
    row = lambda width: pl.BlockSpec((tm, width), lambda i: (i, 0))
    full = lambda a: pl.BlockSpec(a.shape, lambda i: (0, 0))
    out_shape = (
        jax.ShapeDtypeStruct((n, d_att), BF16),
        jax.ShapeDtypeStruct((n, d_att), F32),
        jax.ShapeDtypeStruct((n, d_att), F32),
        jax.ShapeDtypeStruct((n, d_att), BF16),
        jax.ShapeDtypeStruct((n, d_att), BF16),
        jax.ShapeDtypeStruct((n, d_conv), F32),
        jax.ShapeDtypeStruct((n, 2 * d), BF16),
    )
    return pl.pallas_call(
        functools.partial(_in_proj_body, d_att=d_att, d_conv=d_conv),
        grid=(n // tm,),
        in_specs=[row(d), full(w_bf), full(b)],
        out_specs=(row(d_att), row(d_att), row(d_att), row(d_att), row(d_att), row(d_conv), row(2 * d)),
        out_shape=out_shape,
        compiler_params=_cparams("parallel"),
        name="in_proj",
    )(x, w_bf, b)


def _t5_bucket(dist):
    n = jnp.maximum(dist, 0)
    nf = jnp.maximum(n, 1).astype(F32)
    large = MAX_EXACT + (jnp.log(nf / MAX_EXACT) / math.log(MAX_DISTANCE / MAX_EXACT)
                         * (NUM_BUCKETS - MAX_EXACT)).astype(I32)
    large = jnp.minimum(large, NUM_BUCKETS - 1)
    return jnp.where(n < MAX_EXACT, n, large)


def _bias_of(dist, rel_bias, shift=None):
    b = rel_bias[_t5_bucket(dist)].astype(F32)
    if shift is not None:
        b = b - shift
    b = jnp.where((dist >= 0)[..., None], b, NEG_INF)
    return jnp.moveaxis(b, -1, 0)


def _flash_body(lam_ref, qt_ref, k_ref, vt_ref, bias_ref, g_ref, o_ref, m_s, l_s, acc_s, *, ratio, lam_init):
    i = pl.program_id(2)
    qt = qt_ref[...]
    sub = lax.broadcasted_iota(I32, (QK_DIM, 1), 0)
    zero = jnp.zeros_like(qt)
    qs = (jnp.where(sub < HEAD_DIM, qt, zero), jnp.where(sub >= HEAD_DIM, qt, zero))
    m_s[...] = jnp.full(m_s.shape, NEG_INF, F32)
    l_s[...] = jnp.zeros(l_s.shape, F32)
    acc_s[...] = jnp.zeros(acc_s.shape, F32)

    def step(j, bias):
        k = k_ref[j]
        vt = vt_ref[j]
        for c in range(2):
            s = jnp.dot(k, qs[c], preferred_element_type=F32)
            if bias is not None:
                s = s + bias
            m_old = m_s[c]
            m_new = jnp.maximum(m_old, jnp.max(s, 0, keepdims=True))
            alpha = jnp.exp(m_old - m_new)
            p = jnp.exp(s - m_new)
            l_s[c] = alpha * l_s[c] + jnp.sum(p, 0, keepdims=True)
            acc_s[c] = alpha * acc_s[c] + jnp.dot(vt, p.astype(BF16), preferred_element_type=F32)
            m_s[c] = m_new

    def far_step(j, carry):
        step(j, None)
        return carry

    lax.fori_loop(0, jnp.maximum(i * ratio - 1, 0), far_step, 0)

    @pl.when(i >= 1)
    def _():
        step(i * ratio - 1, bias_ref[0])

    for o in range(ratio):
        step(i * ratio + o, bias_ref[o + 1])

    o = acc_s[0] / l_s[0] - lam_ref[0] * (acc_s[1] / l_s[1])
    o = o * lax.rsqrt(jnp.mean(o * o, 0, keepdims=True) + LN_EPS) * g_ref[...]
    o_ref[0] = (o * (1.0 - lam_init)).T.astype(BF16)


def _toeplitz(v, rows, cols):
    period = v.shape[-1]
    reps = -(-(rows * (period + 1)) // period)
    flat = jnp.tile(v, reps)[..., :rows * (period + 1)]
    hankel = flat.reshape(v.shape[:-1] + (rows, period + 1))[..., :cols]
    return jnp.flip(hankel, axis=-2)


def _prompt_attention(q, kb, vb, rel_bias, lam, subln_g, lam_init):
    bsz, seq, width = q.shape
    tq = min(ATTN_TQ, seq)
    tk = min(ATTN_TK, tq)
    assert tk >= MAX_DISTANCE and seq % tq == 0 and tq % tk == 0
    nq, nk, ratio = seq // tq, seq // tk, tq // tk
    far = rel_bias[NUM_BUCKETS - 1].astype(F32)
    period = tq + tk
    y = jnp.arange(period, dtype=I32)
    dist = y[None, :] - (tk - 1) - (jnp.arange(ratio + 1, dtype=I32)[:, None] - 1) * tk
    bias = _toeplitz(_bias_of(dist, rel_bias, far), tk, tq)
    qt = q.reshape(bsz, nq, tq, N_HEADS, QK_DIM).transpose(0, 3, 1, 4, 2)
    k4 = kb.reshape(bsz, nk, tk, N_HEADS, QK_DIM).transpose(0, 3, 1, 2, 4)
    vt = vb.reshape(bsz, nk, tk, N_HEADS, V_DIM).transpose(0, 3, 1, 4, 2)
    return pl.pallas_call(
        functools.partial(_flash_body, ratio=ratio, lam_init=lam_init),
        grid=(bsz, N_HEADS, nq),
        in_specs=[
            pl.BlockSpec(memory_space=pltpu.SMEM),
            pl.BlockSpec((None, None, None, QK_DIM, tq), lambda b, h, i: (b, h, i, 0, 0)),
            pl.BlockSpec((None, None, nk, tk, QK_DIM), lambda b, h, i: (b, h, 0, 0, 0)),
            pl.BlockSpec((None, None, nk, V_DIM, tk), lambda b, h, i: (b, h, 0, 0, 0)),
            pl.BlockSpec((None, ratio + 1, tk, tq), lambda b, h, i: (h, 0, 0, 0)),
            pl.BlockSpec((V_DIM, 1), lambda b, h, i: (0, 0)),
        ],
        out_specs=pl.BlockSpec((1, tq, V_DIM), lambda b, h, i: (b, i, h)),
        out_shape=jax.ShapeDtypeStruct((bsz, seq, width), BF16),
        scratch_shapes=[pltpu.VMEM((2, 1, tq), F32), pltpu.VMEM((2, 1, tq), F32), pltpu.VMEM((2, V_DIM, tq), F32)],
        compiler_params=_cparams("parallel", "parallel", "arbitrary"),
        name="prompt_attention",
    )(lam, qt, k4, vt, bias, subln_g.reshape(V_DIM, 1))


def _paged_body(pt_ref, lam_ref, q_ref, kn_ref, vn_ref, bias_ref, g_ref, *refs, n_pages, t_new, lam_init):
    k_refs = refs[:n_pages]
    v_refs = refs[n_pages:2 * n_pages]
    o_ref = refs[2 * n_pages]
    lam = lam_ref[0]
    rows = 2 * t_new
    cols = PAGE_SIZE * N_HEADS
    nt = (((1,), (1,)), ((), ()))
    q = q_ref[0]
    parts = [lax.dot_general(q, k_refs[p][...].astype(BF16), nt, preferred_element_type=F32)
             for p in range(n_pages)]
    parts.append(lax.dot_general(q, kn_ref[0], nt, preferred_element_type=F32))
    s = jnp.concatenate(parts, axis=1) + bias_ref[...]
    m = jnp.max(s, -1, keepdims=True)
    p_un = jnp.exp(s - m)
    pn = p_un / jnp.sum(p_un, -1, keepdims=True)
    w = (pn - lam * pltpu.roll(pn, N_HEADS * rows - t_new, axis=0)).astype(BF16)
    o = jnp.dot(w[:, n_pages * cols:], vn_ref[0], preferred_element_type=F32)
    for p in range(n_pages):
        o = o + jnp.dot(w[:, p * cols:(p + 1) * cols], v_refs[p][...].astype(BF16), preferred_element_type=F32)
    o = o * lax.rsqrt(jnp.mean(o * o, -1, keepdims=True) + LN_EPS) * g_ref[...]
    o = (o * (1.0 - lam_init)).astype(BF16)
    o_ref[0] = jnp.concatenate([o[h * rows:(h + 1) * rows] for h in range(N_HEADS)], axis=1)


def _sample_attention(q, kb, vb, cache_k, cache_v, page_table, rel_bias, lam, subln_g, lam_init):
    bs, t_new, width = q.shape
    n_pool = cache_k.shape[0]
    n_pages = page_table.shape[1]
    past = n_pages * PAGE_SIZE
    rows = 2 * t_new
    cols = PAGE_SIZE * N_HEADS
    new_pos = LANES // N_HEADS
    assert rows % SUBLANES == 0 and t_new <= new_pos
    qh = q.reshape(bs, t_new, N_HEADS, QK_DIM).transpose(0, 2, 1, 3)
    lane_map = (jnp.arange(QK_DIM) // HEAD_DIM)[None, :] == jnp.arange(2)[:, None]
    q_rows = (qh[:, :, None] * lane_map[None, None, :, None, :].astype(BF16)).reshape(bs, N_HEADS * rows, QK_DIM)
    pad = ((0, 0), (0, LANES - t_new * N_HEADS), (0, 0))
    kn = jnp.pad(kb.reshape(bs, t_new * N_HEADS, QK_DIM), pad)
    vn = jnp.pad(vb.reshape(bs, t_new * N_HEADS, V_DIM), pad)
    qpos = past + jnp.arange(t_new, dtype=I32)
    kpos = jnp.concatenate([jnp.arange(past + t_new, dtype=I32),
                            jnp.full((new_pos - t_new,), past + t_new + new_pos, I32)])
    b = _bias_of(qpos[:, None] - kpos[None, :], rel_bias)
    same = jnp.arange(N_HEADS)[:, None] == jnp.arange(N_HEADS)[None, :]
    b = jnp.where(same[:, None, None, :], b[..., None], NEG_INF)
    b = b.reshape(N_HEADS, 1, t_new, -1)
    bias = jnp.concatenate([b, b], axis=1).reshape(N_HEADS * rows, -1)

    page_spec = lambda j: pl.BlockSpec((None, cols, QK_DIM), lambda b, pt, j=j: (pt[b * n_pages + j], 0, 0))
    per_seq = lambda r, c: pl.BlockSpec((1, r, c), lambda b, pt: (b, 0, 0))
    grid_spec = pltpu.PrefetchScalarGridSpec(
        num_scalar_prefetch=1,
        grid=(bs,),
        in_specs=[
            pl.BlockSpec(memory_space=pltpu.SMEM),
            per_seq(N_HEADS * rows, QK_DIM),
            per_seq(LANES, QK_DIM),
            per_seq(LANES, V_DIM),
            pl.BlockSpec(bias.shape, lambda b, pt: (0, 0)),
            pl.BlockSpec((1, V_DIM), lambda b, pt: (0, 0)),
        ] + [page_spec(j) for j in range(n_pages)] * 2,
        out_specs=pl.BlockSpec((1, rows, width), lambda b, pt: (b, 0, 0)),
    )
    ck = cache_k.reshape(n_pool, cols, QK_DIM)
    cv = cache_v.reshape(n_pool, cols, V_DIM)
    return pl.pallas_call(
        functools.partial(_paged_body, n_pages=n_pages, t_new=t_new, lam_init=lam_init),
        grid_spec=grid_spec,
        out_shape=jax.ShapeDtypeStruct((bs, rows, width), BF16),
        compiler_params=_cparams("parallel"),
        name="sample_attention",
    )(page_table.reshape(-1), lam, q_rows, kn, vn, bias, subln_g, *([ck] * n_pages), *([cv] * n_pages))


def _conv_prompt_body(u_ref, halo_ref, w_ref, cb_ref, g_ref, b_ref, o_ref, buf, *, tc):
    i = pl.program_id(1)
    halo = halo_ref[0]
    buf[0:CONV_HALO, :] = jnp.where(i > 0, halo, jnp.zeros_like(halo))
    buf[CONV_HALO:, :] = u_ref[0]
    first = CONV_HALO - (CONV_WIDTH - 1)
    for r0 in range(0, tc, CONV_CHUNK):
        acc = jnp.zeros((CONV_CHUNK, u_ref.shape[2]), F32) + cb_ref[...]
        for j in range(CONV_WIDTH):
            acc = acc + w_ref[j:j + 1, :] * buf[first + r0 + j:first + r0 + j + CONV_CHUNK, :]
        y = _layer_norm(acc, g_ref[...], b_ref[...])
        o_ref[0, r0:r0 + CONV_CHUNK, :] = (y * jax.nn.sigmoid(y)).astype(BF16)


def _conv_prompt(u, conv_w, conv_b, ln_g, ln_b):
    bsz, seq, ch = u.shape
    tc = min(TOKEN_TILE, seq)
    assert seq % tc == 0 and tc % CONV_CHUNK == 0 and tc % CONV_HALO == 0
    vec = pl.BlockSpec((1, ch), lambda b, i: (0, 0))
    return pl.pallas_call(
        functools.partial(_conv_prompt_body, tc=tc),
        grid=(bsz, seq // tc),
        in_specs=[
            pl.BlockSpec((1, tc, ch), lambda b, i: (b, i, 0)),
            pl.BlockSpec((1, CONV_HALO, ch), lambda b, i: (b, jnp.maximum(i * (tc // CONV_HALO) - 1, 0), 0)),
            pl.BlockSpec((CONV_WIDTH, ch), lambda b, i: (0, 0)),
            vec, vec, vec,
        ],
        out_specs=pl.BlockSpec((1, tc, ch), lambda b, i: (b, i, 0)),
        out_shape=jax.ShapeDtypeStruct((bsz, seq, ch), BF16),
        scratch_shapes=[pltpu.VMEM((CONV_HALO + tc, ch), F32)],
        compiler_params=_cparams("parallel", "arbitrary"),
        name="conv_prompt",
    )(u, u, conv_w, conv_b, ln_g, ln_b)


def _conv_sample_body(buf_ref, w_ref, cb_ref, g_ref, b_ref, o_ref, *, t_new):
    for t in range(t_new):
        acc = jnp.zeros(buf_ref.shape[1:], F32) + cb_ref[...]
        for j in range(CONV_WIDTH):
            acc = acc + w_ref[j:j + 1, :] * buf_ref[t + j]
        y = _layer_norm(acc, g_ref[...], b_ref[...])
        o_ref[t] = (y * jax.nn.sigmoid(y)).astype(BF16)


def _conv_sample(buf_t, conv_w, conv_b, ln_g, ln_b):
    rows, bs, ch = buf_t.shape
    t_new = rows - (CONV_WIDTH - 1)
    gb = min(32, bs)
    assert bs % gb == 0
    vec = pl.BlockSpec((1, ch), lambda i: (0, 0))
    return pl.pallas_call(
        functools.partial(_conv_sample_body, t_new=t_new),
        grid=(bs // gb,),
        in_specs=[pl.BlockSpec((rows, gb, ch), lambda i: (0, i, 0)),
                  pl.BlockSpec((CONV_WIDTH, ch), lambda i: (0, 0)), vec, vec, vec],
        out_specs=pl.BlockSpec((t_new, gb, ch), lambda i: (0, i, 0)),
        out_shape=jax.ShapeDtypeStruct((t_new, bs, ch), BF16),
        compiler_params=_cparams("parallel"),
        name="conv_sample",
    )(buf_t, conv_w, conv_b, ln_g, ln_b)


N_MERGE_OUT = 5


def _merge_body(*refs, alpha, n_alias):
    (o_ref, y_ref, sg_ref, x_ref, wa_ref, wc_ref, bc_ref, wo_ref, g1_ref, b1_ref, rw_ref, rb_ref) = refs[:12]
    h1_ref, ti_ref, rk_ref, wcol_ref, cnt_ref = refs[12 + n_alias:]
    tm, d = x_ref.shape
    a = jnp.dot(o_ref[...], wa_ref[...], preferred_element_type=F32)
    b = jnp.dot(y_ref[...], wc_ref[...], preferred_element_type=F32) + bc_ref[...]
    mix_in = sg_ref[:, :d].astype(F32) * a + sg_ref[:, d:].astype(F32) * b
    mix = jnp.dot(mix_in.astype(BF16), wo_ref[...], preferred_element_type=F32)
    h1 = _layer_norm(alpha * x_ref[...] + mix, g1_ref[...], b1_ref[...])
    h1_ref[...] = h1
    logits = lax.dot_general(rw_ref[...], h1, (((1,), (1,)), ((), ())), preferred_element_type=F32,
                             precision=lax.Precision.HIGHEST) + rb_ref[...]
    n_e = logits.shape[0]
    eid = lax.broadcasted_iota(I32, logits.shape, 0)
    vals, idxs, hots = [], [], []
    for _ in range(TOP_K):
        m = jnp.max(logits, 0, keepdims=True)
        idx = jnp.min(jnp.where(logits == m, eid, n_e), 0, keepdims=True)
        hot = eid == idx
        vals.append(m)
        idxs.append(idx)
        hots.append(hot)
        logits = jnp.where(hot, -jnp.inf, logits)
    ex = [jnp.exp(v - vals[0]) for v in vals]
    den = ex[0]
    for e in ex[1:]:
        den = den + e
    ti_ref[...] = jnp.concatenate(idxs, 0)
    wrows = jnp.concatenate([e / den for e in ex] + [jnp.zeros((LANES - TOP_K, tm), F32)], 0)
    wcol_ref[...] = wrows.T
    sel = hots[0]
    for hot in hots[1:]:
        sel = sel | hot
    sel_f = jnp.where(sel, 1.0, 0.0)
    before = lax.broadcasted_iota(I32, (tm, tm), 0) < lax.broadcasted_iota(I32, (tm, tm), 1)
    upper = jnp.where(before, 1.0, 0.0).astype(BF16)
    ahead = jnp.dot(sel_f.astype(BF16), upper, preferred_element_type=F32)
    rk_ref[...] = jnp.concatenate(
        [jnp.sum(jnp.where(hot, ahead, 0.0), 0, keepdims=True) for hot in hots], 0).astype(I32)
    cnt = jnp.sum(sel_f, 1, keepdims=True).astype(I32)
    cnt_ref[0] = jnp.broadcast_to(cnt, (n_e, LANES))


def _merge(o, yact, sg, x, wts, alpha, n_total, row_off, tm, prev):
    n, d = x.shape
    wa, wc, bc, wo, g1, b1, rwt, rb = wts
    n_e = rwt.shape[0]
    off = row_off // tm
    last = n // tm - 1
    steps = n // tm if prev is not None else n_total // tm
    row = lambda width: pl.BlockSpec((tm, width), lambda i: (jnp.minimum(i, last), 0))
    full = lambda a: pl.BlockSpec(a.shape, lambda i: (0, 0))
    in_specs = [row(o.shape[1]), row(yact.shape[1]), row(2 * d), row(d),
                full(wa), full(wc), full(bc), full(wo), full(g1), full(b1), full(rwt), full(rb)]
    args = [o, yact, sg, x, wa, wc, bc, wo, g1, b1, rwt, rb]
    aliases = {}
    if prev is not None:
        in_specs += [pl.BlockSpec(memory_space=pl.ANY)] * N_MERGE_OUT
        aliases = {len(args) + j: j for j in range(N_MERGE_OUT)}
        args += list(prev)
    slot_major = pl.BlockSpec((TOP_K, tm), lambda i: (0, off + i))
    return pl.pallas_call(
        functools.partial(_merge_body, alpha=alpha, n_alias=0 if prev is None else N_MERGE_OUT),
        grid=(steps,),
        in_specs=in_specs,
        out_specs=(pl.BlockSpec((tm, d), lambda i: (off + i, 0)), slot_major, slot_major,
                   pl.BlockSpec((tm, LANES), lambda i: (off + i, 0)),
                   pl.BlockSpec((1, n_e, LANES), lambda i: (off + i, 0, 0))),
        out_shape=(jax.ShapeDtypeStruct((n_total, d), F32),
                   jax.ShapeDtypeStruct((TOP_K, n_total), I32),
                   jax.ShapeDtypeStruct((TOP_K, n_total), I32),
                   jax.ShapeDtypeStruct((n_total, LANES), F32),
                   jax.ShapeDtypeStruct((n_total // tm, n_e, LANES), I32)),
        input_output_aliases=aliases,
        compiler_params=_cparams("parallel"),
        name="merge_ln1_router",
    )(*args)


def _plan(topi, rank, cnt, tm_tok, tm_moe):
    k, n = topi.shape
    n_e = cnt.shape[1]
    before = jnp.cumsum(cnt, 0) - cnt
    total = jnp.sum(cnt, 0)
    ntile_e = (total + tm_moe - 1) // tm_moe
    tend = jnp.cumsum(ntile_e)
    tstart = tend - ntile_e
    base = (tstart * tm_moe)[None, :] + before
    base_tok = jnp.repeat(base, tm_tok, axis=0)
    hot = topi[:, :, None] == jnp.arange(n_e, dtype=I32)[None, None, :]
    pos = jnp.sum(jnp.where(hot, base_tok[None], 0), -1).astype(I32) + rank
    n_tiles = (k * n + tm_moe - 1) // tm_moe + n_e
    n_used = tend[-1]
    tid = jnp.arange(n_tiles, dtype=I32)
    src = jnp.minimum(tid, n_used - 1)
    te = jnp.minimum(jnp.sum((src[:, None] >= tend[None, :]).astype(I32), 1), n_e - 1)
    first = (tid == tstart[te]).astype(I32)
    last_tile = jnp.where(ntile_e > 0, tend - 1, -1).astype(I32)
    return pos, te.astype(I32), first, n_used.reshape(1).astype(I32), last_tile, n_tiles


def _dispatch_body(lt_ref, nu_ref, pos_ref, x_ref, xs_hbm, zbuf, zsem, sem, *, tm, tm_moe, n_e, n_tiles):
    i = pl.program_id(0)

    def zero_copy(tile):
        start = pl.multiple_of(tile * tm_moe, tm_moe)
        return pltpu.make_async_copy(zbuf, xs_hbm.at[pl.ds(start, tm_moe), :], zsem)

    @pl.when(i == 0)
    def _():
        zbuf[...] = jnp.zeros(zbuf.shape, F32)

        def z_start(e, c):
            @pl.when(lt_ref[e] >= 0)
            def _():
                zero_copy(lt_ref[e]).start()
            return c

        def z_wait(e, c):
            @pl.when(lt_ref[e] >= 0)
            def _():
                zero_copy(lt_ref[e]).wait()
            return c

        def t_start(t, c):
            zero_copy(t).start()
            return c

        def t_wait(t, c):
            zero_copy(t).wait()
            return c

        lax.fori_loop(0, n_e, z_start, 0)
        lax.fori_loop(nu_ref[0], n_tiles, t_start, 0)
        lax.fori_loop(0, n_e, z_wait, 0)
        lax.fori_loop(nu_ref[0], n_tiles, t_wait, 0)

    def issue(r, c):
        src = x_ref.at[pl.ds(r, 1), :]
        for k in range(TOP_K):
            pltpu.make_async_copy(src, xs_hbm.at[pl.ds(pos_ref[k, r], 1), :], sem).start()
        return c

    lax.fori_loop(0, tm, issue, 0)
    for k in range(TOP_K):
        pltpu.make_async_copy(x_ref, xs_hbm.at[pl.ds(0, tm), :], sem).wait()


def _dispatch(h1, pos, last_tile, n_used, n_tiles, tm, tm_moe):
    n, d = h1.shape
    n_e = last_tile.shape[0]
    grid_spec = pltpu.PrefetchScalarGridSpec(
        num_scalar_prefetch=2,
        grid=(n // tm,),
        in_specs=[pl.BlockSpec((TOP_K, tm), lambda i, lt, nu: (0, i), memory_space=pltpu.SMEM),
                  pl.BlockSpec((tm, d), lambda i, lt, nu: (i, 0))],
        out_specs=pl.BlockSpec(memory_space=pl.ANY),
        scratch_shapes=[pltpu.VMEM((tm_moe, d), F32), pltpu.SemaphoreType.DMA, pltpu.SemaphoreType.DMA],
    )
    return pl.pallas_call(
        functools.partial(_dispatch_body, tm=tm, tm_moe=tm_moe, n_e=n_e, n_tiles=n_tiles),
        grid_spec=grid_spec,
        out_shape=jax.ShapeDtypeStruct((n_tiles * tm_moe, d), F32),
        compiler_params=_cparams("arbitrary"),
        name="moe_dispatch",
    )(last_tile, n_used, pos, h1)


def _experts_body(te_ref, first_ref, nu_ref, x_ref, w1_ref, b1_ref, w2_ref, b2_ref, y_ref, w1b, w2b):
    i = pl.program_id(0)
    d_ff = w2_ref.shape[1]

    @pl.when(i >= nu_ref[0])
    def _():
        y_ref[...] = jnp.zeros(y_ref.shape, F32)

    @pl.when(i < nu_ref[0])
    def _():
        @pl.when(first_ref[i] == 1)
        def _():
            w1b[...] = w1_ref[0].astype(BF16)
            w2b[...] = w2_ref[0].astype(BF16)

        x = x_ref[...].astype(BF16)
        hh = jnp.dot(x, w1b[...], preferred_element_type=F32) + b1_ref[0]
        g = jnp.minimum(hh[:, :d_ff], SWIGLU_LIMIT)
        u = jnp.clip(hh[:, d_ff:], -SWIGLU_LIMIT, SWIGLU_LIMIT)
        act = (u + 1.0) * g * jax.nn.sigmoid(SWIGLU_ALPHA * g)
        y_ref[...] = jnp.dot(act.astype(BF16), w2b[...], preferred_element_type=F32) + b2_ref[0]


def _experts(xs, te, first, n_used, w1, b1, w2, b2, tm):
    n_rows, d = xs.shape
    n_e, _, f2 = w1.shape
    by_tile = lambda i, te, first, nu: (i, 0)
    by_expert = lambda i, te, first, nu: (te[i], 0, 0)
    grid_spec = pltpu.PrefetchScalarGridSpec(
        num_scalar_prefetch=3,
        grid=(n_rows // tm,),
        in_specs=[
            pl.BlockSpec((tm, d), by_tile),
            pl.BlockSpec((1, d, f2), by_expert),
            pl.BlockSpec((1, 1, f2), by_expert),
            pl.BlockSpec((1, f2 // 2, d), by_expert),
            pl.BlockSpec((1, 1, d), by_expert),
        ],
        out_specs=pl.BlockSpec((tm, d), by_tile),
        scratch_shapes=[pltpu.VMEM((d, f2), BF16), pltpu.VMEM((f2 // 2, d), BF16)],
    )
    return pl.pallas_call(
        _experts_body,
        grid_spec=grid_spec,
        out_shape=jax.ShapeDtypeStruct((n_rows, d), F32),
        compiler_params=_cparams("arbitrary"),
        name="moe_experts",
    )(te, first, n_used, xs, w1, b1.reshape(n_e, 1, f2), w2, b2.reshape(n_e, 1, d))


def _combine_body(pos_ref, nxt_ref, h1_ref, wcol_ref, ys_hbm, g_ref, b_ref, o_ref, ybuf, sems, *, alpha, tm):
    i = pl.program_id(0)
    n = pl.num_programs(0)

    def fetch(p_ref, slot):
        def issue(r, c):
            for k in range(TOP_K):
                pltpu.make_async_copy(ys_hbm.at[pl.ds(p_ref[k, r], 1), :], ybuf.at[slot, k, pl.ds(r, 1), :],
                                      sems.at[slot]).start()
            return c
        lax.fori_loop(0, tm, issue, 0)

    @pl.when(i == 0)
    def _():
        fetch(pos_ref, 0)

    @pl.when(i + 1 < n)
    def _():
        fetch(nxt_ref, (i + 1) % 2)

    slot = i % 2
    for k in range(TOP_K):
        pltpu.make_async_copy(ys_hbm.at[pl.ds(0, tm), :], ybuf.at[slot, k], sems.at[slot]).wait()
    ff = wcol_ref[:, 0:1] * ybuf[slot, 0]
    for k in range(1, TOP_K):
        ff = ff + wcol_ref[:, k:k + 1] * ybuf[slot, k]
    o_ref[...] = _layer_norm(alpha * h1_ref[...] + ff, g_ref[...], b_ref[...])


def _combine(h1, wcol, pos, ys, g, b, alpha, row_off, n):
    d = h1.shape[1]
    tm = min(COMBINE_TILE, n)
    off = row_off // tm
    last = n // tm - 1
    vec = pl.BlockSpec((1, d), lambda i: (0, 0))
    return pl.pallas_call(
        functools.partial(_combine_body, alpha=alpha, tm=tm),
        grid=(n // tm,),
        in_specs=[pl.BlockSpec((TOP_K, tm), lambda i: (0, off + i), memory_space=pltpu.SMEM),
                  pl.BlockSpec((TOP_K, tm), lambda i: (0, off + jnp.minimum(i + 1, last)), memory_space=pltpu.SMEM),
                  pl.BlockSpec((tm, d), lambda i: (off + i, 0)),
                  pl.BlockSpec((tm, LANES), lambda i: (off + i, 0)),
                  pl.BlockSpec(memory_space=pl.ANY), vec, vec],
        out_specs=pl.BlockSpec((tm, d), lambda i: (i, 0)),
        out_shape=jax.ShapeDtypeStruct((n, d), F32),
        scratch_shapes=[pltpu.VMEM((2, TOP_K, tm, d), F32), pltpu.SemaphoreType.DMA((2,))],
        compiler_params=_cparams("arbitrary"),
        name="combine_ln2",
    )(pos, pos, h1, wcol, ys, g, b)


def _row2(v):
    return v.reshape(1, -1).astype(F32)


def kernel(x_prompt, x_sample, cache_k, cache_v, page_table, state_conv, w_in, b_in, lambda_q1, lambda_k1,
           lambda_q2, lambda_k2, subln_g, rel_bias, w_attn_proj, conv_w, conv_b, conv_ln_g, conv_ln_b,
           w_conv_proj, b_conv_proj, w_out, ln1_g, ln1_b, router_w, router_b, expert_w1, expert_b1,
           expert_w2, expert_b2, ln2_g, ln2_b):
    depth = w_in.shape[0]
    bp, seq, d = x_prompt.shape
    bs, t_new, _ = x_sample.shape
    d_att = N_HEADS * QK_DIM
    d_conv = conv_w.shape[2]
    n_p, n_s = bp * seq, bs * t_new
    n_tot = n_p + n_s
    tm = min(TOKEN_TILE, math.gcd(n_p, n_s))
    assert tm % LANES == 0
    tm_moe = min(MOE_TILE, n_tot)
    alpha = (2 * depth) ** 0.25

    hp = x_prompt.reshape(n_p, d)
    hs = x_sample.reshape(n_s, d)
    outs = [[] for _ in range(6)]
    for l in range(depth):
        lam_init = 0.8 - 0.6 * math.exp(-0.3 * l)
        lam = (jnp.exp(jnp.sum(lambda_q1[l].astype(F32) * lambda_k1[l].astype(F32)))
               - jnp.exp(jnp.sum(lambda_q2[l].astype(F32) * lambda_k2[l].astype(F32))) + lam_init).reshape(1)
        w_in_bf = w_in[l].astype(BF16)
        b_in_l = _row2(b_in[l])
        g_sub = _row2(subln_g[l])
        conv_args = (conv_w[l].astype(F32), _row2(conv_b[l]), _row2(conv_ln_g[l]), _row2(conv_ln_b[l]))
        merge_w = (w_attn_proj[l].astype(BF16), w_conv_proj[l].astype(BF16), _row2(b_conv_proj[l]),
                   w_out[l].astype(BF16), _row2(ln1_g[l]), _row2(ln1_b[l]),
                   router_w[l].astype(F32).T, router_b[l].astype(F32).reshape(-1, 1))

        qp, kp, vp, kbp, vbp, up, sgp = _in_proj(hp, w_in_bf, b_in_l, d_att, d_conv)
        op = _prompt_attention(qp.reshape(bp, seq, d_att), kbp.reshape(bp, seq, d_att), vbp.reshape(bp, seq, d_att),
                               rel_bias, lam, g_sub, lam_init)
        up3 = up.reshape(bp, seq, d_conv)
        yp = _conv_prompt(up3, *conv_args)
        merged = _merge(op.reshape(n_p, d_att), yp.reshape(n_p, d_conv), sgp, hp, merge_w, alpha, n_tot, 0, tm, None)

        qs, ks, vs, kbs, vbs, us, sgs = _in_proj(hs, w_in_bf, b_in_l, d_att, d_conv)
        osr = _sample_attention(qs.reshape(bs, t_new, d_att), kbs.reshape(bs, t_new, d_att),
                                vbs.reshape(bs, t_new, d_att), cache_k[l], cache_v[l], page_table,
                                rel_bias, lam, g_sub, lam_init)
        os_ = osr[:, :t_new].reshape(n_s, d_att)
        buf_s = jnp.concatenate([state_conv[l].astype(F32), us.reshape(bs, t_new, d_conv)], axis=1)
        ys = _conv_sample(buf_s.transpose(1, 0, 2), *conv_args).transpose(1, 0, 2).reshape(n_s, d_conv)
        h1, topi, rank, wcol, cnt = _merge(os_, ys, sgs, hs, merge_w, alpha, n_tot, n_p, tm, merged)

        pos, te, first, n_used, last_tile, n_tiles = _plan(topi, rank, cnt[:, :, 0], tm, tm_moe)
        xs = _dispatch(h1, pos, last_tile, n_used, n_tiles, tm, tm_moe)
        ysort = _experts(xs, te, first, n_used, expert_w1[l], expert_b1[l].astype(F32),
                         expert_w2[l], expert_b2[l].astype(F32), tm_moe)
        g2, b2 = _row2(ln2_g[l]), _row2(ln2_b[l])
        hp = _combine(h1, wcol, pos, ysort, g2, b2, alpha, 0, n_p)
        hs = _combine(h1, wcol, pos, ysort, g2, b2, alpha, n_p, n_s)

        w1 = CONV_WIDTH - 1
        cp = up3[:, seq - w1:] if seq >= w1 else jnp.concatenate(
            [jnp.zeros((bp, w1 - seq, d_conv), F32), up3], axis=1)
        for lst, val in zip(outs, (kp.reshape(bp, seq, N_HEADS, QK_DIM), vp.reshape(bp, seq, N_HEADS, V_DIM), cp,
                                   ks.reshape(bs, t_new, N_HEADS, QK_DIM), vs.reshape(bs, t_new, N_HEADS, V_DIM),
                                   buf_s[:, t_new:])):
            lst.append(val)
    return (hp.reshape(bp, seq, d), hs.reshape(bs, t_new, d)) + tuple(jnp.stack(o) for o in outs)
```

```python
import functools
import math

import jax
import jax.numpy as jnp
from jax import lax
from jax.experimental import pallas as pl
from jax.experimental.pallas import tpu as pltpu

F32 = jnp.float32
BF16 = jnp.bfloat16
I32 = jnp.int32

N_HEADS = 4
HEAD_DIM = 64
QK_DIM = 2 * HEAD_DIM
V_DIM = 2 * HEAD_DIM
ATTN_SCALE = HEAD_DIM ** -0.5
LOG2E = 1.4426950408889634
NEG_INF = -1e30
NUM_BUCKETS = 32
MAX_EXACT = NUM_BUCKETS // 2
MAX_DISTANCE = 128
CONV_WIDTH = 31
TOP_K = 4
SWIGLU_LIMIT = 7.0
SWIGLU_ALPHA = 1.702
LN_EPS = 1e-5
PAGE_SIZE = 128

LANES = 128
SUBLANES = 8
VMEM_LIMIT = 56 * 1024 * 1024

TOKEN_TILE = 512
ATTN_TQ = 1024
ATTN_TK = 512
MOE_TILE = 512
COMBINE_TILE = 256
CONV_HALO = 32
CONV_CHUNK = 64


def _cparams(*sem):
    return pltpu.CompilerParams(dimension_semantics=sem, vmem_limit_bytes=VMEM_LIMIT)


def _layer_norm(x, g, b):
    mu = jnp.mean(x, -1, keepdims=True)
    xc = x - mu
    var = jnp.mean(xc * xc, -1, keepdims=True)
    return xc * lax.rsqrt(var + LN_EPS) * g + b


def _in_proj_body(x_ref, w_ref, b_ref, q_ref, k_ref, v_ref, kb_ref, vb_ref, u_ref, sg_ref, *, d_att, d_conv, by_head):
    x = x_ref[...].astype(BF16)

    def seg(lo, hi):
        return jnp.dot(x, w_ref[:, lo:hi], preferred_element_type=F32) + b_ref[:, lo:hi]

    def head(a, h):
        return a[:, h * QK_DIM:(h + 1) * QK_DIM]

    o = 0
    q = seg(o, o + d_att) * (ATTN_SCALE * LOG2E)
    o += d_att
    k = seg(o, o + d_att)
    k_ref[...] = k
    o += d_att
    v = seg(o, o + d_att)
    v_ref[...] = v
    o += d_att
    if by_head:
        for h in range(N_HEADS):
            q_ref[h] = head(q, h).T.astype(BF16)
            kb_ref[h] = head(k, h).astype(BF16)
            vb_ref[h] = head(v, h).T.astype(BF16)
    else:
        q_ref[...] = q.astype(BF16)
        kb_ref[...] = k.astype(BF16)
        vb_ref[...] = v.astype(BF16)
    c = seg(o, o + 2 * d_conv)
    u_ref[...] = c[:, :d_conv] * jax.nn.sigmoid(c[:, d_conv:])
    o += 2 * d_conv
    sg_ref[...] = jax.nn.sigmoid(seg(o, w_ref.shape[1])).astype(BF16)


def _in_proj(x, w_bf, b, d_att, d_conv, attn_tiles=None):
    n, d = x.shape
    tm = min(TOKEN_TILE, n)
    row = lambda width: pl.BlockSpec((tm, width), lambda i: (i, 0))
    full = lambda a: pl.BlockSpec(a.shape, lambda i: (0, 0))
    f32_out = lambda width: jax.ShapeDtypeStruct((n, width), F32)
    if attn_tiles is None:
        qkv_shapes = [jax.ShapeDtypeStruct((n, d_att), BF16)] * 3
        qkv_specs = [row(d_att)] * 3
    else:
        bsz, seq, tq, tk = attn_tiles
        assert seq % tm == 0 and tq % tm == 0 and tk % tm == 0
        per_seq = seq // tm

        def spec(t, transposed):
            parts = t // tm
            blk = (None, N_HEADS, None, QK_DIM, tm) if transposed else (None, N_HEADS, None, tm, QK_DIM)

            def index(i):
                ti = i % per_seq
                tile, part = ti // parts, ti % parts
                return (i // per_seq, 0, tile, 0, part) if transposed else (i // per_seq, 0, tile, part, 0)
            return pl.BlockSpec(blk, index)

        qkv_shapes = [jax.ShapeDtypeStruct((bsz, N_HEADS, seq // tq, QK_DIM, tq), BF16),
                      jax.ShapeDtypeStruct((bsz, N_HEADS, seq // tk, tk, QK_DIM), BF16),
                      jax.ShapeDtypeStruct((bsz, N_HEADS, seq // tk, V_DIM, tk), BF16)]
        qkv_specs = [spec(tq, True), spec(tk, False), spec(tk, True)]
    out_shape = (qkv_shapes[0], f32_out(d_att), f32_out(d_att), qkv_shapes[1], qkv_shapes[2],
                 f32_out(d_conv), jax.ShapeDtypeStruct((n, 2 * d), BF16))
    return pl.pallas_call(
        functools.partial(_in_proj_body, d_att=d_att, d_conv=d_conv, by_head=attn_tiles is not None),
        grid=(n // tm,),
        in_specs=[row(d), full(w_bf), full(b)],
        out_specs=(qkv_specs[0], row(d_att), row(d_att), qkv_specs[1], qkv_specs[2], row(d_conv), row(2 * d)),
        out_shape=out_shape,
        compiler_params=_cparams("parallel"),
        name="in_proj",
    )(x, w_bf, b)


def _t5_bucket(dist):
    n = jnp.maximum(dist, 0)
    nf = jnp.maximum(n, 1).astype(F32)
    large = MAX_EXACT + (jnp.log(nf / MAX_EXACT) / math.log(MAX_DISTANCE / MAX_EXACT)
                         * (NUM_BUCKETS - MAX_EXACT)).astype(I32)
    large = jnp.minimum(large, NUM_BUCKETS - 1)
    return jnp.where(n < MAX_EXACT, n, large)


def _bias_of(dist, rel_bias, shift=None):
    b = rel_bias[_t5_bucket(dist)].astype(F32)
    if shift is not None:
        b = b - shift
    b = jnp.where((dist >= 0)[..., None], b * LOG2E, NEG_INF)
    return jnp.moveaxis(b, -1, 0)


def _flash_body(lam_ref, qt_ref, k_ref, vt_ref, bias_ref, g_ref, o_ref, m_s, l_s, acc_s, *, ratio, lam_init):
    i = pl.program_id(2)
    qt = qt_ref[...]
    sub = lax.broadcasted_iota(I32, (QK_DIM, 1), 0)
    zero = jnp.zeros_like(qt)
    qs = (jnp.where(sub < HEAD_DIM, qt, zero), jnp.where(sub >= HEAD_DIM, qt, zero))
    m_s[...] = jnp.full(m_s.shape, NEG_INF, F32)
    l_s[...] = jnp.zeros(l_s.shape, F32)
    acc_s[...] = jnp.zeros(acc_s.shape, F32)

    def step(j, bias):
        k = k_ref[j]
        vt = vt_ref[j]
        for c in range(2):
            s = jnp.dot(k, qs[c], preferred_element_type=F32)
            if bias is not None:
                s = s + bias
            m_old = m_s[c]
            m_new = jnp.maximum(m_old, jnp.max(s, 0, keepdims=True))
            alpha = jnp.exp2(m_old - m_new)
            p = jnp.exp2(s - m_new)
            l_s[c] = alpha * l_s[c] + jnp.sum(p, 0, keepdims=True)
            acc_s[c] = alpha * acc_s[c] + jnp.dot(vt, p.astype(BF16), preferred_element_type=F32)
            m_s[c] = m_new

    def far_step(j, carry):
        step(j, None)
        return carry

    lax.fori_loop(0, jnp.maximum(i * ratio - 1, 0), far_step, 0)

    @pl.when(i >= 1)
    def _():
        step(i * ratio - 1, bias_ref[0])

    for o in range(ratio):
        step(i * ratio + o, bias_ref[o + 1])

    o = acc_s[0] / l_s[0] - lam_ref[0] * (acc_s[1] / l_s[1])
    o = o * lax.rsqrt(jnp.mean(o * o, 0, keepdims=True) + LN_EPS) * g_ref[...]
    o_ref[0] = (o * (1.0 - lam_init)).T.astype(BF16)


def _toeplitz(v, rows, cols):
    period = v.shape[-1]
    reps = -(-(rows * (period - 1)) // period)
    flat = jnp.tile(v, reps)[..., :rows * (period - 1)]
    return flat.reshape(v.shape[:-1] + (rows, period - 1))[..., :cols]


def _attn_tiles(seq):
    tq = min(ATTN_TQ, seq)
    tk = min(ATTN_TK, tq)
    assert tk >= MAX_DISTANCE and seq % tq == 0 and tq % tk == 0
    return tq, tk


def _prompt_attention(qt, k4, vt, rel_bias, lam, subln_g, lam_init):
    bsz, _, nq, _, tq = qt.shape
    nk, tk = k4.shape[2], k4.shape[3]
    seq, ratio, width = nq * tq, tq // tk, N_HEADS * V_DIM
    far = rel_bias[NUM_BUCKETS - 1].astype(F32)
    period = tq + tk
    y = jnp.arange(period, dtype=I32)
    r_minus_c = jnp.where(y < tq, y, y - period)
    dist = r_minus_c[None, :] - (jnp.arange(ratio + 1, dtype=I32)[:, None] - 1) * tk
    bias = _toeplitz(_bias_of(dist, rel_bias, far), tk, tq)
    return pl.pallas_call(
        functools.partial(_flash_body, ratio=ratio, lam_init=lam_init),
        grid=(bsz, N_HEADS, nq),
        in_specs=[
            pl.BlockSpec(memory_space=pltpu.SMEM),
            pl.BlockSpec((None, None, None, QK_DIM, tq), lambda b, h, i: (b, h, i, 0, 0)),
            pl.BlockSpec((None, None, nk, tk, QK_DIM), lambda b, h, i: (b, h, 0, 0, 0)),
            pl.BlockSpec((None, None, nk, V_DIM, tk), lambda b, h, i: (b, h, 0, 0, 0)),
            pl.BlockSpec((None, ratio + 1, tk, tq), lambda b, h, i: (h, 0, 0, 0)),
            pl.BlockSpec((V_DIM, 1), lambda b, h, i: (0, 0)),
        ],
        out_specs=pl.BlockSpec((1, tq, V_DIM), lambda b, h, i: (b, i, h)),
        out_shape=jax.ShapeDtypeStruct((bsz, seq, width), BF16),
        scratch_shapes=[pltpu.VMEM((2, 1, tq), F32), pltpu.VMEM((2, 1, tq), F32), pltpu.VMEM((2, V_DIM, tq), F32)],
        compiler_params=_cparams("parallel", "parallel", "arbitrary"),
        name="prompt_attention",
    )(lam, qt, k4, vt, bias, subln_g.reshape(V_DIM, 1))


def _paged_body(pt_ref, lam_ref, q_ref, kn_ref, vn_ref, bias_ref, g_ref, *refs, n_pages, t_new, lam_init):
    k_refs = refs[:n_pages]
    v_refs = refs[n_pages:2 * n_pages]
    o_ref = refs[2 * n_pages]
    lam = lam_ref[0]
    rows = 2 * t_new
    cols = PAGE_SIZE * N_HEADS
    nt = (((1,), (1,)), ((), ()))
    q = q_ref[0]
    parts = [lax.dot_general(q, k_refs[p][...].astype(BF16), nt, preferred_element_type=F32)
             for p in range(n_pages)]
    parts.append(lax.dot_general(q, kn_ref[0], nt, preferred_element_type=F32))
    s = jnp.concatenate(parts, axis=1) + bias_ref[...]
    m = jnp.max(s, -1, keepdims=True)
    p_un = jnp.exp2(s - m)
    pn = p_un / jnp.sum(p_un, -1, keepdims=True)
    w = (pn - lam * pltpu.roll(pn, N_HEADS * rows - t_new, axis=0)).astype(BF16)
    o = jnp.dot(w[:, n_pages * cols:], vn_ref[0], preferred_element_type=F32)
    for p in range(n_pages):
        o = o + jnp.dot(w[:, p * cols:(p + 1) * cols], v_refs[p][...].astype(BF16), preferred_element_type=F32)
    o = o * lax.rsqrt(jnp.mean(o * o, -1, keepdims=True) + LN_EPS) * g_ref[...]
    o = (o * (1.0 - lam_init)).astype(BF16)
    o_ref[0] = jnp.concatenate([o[h * rows:(h + 1) * rows] for h in range(N_HEADS)], axis=1)


def _sample_attention(q, kb, vb, cache_k, cache_v, page_table, rel_bias, lam, subln_g, lam_init):
    bs, t_new, width = q.shape
    n_pool = cache_k.shape[0]
    n_pages = page_table.shape[1]
    past = n_pages * PAGE_SIZE
    rows = 2 * t_new
    cols = PAGE_SIZE * N_HEADS
    new_pos = LANES // N_HEADS
    assert rows % SUBLANES == 0 and t_new <= new_pos
    qh = q.reshape(bs, t_new, N_HEADS, QK_DIM).transpose(0, 2, 1, 3)
    lane_map = (jnp.arange(QK_DIM) // HEAD_DIM)[None, :] == jnp.arange(2)[:, None]
    q_rows = (qh[:, :, None] * lane_map[None, None, :, None, :].astype(BF16)).reshape(bs, N_HEADS * rows, QK_DIM)
    pad = ((0, 0), (0, LANES - t_new * N_HEADS), (0, 0))
    kn = jnp.pad(kb.reshape(bs, t_new * N_HEADS, QK_DIM), pad)
    vn = jnp.pad(vb.reshape(bs, t_new * N_HEADS, V_DIM), pad)
    qpos = past + jnp.arange(t_new, dtype=I32)
    kpos = jnp.concatenate([jnp.arange(past + t_new, dtype=I32),
                            jnp.full((new_pos - t_new,), past + t_new + new_pos, I32)])
    b = _bias_of(qpos[:, None] - kpos[None, :], rel_bias)
    same = jnp.arange(N_HEADS)[:, None] == jnp.arange(N_HEADS)[None, :]
    b = jnp.where(same[:, None, None, :], b[..., None], NEG_INF)
    b = b.reshape(N_HEADS, 1, t_new, -1)
    bias = jnp.concatenate([b, b], axis=1).reshape(N_HEADS * rows, -1)

    page_spec = lambda j: pl.BlockSpec((None, cols, QK_DIM), lambda b, pt, j=j: (pt[b * n_pages + j], 0, 0))
    per_seq = lambda r, c: pl.BlockSpec((1, r, c), lambda b, pt: (b, 0, 0))
    grid_spec = pltpu.PrefetchScalarGridSpec(
        num_scalar_prefetch=1,
        grid=(bs,),
        in_specs=[
            pl.BlockSpec(memory_space=pltpu.SMEM),
            per_seq(N_HEADS * rows, QK_DIM),
            per_seq(LANES, QK_DIM),
            per_seq(LANES, V_DIM),
            pl.BlockSpec(bias.shape, lambda b, pt: (0, 0)),
            pl.BlockSpec((1, V_DIM), lambda b, pt: (0, 0)),
        ] + [page_spec(j) for j in range(n_pages)] * 2,
        out_specs=pl.BlockSpec((1, rows, width), lambda b, pt: (b, 0, 0)),
    )
    ck = cache_k.reshape(n_pool, cols, QK_DIM)
    cv = cache_v.reshape(n_pool, cols, V_DIM)
    return pl.pallas_call(
        functools.partial(_paged_body, n_pages=n_pages, t_new=t_new, lam_init=lam_init),
        grid_spec=grid_spec,
        out_shape=jax.ShapeDtypeStruct((bs, rows, width), BF16),
        compiler_params=_cparams("parallel"),
        name="sample_attention",
    )(page_table.reshape(-1), lam, q_rows, kn, vn, bias, subln_g, *([ck] * n_pages), *([cv] * n_pages))


def _conv_prompt_body(u_ref, halo_ref, w_ref, cb_ref, g_ref, b_ref, o_ref, buf, shifted, *, tc):
    i = pl.program_id(1)
    halo = halo_ref[0]
    buf[0:CONV_HALO, :] = jnp.where(i > 0, halo, jnp.zeros_like(halo))
    buf[CONV_HALO:, :] = u_ref[0]
    rows = CONV_HALO + tc - SUBLANES
    for s in range(1, SUBLANES):
        shifted[s - 1, 0:rows, :] = buf[s:s + rows, :]
    first = CONV_HALO - (CONV_WIDTH - 1)
    for r0 in range(0, tc, CONV_CHUNK):
        acc = jnp.zeros((CONV_CHUNK, u_ref.shape[2]), F32) + cb_ref[...]
        for j in range(CONV_WIDTH):
            s = (first + j) % SUBLANES
            a = first + j - s + r0
            win = buf[a:a + CONV_CHUNK, :] if s == 0 else shifted[s - 1, a:a + CONV_CHUNK, :]
            acc = acc + w_ref[j:j + 1, :] * win
        y = _layer_norm(acc, g_ref[...], b_ref[...])
        o_ref[0, r0:r0 + CONV_CHUNK, :] = (y * jax.nn.sigmoid(y)).astype(BF16)


def _conv_prompt(u, conv_w, conv_b, ln_g, ln_b):
    bsz, seq, ch = u.shape
    tc = min(TOKEN_TILE, seq)
    assert seq % tc == 0 and tc % CONV_CHUNK == 0 and tc % CONV_HALO == 0
    vec = pl.BlockSpec((1, ch), lambda b, i: (0, 0))
    return pl.pallas_call(
        functools.partial(_conv_prompt_body, tc=tc),
        grid=(bsz, seq // tc),
        in_specs=[
            pl.BlockSpec((1, tc, ch), lambda b, i: (b, i, 0)),
            pl.BlockSpec((1, CONV_HALO, ch), lambda b, i: (b, jnp.maximum(i * (tc // CONV_HALO) - 1, 0), 0)),
            pl.BlockSpec((CONV_WIDTH, ch), lambda b, i: (0, 0)),
            vec, vec, vec,
        ],
        out_specs=pl.BlockSpec((1, tc, ch), lambda b, i: (b, i, 0)),
        out_shape=jax.ShapeDtypeStruct((bsz, seq, ch), BF16),
        scratch_shapes=[pltpu.VMEM((CONV_HALO + tc, ch), F32), pltpu.VMEM((SUBLANES - 1, CONV_HALO + tc, ch), F32)],
        compiler_params=_cparams("parallel", "arbitrary"),
        name="conv_prompt",
    )(u, u, conv_w, conv_b, ln_g, ln_b)


def _conv_sample_body(buf_ref, w_ref, cb_ref, g_ref, b_ref, o_ref, *, t_new):
    for t in range(t_new):
        acc = jnp.zeros(buf_ref.shape[1:], F32) + cb_ref[...]
        for j in range(CONV_WIDTH):
            acc = acc + w_ref[j:j + 1, :] * buf_ref[t + j]
        y = _layer_norm(acc, g_ref[...], b_ref[...])
        o_ref[t] = (y * jax.nn.sigmoid(y)).astype(BF16)


def _conv_sample(buf_t, conv_w, conv_b, ln_g, ln_b):
    rows, bs, ch = buf_t.shape
    t_new = rows - (CONV_WIDTH - 1)
    gb = min(32, bs)
    assert bs % gb == 0
    vec = pl.BlockSpec((1, ch), lambda i: (0, 0))
    return pl.pallas_call(
        functools.partial(_conv_sample_body, t_new=t_new),
        grid=(bs // gb,),
        in_specs=[pl.BlockSpec((rows, gb, ch), lambda i: (0, i, 0)),
                  pl.BlockSpec((CONV_WIDTH, ch), lambda i: (0, 0)), vec, vec, vec],
        out_specs=pl.BlockSpec((t_new, gb, ch), lambda i: (0, i, 0)),
        out_shape=jax.ShapeDtypeStruct((t_new, bs, ch), BF16),
        compiler_params=_cparams("parallel"),
        name="conv_sample",
    )(buf_t, conv_w, conv_b, ln_g, ln_b)


N_MERGE_OUT = 5


def _merge_body(*refs, alpha, n_alias):
    (o_ref, y_ref, sg_ref, x_ref, wa_ref, wc_ref, bc_ref, wo_ref, g1_ref, b1_ref, rw_ref, rb_ref) = refs[:12]
    h1_ref, ti_ref, rk_ref, wcol_ref, cnt_ref = refs[12 + n_alias:]
    tm, d = x_ref.shape
    a = jnp.dot(o_ref[...], wa_ref[...], preferred_element_type=F32)
    b = jnp.dot(y_ref[...], wc_ref[...], preferred_element_type=F32) + bc_ref[...]
    mix_in = sg_ref[:, :d].astype(F32) * a + sg_ref[:, d:].astype(F32) * b
    mix = jnp.dot(mix_in.astype(BF16), wo_ref[...], preferred_element_type=F32)
    h1 = _layer_norm(alpha * x_ref[...] + mix, g1_ref[...], b1_ref[...])
    h1_ref[...] = h1
    logits = lax.dot_general(rw_ref[...], h1, (((1,), (1,)), ((), ())), preferred_element_type=F32,
                             precision=lax.Precision.HIGHEST) + rb_ref[...]
    n_e = logits.shape[0]
    eid = lax.broadcasted_iota(I32, logits.shape, 0)
    vals, idxs, hots = [], [], []
    for _ in range(TOP_K):
        m = jnp.max(logits, 0, keepdims=True)
        idx = jnp.min(jnp.where(logits == m, eid, n_e), 0, keepdims=True)
        hot = eid == idx
        vals.append(m)
        idxs.append(idx)
        hots.append(hot)
        logits = jnp.where(hot, -jnp.inf, logits)
    ex = [jnp.exp(v - vals[0]) for v in vals]
    den = ex[0]
    for e in ex[1:]:
        den = den + e
    ti_ref[...] = jnp.concatenate(idxs, 0)
    wrows = jnp.concatenate([e / den for e in ex] + [jnp.zeros((LANES - TOP_K, tm), F32)], 0)
    wcol_ref[...] = wrows.T
    sel = hots[0]
    for hot in hots[1:]:
        sel = sel | hot
    sel_f = jnp.where(sel, 1.0, 0.0)
    before = lax.broadcasted_iota(I32, (tm, tm), 0) < lax.broadcasted_iota(I32, (tm, tm), 1)
    upper = jnp.where(before, 1.0, 0.0).astype(BF16)
    ahead = jnp.dot(sel_f.astype(BF16), upper, preferred_element_type=F32)
    rk_ref[...] = jnp.concatenate(
        [jnp.sum(jnp.where(hot, ahead, 0.0), 0, keepdims=True) for hot in hots], 0).astype(I32)
    cnt = jnp.sum(sel_f, 1, keepdims=True).astype(I32)
    cnt_ref[0] = jnp.broadcast_to(cnt, (n_e, LANES))


def _merge(o, yact, sg, x, wts, alpha, n_total, row_off, tm, prev):
    n, d = x.shape
    wa, wc, bc, wo, g1, b1, rwt, rb = wts
    n_e = rwt.shape[0]
    off = row_off // tm
    last = n // tm - 1
    steps = n // tm if prev is not None else n_total // tm
    row = lambda width: pl.BlockSpec((tm, width), lambda i: (jnp.minimum(i, last), 0))
    full = lambda a: pl.BlockSpec(a.shape, lambda i: (0, 0))
    in_specs = [row(o.shape[1]), row(yact.shape[1]), row(2 * d), row(d),
                full(wa), full(wc), full(bc), full(wo), full(g1), full(b1), full(rwt), full(rb)]
    args = [o, yact, sg, x, wa, wc, bc, wo, g1, b1, rwt, rb]
    aliases = {}
    if prev is not None:
        in_specs += [pl.BlockSpec(memory_space=pl.ANY)] * N_MERGE_OUT
        aliases = {len(args) + j: j for j in range(N_MERGE_OUT)}
        args += list(prev)
    slot_major = pl.BlockSpec((TOP_K, tm), lambda i: (0, off + i))
    return pl.pallas_call(
        functools.partial(_merge_body, alpha=alpha, n_alias=0 if prev is None else N_MERGE_OUT),
        grid=(steps,),
        in_specs=in_specs,
        out_specs=(pl.BlockSpec((tm, d), lambda i: (off + i, 0)), slot_major, slot_major,
                   pl.BlockSpec((tm, LANES), lambda i: (off + i, 0)),
                   pl.BlockSpec((1, n_e, LANES), lambda i: (off + i, 0, 0))),
        out_shape=(jax.ShapeDtypeStruct((n_total, d), F32),
                   jax.ShapeDtypeStruct((TOP_K, n_total), I32),
                   jax.ShapeDtypeStruct((TOP_K, n_total), I32),
                   jax.ShapeDtypeStruct((n_total, LANES), F32),
                   jax.ShapeDtypeStruct((n_total // tm, n_e, LANES), I32)),
        input_output_aliases=aliases,
        compiler_params=_cparams("parallel"),
        name="merge_ln1_router",
    )(*args)


def _plan(topi, rank, cnt, tm_tok, tm_moe):
    k, n = topi.shape
    n_e = cnt.shape[1]
    before = jnp.cumsum(cnt, 0) - cnt
    total = jnp.sum(cnt, 0)
    ntile_e = (total + tm_moe - 1) // tm_moe
    tend = jnp.cumsum(ntile_e)
    tstart = tend - ntile_e
    base = (tstart * tm_moe)[None, :] + before
    base_tok = jnp.repeat(base, tm_tok, axis=0)
    hot = topi[:, :, None] == jnp.arange(n_e, dtype=I32)[None, None, :]
    pos = jnp.sum(jnp.where(hot, base_tok[None], 0), -1).astype(I32) + rank
    n_tiles = (k * n + tm_moe - 1) // tm_moe + n_e
    n_used = tend[-1]
    tid = jnp.arange(n_tiles, dtype=I32)
    src = jnp.minimum(tid, n_used - 1)
    te = jnp.minimum(jnp.sum((src[:, None] >= tend[None, :]).astype(I32), 1), n_e - 1)
    first = (tid == tstart[te]).astype(I32)
    last_tile = jnp.where(ntile_e > 0, tend - 1, -1).astype(I32)
    return pos, te.astype(I32), first, n_used.reshape(1).astype(I32), last_tile, n_tiles


def _dispatch_body(lt_ref, nu_ref, pos_ref, x_ref, xs_hbm, zbuf, zsem, sem, *, tm, tm_moe, n_e, n_tiles):
    i = pl.program_id(0)

    def zero_copy(tile):
        start = pl.multiple_of(tile * tm_moe, tm_moe)
        return pltpu.make_async_copy(zbuf, xs_hbm.at[pl.ds(start, tm_moe), :], zsem)

    @pl.when(i == 0)
    def _():
        zbuf[...] = jnp.zeros(zbuf.shape, F32)

        def z_start(e, c):
            @pl.when(lt_ref[e] >= 0)
            def _():
                zero_copy(lt_ref[e]).start()
            return c

        def z_wait(e, c):
            @pl.when(lt_ref[e] >= 0)
            def _():
                zero_copy(lt_ref[e]).wait()
            return c

        def t_start(t, c):
            zero_copy(t).start()
            return c

        def t_wait(t, c):
            zero_copy(t).wait()
            return c

        lax.fori_loop(0, n_e, z_start, 0)
        lax.fori_loop(nu_ref[0], n_tiles, t_start, 0)
        lax.fori_loop(0, n_e, z_wait, 0)
        lax.fori_loop(nu_ref[0], n_tiles, t_wait, 0)

    def issue(g, c):
        base = pl.multiple_of(g * SUBLANES, SUBLANES)
        for rr in range(SUBLANES):
            src = x_ref.at[pl.ds(base + rr, 1), :]
            for k in range(TOP_K):
                pltpu.make_async_copy(src, xs_hbm.at[pl.ds(pos_ref[k, base + rr], 1), :], sem).start()
        return c

    lax.fori_loop(0, tm // SUBLANES, issue, 0)
    for k in range(TOP_K):
        pltpu.make_async_copy(x_ref, xs_hbm.at[pl.ds(0, tm), :], sem).wait()


def _dispatch(h1, pos, last_tile, n_used, n_tiles, tm, tm_moe):
    n, d = h1.shape
    n_e = last_tile.shape[0]
    grid_spec = pltpu.PrefetchScalarGridSpec(
        num_scalar_prefetch=2,
        grid=(n // tm,),
        in_specs=[pl.BlockSpec((TOP_K, tm), lambda i, lt, nu: (0, i), memory_space=pltpu.SMEM),
                  pl.BlockSpec((tm, d), lambda i, lt, nu: (i, 0))],
        out_specs=pl.BlockSpec(memory_space=pl.ANY),
        scratch_shapes=[pltpu.VMEM((tm_moe, d), F32), pltpu.SemaphoreType.DMA, pltpu.SemaphoreType.DMA],
    )
    return pl.pallas_call(
        functools.partial(_dispatch_body, tm=tm, tm_moe=tm_moe, n_e=n_e, n_tiles=n_tiles),
        grid_spec=grid_spec,
        out_shape=jax.ShapeDtypeStruct((n_tiles * tm_moe, d), F32),
        compiler_params=_cparams("arbitrary"),
        name="moe_dispatch",
    )(last_tile, n_used, pos, h1)


def _experts_body(te_ref, first_ref, nu_ref, x_ref, w1_ref, b1_ref, w2_ref, b2_ref, y_ref, w1b, w2b):
    i = pl.program_id(0)
    d_ff = w2_ref.shape[1]

    @pl.when(i >= nu_ref[0])
    def _():
        y_ref[...] = jnp.zeros(y_ref.shape, F32)

    @pl.when(i < nu_ref[0])
    def _():
        @pl.when(first_ref[i] == 1)
        def _():
            w1b[...] = w1_ref[0].astype(BF16)
            w2b[...] = w2_ref[0].astype(BF16)

        x = x_ref[...].astype(BF16)
        hh = jnp.dot(x, w1b[...], preferred_element_type=F32) + b1_ref[0]
        g = jnp.minimum(hh[:, :d_ff], SWIGLU_LIMIT)
        u = jnp.clip(hh[:, d_ff:], -SWIGLU_LIMIT, SWIGLU_LIMIT)
        act = (u + 1.0) * g * jax.nn.sigmoid(SWIGLU_ALPHA * g)
        y_ref[...] = jnp.dot(act.astype(BF16), w2b[...], preferred_element_type=F32) + b2_ref[0]


def _experts(xs, te, first, n_used, w1, b1, w2, b2, tm):
    n_rows, d = xs.shape
    n_e, _, f2 = w1.shape
    by_tile = lambda i, te, first, nu: (i, 0)
    by_expert = lambda i, te, first, nu: (te[i], 0, 0)
    grid_spec = pltpu.PrefetchScalarGridSpec(
        num_scalar_prefetch=3,
        grid=(n_rows // tm,),
        in_specs=[
            pl.BlockSpec((tm, d), by_tile),
            pl.BlockSpec((1, d, f2), by_expert),
            pl.BlockSpec((1, 1, f2), by_expert),
            pl.BlockSpec((1, f2 // 2, d), by_expert),
            pl.BlockSpec((1, 1, d), by_expert),
        ],
        out_specs=pl.BlockSpec((tm, d), by_tile),
        scratch_shapes=[pltpu.VMEM((d, f2), BF16), pltpu.VMEM((f2 // 2, d), BF16)],
    )
    return pl.pallas_call(
        _experts_body,
        grid_spec=grid_spec,
        out_shape=jax.ShapeDtypeStruct((n_rows, d), F32),
        compiler_params=_cparams("arbitrary"),
        name="moe_experts",
    )(te, first, n_used, xs, w1, b1.reshape(n_e, 1, f2), w2, b2.reshape(n_e, 1, d))


def _combine_body(pos_ref, nxt_ref, h1_ref, wcol_ref, ys_hbm, g_ref, b_ref, o_ref, ybuf, sems, *, alpha, tm):
    i = pl.program_id(0)
    n = pl.num_programs(0)

    def fetch(p_ref, slot):
        def issue(g, c):
            base = pl.multiple_of(g * SUBLANES, SUBLANES)
            for rr in range(SUBLANES):
                for k in range(TOP_K):
                    pltpu.make_async_copy(ys_hbm.at[pl.ds(p_ref[k, base + rr], 1), :],
                                          ybuf.at[slot, k, pl.ds(base + rr, 1), :], sems.at[slot]).start()
            return c
        lax.fori_loop(0, tm // SUBLANES, issue, 0)

    @pl.when(i == 0)
    def _():
        fetch(pos_ref, 0)

    @pl.when(i + 1 < n)
    def _():
        fetch(nxt_ref, (i + 1) % 2)

    slot = i % 2
    for k in range(TOP_K):
        pltpu.make_async_copy(ys_hbm.at[pl.ds(0, tm), :], ybuf.at[slot, k], sems.at[slot]).wait()
    ff = wcol_ref[:, 0:1] * ybuf[slot, 0]
    for k in range(1, TOP_K):
        ff = ff + wcol_ref[:, k:k + 1] * ybuf[slot, k]
    o_ref[...] = _layer_norm(alpha * h1_ref[...] + ff, g_ref[...], b_ref[...])


def _combine(h1, wcol, pos, ys, g, b, alpha, row_off, n):
    d = h1.shape[1]
    tm = min(COMBINE_TILE, n)
    off = row_off // tm
    last = n // tm - 1
    vec = pl.BlockSpec((1, d), lambda i: (0, 0))
    return pl.pallas_call(
        functools.partial(_combine_body, alpha=alpha, tm=tm),
        grid=(n // tm,),
        in_specs=[pl.BlockSpec((TOP_K, tm), lambda i: (0, off + i), memory_space=pltpu.SMEM),
                  pl.BlockSpec((TOP_K, tm), lambda i: (0, off + jnp.minimum(i + 1, last)), memory_space=pltpu.SMEM),
                  pl.BlockSpec((tm, d), lambda i: (off + i, 0)),
                  pl.BlockSpec((tm, LANES), lambda i: (off + i, 0)),
                  pl.BlockSpec(memory_space=pl.ANY), vec, vec],
        out_specs=pl.BlockSpec((tm, d), lambda i: (i, 0)),
        out_shape=jax.ShapeDtypeStruct((n, d), F32),
        scratch_shapes=[pltpu.VMEM((2, TOP_K, tm, d), F32), pltpu.SemaphoreType.DMA((2,))],
        compiler_params=_cparams("arbitrary"),
        name="combine_ln2",
    )(pos, pos, h1, wcol, ys, g, b)


def _row2(v):
    return v.reshape(1, -1).astype(F32)


def kernel(x_prompt, x_sample, cache_k, cache_v, page_table, state_conv, w_in, b_in, lambda_q1, lambda_k1,
           lambda_q2, lambda_k2, subln_g, rel_bias, w_attn_proj, conv_w, conv_b, conv_ln_g, conv_ln_b,
           w_conv_proj, b_conv_proj, w_out, ln1_g, ln1_b, router_w, router_b, expert_w1, expert_b1,
           expert_w2, expert_b2, ln2_g, ln2_b):
    depth = w_in.shape[0]
    bp, seq, d = x_prompt.shape
    bs, t_new, _ = x_sample.shape
    d_att = N_HEADS * QK_DIM
    d_conv = conv_w.shape[2]
    n_p, n_s = bp * seq, bs * t_new
    n_tot = n_p + n_s
    tm = min(TOKEN_TILE, math.gcd(n_p, n_s))
    assert tm % LANES == 0
    tm_moe = min(MOE_TILE, n_tot)
    alpha = (2 * depth) ** 0.25

    hp = x_prompt.reshape(n_p, d)
    hs = x_sample.reshape(n_s, d)
    outs = [[] for _ in range(6)]
    for l in range(depth):
        lam_init = 0.8 - 0.6 * math.exp(-0.3 * l)
        lam = (jnp.exp(jnp.sum(lambda_q1[l].astype(F32) * lambda_k1[l].astype(F32)))
               - jnp.exp(jnp.sum(lambda_q2[l].astype(F32) * lambda_k2[l].astype(F32))) + lam_init).reshape(1)
        w_in_bf = w_in[l].astype(BF16)
        b_in_l = _row2(b_in[l])
        g_sub = _row2(subln_g[l])
        conv_args = (conv_w[l].astype(F32), _row2(conv_b[l]), _row2(conv_ln_g[l]), _row2(conv_ln_b[l]))
        merge_w = (w_attn_proj[l].astype(BF16), w_conv_proj[l].astype(BF16), _row2(b_conv_proj[l]),
                   w_out[l].astype(BF16), _row2(ln1_g[l]), _row2(ln1_b[l]),
                   router_w[l].astype(F32).T, router_b[l].astype(F32).reshape(-1, 1))

        qp, kp, vp, kbp, vbp, up, sgp = _in_proj(hp, w_in_bf, b_in_l, d_att, d_conv, (bp, seq) + _attn_tiles(seq))
        op = _prompt_attention(qp, kbp, vbp, rel_bias, lam, g_sub, lam_init)
        up3 = up.reshape(bp, seq, d_conv)
        yp = _conv_prompt(up3, *conv_args)
        merged = _merge(op.reshape(n_p, d_att), yp.reshape(n_p, d_conv), sgp, hp, merge_w, alpha, n_tot, 0, tm, None)

        qs, ks, vs, kbs, vbs, us, sgs = _in_proj(hs, w_in_bf, b_in_l, d_att, d_conv)
        osr = _sample_attention(qs.reshape(bs, t_new, d_att), kbs.reshape(bs, t_new, d_att),
                                vbs.reshape(bs, t_new, d_att), cache_k[l], cache_v[l], page_table,
                                rel_bias, lam, g_sub, lam_init)
        os_ = osr[:, :t_new].reshape(n_s, d_att)
        buf_s = jnp.concatenate([state_conv[l].astype(F32), us.reshape(bs, t_new, d_conv)], axis=1)
        ys = _conv_sample(buf_s.transpose(1, 0, 2), *conv_args).transpose(1, 0, 2).reshape(n_s, d_conv)
        h1, topi, rank, wcol, cnt = _merge(os_, ys, sgs, hs, merge_w, alpha, n_tot, n_p, tm, merged)

        pos, te, first, n_used, last_tile, n_tiles = _plan(topi, rank, cnt[:, :, 0], tm, tm_moe)
        xs = _dispatch(h1, pos, last_tile, n_used, n_tiles, tm, tm_moe)
        ysort = _experts(xs, te, first, n_used, expert_w1[l], expert_b1[l].astype(F32),
                         expert_w2[l], expert_b2[l].astype(F32), tm_moe)
        g2, b2 = _row2(ln2_g[l]), _row2(ln2_b[l])
        hp = _combine(h1, wcol, pos, ysort, g2, b2, alpha, 0, n_p)
        hs = _combine(h1, wcol, pos, ysort, g2, b2, alpha, n_p, n_s)

        w1 = CONV_WIDTH - 1
        cp = up3[:, seq - w1:] if seq >= w1 else jnp.concatenate(
            [jnp.zeros((bp, w1 - seq, d_conv), F32), up3], axis=1)
        for lst, val in zip(outs, (kp.reshape(bp, seq, N_HEADS, QK_DIM), vp.reshape(bp, seq, N_HEADS, V_DIM), cp,
                                   ks.reshape(bs, t_new, N_HEADS, QK_DIM), vs.reshape(bs, t_new, N_HEADS, V_DIM),
                                   buf_s[:, t_new:])):
            lst.append(val)
    return (hp.reshape(bp, seq, d), hs.reshape(bs, t_new, d)) + tuple(jnp.stack(o) for o in outs)
```

```python
import functools
import math

import jax
import jax.numpy as jnp
from jax import lax
from jax.experimental import pallas as pl
from jax.experimental.pallas import tpu as pltpu

F32 = jnp.float32
BF16 = jnp.bfloat16
I32 = jnp.int32

N_HEADS = 4
HEAD_DIM = 64
QK_DIM = 2 * HEAD_DIM
V_DIM = 2 * HEAD_DIM
ATTN_SCALE = HEAD_DIM ** -0.5
LOG2E = 1.4426950408889634
NEG_INF = -1e30
NUM_BUCKETS = 32
MAX_EXACT = NUM_BUCKETS // 2
MAX_DISTANCE = 128
CONV_WIDTH = 31
TOP_K = 4
SWIGLU_LIMIT = 7.0
SWIGLU_ALPHA = 1.702
LN_EPS = 1e-5
PAGE_SIZE = 128

LANES = 128
SUBLANES = 8
VMEM_LIMIT = 56 * 1024 * 1024

TOKEN_TILE = 512
ATTN_TQ = 1024
ATTN_TK = 512
MOE_TILE = 512
COMBINE_TILE = 256
CONV_HALO = 32
CONV_CHUNK = 64


def _cparams(*sem):
    return pltpu.CompilerParams(dimension_semantics=sem, vmem_limit_bytes=VMEM_LIMIT)


def _load_rows(ref, tm):
    return jnp.concatenate([ref[pl.ds(c, tm, stride=SUBLANES), :] for c in range(SUBLANES)], axis=1)


def _store_rows(ref, x):
    tm = x.shape[0]
    for c in range(SUBLANES):
        ref[pl.ds(c, tm, stride=SUBLANES), :] = x[:, c * LANES:(c + 1) * LANES]


def _layer_norm(x, g, b):
    mu = jnp.mean(x, -1, keepdims=True)
    xc = x - mu
    var = jnp.mean(xc * xc, -1, keepdims=True)
    return xc * lax.rsqrt(var + LN_EPS) * g + b


def _in_proj_body(x_ref, w_ref, b_ref, q_ref, k_ref, v_ref, kb_ref, vb_ref, u_ref, sg_ref, *, d_att, d_conv, by_head):
    x = x_ref[...].astype(BF16)

    def seg(lo, hi):
        return jnp.dot(x, w_ref[:, lo:hi], preferred_element_type=F32) + b_ref[:, lo:hi]

    def head(a, h):
        return a[:, h * QK_DIM:(h + 1) * QK_DIM]

    o = 0
    q = seg(o, o + d_att) * (ATTN_SCALE * LOG2E)
    o += d_att
    k = seg(o, o + d_att)
    k_ref[...] = k
    o += d_att
    v = seg(o, o + d_att)
    v_ref[...] = v
    o += d_att
    if by_head:
        for h in range(N_HEADS):
            q_ref[h] = head(q, h).T.astype(BF16)
            kb_ref[h] = head(k, h).astype(BF16)
            vb_ref[h] = head(v, h).T.astype(BF16)
    else:
        q_ref[...] = q.astype(BF16)
        kb_ref[...] = k.astype(BF16)
        vb_ref[...] = v.astype(BF16)
    c = seg(o, o + 2 * d_conv)
    u_ref[...] = c[:, :d_conv] * jax.nn.sigmoid(c[:, d_conv:])
    o += 2 * d_conv
    sg_ref[...] = jax.nn.sigmoid(seg(o, w_ref.shape[1])).astype(BF16)


def _in_proj(x, w_bf, b, d_att, d_conv, attn_tiles=None):
    n, d = x.shape
    tm = min(TOKEN_TILE, n)
    row = lambda width: pl.BlockSpec((tm, width), lambda i: (i, 0))
    full = lambda a: pl.BlockSpec(a.shape, lambda i: (0, 0))
    f32_out = lambda width: jax.ShapeDtypeStruct((n, width), F32)
    if attn_tiles is None:
        qkv_shapes = [jax.ShapeDtypeStruct((n, d_att), BF16)] * 3
        qkv_specs = [row(d_att)] * 3
    else:
        bsz, seq, tq, tk = attn_tiles
        assert seq % tm == 0 and tq % tm == 0 and tk % tm == 0
        per_seq = seq // tm

        def spec(t, transposed):
            parts = t // tm
            blk = (None, N_HEADS, None, QK_DIM, tm) if transposed else (None, N_HEADS, None, tm, QK_DIM)

            def index(i):
                ti = i % per_seq
                tile, part = ti // parts, ti % parts
                return (i // per_seq, 0, tile, 0, part) if transposed else (i // per_seq, 0, tile, part, 0)
            return pl.BlockSpec(blk, index)

        qkv_shapes = [jax.ShapeDtypeStruct((bsz, N_HEADS, seq // tq, QK_DIM, tq), BF16),
                      jax.ShapeDtypeStruct((bsz, N_HEADS, seq // tk, tk, QK_DIM), BF16),
                      jax.ShapeDtypeStruct((bsz, N_HEADS, seq // tk, V_DIM, tk), BF16)]
        qkv_specs = [spec(tq, True), spec(tk, False), spec(tk, True)]
    out_shape = (qkv_shapes[0], f32_out(d_att), f32_out(d_att), qkv_shapes[1], qkv_shapes[2],
                 f32_out(d_conv), jax.ShapeDtypeStruct((n, 2 * d), BF16))
    return pl.pallas_call(
        functools.partial(_in_proj_body, d_att=d_att, d_conv=d_conv, by_head=attn_tiles is not None),
        grid=(n // tm,),
        in_specs=[row(d), full(w_bf), full(b)],
        out_specs=(qkv_specs[0], row(d_att), row(d_att), qkv_specs[1], qkv_specs[2], row(d_conv), row(2 * d)),
        out_shape=out_shape,
        compiler_params=_cparams("parallel"),
        name="in_proj",
    )(x, w_bf, b)


def _t5_bucket(dist):
    n = jnp.maximum(dist, 0)
    nf = jnp.maximum(n, 1).astype(F32)
    large = MAX_EXACT + (jnp.log(nf / MAX_EXACT) / math.log(MAX_DISTANCE / MAX_EXACT)
                         * (NUM_BUCKETS - MAX_EXACT)).astype(I32)
    large = jnp.minimum(large, NUM_BUCKETS - 1)
    return jnp.where(n < MAX_EXACT, n, large)


def _bias_of(dist, rel_bias, shift=None):
    b = rel_bias[_t5_bucket(dist)].astype(F32)
    if shift is not None:
        b = b - shift
    b = jnp.where((dist >= 0)[..., None], b * LOG2E, NEG_INF)
    return jnp.moveaxis(b, -1, 0)


def _flash_body(lam_ref, qt_ref, k_ref, vt_ref, bias_ref, g_ref, o_ref, m_s, l_s, acc_s, *, ratio, lam_init):
    i = pl.program_id(2)
    qt = qt_ref[...]
    sub = lax.broadcasted_iota(I32, (QK_DIM, 1), 0)
    zero = jnp.zeros_like(qt)
    qs = (jnp.where(sub < HEAD_DIM, qt, zero), jnp.where(sub >= HEAD_DIM, qt, zero))
    m_s[...] = jnp.full(m_s.shape, NEG_INF, F32)
    l_s[...] = jnp.zeros(l_s.shape, F32)
    acc_s[...] = jnp.zeros(acc_s.shape, F32)

    def step(j, bias):
        k = k_ref[j]
        vt = vt_ref[j]
        for c in range(2):
            s = jnp.dot(k, qs[c], preferred_element_type=F32)
            if bias is not None:
                s = s + bias
            m_old = m_s[c]
            m_new = jnp.maximum(m_old, jnp.max(s, 0, keepdims=True))
            alpha = jnp.exp2(m_old - m_new)
            p = jnp.exp2(s - m_new)
            l_s[c] = alpha * l_s[c] + jnp.sum(p, 0, keepdims=True)
            acc_s[c] = alpha * acc_s[c] + jnp.dot(vt, p.astype(BF16), preferred_element_type=F32)
            m_s[c] = m_new

    def far_step(j, carry):
        step(j, None)
        return carry

    lax.fori_loop(0, jnp.maximum(i * ratio - 1, 0), far_step, 0)

    @pl.when(i >= 1)
    def _():
        step(i * ratio - 1, bias_ref[0])

    for o in range(ratio):
        step(i * ratio + o, bias_ref[o + 1])

    o = acc_s[0] / l_s[0] - lam_ref[0] * (acc_s[1] / l_s[1])
    o = o * lax.rsqrt(jnp.mean(o * o, 0, keepdims=True) + LN_EPS) * g_ref[...]
    o_ref[0] = (o * (1.0 - lam_init)).T.astype(BF16)


def _toeplitz(v, rows, cols):
    period = v.shape[-1]
    reps = -(-(rows * (period - 1)) // period)
    flat = jnp.tile(v, reps)[..., :rows * (period - 1)]
    return flat.reshape(v.shape[:-1] + (rows, period - 1))[..., :cols]


def _attn_tiles(seq):
    tq = min(ATTN_TQ, seq)
    tk = min(ATTN_TK, tq)
    assert tk >= MAX_DISTANCE and seq % tq == 0 and tq % tk == 0
    return tq, tk


def _prompt_attention(qt, k4, vt, rel_bias, lam, subln_g, lam_init):
    bsz, _, nq, _, tq = qt.shape
    nk, tk = k4.shape[2], k4.shape[3]
    seq, ratio, width = nq * tq, tq // tk, N_HEADS * V_DIM
    far = rel_bias[NUM_BUCKETS - 1].astype(F32)
    period = tq + tk
    y = jnp.arange(period, dtype=I32)
    r_minus_c = jnp.where(y < tq, y, y - period)
    dist = r_minus_c[None, :] - (jnp.arange(ratio + 1, dtype=I32)[:, None] - 1) * tk
    bias = _toeplitz(_bias_of(dist, rel_bias, far), tk, tq)
    return pl.pallas_call(
        functools.partial(_flash_body, ratio=ratio, lam_init=lam_init),
        grid=(bsz, N_HEADS, nq),
        in_specs=[
            pl.BlockSpec(memory_space=pltpu.SMEM),
            pl.BlockSpec((None, None, None, QK_DIM, tq), lambda b, h, i: (b, h, i, 0, 0)),
            pl.BlockSpec((None, None, nk, tk, QK_DIM), lambda b, h, i: (b, h, 0, 0, 0)),
            pl.BlockSpec((None, None, nk, V_DIM, tk), lambda b, h, i: (b, h, 0, 0, 0)),
            pl.BlockSpec((None, ratio + 1, tk, tq), lambda b, h, i: (h, 0, 0, 0)),
            pl.BlockSpec((V_DIM, 1), lambda b, h, i: (0, 0)),
        ],
        out_specs=pl.BlockSpec((1, tq, V_DIM), lambda b, h, i: (b, i, h)),
        out_shape=jax.ShapeDtypeStruct((bsz, seq, width), BF16),
        scratch_shapes=[pltpu.VMEM((2, 1, tq), F32), pltpu.VMEM((2, 1, tq), F32), pltpu.VMEM((2, V_DIM, tq), F32)],
        compiler_params=_cparams("parallel", "parallel", "arbitrary"),
        name="prompt_attention",
    )(lam, qt, k4, vt, bias, subln_g.reshape(V_DIM, 1))


def _paged_body(pt_ref, lam_ref, q_ref, kn_ref, vn_ref, bias_ref, g_ref, *refs, n_pages, t_new, lam_init):
    k_refs = refs[:n_pages]
    v_refs = refs[n_pages:2 * n_pages]
    o_ref = refs[2 * n_pages]
    lam = lam_ref[0]
    rows = 2 * t_new
    cols = PAGE_SIZE * N_HEADS
    nt = (((1,), (1,)), ((), ()))
    q = q_ref[0]
    parts = [lax.dot_general(q, k_refs[p][...].astype(BF16), nt, preferred_element_type=F32)
             for p in range(n_pages)]
    parts.append(lax.dot_general(q, kn_ref[0], nt, preferred_element_type=F32))
    s = jnp.concatenate(parts, axis=1) + bias_ref[...]
    m = jnp.max(s, -1, keepdims=True)
    p_un = jnp.exp2(s - m)
    pn = p_un / jnp.sum(p_un, -1, keepdims=True)
    w = (pn - lam * pltpu.roll(pn, N_HEADS * rows - t_new, axis=0)).astype(BF16)
    o = jnp.dot(w[:, n_pages * cols:], vn_ref[0], preferred_element_type=F32)
    for p in range(n_pages):
        o = o + jnp.dot(w[:, p * cols:(p + 1) * cols], v_refs[p][...].astype(BF16), preferred_element_type=F32)
    o = o * lax.rsqrt(jnp.mean(o * o, -1, keepdims=True) + LN_EPS) * g_ref[...]
    o = (o * (1.0 - lam_init)).astype(BF16)
    o_ref[0] = jnp.concatenate([o[h * rows:(h + 1) * rows] for h in range(N_HEADS)], axis=1)


def _sample_attention(q, kb, vb, cache_k, cache_v, page_table, rel_bias, lam, subln_g, lam_init):
    bs, t_new, width = q.shape
    n_pool = cache_k.shape[0]
    n_pages = page_table.shape[1]
    past = n_pages * PAGE_SIZE
    rows = 2 * t_new
    cols = PAGE_SIZE * N_HEADS
    new_pos = LANES // N_HEADS
    assert rows % SUBLANES == 0 and t_new <= new_pos
    qh = q.reshape(bs, t_new, N_HEADS, QK_DIM).transpose(0, 2, 1, 3)
    lane_map = (jnp.arange(QK_DIM) // HEAD_DIM)[None, :] == jnp.arange(2)[:, None]
    q_rows = (qh[:, :, None] * lane_map[None, None, :, None, :].astype(BF16)).reshape(bs, N_HEADS * rows, QK_DIM)
    pad = ((0, 0), (0, LANES - t_new * N_HEADS), (0, 0))
    kn = jnp.pad(kb.reshape(bs, t_new * N_HEADS, QK_DIM), pad)
    vn = jnp.pad(vb.reshape(bs, t_new * N_HEADS, V_DIM), pad)
    qpos = past + jnp.arange(t_new, dtype=I32)
    kpos = jnp.concatenate([jnp.arange(past + t_new, dtype=I32),
                            jnp.full((new_pos - t_new,), past + t_new + new_pos, I32)])
    b = _bias_of(qpos[:, None] - kpos[None, :], rel_bias)
    same = jnp.arange(N_HEADS)[:, None] == jnp.arange(N_HEADS)[None, :]
    b = jnp.where(same[:, None, None, :], b[..., None], NEG_INF)
    b = b.reshape(N_HEADS, 1, t_new, -1)
    bias = jnp.concatenate([b, b], axis=1).reshape(N_HEADS * rows, -1)

    page_spec = lambda j: pl.BlockSpec((None, cols, QK_DIM), lambda b, pt, j=j: (pt[b * n_pages + j], 0, 0))
    per_seq = lambda r, c: pl.BlockSpec((1, r, c), lambda b, pt: (b, 0, 0))
    grid_spec = pltpu.PrefetchScalarGridSpec(
        num_scalar_prefetch=1,
        grid=(bs,),
        in_specs=[
            pl.BlockSpec(memory_space=pltpu.SMEM),
            per_seq(N_HEADS * rows, QK_DIM),
            per_seq(LANES, QK_DIM),
            per_seq(LANES, V_DIM),
            pl.BlockSpec(bias.shape, lambda b, pt: (0, 0)),
            pl.BlockSpec((1, V_DIM), lambda b, pt: (0, 0)),
        ] + [page_spec(j) for j in range(n_pages)] * 2,
        out_specs=pl.BlockSpec((1, rows, width), lambda b, pt: (b, 0, 0)),
    )
    ck = cache_k.reshape(n_pool, cols, QK_DIM)
    cv = cache_v.reshape(n_pool, cols, V_DIM)
    return pl.pallas_call(
        functools.partial(_paged_body, n_pages=n_pages, t_new=t_new, lam_init=lam_init),
        grid_spec=grid_spec,
        out_shape=jax.ShapeDtypeStruct((bs, rows, width), BF16),
        compiler_params=_cparams("parallel"),
        name="sample_attention",
    )(page_table.reshape(-1), lam, q_rows, kn, vn, bias, subln_g, *([ck] * n_pages), *([cv] * n_pages))


def _conv_prompt_body(u_ref, halo_ref, w_ref, cb_ref, g_ref, b_ref, o_ref, buf, shifted, *, tc):
    i = pl.program_id(1)
    halo = halo_ref[0]
    buf[0:CONV_HALO, :] = jnp.where(i > 0, halo, jnp.zeros_like(halo))
    buf[CONV_HALO:, :] = u_ref[0]
    rows = CONV_HALO + tc - SUBLANES
    for s in range(1, SUBLANES):
        shifted[s - 1, 0:rows, :] = buf[s:s + rows, :]
    first = CONV_HALO - (CONV_WIDTH - 1)
    for r0 in range(0, tc, CONV_CHUNK):
        acc = jnp.zeros((CONV_CHUNK, u_ref.shape[2]), F32) + cb_ref[...]
        for j in range(CONV_WIDTH):
            s = (first + j) % SUBLANES
            a = first + j - s + r0
            win = buf[a:a + CONV_CHUNK, :] if s == 0 else shifted[s - 1, a:a + CONV_CHUNK, :]
            acc = acc + w_ref[j:j + 1, :] * win
        y = _layer_norm(acc, g_ref[...], b_ref[...])
        o_ref[0, r0:r0 + CONV_CHUNK, :] = (y * jax.nn.sigmoid(y)).astype(BF16)


def _conv_prompt(u, conv_w, conv_b, ln_g, ln_b):
    bsz, seq, ch = u.shape
    tc = min(TOKEN_TILE, seq)
    assert seq % tc == 0 and tc % CONV_CHUNK == 0 and tc % CONV_HALO == 0
    vec = pl.BlockSpec((1, ch), lambda b, i: (0, 0))
    return pl.pallas_call(
        functools.partial(_conv_prompt_body, tc=tc),
        grid=(bsz, seq // tc),
        in_specs=[
            pl.BlockSpec((1, tc, ch), lambda b, i: (b, i, 0)),
            pl.BlockSpec((1, CONV_HALO, ch), lambda b, i: (b, jnp.maximum(i * (tc // CONV_HALO) - 1, 0), 0)),
            pl.BlockSpec((CONV_WIDTH, ch), lambda b, i: (0, 0)),
            vec, vec, vec,
        ],
        out_specs=pl.BlockSpec((1, tc, ch), lambda b, i: (b, i, 0)),
        out_shape=jax.ShapeDtypeStruct((bsz, seq, ch), BF16),
        scratch_shapes=[pltpu.VMEM((CONV_HALO + tc, ch), F32), pltpu.VMEM((SUBLANES - 1, CONV_HALO + tc, ch), F32)],
        compiler_params=_cparams("parallel", "arbitrary"),
        name="conv_prompt",
    )(u, u, conv_w, conv_b, ln_g, ln_b)


def _conv_sample_body(buf_ref, w_ref, cb_ref, g_ref, b_ref, o_ref, *, t_new):
    for t in range(t_new):
        acc = jnp.zeros(buf_ref.shape[1:], F32) + cb_ref[...]
        for j in range(CONV_WIDTH):
            acc = acc + w_ref[j:j + 1, :] * buf_ref[t + j]
        y = _layer_norm(acc, g_ref[...], b_ref[...])
        o_ref[t] = (y * jax.nn.sigmoid(y)).astype(BF16)


def _conv_sample(buf_t, conv_w, conv_b, ln_g, ln_b):
    rows, bs, ch = buf_t.shape
    t_new = rows - (CONV_WIDTH - 1)
    gb = min(32, bs)
    assert bs % gb == 0
    vec = pl.BlockSpec((1, ch), lambda i: (0, 0))
    return pl.pallas_call(
        functools.partial(_conv_sample_body, t_new=t_new),
        grid=(bs // gb,),
        in_specs=[pl.BlockSpec((rows, gb, ch), lambda i: (0, i, 0)),
                  pl.BlockSpec((CONV_WIDTH, ch), lambda i: (0, 0)), vec, vec, vec],
        out_specs=pl.BlockSpec((t_new, gb, ch), lambda i: (0, i, 0)),
        out_shape=jax.ShapeDtypeStruct((t_new, bs, ch), BF16),
        compiler_params=_cparams("parallel"),
        name="conv_sample",
    )(buf_t, conv_w, conv_b, ln_g, ln_b)


N_MERGE_OUT = 5


def _merge_body(*refs, alpha, n_alias):
    (o_ref, y_ref, sg_ref, x_ref, wa_ref, wc_ref, bc_ref, wo_ref, g1_ref, b1_ref, rw_ref, rb_ref) = refs[:12]
    h1_ref, ti_ref, rk_ref, wcol_ref, cnt_ref = refs[12 + n_alias:]
    tm, d = x_ref.shape
    a = jnp.dot(o_ref[...], wa_ref[...], preferred_element_type=F32)
    b = jnp.dot(y_ref[...], wc_ref[...], preferred_element_type=F32) + bc_ref[...]
    mix_in = sg_ref[:, :d].astype(F32) * a + sg_ref[:, d:].astype(F32) * b
    mix = jnp.dot(mix_in.astype(BF16), wo_ref[...], preferred_element_type=F32)
    h1 = _layer_norm(alpha * x_ref[...] + mix, g1_ref[...], b1_ref[...])
    _store_rows(h1_ref, h1)
    logits = lax.dot_general(rw_ref[...], h1, (((1,), (1,)), ((), ())), preferred_element_type=F32,
                             precision=lax.Precision.HIGHEST) + rb_ref[...]
    n_e = logits.shape[0]
    eid = lax.broadcasted_iota(I32, logits.shape, 0)
    vals, idxs, hots = [], [], []
    for _ in range(TOP_K):
        m = jnp.max(logits, 0, keepdims=True)
        idx = jnp.min(jnp.where(logits == m, eid, n_e), 0, keepdims=True)
        hot = eid == idx
        vals.append(m)
        idxs.append(idx)
        hots.append(hot)
        logits = jnp.where(hot, -jnp.inf, logits)
    ex = [jnp.exp(v - vals[0]) for v in vals]
    den = ex[0]
    for e in ex[1:]:
        den = den + e
    ti_ref[...] = jnp.concatenate(idxs, 0)
    wrows = jnp.concatenate([e / den for e in ex] + [jnp.zeros((LANES - TOP_K, tm), F32)], 0)
    wcol_ref[...] = wrows.T
    sel = hots[0]
    for hot in hots[1:]:
        sel = sel | hot
    sel_f = jnp.where(sel, 1.0, 0.0)
    before = lax.broadcasted_iota(I32, (tm, tm), 0) < lax.broadcasted_iota(I32, (tm, tm), 1)
    upper = jnp.where(before, 1.0, 0.0).astype(BF16)
    ahead = jnp.dot(sel_f.astype(BF16), upper, preferred_element_type=F32)
    rk_ref[...] = jnp.concatenate(
        [jnp.sum(jnp.where(hot, ahead, 0.0), 0, keepdims=True) for hot in hots], 0).astype(I32)
    cnt = jnp.sum(sel_f, 1, keepdims=True).astype(I32)
    cnt_ref[0] = jnp.broadcast_to(cnt, (n_e, LANES))


def _merge(o, yact, sg, x, wts, alpha, n_total, row_off, tm, prev):
    n, d = x.shape
    wa, wc, bc, wo, g1, b1, rwt, rb = wts
    n_e = rwt.shape[0]
    off = row_off // tm
    last = n // tm - 1
    steps = n // tm if prev is not None else n_total // tm
    row = lambda width: pl.BlockSpec((tm, width), lambda i: (jnp.minimum(i, last), 0))
    full = lambda a: pl.BlockSpec(a.shape, lambda i: (0, 0))
    in_specs = [row(o.shape[1]), row(yact.shape[1]), row(2 * d), row(d),
                full(wa), full(wc), full(bc), full(wo), full(g1), full(b1), full(rwt), full(rb)]
    args = [o, yact, sg, x, wa, wc, bc, wo, g1, b1, rwt, rb]
    aliases = {}
    if prev is not None:
        in_specs += [pl.BlockSpec(memory_space=pl.ANY)] * N_MERGE_OUT
        aliases = {len(args) + j: j for j in range(N_MERGE_OUT)}
        args += list(prev)
    slot_major = pl.BlockSpec((TOP_K, tm), lambda i: (0, off + i))
    return pl.pallas_call(
        functools.partial(_merge_body, alpha=alpha, n_alias=0 if prev is None else N_MERGE_OUT),
        grid=(steps,),
        in_specs=in_specs,
        out_specs=(pl.BlockSpec((tm * SUBLANES, LANES), lambda i: (off + i, 0)), slot_major, slot_major,
                   pl.BlockSpec((tm, LANES), lambda i: (off + i, 0)),
                   pl.BlockSpec((1, n_e, LANES), lambda i: (off + i, 0, 0))),
        out_shape=(jax.ShapeDtypeStruct((n_total * SUBLANES, LANES), F32),
                   jax.ShapeDtypeStruct((TOP_K, n_total), I32),
                   jax.ShapeDtypeStruct((TOP_K, n_total), I32),
                   jax.ShapeDtypeStruct((n_total, LANES), F32),
                   jax.ShapeDtypeStruct((n_total // tm, n_e, LANES), I32)),
        input_output_aliases=aliases,
        compiler_params=_cparams("parallel"),
        name="merge_ln1_router",
    )(*args)


def _plan(topi, rank, cnt, tm_tok, tm_moe):
    k, n = topi.shape
    n_e = cnt.shape[1]
    before = jnp.cumsum(cnt, 0) - cnt
    total = jnp.sum(cnt, 0)
    ntile_e = (total + tm_moe - 1) // tm_moe
    tend = jnp.cumsum(ntile_e)
    tstart = tend - ntile_e
    base = (tstart * tm_moe)[None, :] + before
    base_tok = jnp.repeat(base, tm_tok, axis=0)
    hot = topi[:, :, None] == jnp.arange(n_e, dtype=I32)[None, None, :]
    pos = (jnp.sum(jnp.where(hot, base_tok[None], 0), -1).astype(I32) + rank) * SUBLANES
    n_tiles = (k * n + tm_moe - 1) // tm_moe + n_e
    n_used = tend[-1]
    tid = jnp.arange(n_tiles, dtype=I32)
    src = jnp.minimum(tid, n_used - 1)
    te = jnp.minimum(jnp.sum((src[:, None] >= tend[None, :]).astype(I32), 1), n_e - 1)
    first = (tid == tstart[te]).astype(I32)
    last_tile = jnp.where(ntile_e > 0, tend - 1, -1).astype(I32)
    return pos, te.astype(I32), first, n_used.reshape(1).astype(I32), last_tile, n_tiles


def _dispatch_body(lt_ref, nu_ref, pos_ref, x_ref, xs_hbm, zbuf, zsem, sem, *, tm, tm_moe, n_e, n_tiles):
    i = pl.program_id(0)

    def zero_copy(tile):
        start = pl.multiple_of(tile * (tm_moe * SUBLANES), tm_moe * SUBLANES)
        return pltpu.make_async_copy(zbuf, xs_hbm.at[pl.ds(start, tm_moe * SUBLANES), :], zsem)

    @pl.when(i == 0)
    def _():
        zbuf[...] = jnp.zeros(zbuf.shape, F32)

        def z_start(e, c):
            @pl.when(lt_ref[e] >= 0)
            def _():
                zero_copy(lt_ref[e]).start()
            return c

        def z_wait(e, c):
            @pl.when(lt_ref[e] >= 0)
            def _():
                zero_copy(lt_ref[e]).wait()
            return c

        def t_start(t, c):
            zero_copy(t).start()
            return c

        def t_wait(t, c):
            zero_copy(t).wait()
            return c

        lax.fori_loop(0, n_e, z_start, 0)
        lax.fori_loop(nu_ref[0], n_tiles, t_start, 0)
        lax.fori_loop(0, n_e, z_wait, 0)
        lax.fori_loop(nu_ref[0], n_tiles, t_wait, 0)

    def issue(g, c):
        base = pl.multiple_of(g * SUBLANES, SUBLANES)
        for rr in range(SUBLANES):
            src = x_ref.at[pl.ds(pl.multiple_of((base + rr) * SUBLANES, SUBLANES), SUBLANES), :]
            for k in range(TOP_K):
                dst = pl.multiple_of(pos_ref[k, base + rr], SUBLANES)
                pltpu.make_async_copy(src, xs_hbm.at[pl.ds(dst, SUBLANES), :], sem).start()
        return c

    lax.fori_loop(0, tm // SUBLANES, issue, 0)
    for k in range(TOP_K):
        pltpu.make_async_copy(x_ref, xs_hbm.at[pl.ds(0, tm * SUBLANES), :], sem).wait()


def _dispatch(h1, pos, last_tile, n_used, n_tiles, tm, tm_moe):
    n = h1.shape[0] // SUBLANES
    n_e = last_tile.shape[0]
    grid_spec = pltpu.PrefetchScalarGridSpec(
        num_scalar_prefetch=2,
        grid=(n // tm,),
        in_specs=[pl.BlockSpec((TOP_K, tm), lambda i, lt, nu: (0, i), memory_space=pltpu.SMEM),
                  pl.BlockSpec((tm * SUBLANES, LANES), lambda i, lt, nu: (i, 0))],
        out_specs=pl.BlockSpec(memory_space=pl.ANY),
        scratch_shapes=[pltpu.VMEM((tm_moe * SUBLANES, LANES), F32), pltpu.SemaphoreType.DMA,
                        pltpu.SemaphoreType.DMA],
    )
    return pl.pallas_call(
        functools.partial(_dispatch_body, tm=tm, tm_moe=tm_moe, n_e=n_e, n_tiles=n_tiles),
        grid_spec=grid_spec,
        out_shape=jax.ShapeDtypeStruct((n_tiles * tm_moe * SUBLANES, LANES), F32),
        compiler_params=_cparams("arbitrary"),
        name="moe_dispatch",
    )(last_tile, n_used, pos, h1)


def _experts_body(te_ref, first_ref, nu_ref, x_ref, w1_ref, b1_ref, w2_ref, b2_ref, y_ref, w1b, w2b, *, tm):
    i = pl.program_id(0)
    d_ff = w2_ref.shape[1]

    @pl.when(i >= nu_ref[0])
    def _():
        y_ref[...] = jnp.zeros(y_ref.shape, F32)

    @pl.when(i < nu_ref[0])
    def _():
        @pl.when(first_ref[i] == 1)
        def _():
            w1b[...] = w1_ref[0].astype(BF16)
            w2b[...] = w2_ref[0].astype(BF16)

        x = _load_rows(x_ref, tm).astype(BF16)
        hh = jnp.dot(x, w1b[...], preferred_element_type=F32) + b1_ref[0]
        g = jnp.minimum(hh[:, :d_ff], SWIGLU_LIMIT)
        u = jnp.clip(hh[:, d_ff:], -SWIGLU_LIMIT, SWIGLU_LIMIT)
        act = (u + 1.0) * g * jax.nn.sigmoid(SWIGLU_ALPHA * g)
        _store_rows(y_ref, jnp.dot(act.astype(BF16), w2b[...], preferred_element_type=F32) + b2_ref[0])


def _experts(xs, te, first, n_used, w1, b1, w2, b2, tm):
    n_e, d, f2 = w1.shape
    rows = tm * SUBLANES
    by_tile = lambda i, te, first, nu: (i, 0)
    by_expert = lambda i, te, first, nu: (te[i], 0, 0)
    grid_spec = pltpu.PrefetchScalarGridSpec(
        num_scalar_prefetch=3,
        grid=(xs.shape[0] // rows,),
        in_specs=[
            pl.BlockSpec((rows, LANES), by_tile),
            pl.BlockSpec((1, d, f2), by_expert),
            pl.BlockSpec((1, 1, f2), by_expert),
            pl.BlockSpec((1, f2 // 2, d), by_expert),
            pl.BlockSpec((1, 1, d), by_expert),
        ],
        out_specs=pl.BlockSpec((rows, LANES), by_tile),
        scratch_shapes=[pltpu.VMEM((d, f2), BF16), pltpu.VMEM((f2 // 2, d), BF16)],
    )
    return pl.pallas_call(
        functools.partial(_experts_body, tm=tm),
        grid_spec=grid_spec,
        out_shape=jax.ShapeDtypeStruct(xs.shape, F32),
        compiler_params=_cparams("arbitrary"),
        name="moe_experts",
    )(te, first, n_used, xs, w1, b1.reshape(n_e, 1, f2), w2, b2.reshape(n_e, 1, d))


def _combine_body(pos_ref, nxt_ref, h1_ref, wcol_ref, ys_hbm, g_ref, b_ref, o_ref, ybuf, sems, *, alpha, tm):
    i = pl.program_id(0)
    n = pl.num_programs(0)

    def fetch(p_ref, slot):
        def issue(g, c):
            base = pl.multiple_of(g * SUBLANES, SUBLANES)
            for rr in range(SUBLANES):
                dst = pl.ds(pl.multiple_of((base + rr) * SUBLANES, SUBLANES), SUBLANES)
                for k in range(TOP_K):
                    src = pl.multiple_of(p_ref[k, base + rr], SUBLANES)
                    pltpu.make_async_copy(ys_hbm.at[pl.ds(src, SUBLANES), :], ybuf.at[slot, k, dst, :],
                                          sems.at[slot]).start()
            return c
        lax.fori_loop(0, tm // SUBLANES, issue, 0)

    @pl.when(i == 0)
    def _():
        fetch(pos_ref, 0)

    @pl.when(i + 1 < n)
    def _():
        fetch(nxt_ref, (i + 1) % 2)

    slot = i % 2
    for k in range(TOP_K):
        pltpu.make_async_copy(ys_hbm.at[pl.ds(0, tm * SUBLANES), :], ybuf.at[slot, k], sems.at[slot]).wait()
    ff = wcol_ref[:, 0:1] * _load_rows(ybuf.at[slot, 0], tm)
    for k in range(1, TOP_K):
        ff = ff + wcol_ref[:, k:k + 1] * _load_rows(ybuf.at[slot, k], tm)
    o_ref[...] = _layer_norm(alpha * _load_rows(h1_ref, tm) + ff, g_ref[...], b_ref[...])


def _combine(h1, wcol, pos, ys, g, b, alpha, row_off, n):
    d = g.shape[1]
    tm = min(COMBINE_TILE, n)
    off = row_off // tm
    last = n // tm - 1
    vec = pl.BlockSpec((1, d), lambda i: (0, 0))
    return pl.pallas_call(
        functools.partial(_combine_body, alpha=alpha, tm=tm),
        grid=(n // tm,),
        in_specs=[pl.BlockSpec((TOP_K, tm), lambda i: (0, off + i), memory_space=pltpu.SMEM),
                  pl.BlockSpec((TOP_K, tm), lambda i: (0, off + jnp.minimum(i + 1, last)), memory_space=pltpu.SMEM),
                  pl.BlockSpec((tm * SUBLANES, LANES), lambda i: (off + i, 0)),
                  pl.BlockSpec((tm, LANES), lambda i: (off + i, 0)),
                  pl.BlockSpec(memory_space=pl.ANY), vec, vec],
        out_specs=pl.BlockSpec((tm, d), lambda i: (i, 0)),
        out_shape=jax.ShapeDtypeStruct((n, d), F32),
        scratch_shapes=[pltpu.VMEM((2, TOP_K, tm * SUBLANES, LANES), F32), pltpu.SemaphoreType.DMA((2,))],
        compiler_params=_cparams("arbitrary"),
        name="combine_ln2",
    )(pos, pos, h1, wcol, ys, g, b)


def _row2(v):
    return v.reshape(1, -1).astype(F32)


def kernel(x_prompt, x_sample, cache_k, cache_v, page_table, state_conv, w_in, b_in, lambda_q1, lambda_k1,
           lambda_q2, lambda_k2, subln_g, rel_bias, w_attn_proj, conv_w, conv_b, conv_ln_g, conv_ln_b,
           w_conv_proj, b_conv_proj, w_out, ln1_g, ln1_b, router_w, router_b, expert_w1, expert_b1,
           expert_w2, expert_b2, ln2_g, ln2_b):
    depth = w_in.shape[0]
    bp, seq, d = x_prompt.shape
    bs, t_new, _ = x_sample.shape
    d_att = N_HEADS * QK_DIM
    d_conv = conv_w.shape[2]
    n_p, n_s = bp * seq, bs * t_new
    n_tot = n_p + n_s
    tm = min(TOKEN_TILE, math.gcd(n_p, n_s))
    assert tm % LANES == 0
    tm_moe = min(MOE_TILE, n_tot)
    alpha = (2 * depth) ** 0.25

    hp = x_prompt.reshape(n_p, d)
    hs = x_sample.reshape(n_s, d)
    outs = [[] for _ in range(6)]
    for l in range(depth):
        lam_init = 0.8 - 0.6 * math.exp(-0.3 * l)
        lam = (jnp.exp(jnp.sum(lambda_q1[l].astype(F32) * lambda_k1[l].astype(F32)))
               - jnp.exp(jnp.sum(lambda_q2[l].astype(F32) * lambda_k2[l].astype(F32))) + lam_init).reshape(1)
        w_in_bf = w_in[l].astype(BF16)
        b_in_l = _row2(b_in[l])
        g_sub = _row2(subln_g[l])
        conv_args = (conv_w[l].astype(F32), _row2(conv_b[l]), _row2(conv_ln_g[l]), _row2(conv_ln_b[l]))
        merge_w = (w_attn_proj[l].astype(BF16), w_conv_proj[l].astype(BF16), _row2(b_conv_proj[l]),
                   w_out[l].astype(BF16), _row2(ln1_g[l]), _row2(ln1_b[l]),
                   router_w[l].astype(F32).T, router_b[l].astype(F32).reshape(-1, 1))

        qp, kp, vp, kbp, vbp, up, sgp = _in_proj(hp, w_in_bf, b_in_l, d_att, d_conv, (bp, seq) + _attn_tiles(seq))
        op = _prompt_attention(qp, kbp, vbp, rel_bias, lam, g_sub, lam_init)
        up3 = up.reshape(bp, seq, d_conv)
        yp = _conv_prompt(up3, *conv_args)
        merged = _merge(op.reshape(n_p, d_att), yp.reshape(n_p, d_conv), sgp, hp, merge_w, alpha, n_tot, 0, tm, None)

        qs, ks, vs, kbs, vbs, us, sgs = _in_proj(hs, w_in_bf, b_in_l, d_att, d_conv)
        osr = _sample_attention(qs.reshape(bs, t_new, d_att), kbs.reshape(bs, t_new, d_att),
                                vbs.reshape(bs, t_new, d_att), cache_k[l], cache_v[l], page_table,
                                rel_bias, lam, g_sub, lam_init)
        os_ = osr[:, :t_new].reshape(n_s, d_att)
        buf_s = jnp.concatenate([state_conv[l].astype(F32), us.reshape(bs, t_new, d_conv)], axis=1)
        ys = _conv_sample(buf_s.transpose(1, 0, 2), *conv_args).transpose(1, 0, 2).reshape(n_s, d_conv)
        h1, topi, rank, wcol, cnt = _merge(os_, ys, sgs, hs, merge_w, alpha, n_tot, n_p, tm, merged)

        pos, te, first, n_used, last_tile, n_tiles = _plan(topi, rank, cnt[:, :, 0], tm, tm_moe)
        xs = _dispatch(h1, pos, last_tile, n_used, n_tiles, tm, tm_moe)
        ysort = _experts(xs, te, first, n_used, expert_w1[l], expert_b1[l].astype(F32),
                         expert_w2[l], expert_b2[l].astype(F32), tm_moe)
        g2, b2 = _row2(ln2_g[l]), _row2(ln2_b[l])
        hp = _combine(h1, wcol, pos, ysort, g2, b2, alpha, 0, n_p)
        hs = _combine(h1, wcol, pos, ysort, g2, b2, alpha, n_p, n_s)

        w1 = CONV_WIDTH - 1
        cp = up3[:, seq - w1:] if seq >= w1 else jnp.concatenate(
            [jnp.zeros((bp, w1 - seq, d_conv), F32), up3], axis=1)
        for lst, val in zip(outs, (kp.reshape(bp, seq, N_HEADS, QK_DIM), vp.reshape(bp, seq, N_HEADS, V_DIM), cp,
                                   ks.reshape(bs, t_new, N_HEADS, QK_DIM), vs.reshape(bs, t_new, N_HEADS, V_DIM),
                                   buf_s[:, t_new:])):
            lst.append(val)
    return (hp.reshape(bp, seq, d), hs.reshape(bs, t_new, d)) + tuple(jnp.stack(o) for o in outs)
```

```python
import functools
import math

import jax
import jax.numpy as jnp
from jax import lax
from jax.experimental import pallas as pl
from jax.experimental.pallas import tpu as pltpu

F32 = jnp.float32
BF16 = jnp.bfloat16
I32 = jnp.int32

N_HEADS = 4
HEAD_DIM = 64
QK_DIM = 2 * HEAD_DIM
V_DIM = 2 * HEAD_DIM
ATTN_SCALE = HEAD_DIM ** -0.5
LOG2E = 1.4426950408889634
NEG_INF = -1e30
NUM_BUCKETS = 32
MAX_EXACT = NUM_BUCKETS // 2
MAX_DISTANCE = 128
CONV_WIDTH = 31
TOP_K = 4
SWIGLU_LIMIT = 7.0
SWIGLU_ALPHA = 1.702
LN_EPS = 1e-5
PAGE_SIZE = 128

LANES = 128
SUBLANES = 8
VMEM_LIMIT = 56 * 1024 * 1024

TOKEN_TILE = 512
ATTN_TQ = 1024
ATTN_TK = 512
ATTN_GROUP = 2
MOE_TILE = 512
COMBINE_TILE = 256
CONV_HALO = 32
CONV_CHUNK = 64


def _cparams(*sem):
    return pltpu.CompilerParams(dimension_semantics=sem, vmem_limit_bytes=VMEM_LIMIT)


def _load_rows(ref, tm):
    return jnp.concatenate([ref[pl.ds(c, tm, stride=SUBLANES), :] for c in range(SUBLANES)], axis=1)


def _store_rows(ref, x):
    tm = x.shape[0]
    for c in range(SUBLANES):
        ref[pl.ds(c, tm, stride=SUBLANES), :] = x[:, c * LANES:(c + 1) * LANES]


def _layer_norm(x, g, b):
    mu = jnp.mean(x, -1, keepdims=True)
    xc = x - mu
    var = jnp.mean(xc * xc, -1, keepdims=True)
    return xc * lax.rsqrt(var + LN_EPS) * g + b


def _in_proj_body(x_ref, w_ref, b_ref, q_ref, k_ref, v_ref, kb_ref, vb_ref, u_ref, sg_ref, *, d_att, d_conv, by_head):
    x = x_ref[...].astype(BF16)

    def seg(lo, hi):
        return jnp.dot(x, w_ref[:, lo:hi], preferred_element_type=F32) + b_ref[:, lo:hi]

    def head(a, h):
        return a[:, h * QK_DIM:(h + 1) * QK_DIM]

    o = 0
    q = seg(o, o + d_att) * (ATTN_SCALE * LOG2E)
    o += d_att
    k = seg(o, o + d_att)
    o += d_att
    v = seg(o, o + d_att)
    o += d_att
    tm = x.shape[0]
    for h in range(N_HEADS):
        k_ref[pl.ds(h, tm, stride=N_HEADS), :] = head(k, h)
        v_ref[pl.ds(h, tm, stride=N_HEADS), :] = head(v, h)
    if by_head:
        for h in range(N_HEADS):
            q_ref[h] = head(q, h).T.astype(BF16)
            kb_ref[h] = head(k, h).astype(BF16)
            vb_ref[h] = head(v, h).T.astype(BF16)
    else:
        q_ref[...] = q.astype(BF16)
        kb_ref[...] = k.astype(BF16)
        vb_ref[...] = v.astype(BF16)
    c = seg(o, o + 2 * d_conv)
    u_ref[...] = c[:, :d_conv] * jax.nn.sigmoid(c[:, d_conv:])
    o += 2 * d_conv
    sg_ref[...] = jax.nn.sigmoid(seg(o, w_ref.shape[1])).astype(BF16)


def _in_proj(x, w_bf, b, d_att, d_conv, attn_tiles=None):
    n, d = x.shape
    tm = min(TOKEN_TILE, n)
    row = lambda width: pl.BlockSpec((tm, width), lambda i: (i, 0))
    full = lambda a: pl.BlockSpec(a.shape, lambda i: (0, 0))
    f32_out = lambda width: jax.ShapeDtypeStruct((n, width), F32)
    kv_shape = jax.ShapeDtypeStruct((n * N_HEADS, QK_DIM), F32)
    kv_spec = pl.BlockSpec((tm * N_HEADS, QK_DIM), lambda i: (i, 0))
    if attn_tiles is None:
        qkv_shapes = [jax.ShapeDtypeStruct((n, d_att), BF16)] * 3
        qkv_specs = [row(d_att)] * 3
    else:
        bsz, seq, tq, tk = attn_tiles
        assert seq % tm == 0 and tq % tm == 0 and tk % tm == 0
        per_seq = seq // tm

        def spec(t, transposed):
            parts = t // tm
            blk = (None, N_HEADS, None, QK_DIM, tm) if transposed else (None, N_HEADS, None, tm, QK_DIM)

            def index(i):
                ti = i % per_seq
                tile, part = ti // parts, ti % parts
                return (i // per_seq, 0, tile, 0, part) if transposed else (i // per_seq, 0, tile, part, 0)
            return pl.BlockSpec(blk, index)

        qkv_shapes = [jax.ShapeDtypeStruct((bsz, N_HEADS, seq // tq, QK_DIM, tq), BF16),
                      jax.ShapeDtypeStruct((bsz, N_HEADS, seq // tk, tk, QK_DIM), BF16),
                      jax.ShapeDtypeStruct((bsz, N_HEADS, seq // tk, V_DIM, tk), BF16)]
        qkv_specs = [spec(tq, True), spec(tk, False), spec(tk, True)]
    out_shape = (qkv_shapes[0], kv_shape, kv_shape, qkv_shapes[1], qkv_shapes[2],
                 f32_out(d_conv), jax.ShapeDtypeStruct((n, 2 * d), BF16))
    return pl.pallas_call(
        functools.partial(_in_proj_body, d_att=d_att, d_conv=d_conv, by_head=attn_tiles is not None),
        grid=(n // tm,),
        in_specs=[row(d), full(w_bf), full(b)],
        out_specs=(qkv_specs[0], kv_spec, kv_spec, qkv_specs[1], qkv_specs[2], row(d_conv), row(2 * d)),
        out_shape=out_shape,
        compiler_params=_cparams("parallel"),
        name="in_proj",
    )(x, w_bf, b)


def _t5_bucket(dist):
    n = jnp.maximum(dist, 0)
    nf = jnp.maximum(n, 1).astype(F32)
    large = MAX_EXACT + (jnp.log(nf / MAX_EXACT) / math.log(MAX_DISTANCE / MAX_EXACT)
                         * (NUM_BUCKETS - MAX_EXACT)).astype(I32)
    large = jnp.minimum(large, NUM_BUCKETS - 1)
    return jnp.where(n < MAX_EXACT, n, large)


def _bias_of(dist, rel_bias, shift=None):
    b = rel_bias[_t5_bucket(dist)].astype(F32)
    if shift is not None:
        b = b - shift
    b = jnp.where((dist >= 0)[..., None], b * LOG2E, NEG_INF)
    return jnp.moveaxis(b, -1, 0)


def _flash_body(lam_ref, qt_ref, k_ref, vt_ref, bias_ref, g_ref, o_ref, m_s, l_s, acc_s, *, ratio, group, lam_init):
    i = pl.program_id(2)
    sub = lax.broadcasted_iota(I32, (QK_DIM, 1), 0)
    qs = []
    for b in range(group):
        qt = qt_ref[b]
        zero = jnp.zeros_like(qt)
        qs.append((jnp.where(sub < HEAD_DIM, qt, zero), jnp.where(sub >= HEAD_DIM, qt, zero)))
    m_s[...] = jnp.full(m_s.shape, NEG_INF, F32)
    l_s[...] = jnp.zeros(l_s.shape, F32)
    acc_s[...] = jnp.zeros(acc_s.shape, F32)

    def step(j, bias):
        for b in range(group):
            k = k_ref[b, j]
            vt = vt_ref[b, j]
            for c in range(2):
                s = jnp.dot(k, qs[b][c], preferred_element_type=F32)
                if bias is not None:
                    s = s + bias
                m_old = m_s[b, c]
                m_new = jnp.maximum(m_old, jnp.max(s, 0, keepdims=True))
                alpha = jnp.exp2(m_old - m_new)
                p = jnp.exp2(s - m_new)
                l_s[b, c] = alpha * l_s[b, c] + jnp.sum(p, 0, keepdims=True)
                acc_s[b, c] = alpha * acc_s[b, c] + jnp.dot(vt, p.astype(BF16), preferred_element_type=F32)
                m_s[b, c] = m_new

    def far_step(j, carry):
        step(j, None)
        return carry

    lax.fori_loop(0, jnp.maximum(i * ratio - 1, 0), far_step, 0)

    @pl.when(i >= 1)
    def _():
        step(i * ratio - 1, bias_ref[0])

    for o in range(ratio):
        step(i * ratio + o, bias_ref[o + 1])

    for b in range(group):
        o = acc_s[b, 0] / l_s[b, 0] - lam_ref[0] * (acc_s[b, 1] / l_s[b, 1])
        o = o * lax.rsqrt(jnp.mean(o * o, 0, keepdims=True) + LN_EPS) * g_ref[...]
        o_ref[b] = (o * (1.0 - lam_init)).T.astype(BF16)


def _toeplitz(v, rows, cols):
    period = v.shape[-1]
    reps = -(-(rows * (period - 1)) // period)
    flat = jnp.tile(v, reps)[..., :rows * (period - 1)]
    return flat.reshape(v.shape[:-1] + (rows, period - 1))[..., :cols]


def _attn_tiles(seq):
    tq = min(ATTN_TQ, seq)
    tk = min(ATTN_TK, tq)
    assert tk >= MAX_DISTANCE and seq % tq == 0 and tq % tk == 0
    return tq, tk


def _prompt_attention(qt, k4, vt, rel_bias, lam, subln_g, lam_init):
    bsz, _, nq, _, tq = qt.shape
    nk, tk = k4.shape[2], k4.shape[3]
    seq, ratio, width = nq * tq, tq // tk, N_HEADS * V_DIM
    group = ATTN_GROUP if bsz % ATTN_GROUP == 0 else 1
    far = rel_bias[NUM_BUCKETS - 1].astype(F32)
    period = tq + tk
    y = jnp.arange(period, dtype=I32)
    r_minus_c = jnp.where(y < tq, y, y - period)
    dist = r_minus_c[None, :] - (jnp.arange(ratio + 1, dtype=I32)[:, None] - 1) * tk
    bias = _toeplitz(_bias_of(dist, rel_bias, far), tk, tq)
    return pl.pallas_call(
        functools.partial(_flash_body, ratio=ratio, group=group, lam_init=lam_init),
        grid=(bsz // group, N_HEADS, nq),
        in_specs=[
            pl.BlockSpec(memory_space=pltpu.SMEM),
            pl.BlockSpec((group, None, None, QK_DIM, tq), lambda b, h, i: (b, h, i, 0, 0)),
            pl.BlockSpec((group, None, nk, tk, QK_DIM), lambda b, h, i: (b, h, 0, 0, 0)),
            pl.BlockSpec((group, None, nk, V_DIM, tk), lambda b, h, i: (b, h, 0, 0, 0)),
            pl.BlockSpec((None, ratio + 1, tk, tq), lambda b, h, i: (h, 0, 0, 0)),
            pl.BlockSpec((V_DIM, 1), lambda b, h, i: (0, 0)),
        ],
        out_specs=pl.BlockSpec((group, tq, V_DIM), lambda b, h, i: (b, i, h)),
        out_shape=jax.ShapeDtypeStruct((bsz, seq, width), BF16),
        scratch_shapes=[pltpu.VMEM((group, 2, 1, tq), F32), pltpu.VMEM((group, 2, 1, tq), F32),
                        pltpu.VMEM((group, 2, V_DIM, tq), F32)],
        compiler_params=_cparams("parallel", "parallel", "arbitrary"),
        name="prompt_attention",
    )(lam, qt, k4, vt, bias, subln_g.reshape(V_DIM, 1))


def _paged_body(pt_ref, lam_ref, q_ref, kn_ref, vn_ref, bias_ref, g_ref, *refs, n_pages, t_new, lam_init):
    k_refs = refs[:n_pages]
    v_refs = refs[n_pages:2 * n_pages]
    o_ref = refs[2 * n_pages]
    lam = lam_ref[0]
    rows = 2 * t_new
    cols = PAGE_SIZE * N_HEADS
    nt = (((1,), (1,)), ((), ()))
    q = q_ref[0]
    parts = [lax.dot_general(q, k_refs[p][...].astype(BF16), nt, preferred_element_type=F32)
             for p in range(n_pages)]
    parts.append(lax.dot_general(q, kn_ref[0], nt, preferred_element_type=F32))
    s = jnp.concatenate(parts, axis=1) + bias_ref[...]
    m = jnp.max(s, -1, keepdims=True)
    p_un = jnp.exp2(s - m)
    pn = p_un / jnp.sum(p_un, -1, keepdims=True)
    w = (pn - lam * pltpu.roll(pn, N_HEADS * rows - t_new, axis=0)).astype(BF16)
    o = jnp.dot(w[:, n_pages * cols:], vn_ref[0], preferred_element_type=F32)
    for p in range(n_pages):
        o = o + jnp.dot(w[:, p * cols:(p + 1) * cols], v_refs[p][...].astype(BF16), preferred_element_type=F32)
    o = o * lax.rsqrt(jnp.mean(o * o, -1, keepdims=True) + LN_EPS) * g_ref[...]
    o = (o * (1.0 - lam_init)).astype(BF16)
    o_ref[0] = jnp.concatenate([o[h * rows:(h + 1) * rows] for h in range(N_HEADS)], axis=1)


def _sample_attention(q, kb, vb, cache_k, cache_v, page_table, rel_bias, lam, subln_g, lam_init):
    bs, t_new, width = q.shape
    n_pool = cache_k.shape[0]
    n_pages = page_table.shape[1]
    past = n_pages * PAGE_SIZE
    rows = 2 * t_new
    cols = PAGE_SIZE * N_HEADS
    new_pos = LANES // N_HEADS
    assert rows % SUBLANES == 0 and t_new <= new_pos
    qh = q.reshape(bs, t_new, N_HEADS, QK_DIM).transpose(0, 2, 1, 3)
    lane_map = (jnp.arange(QK_DIM) // HEAD_DIM)[None, :] == jnp.arange(2)[:, None]
    q_rows = (qh[:, :, None] * lane_map[None, None, :, None, :].astype(BF16)).reshape(bs, N_HEADS * rows, QK_DIM)
    pad = ((0, 0), (0, LANES - t_new * N_HEADS), (0, 0))
    kn = jnp.pad(kb.reshape(bs, t_new * N_HEADS, QK_DIM), pad)
    vn = jnp.pad(vb.reshape(bs, t_new * N_HEADS, V_DIM), pad)
    qpos = past + jnp.arange(t_new, dtype=I32)
    kpos = jnp.concatenate([jnp.arange(past + t_new, dtype=I32),
                            jnp.full((new_pos - t_new,), past + t_new + new_pos, I32)])
    b = _bias_of(qpos[:, None] - kpos[None, :], rel_bias)
    same = jnp.arange(N_HEADS)[:, None] == jnp.arange(N_HEADS)[None, :]
    b = jnp.where(same[:, None, None, :], b[..., None], NEG_INF)
    b = b.reshape(N_HEADS, 1, t_new, -1)
    bias = jnp.concatenate([b, b], axis=1).reshape(N_HEADS * rows, -1)

    page_spec = lambda j: pl.BlockSpec((None, cols, QK_DIM), lambda b, pt, j=j: (pt[b * n_pages + j], 0, 0))
    per_seq = lambda r, c: pl.BlockSpec((1, r, c), lambda b, pt: (b, 0, 0))
    grid_spec = pltpu.PrefetchScalarGridSpec(
        num_scalar_prefetch=1,
        grid=(bs,),
        in_specs=[
            pl.BlockSpec(memory_space=pltpu.SMEM),
            per_seq(N_HEADS * rows, QK_DIM),
            per_seq(LANES, QK_DIM),
            per_seq(LANES, V_DIM),
            pl.BlockSpec(bias.shape, lambda b, pt: (0, 0)),
            pl.BlockSpec((1, V_DIM), lambda b, pt: (0, 0)),
        ] + [page_spec(j) for j in range(n_pages)] * 2,
        out_specs=pl.BlockSpec((1, rows, width), lambda b, pt: (b, 0, 0)),
    )
    ck = cache_k.reshape(n_pool, cols, QK_DIM)
    cv = cache_v.reshape(n_pool, cols, V_DIM)
    return pl.pallas_call(
        functools.partial(_paged_body, n_pages=n_pages, t_new=t_new, lam_init=lam_init),
        grid_spec=grid_spec,
        out_shape=jax.ShapeDtypeStruct((bs, rows, width), BF16),
        compiler_params=_cparams("parallel"),
        name="sample_attention",
    )(page_table.reshape(-1), lam, q_rows, kn, vn, bias, subln_g, *([ck] * n_pages), *([cv] * n_pages))


def _conv_prompt_body(u_ref, halo_ref, w_ref, cb_ref, g_ref, b_ref, o_ref, buf, shifted, *, tc):
    i = pl.program_id(1)
    halo = halo_ref[0]
    buf[0:CONV_HALO, :] = jnp.where(i > 0, halo, jnp.zeros_like(halo))
    buf[CONV_HALO:, :] = u_ref[0]
    rows = CONV_HALO + tc - SUBLANES
    for s in range(1, SUBLANES):
        shifted[s - 1, 0:rows, :] = buf[s:s + rows, :]
    first = CONV_HALO - (CONV_WIDTH - 1)
    for r0 in range(0, tc, CONV_CHUNK):
        acc = jnp.zeros((CONV_CHUNK, u_ref.shape[2]), F32) + cb_ref[...]
        for j in range(CONV_WIDTH):
            s = (first + j) % SUBLANES
            a = first + j - s + r0
            win = buf[a:a + CONV_CHUNK, :] if s == 0 else shifted[s - 1, a:a + CONV_CHUNK, :]
            acc = acc + w_ref[j:j + 1, :] * win
        y = _layer_norm(acc, g_ref[...], b_ref[...])
        o_ref[0, r0:r0 + CONV_CHUNK, :] = (y * jax.nn.sigmoid(y)).astype(BF16)


def _conv_prompt(u, conv_w, conv_b, ln_g, ln_b):
    bsz, seq, ch = u.shape
    tc = min(TOKEN_TILE, seq)
    assert seq % tc == 0 and tc % CONV_CHUNK == 0 and tc % CONV_HALO == 0
    vec = pl.BlockSpec((1, ch), lambda b, i: (0, 0))
    return pl.pallas_call(
        functools.partial(_conv_prompt_body, tc=tc),
        grid=(bsz, seq // tc),
        in_specs=[
            pl.BlockSpec((1, tc, ch), lambda b, i: (b, i, 0)),
            pl.BlockSpec((1, CONV_HALO, ch), lambda b, i: (b, jnp.maximum(i * (tc // CONV_HALO) - 1, 0), 0)),
            pl.BlockSpec((CONV_WIDTH, ch), lambda b, i: (0, 0)),
            vec, vec, vec,
        ],
        out_specs=pl.BlockSpec((1, tc, ch), lambda b, i: (b, i, 0)),
        out_shape=jax.ShapeDtypeStruct((bsz, seq, ch), BF16),
        scratch_shapes=[pltpu.VMEM((CONV_HALO + tc, ch), F32), pltpu.VMEM((SUBLANES - 1, CONV_HALO + tc, ch), F32)],
        compiler_params=_cparams("parallel", "arbitrary"),
        name="conv_prompt",
    )(u, u, conv_w, conv_b, ln_g, ln_b)


def _conv_sample_body(buf_ref, w_ref, cb_ref, g_ref, b_ref, o_ref, *, t_new):
    for t in range(t_new):
        acc = jnp.zeros(buf_ref.shape[1:], F32) + cb_ref[...]
        for j in range(CONV_WIDTH):
            acc = acc + w_ref[j:j + 1, :] * buf_ref[t + j]
        y = _layer_norm(acc, g_ref[...], b_ref[...])
        o_ref[t] = (y * jax.nn.sigmoid(y)).astype(BF16)


def _conv_sample(buf_t, conv_w, conv_b, ln_g, ln_b):
    rows, bs, ch = buf_t.shape
    t_new = rows - (CONV_WIDTH - 1)
    gb = min(32, bs)
    assert bs % gb == 0
    vec = pl.BlockSpec((1, ch), lambda i: (0, 0))
    return pl.pallas_call(
        functools.partial(_conv_sample_body, t_new=t_new),
        grid=(bs // gb,),
        in_specs=[pl.BlockSpec((rows, gb, ch), lambda i: (0, i, 0)),
                  pl.BlockSpec((CONV_WIDTH, ch), lambda i: (0, 0)), vec, vec, vec],
        out_specs=pl.BlockSpec((t_new, gb, ch), lambda i: (0, i, 0)),
        out_shape=jax.ShapeDtypeStruct((t_new, bs, ch), BF16),
        compiler_params=_cparams("parallel"),
        name="conv_sample",
    )(buf_t, conv_w, conv_b, ln_g, ln_b)


N_MERGE_OUT = 5


def _merge_body(*refs, alpha, n_alias):
    (o_ref, y_ref, sg_ref, x_ref, wa_ref, wc_ref, bc_ref, wo_ref, g1_ref, b1_ref, rw_ref, rb_ref) = refs[:12]
    h1_ref, ti_ref, rk_ref, wcol_ref, cnt_ref = refs[12 + n_alias:]
    tm, d = x_ref.shape
    a = jnp.dot(o_ref[...], wa_ref[...], preferred_element_type=F32)
    b = jnp.dot(y_ref[...], wc_ref[...], preferred_element_type=F32) + bc_ref[...]
    mix_in = sg_ref[:, :d].astype(F32) * a + sg_ref[:, d:].astype(F32) * b
    mix = jnp.dot(mix_in.astype(BF16), wo_ref[...], preferred_element_type=F32)
    h1 = _layer_norm(alpha * x_ref[...] + mix, g1_ref[...], b1_ref[...])
    _store_rows(h1_ref, h1)
    logits = lax.dot_general(rw_ref[...], h1, (((1,), (1,)), ((), ())), preferred_element_type=F32,
                             precision=lax.Precision.HIGHEST) + rb_ref[...]
    n_e = logits.shape[0]
    eid = lax.broadcasted_iota(I32, logits.shape, 0)
    vals, idxs, hots = [], [], []
    for _ in range(TOP_K):
        m = jnp.max(logits, 0, keepdims=True)
        idx = jnp.min(jnp.where(logits == m, eid, n_e), 0, keepdims=True)
        hot = eid == idx
        vals.append(m)
        idxs.append(idx)
        hots.append(hot)
        logits = jnp.where(hot, -jnp.inf, logits)
    ex = [jnp.exp(v - vals[0]) for v in vals]
    den = ex[0]
    for e in ex[1:]:
        den = den + e
    ti_ref[...] = jnp.concatenate(idxs, 0)
    wrows = jnp.concatenate([e / den for e in ex] + [jnp.zeros((LANES - TOP_K, tm), F32)], 0)
    wcol_ref[...] = wrows.T
    sel = hots[0]
    for hot in hots[1:]:
        sel = sel | hot
    sel_f = jnp.where(sel, 1.0, 0.0)
    before = lax.broadcasted_iota(I32, (tm, tm), 0) < lax.broadcasted_iota(I32, (tm, tm), 1)
    upper = jnp.where(before, 1.0, 0.0).astype(BF16)
    ahead = jnp.dot(sel_f.astype(BF16), upper, preferred_element_type=F32)
    rk_ref[...] = jnp.concatenate(
        [jnp.sum(jnp.where(hot, ahead, 0.0), 0, keepdims=True) for hot in hots], 0).astype(I32)
    cnt = jnp.sum(sel_f, 1, keepdims=True).astype(I32)
    cnt_ref[0] = jnp.broadcast_to(cnt, (n_e, LANES))


def _merge(o, yact, sg, x, wts, alpha, n_total, row_off, tm, prev):
    n, d = x.shape
    wa, wc, bc, wo, g1, b1, rwt, rb = wts
    n_e = rwt.shape[0]
    off = row_off // tm
    last = n // tm - 1
    steps = n // tm if prev is not None else n_total // tm
    row = lambda width: pl.BlockSpec((tm, width), lambda i: (jnp.minimum(i, last), 0))
    full = lambda a: pl.BlockSpec(a.shape, lambda i: (0, 0))
    in_specs = [row(o.shape[1]), row(yact.shape[1]), row(2 * d), row(d),
                full(wa), full(wc), full(bc), full(wo), full(g1), full(b1), full(rwt), full(rb)]
    args = [o, yact, sg, x, wa, wc, bc, wo, g1, b1, rwt, rb]
    aliases = {}
    if prev is not None:
        in_specs += [pl.BlockSpec(memory_space=pl.ANY)] * N_MERGE_OUT
        aliases = {len(args) + j: j for j in range(N_MERGE_OUT)}
        args += list(prev)
    slot_major = pl.BlockSpec((TOP_K, tm), lambda i: (0, off + i))
    return pl.pallas_call(
        functools.partial(_merge_body, alpha=alpha, n_alias=0 if prev is None else N_MERGE_OUT),
        grid=(steps,),
        in_specs=in_specs,
        out_specs=(pl.BlockSpec((tm * SUBLANES, LANES), lambda i: (off + i, 0)), slot_major, slot_major,
                   pl.BlockSpec((tm, LANES), lambda i: (off + i, 0)),
                   pl.BlockSpec((1, n_e, LANES), lambda i: (off + i, 0, 0))),
        out_shape=(jax.ShapeDtypeStruct((n_total * SUBLANES, LANES), F32),
                   jax.ShapeDtypeStruct((TOP_K, n_total), I32),
                   jax.ShapeDtypeStruct((TOP_K, n_total), I32),
                   jax.ShapeDtypeStruct((n_total, LANES), F32),
                   jax.ShapeDtypeStruct((n_total // tm, n_e, LANES), I32)),
        input_output_aliases=aliases,
        compiler_params=_cparams("parallel"),
        name="merge_ln1_router",
    )(*args)


def _plan(topi, rank, cnt, tm_tok, tm_moe):
    k, n = topi.shape
    n_e = cnt.shape[1]
    before = jnp.cumsum(cnt, 0) - cnt
    total = jnp.sum(cnt, 0)
    ntile_e = (total + tm_moe - 1) // tm_moe
    tend = jnp.cumsum(ntile_e)
    tstart = tend - ntile_e
    base = (tstart * tm_moe)[None, :] + before
    base_tok = jnp.repeat(base, tm_tok, axis=0)
    hot = topi[:, :, None] == jnp.arange(n_e, dtype=I32)[None, None, :]
    pos = (jnp.sum(jnp.where(hot, base_tok[None], 0), -1).astype(I32) + rank) * SUBLANES
    n_tiles = (k * n + tm_moe - 1) // tm_moe + n_e
    n_used = tend[-1]
    tid = jnp.arange(n_tiles, dtype=I32)
    src = jnp.minimum(tid, n_used - 1)
    te = jnp.minimum(jnp.sum((src[:, None] >= tend[None, :]).astype(I32), 1), n_e - 1)
    first = (tid == tstart[te]).astype(I32)
    half = (total[te] - (tid - tstart[te]) * tm_moe <= tm_moe // 2).astype(I32)
    last_tile = jnp.where(ntile_e > 0, tend - 1, -1).astype(I32)
    return pos, te.astype(I32), first, half, n_used.reshape(1).astype(I32), last_tile, n_tiles


def _dispatch_body(lt_ref, nu_ref, pos_ref, x_ref, xs_hbm, zbuf, zsem, sem, *, tm, tm_moe, n_e, n_tiles):
    i = pl.program_id(0)

    def zero_copy(tile):
        start = pl.multiple_of(tile * (tm_moe * SUBLANES), tm_moe * SUBLANES)
        return pltpu.make_async_copy(zbuf, xs_hbm.at[pl.ds(start, tm_moe * SUBLANES), :], zsem)

    @pl.when(i == 0)
    def _():
        zbuf[...] = jnp.zeros(zbuf.shape, F32)

        def z_start(e, c):
            @pl.when(lt_ref[e] >= 0)
            def _():
                zero_copy(lt_ref[e]).start()
            return c

        def z_wait(e, c):
            @pl.when(lt_ref[e] >= 0)
            def _():
                zero_copy(lt_ref[e]).wait()
            return c

        def t_start(t, c):
            zero_copy(t).start()
            return c

        def t_wait(t, c):
            zero_copy(t).wait()
            return c

        lax.fori_loop(0, n_e, z_start, 0)
        lax.fori_loop(nu_ref[0], n_tiles, t_start, 0)
        lax.fori_loop(0, n_e, z_wait, 0)
        lax.fori_loop(nu_ref[0], n_tiles, t_wait, 0)

    def issue(g, c):
        base = pl.multiple_of(g * SUBLANES, SUBLANES)
        for rr in range(SUBLANES):
            src = x_ref.at[pl.ds(pl.multiple_of((base + rr) * SUBLANES, SUBLANES), SUBLANES), :]
            for k in range(TOP_K):
                dst = pl.multiple_of(pos_ref[k, base + rr], SUBLANES)
                pltpu.make_async_copy(src, xs_hbm.at[pl.ds(dst, SUBLANES), :], sem).start()
        return c

    lax.fori_loop(0, tm // SUBLANES, issue, 0)
    for k in range(TOP_K):
        pltpu.make_async_copy(x_ref, xs_hbm.at[pl.ds(0, tm * SUBLANES), :], sem).wait()


def _dispatch(h1, pos, last_tile, n_used, n_tiles, tm, tm_moe):
    n = h1.shape[0] // SUBLANES
    n_e = last_tile.shape[0]
    grid_spec = pltpu.PrefetchScalarGridSpec(
        num_scalar_prefetch=2,
        grid=(n // tm,),
        in_specs=[pl.BlockSpec((TOP_K, tm), lambda i, lt, nu: (0, i), memory_space=pltpu.SMEM),
                  pl.BlockSpec((tm * SUBLANES, LANES), lambda i, lt, nu: (i, 0))],
        out_specs=pl.BlockSpec(memory_space=pl.ANY),
        scratch_shapes=[pltpu.VMEM((tm_moe * SUBLANES, LANES), F32), pltpu.SemaphoreType.DMA,
                        pltpu.SemaphoreType.DMA],
    )
    return pl.pallas_call(
        functools.partial(_dispatch_body, tm=tm, tm_moe=tm_moe, n_e=n_e, n_tiles=n_tiles),
        grid_spec=grid_spec,
        out_shape=jax.ShapeDtypeStruct((n_tiles * tm_moe * SUBLANES, LANES), F32),
        compiler_params=_cparams("arbitrary"),
        name="moe_dispatch",
    )(last_tile, n_used, pos, h1)


def _experts_body(te_ref, first_ref, half_ref, nu_ref, x_ref, w1_ref, b1_ref, w2_ref, b2_ref, y_ref, w1b, w2b, *,
                  tm):
    i = pl.program_id(0)
    d_ff = w2_ref.shape[1]

    def run(rows):
        x = _load_rows(x_ref, rows).astype(BF16)
        hh = jnp.dot(x, w1b[...], preferred_element_type=F32) + b1_ref[0]
        g = jnp.minimum(hh[:, :d_ff], SWIGLU_LIMIT)
        u = jnp.clip(hh[:, d_ff:], -SWIGLU_LIMIT, SWIGLU_LIMIT)
        act = (u + 1.0) * g * jax.nn.sigmoid(SWIGLU_ALPHA * g)
        _store_rows(y_ref, jnp.dot(act.astype(BF16), w2b[...], preferred_element_type=F32) + b2_ref[0])

    @pl.when(i >= nu_ref[0])
    def _():
        y_ref[...] = jnp.zeros(y_ref.shape, F32)

    @pl.when(i < nu_ref[0])
    def _():
        @pl.when(first_ref[i] == 1)
        def _():
            w1b[...] = w1_ref[0].astype(BF16)
            w2b[...] = w2_ref[0].astype(BF16)

        @pl.when(half_ref[i] == 0)
        def _():
            run(tm)

        @pl.when(half_ref[i] == 1)
        def _():
            run(tm // 2)
            y_ref[pl.ds(tm // 2 * SUBLANES, tm // 2 * SUBLANES), :] = jnp.zeros((tm // 2 * SUBLANES, LANES), F32)


def _experts(xs, te, first, half, n_used, w1, b1, w2, b2, tm):
    n_e, d, f2 = w1.shape
    rows = tm * SUBLANES
    by_tile = lambda i, te, first, half, nu: (i, 0)
    by_expert = lambda i, te, first, half, nu: (te[i], 0, 0)
    grid_spec = pltpu.PrefetchScalarGridSpec(
        num_scalar_prefetch=4,
        grid=(xs.shape[0] // rows,),
        in_specs=[
            pl.BlockSpec((rows, LANES), by_tile),
            pl.BlockSpec((1, d, f2), by_expert),
            pl.BlockSpec((1, 1, f2), by_expert),
            pl.BlockSpec((1, f2 // 2, d), by_expert),
            pl.BlockSpec((1, 1, d), by_expert),
        ],
        out_specs=pl.BlockSpec((rows, LANES), by_tile),
        scratch_shapes=[pltpu.VMEM((d, f2), BF16), pltpu.VMEM((f2 // 2, d), BF16)],
    )
    return pl.pallas_call(
        functools.partial(_experts_body, tm=tm),
        grid_spec=grid_spec,
        out_shape=jax.ShapeDtypeStruct(xs.shape, F32),
        compiler_params=_cparams("arbitrary"),
        name="moe_experts",
    )(te, first, half, n_used, xs, w1, b1.reshape(n_e, 1, f2), w2, b2.reshape(n_e, 1, d))


def _combine_body(pos_ref, nxt_ref, h1_ref, wcol_ref, ys_hbm, g_ref, b_ref, o_ref, ybuf, sems, *, alpha, tm):
    i = pl.program_id(0)
    n = pl.num_programs(0)

    def fetch(p_ref, slot):
        def issue(g, c):
            base = pl.multiple_of(g * SUBLANES, SUBLANES)
            for rr in range(SUBLANES):
                dst = pl.ds(pl.multiple_of((base + rr) * SUBLANES, SUBLANES), SUBLANES)
                for k in range(TOP_K):
                    src = pl.multiple_of(p_ref[k, base + rr], SUBLANES)
                    pltpu.make_async_copy(ys_hbm.at[pl.ds(src, SUBLANES), :], ybuf.at[slot, k, dst, :],
                                          sems.at[slot]).start()
            return c
        lax.fori_loop(0, tm // SUBLANES, issue, 0)

    @pl.when(i == 0)
    def _():
        fetch(pos_ref, 0)

    @pl.when(i + 1 < n)
    def _():
        fetch(nxt_ref, (i + 1) % 2)

    slot = i % 2
    for k in range(TOP_K):
        pltpu.make_async_copy(ys_hbm.at[pl.ds(0, tm * SUBLANES), :], ybuf.at[slot, k], sems.at[slot]).wait()
    ff = wcol_ref[:, 0:1] * _load_rows(ybuf.at[slot, 0], tm)
    for k in range(1, TOP_K):
        ff = ff + wcol_ref[:, k:k + 1] * _load_rows(ybuf.at[slot, k], tm)
    o_ref[...] = _layer_norm(alpha * _load_rows(h1_ref, tm) + ff, g_ref[...], b_ref[...])


def _combine(h1, wcol, pos, ys, g, b, alpha, row_off, n):
    d = g.shape[1]
    tm = min(COMBINE_TILE, n)
    off = row_off // tm
    last = n // tm - 1
    vec = pl.BlockSpec((1, d), lambda i: (0, 0))
    return pl.pallas_call(
        functools.partial(_combine_body, alpha=alpha, tm=tm),
        grid=(n // tm,),
        in_specs=[pl.BlockSpec((TOP_K, tm), lambda i: (0, off + i), memory_space=pltpu.SMEM),
                  pl.BlockSpec((TOP_K, tm), lambda i: (0, off + jnp.minimum(i + 1, last)), memory_space=pltpu.SMEM),
                  pl.BlockSpec((tm * SUBLANES, LANES), lambda i: (off + i, 0)),
                  pl.BlockSpec((tm, LANES), lambda i: (off + i, 0)),
                  pl.BlockSpec(memory_space=pl.ANY), vec, vec],
        out_specs=pl.BlockSpec((tm, d), lambda i: (i, 0)),
        out_shape=jax.ShapeDtypeStruct((n, d), F32),
        scratch_shapes=[pltpu.VMEM((2, TOP_K, tm * SUBLANES, LANES), F32), pltpu.SemaphoreType.DMA((2,))],
        compiler_params=_cparams("arbitrary"),
        name="combine_ln2",
    )(pos, pos, h1, wcol, ys, g, b)


def _row2(v):
    return v.reshape(1, -1).astype(F32)


def kernel(x_prompt, x_sample, cache_k, cache_v, page_table, state_conv, w_in, b_in, lambda_q1, lambda_k1,
           lambda_q2, lambda_k2, subln_g, rel_bias, w_attn_proj, conv_w, conv_b, conv_ln_g, conv_ln_b,
           w_conv_proj, b_conv_proj, w_out, ln1_g, ln1_b, router_w, router_b, expert_w1, expert_b1,
           expert_w2, expert_b2, ln2_g, ln2_b):
    depth = w_in.shape[0]
    bp, seq, d = x_prompt.shape
    bs, t_new, _ = x_sample.shape
    d_att = N_HEADS * QK_DIM
    d_conv = conv_w.shape[2]
    n_p, n_s = bp * seq, bs * t_new
    n_tot = n_p + n_s
    tm = min(TOKEN_TILE, math.gcd(n_p, n_s))
    assert tm % LANES == 0
    tm_moe = min(MOE_TILE, n_tot)
    alpha = (2 * depth) ** 0.25

    hp = x_prompt.reshape(n_p, d)
    hs = x_sample.reshape(n_s, d)
    outs = [[] for _ in range(6)]
    for l in range(depth):
        lam_init = 0.8 - 0.6 * math.exp(-0.3 * l)
        lam = (jnp.exp(jnp.sum(lambda_q1[l].astype(F32) * lambda_k1[l].astype(F32)))
               - jnp.exp(jnp.sum(lambda_q2[l].astype(F32) * lambda_k2[l].astype(F32))) + lam_init).reshape(1)
        w_in_bf = w_in[l].astype(BF16)
        b_in_l = _row2(b_in[l])
        g_sub = _row2(subln_g[l])
        conv_args = (conv_w[l].astype(F32), _row2(conv_b[l]), _row2(conv_ln_g[l]), _row2(conv_ln_b[l]))
        merge_w = (w_attn_proj[l].astype(BF16), w_conv_proj[l].astype(BF16), _row2(b_conv_proj[l]),
                   w_out[l].astype(BF16), _row2(ln1_g[l]), _row2(ln1_b[l]),
                   router_w[l].astype(F32).T, router_b[l].astype(F32).reshape(-1, 1))

        qp, kp, vp, kbp, vbp, up, sgp = _in_proj(hp, w_in_bf, b_in_l, d_att, d_conv, (bp, seq) + _attn_tiles(seq))
        op = _prompt_attention(qp, kbp, vbp, rel_bias, lam, g_sub, lam_init)
        up3 = up.reshape(bp, seq, d_conv)
        yp = _conv_prompt(up3, *conv_args)
        merged = _merge(op.reshape(n_p, d_att), yp.reshape(n_p, d_conv), sgp, hp, merge_w, alpha, n_tot, 0, tm, None)

        qs, ks, vs, kbs, vbs, us, sgs = _in_proj(hs, w_in_bf, b_in_l, d_att, d_conv)
        osr = _sample_attention(qs.reshape(bs, t_new, d_att), kbs.reshape(bs, t_new, d_att),
                                vbs.reshape(bs, t_new, d_att), cache_k[l], cache_v[l], page_table,
                                rel_bias, lam, g_sub, lam_init)
        os_ = osr[:, :t_new].reshape(n_s, d_att)
        buf_s = jnp.concatenate([state_conv[l].astype(F32), us.reshape(bs, t_new, d_conv)], axis=1)
        ys = _conv_sample(buf_s.transpose(1, 0, 2), *conv_args).transpose(1, 0, 2).reshape(n_s, d_conv)
        h1, topi, rank, wcol, cnt = _merge(os_, ys, sgs, hs, merge_w, alpha, n_tot, n_p, tm, merged)

        pos, te, first, half, n_used, last_tile, n_tiles = _plan(topi, rank, cnt[:, :, 0], tm, tm_moe)
        xs = _dispatch(h1, pos, last_tile, n_used, n_tiles, tm, tm_moe)
        ysort = _experts(xs, te, first, half, n_used, expert_w1[l], expert_b1[l].astype(F32),
                         expert_w2[l], expert_b2[l].astype(F32), tm_moe)
        g2, b2 = _row2(ln2_g[l]), _row2(ln2_b[l])
        hp = _combine(h1, wcol, pos, ysort, g2, b2, alpha, 0, n_p)
        hs = _combine(h1, wcol, pos, ysort, g2, b2, alpha, n_p, n_s)

        w1 = CONV_WIDTH - 1
        cp = up3[:, seq - w1:] if seq >= w1 else jnp.concatenate(
            [jnp.zeros((bp, w1 - seq, d_conv), F32), up3], axis=1)
        for lst, val in zip(outs, (kp.reshape(bp, seq, N_HEADS, QK_DIM), vp.reshape(bp, seq, N_HEADS, V_DIM), cp,
                                   ks.reshape(bs, t_new, N_HEADS, QK_DIM), vs.reshape(bs, t_new, N_HEADS, V_DIM),
                                   buf_s[:, t_new:])):
            lst.append(val)
    return (hp.reshape(bp, seq, d), hs.reshape(bs, t_new, d)) + tuple(jnp.stack(o) for o in outs)
```

```python
import functools
import math

import jax
import jax.numpy as jnp
from jax import lax
from jax.experimental import pallas as pl
from jax.experimental.pallas import tpu as pltpu

F32 = jnp.float32
BF16 = jnp.bfloat16
I32 = jnp.int32

N_HEADS = 4
HEAD_DIM = 64
QK_DIM = 2 * HEAD_DIM
V_DIM = 2 * HEAD_DIM
ATTN_SCALE = HEAD_DIM ** -0.5
LOG2E = 1.4426950408889634
NEG_INF = -1e30
NUM_BUCKETS = 32
MAX_EXACT = NUM_BUCKETS // 2
MAX_DISTANCE = 128
CONV_WIDTH = 31
TOP_K = 4
SWIGLU_LIMIT = 7.0
SWIGLU_ALPHA = 1.702
LN_EPS = 1e-5
PAGE_SIZE = 128

LANES = 128
SUBLANES = 8
VMEM_LIMIT = 56 * 1024 * 1024

TOKEN_TILE = 512
ATTN_TQ = 1024
ATTN_TK = 512
ATTN_GROUP = 2
MOE_TILE = 512
COMBINE_TILE = 256
CONV_HALO = 32
CONV_CHUNK = 64


def _cparams(*sem):
    return pltpu.CompilerParams(dimension_semantics=sem, vmem_limit_bytes=VMEM_LIMIT)


def _load_rows(ref, tm):
    return jnp.concatenate([ref[pl.ds(c, tm, stride=SUBLANES), :] for c in range(SUBLANES)], axis=1)


def _store_rows(ref, x):
    tm = x.shape[0]
    for c in range(SUBLANES):
        ref[pl.ds(c, tm, stride=SUBLANES), :] = x[:, c * LANES:(c + 1) * LANES]


def _layer_norm(x, g, b):
    mu = jnp.mean(x, -1, keepdims=True)
    xc = x - mu
    var = jnp.mean(xc * xc, -1, keepdims=True)
    return xc * lax.rsqrt(var + LN_EPS) * g + b


def _in_proj_body(x_ref, w_ref, b_ref, q_ref, k_ref, v_ref, kb_ref, vb_ref, u_ref, sg_ref, *, d_att, d_conv, by_head):
    x = x_ref[...].astype(BF16)

    def seg(lo, hi):
        return jnp.dot(x, w_ref[:, lo:hi], preferred_element_type=F32) + b_ref[:, lo:hi]

    def head(a, h):
        return a[:, h * QK_DIM:(h + 1) * QK_DIM]

    o = 0
    q = seg(o, o + d_att) * (ATTN_SCALE * LOG2E)
    o += d_att
    k = seg(o, o + d_att)
    o += d_att
    v = seg(o, o + d_att)
    o += d_att
    tm = x.shape[0]
    for h in range(N_HEADS):
        k_ref[pl.ds(h, tm, stride=N_HEADS), :] = head(k, h)
        v_ref[pl.ds(h, tm, stride=N_HEADS), :] = head(v, h)
    if by_head:
        for h in range(N_HEADS):
            q_ref[h] = head(q, h).T.astype(BF16)
            kb_ref[h] = head(k, h).astype(BF16)
            vb_ref[h] = head(v, h).T.astype(BF16)
    else:
        q_ref[...] = q.astype(BF16)
        kb_ref[...] = k.astype(BF16)
        vb_ref[...] = v.astype(BF16)
    c = seg(o, o + 2 * d_conv)
    u_ref[...] = c[:, :d_conv] * jax.nn.sigmoid(c[:, d_conv:])
    o += 2 * d_conv
    sg_ref[...] = jax.nn.sigmoid(seg(o, w_ref.shape[1])).astype(BF16)


def _in_proj(x, w_bf, b, d_att, d_conv, attn_tiles=None):
    n, d = x.shape
    tm = min(TOKEN_TILE, n)
    row = lambda width: pl.BlockSpec((tm, width), lambda i: (i, 0))
    full = lambda a: pl.BlockSpec(a.shape, lambda i: (0, 0))
    f32_out = lambda width: jax.ShapeDtypeStruct((n, width), F32)
    kv_shape = jax.ShapeDtypeStruct((n * N_HEADS, QK_DIM), F32)
    kv_spec = pl.BlockSpec((tm * N_HEADS, QK_DIM), lambda i: (i, 0))
    if attn_tiles is None:
        qkv_shapes = [jax.ShapeDtypeStruct((n, d_att), BF16)] * 3
        qkv_specs = [row(d_att)] * 3
    else:
        bsz, seq, tq, tk = attn_tiles
        assert seq % tm == 0 and tq % tm == 0 and tk % tm == 0
        per_seq = seq // tm

        def spec(t, transposed):
            parts = t // tm
            blk = (None, N_HEADS, None, QK_DIM, tm) if transposed else (None, N_HEADS, None, tm, QK_DIM)

            def index(i):
                ti = i % per_seq
                tile, part = ti // parts, ti % parts
                return (i // per_seq, 0, tile, 0, part) if transposed else (i // per_seq, 0, tile, part, 0)
            return pl.BlockSpec(blk, index)

        qkv_shapes = [jax.ShapeDtypeStruct((bsz, N_HEADS, seq // tq, QK_DIM, tq), BF16),
                      jax.ShapeDtypeStruct((bsz, N_HEADS, seq // tk, tk, QK_DIM), BF16),
                      jax.ShapeDtypeStruct((bsz, N_HEADS, seq // tk, V_DIM, tk), BF16)]
        qkv_specs = [spec(tq, True), spec(tk, False), spec(tk, True)]
    out_shape = (qkv_shapes[0], kv_shape, kv_shape, qkv_shapes[1], qkv_shapes[2],
                 f32_out(d_conv), jax.ShapeDtypeStruct((n, 2 * d), BF16))
    return pl.pallas_call(
        functools.partial(_in_proj_body, d_att=d_att, d_conv=d_conv, by_head=attn_tiles is not None),
        grid=(n // tm,),
        in_specs=[row(d), full(w_bf), full(b)],
        out_specs=(qkv_specs[0], kv_spec, kv_spec, qkv_specs[1], qkv_specs[2], row(d_conv), row(2 * d)),
        out_shape=out_shape,
        compiler_params=_cparams("parallel"),
        name="in_proj",
    )(x, w_bf, b)


def _t5_bucket(dist):
    n = jnp.maximum(dist, 0)
    nf = jnp.maximum(n, 1).astype(F32)
    large = MAX_EXACT + (jnp.log(nf / MAX_EXACT) / math.log(MAX_DISTANCE / MAX_EXACT)
                         * (NUM_BUCKETS - MAX_EXACT)).astype(I32)
    large = jnp.minimum(large, NUM_BUCKETS - 1)
    return jnp.where(n < MAX_EXACT, n, large)


def _bias_of(dist, rel_bias, shift=None):
    b = rel_bias[_t5_bucket(dist)].astype(F32)
    if shift is not None:
        b = b - shift
    b = jnp.where((dist >= 0)[..., None], b * LOG2E, NEG_INF)
    return jnp.moveaxis(b, -1, 0)


def _flash_body(lam_ref, qt_ref, k_ref, vt_ref, btab_ref, g_ref, o_ref, m_s, l_s, acc_s, bias_ref, *, ratio, group,
                lam_init):
    i = pl.program_id(2)
    tk, tq = bias_ref.shape[1:]

    @pl.when(i == 0)
    def _():
        for o in range(ratio + 1):
            table = jnp.broadcast_to(btab_ref[o], (tk, btab_ref.shape[2]))
            bias_ref[o] = pltpu.roll(table, 0, 1, stride=1, stride_axis=0)[:, :tq]

    sub = lax.broadcasted_iota(I32, (QK_DIM, 1), 0)
    qs = []
    for b in range(group):
        qt = qt_ref[b]
        zero = jnp.zeros_like(qt)
        qs.append((jnp.where(sub < HEAD_DIM, qt, zero), jnp.where(sub >= HEAD_DIM, qt, zero)))
    m_s[...] = jnp.full(m_s.shape, NEG_INF, F32)
    l_s[...] = jnp.zeros(l_s.shape, F32)
    acc_s[...] = jnp.zeros(acc_s.shape, F32)

    def step(j, bias):
        for b in range(group):
            k = k_ref[b, j]
            vt = vt_ref[b, j]
            for c in range(2):
                s = jnp.dot(k, qs[b][c], preferred_element_type=F32)
                if bias is not None:
                    s = s + bias
                m_old = m_s[b, c]
                m_new = jnp.maximum(m_old, jnp.max(s, 0, keepdims=True))
                alpha = jnp.exp2(m_old - m_new)
                p = jnp.exp2(s - m_new)
                l_s[b, c] = alpha * l_s[b, c] + jnp.sum(p, 0, keepdims=True)
                acc_s[b, c] = alpha * acc_s[b, c] + jnp.dot(vt, p.astype(BF16), preferred_element_type=F32)
                m_s[b, c] = m_new

    def far_step(j, carry):
        step(j, None)
        return carry

    lax.fori_loop(0, jnp.maximum(i * ratio - 1, 0), far_step, 0)

    @pl.when(i >= 1)
    def _():
        step(i * ratio - 1, bias_ref[0])

    for o in range(ratio):
        step(i * ratio + o, bias_ref[o + 1])

    for b in range(group):
        o = acc_s[b, 0] / l_s[b, 0] - lam_ref[0] * (acc_s[b, 1] / l_s[b, 1])
        o = o * lax.rsqrt(jnp.mean(o * o, 0, keepdims=True) + LN_EPS) * g_ref[...]
        o_ref[b] = (o * (1.0 - lam_init)).T.astype(BF16)


def _attn_tiles(seq):
    tq = min(ATTN_TQ, seq)
    tk = min(ATTN_TK, tq)
    assert tk >= MAX_DISTANCE and seq % tq == 0 and tq % tk == 0
    return tq, tk


def _prompt_attention(qt, k4, vt, rel_bias, lam, subln_g, lam_init):
    bsz, _, nq, _, tq = qt.shape
    nk, tk = k4.shape[2], k4.shape[3]
    seq, ratio, width = nq * tq, tq // tk, N_HEADS * V_DIM
    group = ATTN_GROUP if bsz % ATTN_GROUP == 0 else 1
    far = rel_bias[NUM_BUCKETS - 1].astype(F32)
    period = tq + tk
    assert period % LANES == 0
    y = jnp.arange(period, dtype=I32)
    r_minus_c = jnp.where(y < tq, y, y - period)
    dist = r_minus_c[None, :] - (jnp.arange(ratio + 1, dtype=I32)[:, None] - 1) * tk
    btab = _bias_of(dist, rel_bias, far).reshape(N_HEADS, ratio + 1, 1, period)
    return pl.pallas_call(
        functools.partial(_flash_body, ratio=ratio, group=group, lam_init=lam_init),
        grid=(bsz // group, N_HEADS, nq),
        in_specs=[
            pl.BlockSpec(memory_space=pltpu.SMEM),
            pl.BlockSpec((group, None, None, QK_DIM, tq), lambda b, h, i: (b, h, i, 0, 0)),
            pl.BlockSpec((group, None, nk, tk, QK_DIM), lambda b, h, i: (b, h, 0, 0, 0)),
            pl.BlockSpec((group, None, nk, V_DIM, tk), lambda b, h, i: (b, h, 0, 0, 0)),
            pl.BlockSpec((None, ratio + 1, 1, period), lambda b, h, i: (h, 0, 0, 0)),
            pl.BlockSpec((V_DIM, 1), lambda b, h, i: (0, 0)),
        ],
        out_specs=pl.BlockSpec((group, tq, V_DIM), lambda b, h, i: (b, i, h)),
        out_shape=jax.ShapeDtypeStruct((bsz, seq, width), BF16),
        scratch_shapes=[pltpu.VMEM((group, 2, 1, tq), F32), pltpu.VMEM((group, 2, 1, tq), F32),
                        pltpu.VMEM((group, 2, V_DIM, tq), F32), pltpu.VMEM((ratio + 1, tk, tq), F32)],
        compiler_params=_cparams("parallel", "parallel", "arbitrary"),
        name="prompt_attention",
    )(lam, qt, k4, vt, btab, subln_g.reshape(V_DIM, 1))


def _paged_body(pt_ref, lam_ref, q_ref, kn_ref, vn_ref, bias_ref, g_ref, *refs, n_pages, t_new, lam_init):
    k_refs = refs[:n_pages]
    v_refs = refs[n_pages:2 * n_pages]
    o_ref = refs[2 * n_pages]
    lam = lam_ref[0]
    rows = 2 * t_new
    cols = PAGE_SIZE * N_HEADS
    nt = (((1,), (1,)), ((), ()))
    q = q_ref[0]
    parts = [lax.dot_general(q, k_refs[p][...].astype(BF16), nt, preferred_element_type=F32)
             for p in range(n_pages)]
    parts.append(lax.dot_general(q, kn_ref[0], nt, preferred_element_type=F32))
    s = jnp.concatenate(parts, axis=1) + bias_ref[...]
    m = jnp.max(s, -1, keepdims=True)
    p_un = jnp.exp2(s - m)
    pn = p_un / jnp.sum(p_un, -1, keepdims=True)
    w = (pn - lam * pltpu.roll(pn, N_HEADS * rows - t_new, axis=0)).astype(BF16)
    o = jnp.dot(w[:, n_pages * cols:], vn_ref[0], preferred_element_type=F32)
    for p in range(n_pages):
        o = o + jnp.dot(w[:, p * cols:(p + 1) * cols], v_refs[p][...].astype(BF16), preferred_element_type=F32)
    o = o * lax.rsqrt(jnp.mean(o * o, -1, keepdims=True) + LN_EPS) * g_ref[...]
    o = (o * (1.0 - lam_init)).astype(BF16)
    o_ref[0] = jnp.concatenate([o[h * rows:(h + 1) * rows] for h in range(N_HEADS)], axis=1)


def _sample_attention(q, kb, vb, cache_k, cache_v, page_table, rel_bias, lam, subln_g, lam_init):
    bs, t_new, width = q.shape
    n_pool = cache_k.shape[0]
    n_pages = page_table.shape[1]
    past = n_pages * PAGE_SIZE
    rows = 2 * t_new
    cols = PAGE_SIZE * N_HEADS
    new_pos = LANES // N_HEADS
    assert rows % SUBLANES == 0 and t_new <= new_pos
    qh = q.reshape(bs, t_new, N_HEADS, QK_DIM).transpose(0, 2, 1, 3)
    lane_map = (jnp.arange(QK_DIM) // HEAD_DIM)[None, :] == jnp.arange(2)[:, None]
    q_rows = (qh[:, :, None] * lane_map[None, None, :, None, :].astype(BF16)).reshape(bs, N_HEADS * rows, QK_DIM)
    pad = ((0, 0), (0, LANES - t_new * N_HEADS), (0, 0))
    kn = jnp.pad(kb.reshape(bs, t_new * N_HEADS, QK_DIM), pad)
    vn = jnp.pad(vb.reshape(bs, t_new * N_HEADS, V_DIM), pad)
    qpos = past + jnp.arange(t_new, dtype=I32)
    kpos = jnp.concatenate([jnp.arange(past + t_new, dtype=I32),
                            jnp.full((new_pos - t_new,), past + t_new + new_pos, I32)])
    b = _bias_of(qpos[:, None] - kpos[None, :], rel_bias)
    same = jnp.arange(N_HEADS)[:, None] == jnp.arange(N_HEADS)[None, :]
    b = jnp.where(same[:, None, None, :], b[..., None], NEG_INF)
    b = b.reshape(N_HEADS, 1, t_new, -1)
    bias = jnp.concatenate([b, b], axis=1).reshape(N_HEADS * rows, -1)

    page_spec = lambda j: pl.BlockSpec((None, cols, QK_DIM), lambda b, pt, j=j: (pt[b * n_pages + j], 0, 0))
    per_seq = lambda r, c: pl.BlockSpec((1, r, c), lambda b, pt: (b, 0, 0))
    grid_spec = pltpu.PrefetchScalarGridSpec(
        num_scalar_prefetch=1,
        grid=(bs,),
        in_specs=[
            pl.BlockSpec(memory_space=pltpu.SMEM),
            per_seq(N_HEADS * rows, QK_DIM),
            per_seq(LANES, QK_DIM),
            per_seq(LANES, V_DIM),
            pl.BlockSpec(bias.shape, lambda b, pt: (0, 0)),
            pl.BlockSpec((1, V_DIM), lambda b, pt: (0, 0)),
        ] + [page_spec(j) for j in range(n_pages)] * 2,
        out_specs=pl.BlockSpec((1, rows, width), lambda b, pt: (b, 0, 0)),
    )
    ck = cache_k.reshape(n_pool, cols, QK_DIM)
    cv = cache_v.reshape(n_pool, cols, V_DIM)
    return pl.pallas_call(
        functools.partial(_paged_body, n_pages=n_pages, t_new=t_new, lam_init=lam_init),
        grid_spec=grid_spec,
        out_shape=jax.ShapeDtypeStruct((bs, rows, width), BF16),
        compiler_params=_cparams("parallel"),
        name="sample_attention",
    )(page_table.reshape(-1), lam, q_rows, kn, vn, bias, subln_g, *([ck] * n_pages), *([cv] * n_pages))


def _conv_prompt_body(u_ref, halo_ref, w_ref, cb_ref, g_ref, b_ref, o_ref, buf, shifted, *, tc):
    i = pl.program_id(1)
    halo = halo_ref[0]
    buf[0:CONV_HALO, :] = jnp.where(i > 0, halo, jnp.zeros_like(halo))
    buf[CONV_HALO:, :] = u_ref[0]
    rows = CONV_HALO + tc - SUBLANES
    for s in range(1, SUBLANES):
        shifted[s - 1, 0:rows, :] = buf[s:s + rows, :]
    first = CONV_HALO - (CONV_WIDTH - 1)
    for r0 in range(0, tc, CONV_CHUNK):
        acc = jnp.zeros((CONV_CHUNK, u_ref.shape[2]), F32) + cb_ref[...]
        for j in range(CONV_WIDTH):
            s = (first + j) % SUBLANES
            a = first + j - s + r0
            win = buf[a:a + CONV_CHUNK, :] if s == 0 else shifted[s - 1, a:a + CONV_CHUNK, :]
            acc = acc + w_ref[j:j + 1, :] * win
        y = _layer_norm(acc, g_ref[...], b_ref[...])
        o_ref[0, r0:r0 + CONV_CHUNK, :] = (y * jax.nn.sigmoid(y)).astype(BF16)


def _conv_prompt(u, conv_w, conv_b, ln_g, ln_b):
    bsz, seq, ch = u.shape
    tc = min(TOKEN_TILE, seq)
    assert seq % tc == 0 and tc % CONV_CHUNK == 0 and tc % CONV_HALO == 0
    vec = pl.BlockSpec((1, ch), lambda b, i: (0, 0))
    return pl.pallas_call(
        functools.partial(_conv_prompt_body, tc=tc),
        grid=(bsz, seq // tc),
        in_specs=[
            pl.BlockSpec((1, tc, ch), lambda b, i: (b, i, 0)),
            pl.BlockSpec((1, CONV_HALO, ch), lambda b, i: (b, jnp.maximum(i * (tc // CONV_HALO) - 1, 0), 0)),
            pl.BlockSpec((CONV_WIDTH, ch), lambda b, i: (0, 0)),
            vec, vec, vec,
        ],
        out_specs=pl.BlockSpec((1, tc, ch), lambda b, i: (b, i, 0)),
        out_shape=jax.ShapeDtypeStruct((bsz, seq, ch), BF16),
        scratch_shapes=[pltpu.VMEM((CONV_HALO + tc, ch), F32), pltpu.VMEM((SUBLANES - 1, CONV_HALO + tc, ch), F32)],
        compiler_params=_cparams("parallel", "arbitrary"),
        name="conv_prompt",
    )(u, u, conv_w, conv_b, ln_g, ln_b)


def _conv_sample_body(buf_ref, w_ref, cb_ref, g_ref, b_ref, o_ref, *, t_new):
    for t in range(t_new):
        acc = jnp.zeros(buf_ref.shape[1:], F32) + cb_ref[...]
        for j in range(CONV_WIDTH):
            acc = acc + w_ref[j:j + 1, :] * buf_ref[t + j]
        y = _layer_norm(acc, g_ref[...], b_ref[...])
        o_ref[t] = (y * jax.nn.sigmoid(y)).astype(BF16)


def _conv_sample(buf_t, conv_w, conv_b, ln_g, ln_b):
    rows, bs, ch = buf_t.shape
    t_new = rows - (CONV_WIDTH - 1)
    gb = min(32, bs)
    assert bs % gb == 0
    vec = pl.BlockSpec((1, ch), lambda i: (0, 0))
    return pl.pallas_call(
        functools.partial(_conv_sample_body, t_new=t_new),
        grid=(bs // gb,),
        in_specs=[pl.BlockSpec((rows, gb, ch), lambda i: (0, i, 0)),
                  pl.BlockSpec((CONV_WIDTH, ch), lambda i: (0, 0)), vec, vec, vec],
        out_specs=pl.BlockSpec((t_new, gb, ch), lambda i: (0, i, 0)),
        out_shape=jax.ShapeDtypeStruct((t_new, bs, ch), BF16),
        compiler_params=_cparams("parallel"),
        name="conv_sample",
    )(buf_t, conv_w, conv_b, ln_g, ln_b)


N_MERGE_OUT = 5


def _merge_body(*refs, alpha, n_alias):
    (o_ref, y_ref, sg_ref, x_ref, wa_ref, wc_ref, bc_ref, wo_ref, g1_ref, b1_ref, rw_ref, rb_ref) = refs[:12]
    h1_ref, ti_ref, rk_ref, wcol_ref, cnt_ref = refs[12 + n_alias:]
    tm, d = x_ref.shape
    a = jnp.dot(o_ref[...], wa_ref[...], preferred_element_type=F32)
    b = jnp.dot(y_ref[...], wc_ref[...], preferred_element_type=F32) + bc_ref[...]
    mix_in = sg_ref[:, :d].astype(F32) * a + sg_ref[:, d:].astype(F32) * b
    mix = jnp.dot(mix_in.astype(BF16), wo_ref[...], preferred_element_type=F32)
    h1 = _layer_norm(alpha * x_ref[...] + mix, g1_ref[...], b1_ref[...])
    _store_rows(h1_ref, h1)
    logits = lax.dot_general(rw_ref[...], h1, (((1,), (1,)), ((), ())), preferred_element_type=F32,
                             precision=lax.Precision.HIGHEST) + rb_ref[...]
    n_e = logits.shape[0]
    eid = lax.broadcasted_iota(I32, logits.shape, 0)
    vals, idxs, hots = [], [], []
    for _ in range(TOP_K):
        m = jnp.max(logits, 0, keepdims=True)
        idx = jnp.min(jnp.where(logits == m, eid, n_e), 0, keepdims=True)
        hot = eid == idx
        vals.append(m)
        idxs.append(idx)
        hots.append(hot)
        logits = jnp.where(hot, -jnp.inf, logits)
    ex = [jnp.exp(v - vals[0]) for v in vals]
    den = ex[0]
    for e in ex[1:]:
        den = den + e
    ti_ref[...] = jnp.concatenate(idxs, 0)
    wrows = jnp.concatenate([e / den for e in ex] + [jnp.zeros((LANES - TOP_K, tm), F32)], 0)
    wcol_ref[...] = wrows.T
    sel = hots[0]
    for hot in hots[1:]:
        sel = sel | hot
    sel_f = jnp.where(sel, 1.0, 0.0)
    before = lax.broadcasted_iota(I32, (tm, tm), 0) < lax.broadcasted_iota(I32, (tm, tm), 1)
    upper = jnp.where(before, 1.0, 0.0).astype(BF16)
    ahead = jnp.dot(sel_f.astype(BF16), upper, preferred_element_type=F32)
    rk_ref[...] = jnp.concatenate(
        [jnp.sum(jnp.where(hot, ahead, 0.0), 0, keepdims=True) for hot in hots], 0).astype(I32)
    cnt = jnp.sum(sel_f, 1, keepdims=True).astype(I32)
    cnt_ref[0] = jnp.broadcast_to(cnt, (n_e, LANES))


def _merge(o, yact, sg, x, wts, alpha, n_total, row_off, tm, prev):
    n, d = x.shape
    wa, wc, bc, wo, g1, b1, rwt, rb = wts
    n_e = rwt.shape[0]
    off = row_off // tm
    last = n // tm - 1
    steps = n // tm if prev is not None else n_total // tm
    row = lambda width: pl.BlockSpec((tm, width), lambda i: (jnp.minimum(i, last), 0))
    full = lambda a: pl.BlockSpec(a.shape, lambda i: (0, 0))
    in_specs = [row(o.shape[1]), row(yact.shape[1]), row(2 * d), row(d),
                full(wa), full(wc), full(bc), full(wo), full(g1), full(b1), full(rwt), full(rb)]
    args = [o, yact, sg, x, wa, wc, bc, wo, g1, b1, rwt, rb]
    aliases = {}
    if prev is not None:
        in_specs += [pl.BlockSpec(memory_space=pl.ANY)] * N_MERGE_OUT
        aliases = {len(args) + j: j for j in range(N_MERGE_OUT)}
        args += list(prev)
    slot_major = pl.BlockSpec((TOP_K, tm), lambda i: (0, off + i))
    return pl.pallas_call(
        functools.partial(_merge_body, alpha=alpha, n_alias=0 if prev is None else N_MERGE_OUT),
        grid=(steps,),
        in_specs=in_specs,
        out_specs=(pl.BlockSpec((tm * SUBLANES, LANES), lambda i: (off + i, 0)), slot_major, slot_major,
                   pl.BlockSpec((tm, LANES), lambda i: (off + i, 0)),
                   pl.BlockSpec((1, n_e, LANES), lambda i: (off + i, 0, 0))),
        out_shape=(jax.ShapeDtypeStruct((n_total * SUBLANES, LANES), F32),
                   jax.ShapeDtypeStruct((TOP_K, n_total), I32),
                   jax.ShapeDtypeStruct((TOP_K, n_total), I32),
                   jax.ShapeDtypeStruct((n_total, LANES), F32),
                   jax.ShapeDtypeStruct((n_total // tm, n_e, LANES), I32)),
        input_output_aliases=aliases,
        compiler_params=_cparams("parallel"),
        name="merge_ln1_router",
    )(*args)


def _plan(topi, rank, cnt, tm_tok, tm_moe):
    k, n = topi.shape
    n_e = cnt.shape[1]
    before = jnp.cumsum(cnt, 0) - cnt
    total = jnp.sum(cnt, 0)
    ntile_e = (total + tm_moe - 1) // tm_moe
    tend = jnp.cumsum(ntile_e)
    tstart = tend - ntile_e
    base = (tstart * tm_moe)[None, :] + before
    base_tok = jnp.repeat(base, tm_tok, axis=0)
    hot = topi[:, :, None] == jnp.arange(n_e, dtype=I32)[None, None, :]
    pos = (jnp.sum(jnp.where(hot, base_tok[None], 0), -1).astype(I32) + rank) * SUBLANES
    n_tiles = (k * n + tm_moe - 1) // tm_moe + n_e
    n_used = tend[-1]
    tid = jnp.arange(n_tiles, dtype=I32)
    src = jnp.minimum(tid, n_used - 1)
    te = jnp.minimum(jnp.sum((src[:, None] >= tend[None, :]).astype(I32), 1), n_e - 1)
    first = (tid == tstart[te]).astype(I32)
    half = (total[te] - (tid - tstart[te]) * tm_moe <= tm_moe // 2).astype(I32)
    last_tile = jnp.where(ntile_e > 0, tend - 1, -1).astype(I32)
    return pos, te.astype(I32), first, half, n_used.reshape(1).astype(I32), last_tile, n_tiles


def _dispatch_body(lt_ref, nu_ref, pos_ref, x_ref, xs_hbm, zbuf, zsem, sem, *, tm, tm_moe, n_e, n_tiles):
    i = pl.program_id(0)

    def zero_copy(tile):
        start = pl.multiple_of(tile * (tm_moe * SUBLANES), tm_moe * SUBLANES)
        return pltpu.make_async_copy(zbuf, xs_hbm.at[pl.ds(start, tm_moe * SUBLANES), :], zsem)

    @pl.when(i == 0)
    def _():
        zbuf[...] = jnp.zeros(zbuf.shape, F32)

        def z_start(e, c):
            @pl.when(lt_ref[e] >= 0)
            def _():
                zero_copy(lt_ref[e]).start()
            return c

        def z_wait(e, c):
            @pl.when(lt_ref[e] >= 0)
            def _():
                zero_copy(lt_ref[e]).wait()
            return c

        def t_start(t, c):
            zero_copy(t).start()
            return c

        def t_wait(t, c):
            zero_copy(t).wait()
            return c

        lax.fori_loop(0, n_e, z_start, 0)
        lax.fori_loop(nu_ref[0], n_tiles, t_start, 0)
        lax.fori_loop(0, n_e, z_wait, 0)
        lax.fori_loop(nu_ref[0], n_tiles, t_wait, 0)

    def issue(g, c):
        base = pl.multiple_of(g * SUBLANES, SUBLANES)
        for rr in range(SUBLANES):
            src = x_ref.at[pl.ds(pl.multiple_of((base + rr) * SUBLANES, SUBLANES), SUBLANES), :]
            for k in range(TOP_K):
                dst = pl.multiple_of(pos_ref[k, base + rr], SUBLANES)
                pltpu.make_async_copy(src, xs_hbm.at[pl.ds(dst, SUBLANES), :], sem).start()
        return c

    lax.fori_loop(0, tm // SUBLANES, issue, 0)
    for k in range(TOP_K):
        pltpu.make_async_copy(x_ref, xs_hbm.at[pl.ds(0, tm * SUBLANES), :], sem).wait()


def _dispatch(h1, pos, last_tile, n_used, n_tiles, tm, tm_moe):
    n = h1.shape[0] // SUBLANES
    n_e = last_tile.shape[0]
    grid_spec = pltpu.PrefetchScalarGridSpec(
        num_scalar_prefetch=2,
        grid=(n // tm,),
        in_specs=[pl.BlockSpec((TOP_K, tm), lambda i, lt, nu: (0, i), memory_space=pltpu.SMEM),
                  pl.BlockSpec((tm * SUBLANES, LANES), lambda i, lt, nu: (i, 0))],
        out_specs=pl.BlockSpec(memory_space=pl.ANY),
        scratch_shapes=[pltpu.VMEM((tm_moe * SUBLANES, LANES), F32), pltpu.SemaphoreType.DMA,
                        pltpu.SemaphoreType.DMA],
    )
    return pl.pallas_call(
        functools.partial(_dispatch_body, tm=tm, tm_moe=tm_moe, n_e=n_e, n_tiles=n_tiles),
        grid_spec=grid_spec,
        out_shape=jax.ShapeDtypeStruct((n_tiles * tm_moe * SUBLANES, LANES), F32),
        compiler_params=_cparams("arbitrary"),
        name="moe_dispatch",
    )(last_tile, n_used, pos, h1)


def _experts_body(te_ref, first_ref, half_ref, nu_ref, x_ref, w1_ref, b1_ref, w2_ref, b2_ref, y_ref, w1b, w2b, *,
                  tm):
    i = pl.program_id(0)
    d_ff = w2_ref.shape[1]

    def run(rows):
        x = _load_rows(x_ref, rows).astype(BF16)
        hh = jnp.dot(x, w1b[...], preferred_element_type=F32) + b1_ref[0]
        g = jnp.minimum(hh[:, :d_ff], SWIGLU_LIMIT)
        u = jnp.clip(hh[:, d_ff:], -SWIGLU_LIMIT, SWIGLU_LIMIT)
        act = (u + 1.0) * g * jax.nn.sigmoid(SWIGLU_ALPHA * g)
        _store_rows(y_ref, jnp.dot(act.astype(BF16), w2b[...], preferred_element_type=F32) + b2_ref[0])

    @pl.when(i >= nu_ref[0])
    def _():
        y_ref[...] = jnp.zeros(y_ref.shape, F32)

    @pl.when(i < nu_ref[0])
    def _():
        @pl.when(first_ref[i] == 1)
        def _():
            w1b[...] = w1_ref[0].astype(BF16)
            w2b[...] = w2_ref[0].astype(BF16)

        @pl.when(half_ref[i] == 0)
        def _():
            run(tm)

        @pl.when(half_ref[i] == 1)
        def _():
            run(tm // 2)
            y_ref[pl.ds(tm // 2 * SUBLANES, tm // 2 * SUBLANES), :] = jnp.zeros((tm // 2 * SUBLANES, LANES), F32)


def _experts(xs, te, first, half, n_used, w1, b1, w2, b2, tm):
    n_e, d, f2 = w1.shape
    rows = tm * SUBLANES
    by_tile = lambda i, te, first, half, nu: (i, 0)
    by_expert = lambda i, te, first, half, nu: (te[i], 0, 0)
    grid_spec = pltpu.PrefetchScalarGridSpec(
        num_scalar_prefetch=4,
        grid=(xs.shape[0] // rows,),
        in_specs=[
            pl.BlockSpec((rows, LANES), by_tile),
            pl.BlockSpec((1, d, f2), by_expert),
            pl.BlockSpec((1, 1, f2), by_expert),
            pl.BlockSpec((1, f2 // 2, d), by_expert),
            pl.BlockSpec((1, 1, d), by_expert),
        ],
        out_specs=pl.BlockSpec((rows, LANES), by_tile),
        scratch_shapes=[pltpu.VMEM((d, f2), BF16), pltpu.VMEM((f2 // 2, d), BF16)],
    )
    return pl.pallas_call(
        functools.partial(_experts_body, tm=tm),
        grid_spec=grid_spec,
        out_shape=jax.ShapeDtypeStruct(xs.shape, F32),
        compiler_params=_cparams("arbitrary"),
        name="moe_experts",
    )(te, first, half, n_used, xs, w1, b1.reshape(n_e, 1, f2), w2, b2.reshape(n_e, 1, d))


def _combine_body(pos_ref, nxt_ref, h1_ref, wcol_ref, ys_hbm, g_ref, b_ref, o_ref, ybuf, sems, *, alpha, tm):
    i = pl.program_id(0)
    n = pl.num_programs(0)

    def fetch(p_ref, slot):
        def issue(g, c):
            base = pl.multiple_of(g * SUBLANES, SUBLANES)
            for rr in range(SUBLANES):
                dst = pl.ds(pl.multiple_of((base + rr) * SUBLANES, SUBLANES), SUBLANES)
                for k in range(TOP_K):
                    src = pl.multiple_of(p_ref[k, base + rr], SUBLANES)
                    pltpu.make_async_copy(ys_hbm.at[pl.ds(src, SUBLANES), :], ybuf.at[slot, k, dst, :],
                                          sems.at[slot]).start()
            return c
        lax.fori_loop(0, tm // SUBLANES, issue, 0)

    @pl.when(i == 0)
    def _():
        fetch(pos_ref, 0)

    @pl.when(i + 1 < n)
    def _():
        fetch(nxt_ref, (i + 1) % 2)

    slot = i % 2
    for k in range(TOP_K):
        pltpu.make_async_copy(ys_hbm.at[pl.ds(0, tm * SUBLANES), :], ybuf.at[slot, k], sems.at[slot]).wait()
    ff = wcol_ref[:, 0:1] * _load_rows(ybuf.at[slot, 0], tm)
    for k in range(1, TOP_K):
        ff = ff + wcol_ref[:, k:k + 1] * _load_rows(ybuf.at[slot, k], tm)
    o_ref[...] = _layer_norm(alpha * _load_rows(h1_ref, tm) + ff, g_ref[...], b_ref[...])


def _combine(h1, wcol, pos, ys, g, b, alpha, row_off, n):
    d = g.shape[1]
    tm = min(COMBINE_TILE, n)
    off = row_off // tm
    last = n // tm - 1
    vec = pl.BlockSpec((1, d), lambda i: (0, 0))
    return pl.pallas_call(
        functools.partial(_combine_body, alpha=alpha, tm=tm),
        grid=(n // tm,),
        in_specs=[pl.BlockSpec((TOP_K, tm), lambda i: (0, off + i), memory_space=pltpu.SMEM),
                  pl.BlockSpec((TOP_K, tm), lambda i: (0, off + jnp.minimum(i + 1, last)), memory_space=pltpu.SMEM),
                  pl.BlockSpec((tm * SUBLANES, LANES), lambda i: (off + i, 0)),
                  pl.BlockSpec((tm, LANES), lambda i: (off + i, 0)),
                  pl.BlockSpec(memory_space=pl.ANY), vec, vec],
        out_specs=pl.BlockSpec((tm, d), lambda i: (i, 0)),
        out_shape=jax.ShapeDtypeStruct((n, d), F32),
        scratch_shapes=[pltpu.VMEM((2, TOP_K, tm * SUBLANES, LANES), F32), pltpu.SemaphoreType.DMA((2,))],
        compiler_params=_cparams("arbitrary"),
        name="combine_ln2",
    )(pos, pos, h1, wcol, ys, g, b)


def _row2(v):
    return v.reshape(1, -1).astype(F32)


def kernel(x_prompt, x_sample, cache_k, cache_v, page_table, state_conv, w_in, b_in, lambda_q1, lambda_k1,
           lambda_q2, lambda_k2, subln_g, rel_bias, w_attn_proj, conv_w, conv_b, conv_ln_g, conv_ln_b,
           w_conv_proj, b_conv_proj, w_out, ln1_g, ln1_b, router_w, router_b, expert_w1, expert_b1,
           expert_w2, expert_b2, ln2_g, ln2_b):
    depth = w_in.shape[0]
    bp, seq, d = x_prompt.shape
    bs, t_new, _ = x_sample.shape
    d_att = N_HEADS * QK_DIM
    d_conv = conv_w.shape[2]
    n_p, n_s = bp * seq, bs * t_new
    n_tot = n_p + n_s
    tm = min(TOKEN_TILE, math.gcd(n_p, n_s))
    assert tm % LANES == 0
    tm_moe = min(MOE_TILE, n_tot)
    alpha = (2 * depth) ** 0.25

    hp = x_prompt.reshape(n_p, d)
    hs = x_sample.reshape(n_s, d)
    outs = [[] for _ in range(6)]
    for l in range(depth):
        lam_init = 0.8 - 0.6 * math.exp(-0.3 * l)
        lam = (jnp.exp(jnp.sum(lambda_q1[l].astype(F32) * lambda_k1[l].astype(F32)))
               - jnp.exp(jnp.sum(lambda_q2[l].astype(F32) * lambda_k2[l].astype(F32))) + lam_init).reshape(1)
        w_in_bf = w_in[l].astype(BF16)
        b_in_l = _row2(b_in[l])
        g_sub = _row2(subln_g[l])
        conv_args = (conv_w[l].astype(F32), _row2(conv_b[l]), _row2(conv_ln_g[l]), _row2(conv_ln_b[l]))
        merge_w = (w_attn_proj[l].astype(BF16), w_conv_proj[l].astype(BF16), _row2(b_conv_proj[l]),
                   w_out[l].astype(BF16), _row2(ln1_g[l]), _row2(ln1_b[l]),
                   router_w[l].astype(F32).T, router_b[l].astype(F32).reshape(-1, 1))

        qp, kp, vp, kbp, vbp, up, sgp = _in_proj(hp, w_in_bf, b_in_l, d_att, d_conv, (bp, seq) + _attn_tiles(seq))
        op = _prompt_attention(qp, kbp, vbp, rel_bias, lam, g_sub, lam_init)
        up3 = up.reshape(bp, seq, d_conv)
        yp = _conv_prompt(up3, *conv_args)
        merged = _merge(op.reshape(n_p, d_att), yp.reshape(n_p, d_conv), sgp, hp, merge_w, alpha, n_tot, 0, tm, None)

        qs, ks, vs, kbs, vbs, us, sgs = _in_proj(hs, w_in_bf, b_in_l, d_att, d_conv)
        osr = _sample_attention(qs.reshape(bs, t_new, d_att), kbs.reshape(bs, t_new, d_att),
                                vbs.reshape(bs, t_new, d_att), cache_k[l], cache_v[l], page_table,
                                rel_bias, lam, g_sub, lam_init)
        os_ = osr[:, :t_new].reshape(n_s, d_att)
        buf_s = jnp.concatenate([state_conv[l].astype(F32), us.reshape(bs, t_new, d_conv)], axis=1)
        ys = _conv_sample(buf_s.transpose(1, 0, 2), *conv_args).transpose(1, 0, 2).reshape(n_s, d_conv)
        h1, topi, rank, wcol, cnt = _merge(os_, ys, sgs, hs, merge_w, alpha, n_tot, n_p, tm, merged)

        pos, te, first, half, n_used, last_tile, n_tiles = _plan(topi, rank, cnt[:, :, 0], tm, tm_moe)
        xs = _dispatch(h1, pos, last_tile, n_used, n_tiles, tm, tm_moe)
        ysort = _experts(xs, te, first, half, n_used, expert_w1[l], expert_b1[l].astype(F32),
                         expert_w2[l], expert_b2[l].astype(F32), tm_moe)
        g2, b2 = _row2(ln2_g[l]), _row2(ln2_b[l])
        hp = _combine(h1, wcol, pos, ysort, g2, b2, alpha, 0, n_p)
        hs = _combine(h1, wcol, pos, ysort, g2, b2, alpha, n_p, n_s)

        w1 = CONV_WIDTH - 1
        cp = up3[:, seq - w1:] if seq >= w1 else jnp.concatenate(
            [jnp.zeros((bp, w1 - seq, d_conv), F32), up3], axis=1)
        for lst, val in zip(outs, (kp.reshape(bp, seq, N_HEADS, QK_DIM), vp.reshape(bp, seq, N_HEADS, V_DIM), cp,
                                   ks.reshape(bs, t_new, N_HEADS, QK_DIM), vs.reshape(bs, t_new, N_HEADS, V_DIM),
                                   buf_s[:, t_new:])):
            lst.append(val)
    return (hp.reshape(bp, seq, d), hs.reshape(bs, t_new, d)) + tuple(jnp.stack(o) for o in outs)
```

```python
import functools
import math

import jax
import jax.numpy as jnp
from jax import lax
from jax.experimental import pallas as pl
from jax.experimental.pallas import tpu as pltpu

F32 = jnp.float32
BF16 = jnp.bfloat16
I32 = jnp.int32

N_HEADS = 4
HEAD_DIM = 64
QK_DIM = 2 * HEAD_DIM
V_DIM = 2 * HEAD_DIM
ATTN_SCALE = HEAD_DIM ** -0.5
LOG2E = 1.4426950408889634
NEG_INF = -1e30
NUM_BUCKETS = 32
MAX_EXACT = NUM_BUCKETS // 2
MAX_DISTANCE = 128
CONV_WIDTH = 31
TOP_K = 4
SWIGLU_LIMIT = 7.0
SWIGLU_ALPHA = 1.702
LN_EPS = 1e-5
PAGE_SIZE = 128

LANES = 128
SUBLANES = 8
VMEM_LIMIT = 56 * 1024 * 1024

TOKEN_TILE = 512
ATTN_TQ = 1024
ATTN_TK = 1024
ATTN_GROUP = 2
MOE_TILE = 512
COMBINE_TILE = 256
CONV_HALO = 32
CONV_CHUNK = 64


def _cparams(*sem):
    return pltpu.CompilerParams(dimension_semantics=sem, vmem_limit_bytes=VMEM_LIMIT)


def _load_rows(ref, tm):
    return jnp.concatenate([ref[pl.ds(c, tm, stride=SUBLANES), :] for c in range(SUBLANES)], axis=1)


def _store_rows(ref, x):
    tm = x.shape[0]
    for c in range(SUBLANES):
        ref[pl.ds(c, tm, stride=SUBLANES), :] = x[:, c * LANES:(c + 1) * LANES]


def _layer_norm(x, g, b):
    mu = jnp.mean(x, -1, keepdims=True)
    xc = x - mu
    var = jnp.mean(xc * xc, -1, keepdims=True)
    return xc * lax.rsqrt(var + LN_EPS) * g + b


def _in_proj_body(x_ref, w_ref, b_ref, q_ref, k_ref, v_ref, kb_ref, vb_ref, u_ref, sg_ref, *, d_att, d_conv, by_head):
    x = x_ref[...].astype(BF16)

    def seg(lo, hi):
        return jnp.dot(x, w_ref[:, lo:hi], preferred_element_type=F32) + b_ref[:, lo:hi]

    def head(a, h):
        return a[:, h * QK_DIM:(h + 1) * QK_DIM]

    o = 0
    q = seg(o, o + d_att) * (ATTN_SCALE * LOG2E)
    o += d_att
    k = seg(o, o + d_att)
    o += d_att
    v = seg(o, o + d_att)
    o += d_att
    tm = x.shape[0]
    for h in range(N_HEADS):
        k_ref[pl.ds(h, tm, stride=N_HEADS), :] = head(k, h)
        v_ref[pl.ds(h, tm, stride=N_HEADS), :] = head(v, h)
    if by_head:
        for h in range(N_HEADS):
            q_ref[h] = head(q, h).T.astype(BF16)
            kb_ref[h] = head(k, h).astype(BF16)
            vb_ref[h] = head(v, h).T.astype(BF16)
    else:
        q_ref[...] = q.astype(BF16)
        kb_ref[...] = k.astype(BF16)
        vb_ref[...] = v.astype(BF16)
    c = seg(o, o + 2 * d_conv)
    u_ref[...] = c[:, :d_conv] * jax.nn.sigmoid(c[:, d_conv:])
    o += 2 * d_conv
    sg_ref[...] = jax.nn.sigmoid(seg(o, w_ref.shape[1])).astype(BF16)


def _in_proj(x, w_bf, b, d_att, d_conv, attn_tiles=None):
    n, d = x.shape
    tm = min(TOKEN_TILE, n)
    row = lambda width: pl.BlockSpec((tm, width), lambda i: (i, 0))
    full = lambda a: pl.BlockSpec(a.shape, lambda i: (0, 0))
    f32_out = lambda width: jax.ShapeDtypeStruct((n, width), F32)
    kv_shape = jax.ShapeDtypeStruct((n * N_HEADS, QK_DIM), F32)
    kv_spec = pl.BlockSpec((tm * N_HEADS, QK_DIM), lambda i: (i, 0))
    if attn_tiles is None:
        qkv_shapes = [jax.ShapeDtypeStruct((n, d_att), BF16)] * 3
        qkv_specs = [row(d_att)] * 3
    else:
        bsz, seq, tq, tk = attn_tiles
        assert seq % tm == 0 and tq % tm == 0 and tk % tm == 0
        per_seq = seq // tm

        def spec(t, transposed):
            parts = t // tm
            blk = (None, N_HEADS, None, QK_DIM, tm) if transposed else (None, N_HEADS, None, tm, QK_DIM)

            def index(i):
                ti = i % per_seq
                tile, part = ti // parts, ti % parts
                return (i // per_seq, 0, tile, 0, part) if transposed else (i // per_seq, 0, tile, part, 0)
            return pl.BlockSpec(blk, index)

        qkv_shapes = [jax.ShapeDtypeStruct((bsz, N_HEADS, seq // tq, QK_DIM, tq), BF16),
                      jax.ShapeDtypeStruct((bsz, N_HEADS, seq // tk, tk, QK_DIM), BF16),
                      jax.ShapeDtypeStruct((bsz, N_HEADS, seq // tk, V_DIM, tk), BF16)]
        qkv_specs = [spec(tq, True), spec(tk, False), spec(tk, True)]
    out_shape = (qkv_shapes[0], kv_shape, kv_shape, qkv_shapes[1], qkv_shapes[2],
                 f32_out(d_conv), jax.ShapeDtypeStruct((n, 2 * d), BF16))
    return pl.pallas_call(
        functools.partial(_in_proj_body, d_att=d_att, d_conv=d_conv, by_head=attn_tiles is not None),
        grid=(n // tm,),
        in_specs=[row(d), full(w_bf), full(b)],
        out_specs=(qkv_specs[0], kv_spec, kv_spec, qkv_specs[1], qkv_specs[2], row(d_conv), row(2 * d)),
        out_shape=out_shape,
        compiler_params=_cparams("parallel"),
        name="in_proj",
    )(x, w_bf, b)


def _t5_bucket(dist):
    n = jnp.maximum(dist, 0)
    nf = jnp.maximum(n, 1).astype(F32)
    large = MAX_EXACT + (jnp.log(nf / MAX_EXACT) / math.log(MAX_DISTANCE / MAX_EXACT)
                         * (NUM_BUCKETS - MAX_EXACT)).astype(I32)
    large = jnp.minimum(large, NUM_BUCKETS - 1)
    return jnp.where(n < MAX_EXACT, n, large)


def _bias_of(dist, rel_bias, shift=None):
    b = rel_bias[_t5_bucket(dist)].astype(F32)
    if shift is not None:
        b = b - shift
    b = jnp.where((dist >= 0)[..., None], b * LOG2E, NEG_INF)
    return jnp.moveaxis(b, -1, 0)


def _flash_body(lam_ref, qt_ref, k_ref, vt_ref, btab_ref, g_ref, o_ref, m_s, l_s, acc_s, bias_ref, *, ratio, group,
                lam_init):
    i = pl.program_id(2)
    tk, tq = bias_ref.shape[1:]

    @pl.when(i == 0)
    def _():
        for o in range(ratio + 1):
            table = jnp.broadcast_to(btab_ref[o], (tk, btab_ref.shape[2]))
            bias_ref[o] = pltpu.roll(table, 0, 1, stride=1, stride_axis=0)[:, :tq]

    sub = lax.broadcasted_iota(I32, (QK_DIM, 1), 0)
    qs = []
    for b in range(group):
        qt = qt_ref[b]
        zero = jnp.zeros_like(qt)
        qs.append((jnp.where(sub < HEAD_DIM, qt, zero), jnp.where(sub >= HEAD_DIM, qt, zero)))
    m_s[...] = jnp.full(m_s.shape, NEG_INF, F32)
    l_s[...] = jnp.zeros(l_s.shape, F32)
    acc_s[...] = jnp.zeros(acc_s.shape, F32)

    def step(j, bias):
        for b in range(group):
            k = k_ref[b, j]
            vt = vt_ref[b, j]
            for c in range(2):
                s = jnp.dot(k, qs[b][c], preferred_element_type=F32)
                if bias is not None:
                    s = s + bias
                m_old = m_s[b, c]
                m_new = jnp.maximum(m_old, jnp.max(s, 0, keepdims=True))
                alpha = jnp.exp2(m_old - m_new)
                p = jnp.exp2(s - m_new)
                l_s[b, c] = alpha * l_s[b, c] + jnp.sum(p, 0, keepdims=True)
                acc_s[b, c] = alpha * acc_s[b, c] + jnp.dot(vt, p.astype(BF16), preferred_element_type=F32)
                m_s[b, c] = m_new

    def far_step(j, carry):
        step(j, None)
        return carry

    lax.fori_loop(0, jnp.maximum(i * ratio - 1, 0), far_step, 0)

    @pl.when(i >= 1)
    def _():
        step(i * ratio - 1, bias_ref[0])

    for o in range(ratio):
        step(i * ratio + o, bias_ref[o + 1])

    for b in range(group):
        o = acc_s[b, 0] / l_s[b, 0] - lam_ref[0] * (acc_s[b, 1] / l_s[b, 1])
        o = o * lax.rsqrt(jnp.mean(o * o, 0, keepdims=True) + LN_EPS) * g_ref[...]
        o_ref[b] = (o * (1.0 - lam_init)).T.astype(BF16)


def _attn_tiles(seq):
    tq = min(ATTN_TQ, seq)
    tk = min(ATTN_TK, tq)
    assert tk >= MAX_DISTANCE and seq % tq == 0 and tq % tk == 0
    return tq, tk


def _prompt_attention(qt, k4, vt, rel_bias, lam, subln_g, lam_init):
    bsz, _, nq, _, tq = qt.shape
    nk, tk = k4.shape[2], k4.shape[3]
    seq, ratio, width = nq * tq, tq // tk, N_HEADS * V_DIM
    group = ATTN_GROUP if bsz % ATTN_GROUP == 0 else 1
    far = rel_bias[NUM_BUCKETS - 1].astype(F32)
    period = tq + tk
    assert period % LANES == 0
    y = jnp.arange(period, dtype=I32)
    r_minus_c = jnp.where(y < tq, y, y - period)
    dist = r_minus_c[None, :] - (jnp.arange(ratio + 1, dtype=I32)[:, None] - 1) * tk
    btab = _bias_of(dist, rel_bias, far).reshape(N_HEADS, ratio + 1, 1, period)
    return pl.pallas_call(
        functools.partial(_flash_body, ratio=ratio, group=group, lam_init=lam_init),
        grid=(bsz // group, N_HEADS, nq),
        in_specs=[
            pl.BlockSpec(memory_space=pltpu.SMEM),
            pl.BlockSpec((group, None, None, QK_DIM, tq), lambda b, h, i: (b, h, i, 0, 0)),
            pl.BlockSpec((group, None, nk, tk, QK_DIM), lambda b, h, i: (b, h, 0, 0, 0)),
            pl.BlockSpec((group, None, nk, V_DIM, tk), lambda b, h, i: (b, h, 0, 0, 0)),
            pl.BlockSpec((None, ratio + 1, 1, period), lambda b, h, i: (h, 0, 0, 0)),
            pl.BlockSpec((V_DIM, 1), lambda b, h, i: (0, 0)),
        ],
        out_specs=pl.BlockSpec((group, tq, V_DIM), lambda b, h, i: (b, i, h)),
        out_shape=jax.ShapeDtypeStruct((bsz, seq, width), BF16),
        scratch_shapes=[pltpu.VMEM((group, 2, 1, tq), F32), pltpu.VMEM((group, 2, 1, tq), F32),
                        pltpu.VMEM((group, 2, V_DIM, tq), F32), pltpu.VMEM((ratio + 1, tk, tq), F32)],
        compiler_params=_cparams("parallel", "parallel", "arbitrary"),
        name="prompt_attention",
    )(lam, qt, k4, vt, btab, subln_g.reshape(V_DIM, 1))


def _paged_body(pt_ref, lam_ref, q_ref, kn_ref, vn_ref, bias_ref, g_ref, *refs, n_pages, t_new, lam_init):
    k_refs = refs[:n_pages]
    v_refs = refs[n_pages:2 * n_pages]
    o_ref = refs[2 * n_pages]
    lam = lam_ref[0]
    rows = 2 * t_new
    cols = PAGE_SIZE * N_HEADS
    nt = (((1,), (1,)), ((), ()))
    q = q_ref[0]
    parts = [lax.dot_general(q, k_refs[p][...].astype(BF16), nt, preferred_element_type=F32)
             for p in range(n_pages)]
    parts.append(lax.dot_general(q, kn_ref[0], nt, preferred_element_type=F32))
    s = jnp.concatenate(parts, axis=1) + bias_ref[...]
    m = jnp.max(s, -1, keepdims=True)
    p_un = jnp.exp2(s - m)
    pn = p_un / jnp.sum(p_un, -1, keepdims=True)
    w = (pn - lam * pltpu.roll(pn, N_HEADS * rows - t_new, axis=0)).astype(BF16)
    o = jnp.dot(w[:, n_pages * cols:], vn_ref[0], preferred_element_type=F32)
    for p in range(n_pages):
        o = o + jnp.dot(w[:, p * cols:(p + 1) * cols], v_refs[p][...].astype(BF16), preferred_element_type=F32)
    o = o * lax.rsqrt(jnp.mean(o * o, -1, keepdims=True) + LN_EPS) * g_ref[...]
    o = (o * (1.0 - lam_init)).astype(BF16)
    o_ref[0] = jnp.concatenate([o[h * rows:(h + 1) * rows] for h in range(N_HEADS)], axis=1)


def _sample_attention(q, kb, vb, cache_k, cache_v, page_table, rel_bias, lam, subln_g, lam_init):
    bs, t_new, width = q.shape
    n_pool = cache_k.shape[0]
    n_pages = page_table.shape[1]
    past = n_pages * PAGE_SIZE
    rows = 2 * t_new
    cols = PAGE_SIZE * N_HEADS
    new_pos = LANES // N_HEADS
    assert rows % SUBLANES == 0 and t_new <= new_pos
    qh = q.reshape(bs, t_new, N_HEADS, QK_DIM).transpose(0, 2, 1, 3)
    lane_map = (jnp.arange(QK_DIM) // HEAD_DIM)[None, :] == jnp.arange(2)[:, None]
    q_rows = (qh[:, :, None] * lane_map[None, None, :, None, :].astype(BF16)).reshape(bs, N_HEADS * rows, QK_DIM)
    pad = ((0, 0), (0, LANES - t_new * N_HEADS), (0, 0))
    kn = jnp.pad(kb.reshape(bs, t_new * N_HEADS, QK_DIM), pad)
    vn = jnp.pad(vb.reshape(bs, t_new * N_HEADS, V_DIM), pad)
    qpos = past + jnp.arange(t_new, dtype=I32)
    kpos = jnp.concatenate([jnp.arange(past + t_new, dtype=I32),
                            jnp.full((new_pos - t_new,), past + t_new + new_pos, I32)])
    b = _bias_of(qpos[:, None] - kpos[None, :], rel_bias)
    same = jnp.arange(N_HEADS)[:, None] == jnp.arange(N_HEADS)[None, :]
    b = jnp.where(same[:, None, None, :], b[..., None], NEG_INF)
    b = b.reshape(N_HEADS, 1, t_new, -1)
    bias = jnp.concatenate([b, b], axis=1).reshape(N_HEADS * rows, -1)

    page_spec = lambda j: pl.BlockSpec((None, cols, QK_DIM), lambda b, pt, j=j: (pt[b * n_pages + j], 0, 0))
    per_seq = lambda r, c: pl.BlockSpec((1, r, c), lambda b, pt: (b, 0, 0))
    grid_spec = pltpu.PrefetchScalarGridSpec(
        num_scalar_prefetch=1,
        grid=(bs,),
        in_specs=[
            pl.BlockSpec(memory_space=pltpu.SMEM),
            per_seq(N_HEADS * rows, QK_DIM),
            per_seq(LANES, QK_DIM),
            per_seq(LANES, V_DIM),
            pl.BlockSpec(bias.shape, lambda b, pt: (0, 0)),
            pl.BlockSpec((1, V_DIM), lambda b, pt: (0, 0)),
        ] + [page_spec(j) for j in range(n_pages)] * 2,
        out_specs=pl.BlockSpec((1, rows, width), lambda b, pt: (b, 0, 0)),
    )
    ck = cache_k.reshape(n_pool, cols, QK_DIM)
    cv = cache_v.reshape(n_pool, cols, V_DIM)
    return pl.pallas_call(
        functools.partial(_paged_body, n_pages=n_pages, t_new=t_new, lam_init=lam_init),
        grid_spec=grid_spec,
        out_shape=jax.ShapeDtypeStruct((bs, rows, width), BF16),
        compiler_params=_cparams("parallel"),
        name="sample_attention",
    )(page_table.reshape(-1), lam, q_rows, kn, vn, bias, subln_g, *([ck] * n_pages), *([cv] * n_pages))


def _conv_prompt_body(u_ref, halo_ref, w_ref, cb_ref, g_ref, b_ref, o_ref, buf, shifted, *, tc):
    i = pl.program_id(1)
    halo = halo_ref[0]
    buf[0:CONV_HALO, :] = jnp.where(i > 0, halo, jnp.zeros_like(halo))
    buf[CONV_HALO:, :] = u_ref[0]
    rows = CONV_HALO + tc - SUBLANES
    for s in range(1, SUBLANES):
        shifted[s - 1, 0:rows, :] = buf[s:s + rows, :]
    first = CONV_HALO - (CONV_WIDTH - 1)
    for r0 in range(0, tc, CONV_CHUNK):
        acc = jnp.zeros((CONV_CHUNK, u_ref.shape[2]), F32) + cb_ref[...]
        for j in range(CONV_WIDTH):
            s = (first + j) % SUBLANES
            a = first + j - s + r0
            win = buf[a:a + CONV_CHUNK, :] if s == 0 else shifted[s - 1, a:a + CONV_CHUNK, :]
            acc = acc + w_ref[j:j + 1, :] * win
        y = _layer_norm(acc, g_ref[...], b_ref[...])
        o_ref[0, r0:r0 + CONV_CHUNK, :] = (y * jax.nn.sigmoid(y)).astype(BF16)


def _conv_prompt(u, conv_w, conv_b, ln_g, ln_b):
    bsz, seq, ch = u.shape
    tc = min(TOKEN_TILE, seq)
    assert seq % tc == 0 and tc % CONV_CHUNK == 0 and tc % CONV_HALO == 0
    vec = pl.BlockSpec((1, ch), lambda b, i: (0, 0))
    return pl.pallas_call(
        functools.partial(_conv_prompt_body, tc=tc),
        grid=(bsz, seq // tc),
        in_specs=[
            pl.BlockSpec((1, tc, ch), lambda b, i: (b, i, 0)),
            pl.BlockSpec((1, CONV_HALO, ch), lambda b, i: (b, jnp.maximum(i * (tc // CONV_HALO) - 1, 0), 0)),
            pl.BlockSpec((CONV_WIDTH, ch), lambda b, i: (0, 0)),
            vec, vec, vec,
        ],
        out_specs=pl.BlockSpec((1, tc, ch), lambda b, i: (b, i, 0)),
        out_shape=jax.ShapeDtypeStruct((bsz, seq, ch), BF16),
        scratch_shapes=[pltpu.VMEM((CONV_HALO + tc, ch), F32), pltpu.VMEM((SUBLANES - 1, CONV_HALO + tc, ch), F32)],
        compiler_params=_cparams("parallel", "arbitrary"),
        name="conv_prompt",
    )(u, u, conv_w, conv_b, ln_g, ln_b)


def _conv_sample_body(buf_ref, w_ref, cb_ref, g_ref, b_ref, o_ref, *, t_new):
    for t in range(t_new):
        acc = jnp.zeros(buf_ref.shape[1:], F32) + cb_ref[...]
        for j in range(CONV_WIDTH):
            acc = acc + w_ref[j:j + 1, :] * buf_ref[t + j]
        y = _layer_norm(acc, g_ref[...], b_ref[...])
        o_ref[t] = (y * jax.nn.sigmoid(y)).astype(BF16)


def _conv_sample(buf_t, conv_w, conv_b, ln_g, ln_b):
    rows, bs, ch = buf_t.shape
    t_new = rows - (CONV_WIDTH - 1)
    gb = min(32, bs)
    assert bs % gb == 0
    vec = pl.BlockSpec((1, ch), lambda i: (0, 0))
    return pl.pallas_call(
        functools.partial(_conv_sample_body, t_new=t_new),
        grid=(bs // gb,),
        in_specs=[pl.BlockSpec((rows, gb, ch), lambda i: (0, i, 0)),
                  pl.BlockSpec((CONV_WIDTH, ch), lambda i: (0, 0)), vec, vec, vec],
        out_specs=pl.BlockSpec((t_new, gb, ch), lambda i: (0, i, 0)),
        out_shape=jax.ShapeDtypeStruct((t_new, bs, ch), BF16),
        compiler_params=_cparams("parallel"),
        name="conv_sample",
    )(buf_t, conv_w, conv_b, ln_g, ln_b)


N_MERGE_OUT = 5


def _merge_body(*refs, alpha, n_alias):
    (o_ref, y_ref, sg_ref, x_ref, wa_ref, wc_ref, bc_ref, wo_ref, g1_ref, b1_ref, rw_ref, rb_ref) = refs[:12]
    h1_ref, ti_ref, rk_ref, wcol_ref, cnt_ref = refs[12 + n_alias:]
    tm, d = x_ref.shape
    a = jnp.dot(o_ref[...], wa_ref[...], preferred_element_type=F32)
    b = jnp.dot(y_ref[...], wc_ref[...], preferred_element_type=F32) + bc_ref[...]
    mix_in = sg_ref[:, :d].astype(F32) * a + sg_ref[:, d:].astype(F32) * b
    mix = jnp.dot(mix_in.astype(BF16), wo_ref[...], preferred_element_type=F32)
    h1 = _layer_norm(alpha * x_ref[...] + mix, g1_ref[...], b1_ref[...])
    _store_rows(h1_ref, h1)
    logits = lax.dot_general(rw_ref[...], h1, (((1,), (1,)), ((), ())), preferred_element_type=F32,
                             precision=lax.Precision.HIGHEST) + rb_ref[...]
    n_e = logits.shape[0]
    eid = lax.broadcasted_iota(I32, logits.shape, 0)
    vals, idxs, hots = [], [], []
    for _ in range(TOP_K):
        m = jnp.max(logits, 0, keepdims=True)
        idx = jnp.min(jnp.where(logits == m, eid, n_e), 0, keepdims=True)
        hot = eid == idx
        vals.append(m)
        idxs.append(idx)
        hots.append(hot)
        logits = jnp.where(hot, -jnp.inf, logits)
    ex = [jnp.exp(v - vals[0]) for v in vals]
    den = ex[0]
    for e in ex[1:]:
        den = den + e
    ti_ref[...] = jnp.concatenate(idxs, 0)
    wrows = jnp.concatenate([e / den for e in ex] + [jnp.zeros((LANES - TOP_K, tm), F32)], 0)
    wcol_ref[...] = wrows.T
    sel = hots[0]
    for hot in hots[1:]:
        sel = sel | hot
    sel_f = jnp.where(sel, 1.0, 0.0)
    before = lax.broadcasted_iota(I32, (tm, tm), 0) < lax.broadcasted_iota(I32, (tm, tm), 1)
    upper = jnp.where(before, 1.0, 0.0).astype(BF16)
    ahead = jnp.dot(sel_f.astype(BF16), upper, preferred_element_type=F32)
    rk_ref[...] = jnp.concatenate(
        [jnp.sum(jnp.where(hot, ahead, 0.0), 0, keepdims=True) for hot in hots], 0).astype(I32)
    cnt = jnp.sum(sel_f, 1, keepdims=True).astype(I32)
    cnt_ref[0] = jnp.broadcast_to(cnt, (n_e, LANES))


def _merge(o, yact, sg, x, wts, alpha, n_total, row_off, tm, prev):
    n, d = x.shape
    wa, wc, bc, wo, g1, b1, rwt, rb = wts
    n_e = rwt.shape[0]
    off = row_off // tm
    last = n // tm - 1
    steps = n // tm if prev is not None else n_total // tm
    row = lambda width: pl.BlockSpec((tm, width), lambda i: (jnp.minimum(i, last), 0))
    full = lambda a: pl.BlockSpec(a.shape, lambda i: (0, 0))
    in_specs = [row(o.shape[1]), row(yact.shape[1]), row(2 * d), row(d),
                full(wa), full(wc), full(bc), full(wo), full(g1), full(b1), full(rwt), full(rb)]
    args = [o, yact, sg, x, wa, wc, bc, wo, g1, b1, rwt, rb]
    aliases = {}
    if prev is not None:
        in_specs += [pl.BlockSpec(memory_space=pl.ANY)] * N_MERGE_OUT
        aliases = {len(args) + j: j for j in range(N_MERGE_OUT)}
        args += list(prev)
    slot_major = pl.BlockSpec((TOP_K, tm), lambda i: (0, off + i))
    return pl.pallas_call(
        functools.partial(_merge_body, alpha=alpha, n_alias=0 if prev is None else N_MERGE_OUT),
        grid=(steps,),
        in_specs=in_specs,
        out_specs=(pl.BlockSpec((tm * SUBLANES, LANES), lambda i: (off + i, 0)), slot_major, slot_major,
                   pl.BlockSpec((tm, LANES), lambda i: (off + i, 0)),
                   pl.BlockSpec((1, n_e, LANES), lambda i: (off + i, 0, 0))),
        out_shape=(jax.ShapeDtypeStruct((n_total * SUBLANES, LANES), F32),
                   jax.ShapeDtypeStruct((TOP_K, n_total), I32),
                   jax.ShapeDtypeStruct((TOP_K, n_total), I32),
                   jax.ShapeDtypeStruct((n_total, LANES), F32),
                   jax.ShapeDtypeStruct((n_total // tm, n_e, LANES), I32)),
        input_output_aliases=aliases,
        compiler_params=_cparams("parallel"),
        name="merge_ln1_router",
    )(*args)


def _plan(topi, rank, cnt, tm_tok, tm_moe):
    k, n = topi.shape
    n_e = cnt.shape[1]
    before = jnp.cumsum(cnt, 0) - cnt
    total = jnp.sum(cnt, 0)
    ntile_e = (total + tm_moe - 1) // tm_moe
    tend = jnp.cumsum(ntile_e)
    tstart = tend - ntile_e
    base = (tstart * tm_moe)[None, :] + before
    base_tok = jnp.repeat(base, tm_tok, axis=0)
    hot = topi[:, :, None] == jnp.arange(n_e, dtype=I32)[None, None, :]
    pos = (jnp.sum(jnp.where(hot, base_tok[None], 0), -1).astype(I32) + rank) * SUBLANES
    n_tiles = (k * n + tm_moe - 1) // tm_moe + n_e
    n_used = tend[-1]
    tid = jnp.arange(n_tiles, dtype=I32)
    src = jnp.minimum(tid, n_used - 1)
    te = jnp.minimum(jnp.sum((src[:, None] >= tend[None, :]).astype(I32), 1), n_e - 1)
    first = (tid == tstart[te]).astype(I32)
    half = (total[te] - (tid - tstart[te]) * tm_moe <= tm_moe // 2).astype(I32)
    last_tile = jnp.where(ntile_e > 0, tend - 1, -1).astype(I32)
    return pos, te.astype(I32), first, half, n_used.reshape(1).astype(I32), last_tile, n_tiles


def _dispatch_body(lt_ref, nu_ref, pos_ref, x_ref, xs_hbm, zbuf, zsem, sem, *, tm, tm_moe, n_e, n_tiles):
    i = pl.program_id(0)

    def zero_copy(tile):
        start = pl.multiple_of(tile * (tm_moe * SUBLANES), tm_moe * SUBLANES)
        return pltpu.make_async_copy(zbuf, xs_hbm.at[pl.ds(start, tm_moe * SUBLANES), :], zsem)

    @pl.when(i == 0)
    def _():
        zbuf[...] = jnp.zeros(zbuf.shape, F32)

        def z_start(e, c):
            @pl.when(lt_ref[e] >= 0)
            def _():
                zero_copy(lt_ref[e]).start()
            return c

        def z_wait(e, c):
            @pl.when(lt_ref[e] >= 0)
            def _():
                zero_copy(lt_ref[e]).wait()
            return c

        def t_start(t, c):
            zero_copy(t).start()
            return c

        def t_wait(t, c):
            zero_copy(t).wait()
            return c

        lax.fori_loop(0, n_e, z_start, 0)
        lax.fori_loop(nu_ref[0], n_tiles, t_start, 0)
        lax.fori_loop(0, n_e, z_wait, 0)
        lax.fori_loop(nu_ref[0], n_tiles, t_wait, 0)

    def issue(g, c):
        base = pl.multiple_of(g * SUBLANES, SUBLANES)
        for rr in range(SUBLANES):
            src = x_ref.at[pl.ds(pl.multiple_of((base + rr) * SUBLANES, SUBLANES), SUBLANES), :]
            for k in range(TOP_K):
                dst = pl.multiple_of(pos_ref[k, base + rr], SUBLANES)
                pltpu.make_async_copy(src, xs_hbm.at[pl.ds(dst, SUBLANES), :], sem).start()
        return c

    lax.fori_loop(0, tm // SUBLANES, issue, 0)
    for k in range(TOP_K):
        pltpu.make_async_copy(x_ref, xs_hbm.at[pl.ds(0, tm * SUBLANES), :], sem).wait()


def _dispatch(h1, pos, last_tile, n_used, n_tiles, tm, tm_moe):
    n = h1.shape[0] // SUBLANES
    n_e = last_tile.shape[0]
    grid_spec = pltpu.PrefetchScalarGridSpec(
        num_scalar_prefetch=2,
        grid=(n // tm,),
        in_specs=[pl.BlockSpec((TOP_K, tm), lambda i, lt, nu: (0, i), memory_space=pltpu.SMEM),
                  pl.BlockSpec((tm * SUBLANES, LANES), lambda i, lt, nu: (i, 0))],
        out_specs=pl.BlockSpec(memory_space=pl.ANY),
        scratch_shapes=[pltpu.VMEM((tm_moe * SUBLANES, LANES), F32), pltpu.SemaphoreType.DMA,
                        pltpu.SemaphoreType.DMA],
    )
    return pl.pallas_call(
        functools.partial(_dispatch_body, tm=tm, tm_moe=tm_moe, n_e=n_e, n_tiles=n_tiles),
        grid_spec=grid_spec,
        out_shape=jax.ShapeDtypeStruct((n_tiles * tm_moe * SUBLANES, LANES), F32),
        compiler_params=_cparams("arbitrary"),
        name="moe_dispatch",
    )(last_tile, n_used, pos, h1)


def _experts_body(te_ref, first_ref, half_ref, nu_ref, x_ref, w1_ref, b1_ref, w2_ref, b2_ref, y_ref, w1b, w2b, *,
                  tm):
    i = pl.program_id(0)
    d_ff = w2_ref.shape[1]

    def run(rows):
        x = _load_rows(x_ref, rows).astype(BF16)
        hh = jnp.dot(x, w1b[...], preferred_element_type=F32) + b1_ref[0]
        g = jnp.minimum(hh[:, :d_ff], SWIGLU_LIMIT)
        u = jnp.clip(hh[:, d_ff:], -SWIGLU_LIMIT, SWIGLU_LIMIT)
        act = (u + 1.0) * g * jax.nn.sigmoid(SWIGLU_ALPHA * g)
        _store_rows(y_ref, jnp.dot(act.astype(BF16), w2b[...], preferred_element_type=F32) + b2_ref[0])

    @pl.when(i >= nu_ref[0])
    def _():
        y_ref[...] = jnp.zeros(y_ref.shape, F32)

    @pl.when(i < nu_ref[0])
    def _():
        @pl.when(first_ref[i] == 1)
        def _():
            w1b[...] = w1_ref[0].astype(BF16)
            w2b[...] = w2_ref[0].astype(BF16)

        @pl.when(half_ref[i] == 0)
        def _():
            run(tm)

        @pl.when(half_ref[i] == 1)
        def _():
            run(tm // 2)
            y_ref[pl.ds(tm // 2 * SUBLANES, tm // 2 * SUBLANES), :] = jnp.zeros((tm // 2 * SUBLANES, LANES), F32)


def _experts(xs, te, first, half, n_used, w1, b1, w2, b2, tm):
    n_e, d, f2 = w1.shape
    rows = tm * SUBLANES
    by_tile = lambda i, te, first, half, nu: (i, 0)
    by_expert = lambda i, te, first, half, nu: (te[i], 0, 0)
    grid_spec = pltpu.PrefetchScalarGridSpec(
        num_scalar_prefetch=4,
        grid=(xs.shape[0] // rows,),
        in_specs=[
            pl.BlockSpec((rows, LANES), by_tile),
            pl.BlockSpec((1, d, f2), by_expert),
            pl.BlockSpec((1, 1, f2), by_expert),
            pl.BlockSpec((1, f2 // 2, d), by_expert),
            pl.BlockSpec((1, 1, d), by_expert),
        ],
        out_specs=pl.BlockSpec((rows, LANES), by_tile),
        scratch_shapes=[pltpu.VMEM((d, f2), BF16), pltpu.VMEM((f2 // 2, d), BF16)],
    )
    return pl.pallas_call(
        functools.partial(_experts_body, tm=tm),
        grid_spec=grid_spec,
        out_shape=jax.ShapeDtypeStruct(xs.shape, F32),
        compiler_params=_cparams("arbitrary"),
        name="moe_experts",
    )(te, first, half, n_used, xs, w1, b1.reshape(n_e, 1, f2), w2, b2.reshape(n_e, 1, d))


def _combine_body(pos_ref, nxt_ref, h1_ref, wcol_ref, ys_hbm, g_ref, b_ref, o_ref, ybuf, sems, *, alpha, tm):
    i = pl.program_id(0)
    n = pl.num_programs(0)

    def fetch(p_ref, slot):
        def issue(g, c):
            base = pl.multiple_of(g * SUBLANES, SUBLANES)
            for rr in range(SUBLANES):
                dst = pl.ds(pl.multiple_of((base + rr) * SUBLANES, SUBLANES), SUBLANES)
                for k in range(TOP_K):
                    src = pl.multiple_of(p_ref[k, base + rr], SUBLANES)
                    pltpu.make_async_copy(ys_hbm.at[pl.ds(src, SUBLANES), :], ybuf.at[slot, k, dst, :],
                                          sems.at[slot]).start()
            return c
        lax.fori_loop(0, tm // SUBLANES, issue, 0)

    @pl.when(i == 0)
    def _():
        fetch(pos_ref, 0)

    @pl.when(i + 1 < n)
    def _():
        fetch(nxt_ref, (i + 1) % 2)

    slot = i % 2
    for k in range(TOP_K):
        pltpu.make_async_copy(ys_hbm.at[pl.ds(0, tm * SUBLANES), :], ybuf.at[slot, k], sems.at[slot]).wait()
    ff = wcol_ref[:, 0:1] * _load_rows(ybuf.at[slot, 0], tm)
    for k in range(1, TOP_K):
        ff = ff + wcol_ref[:, k:k + 1] * _load_rows(ybuf.at[slot, k], tm)
    o_ref[...] = _layer_norm(alpha * _load_rows(h1_ref, tm) + ff, g_ref[...], b_ref[...])


def _combine(h1, wcol, pos, ys, g, b, alpha, row_off, n):
    d = g.shape[1]
    tm = min(COMBINE_TILE, n)
    off = row_off // tm
    last = n // tm - 1
    vec = pl.BlockSpec((1, d), lambda i: (0, 0))
    return pl.pallas_call(
        functools.partial(_combine_body, alpha=alpha, tm=tm),
        grid=(n // tm,),
        in_specs=[pl.BlockSpec((TOP_K, tm), lambda i: (0, off + i), memory_space=pltpu.SMEM),
                  pl.BlockSpec((TOP_K, tm), lambda i: (0, off + jnp.minimum(i + 1, last)), memory_space=pltpu.SMEM),
                  pl.BlockSpec((tm * SUBLANES, LANES), lambda i: (off + i, 0)),
                  pl.BlockSpec((tm, LANES), lambda i: (off + i, 0)),
                  pl.BlockSpec(memory_space=pl.ANY), vec, vec],
        out_specs=pl.BlockSpec((tm, d), lambda i: (i, 0)),
        out_shape=jax.ShapeDtypeStruct((n, d), F32),
        scratch_shapes=[pltpu.VMEM((2, TOP_K, tm * SUBLANES, LANES), F32), pltpu.SemaphoreType.DMA((2,))],
        compiler_params=_cparams("arbitrary"),
        name="combine_ln2",
    )(pos, pos, h1, wcol, ys, g, b)


def _row2(v):
    return v.reshape(1, -1).astype(F32)


def kernel(x_prompt, x_sample, cache_k, cache_v, page_table, state_conv, w_in, b_in, lambda_q1, lambda_k1,
           lambda_q2, lambda_k2, subln_g, rel_bias, w_attn_proj, conv_w, conv_b, conv_ln_g, conv_ln_b,
           w_conv_proj, b_conv_proj, w_out, ln1_g, ln1_b, router_w, router_b, expert_w1, expert_b1,
           expert_w2, expert_b2, ln2_g, ln2_b):
    depth = w_in.shape[0]
    bp, seq, d = x_prompt.shape
    bs, t_new, _ = x_sample.shape
    d_att = N_HEADS * QK_DIM
    d_conv = conv_w.shape[2]
    n_p, n_s = bp * seq, bs * t_new
    n_tot = n_p + n_s
    tm = min(TOKEN_TILE, math.gcd(n_p, n_s))
    assert tm % LANES == 0
    tm_moe = min(MOE_TILE, n_tot)
    alpha = (2 * depth) ** 0.25

    hp = x_prompt.reshape(n_p, d)
    hs = x_sample.reshape(n_s, d)
    outs = [[] for _ in range(6)]
    for l in range(depth):
        lam_init = 0.8 - 0.6 * math.exp(-0.3 * l)
        lam = (jnp.exp(jnp.sum(lambda_q1[l].astype(F32) * lambda_k1[l].astype(F32)))
               - jnp.exp(jnp.sum(lambda_q2[l].astype(F32) * lambda_k2[l].astype(F32))) + lam_init).reshape(1)
        w_in_bf = w_in[l].astype(BF16)
        b_in_l = _row2(b_in[l])
        g_sub = _row2(subln_g[l])
        conv_args = (conv_w[l].astype(F32), _row2(conv_b[l]), _row2(conv_ln_g[l]), _row2(conv_ln_b[l]))
        merge_w = (w_attn_proj[l].astype(BF16), w_conv_proj[l].astype(BF16), _row2(b_conv_proj[l]),
                   w_out[l].astype(BF16), _row2(ln1_g[l]), _row2(ln1_b[l]),
                   router_w[l].astype(F32).T, router_b[l].astype(F32).reshape(-1, 1))

        qp, kp, vp, kbp, vbp, up, sgp = _in_proj(hp, w_in_bf, b_in_l, d_att, d_conv, (bp, seq) + _attn_tiles(seq))
        op = _prompt_attention(qp, kbp, vbp, rel_bias, lam, g_sub, lam_init)
        up3 = up.reshape(bp, seq, d_conv)
        yp = _conv_prompt(up3, *conv_args)
        merged = _merge(op.reshape(n_p, d_att), yp.reshape(n_p, d_conv), sgp, hp, merge_w, alpha, n_tot, 0, tm, None)

        qs, ks, vs, kbs, vbs, us, sgs = _in_proj(hs, w_in_bf, b_in_l, d_att, d_conv)
        osr = _sample_attention(qs.reshape(bs, t_new, d_att), kbs.reshape(bs, t_new, d_att),
                                vbs.reshape(bs, t_new, d_att), cache_k[l], cache_v[l], page_table,
                                rel_bias, lam, g_sub, lam_init)
        os_ = osr[:, :t_new].reshape(n_s, d_att)
        buf_s = jnp.concatenate([state_conv[l].astype(F32), us.reshape(bs, t_new, d_conv)], axis=1)
        ys = _conv_sample(buf_s.transpose(1, 0, 2), *conv_args).transpose(1, 0, 2).reshape(n_s, d_conv)
        h1, topi, rank, wcol, cnt = _merge(os_, ys, sgs, hs, merge_w, alpha, n_tot, n_p, tm, merged)

        pos, te, first, half, n_used, last_tile, n_tiles = _plan(topi, rank, cnt[:, :, 0], tm, tm_moe)
        xs = _dispatch(h1, pos, last_tile, n_used, n_tiles, tm, tm_moe)
        ysort = _experts(xs, te, first, half, n_used, expert_w1[l], expert_b1[l].astype(F32),
                         expert_w2[l], expert_b2[l].astype(F32), tm_moe)
        g2, b2 = _row2(ln2_g[l]), _row2(ln2_b[l])
        hp = _combine(h1, wcol, pos, ysort, g2, b2, alpha, 0, n_p)
        hs = _combine(h1, wcol, pos, ysort, g2, b2, alpha, n_p, n_s)

        w1 = CONV_WIDTH - 1
        cp = up3[:, seq - w1:] if seq >= w1 else jnp.concatenate(
            [jnp.zeros((bp, w1 - seq, d_conv), F32), up3], axis=1)
        for lst, val in zip(outs, (kp.reshape(bp, seq, N_HEADS, QK_DIM), vp.reshape(bp, seq, N_HEADS, V_DIM), cp,
                                   ks.reshape(bs, t_new, N_HEADS, QK_DIM), vs.reshape(bs, t_new, N_HEADS, V_DIM),
                                   buf_s[:, t_new:])):
            lst.append(val)
    return (hp.reshape(bp, seq, d), hs.reshape(bs, t_new, d)) + tuple(jnp.stack(o) for o in outs)
```

```python
import functools
import math

import jax
import jax.numpy as jnp
from jax import lax
from jax.experimental import pallas as pl
from jax.experimental.pallas import tpu as pltpu

F32 = jnp.float32
BF16 = jnp.bfloat16
I32 = jnp.int32

N_HEADS = 4
HEAD_DIM = 64
QK_DIM = 2 * HEAD_DIM
V_DIM = 2 * HEAD_DIM
ATTN_SCALE = HEAD_DIM ** -0.5
LOG2E = 1.4426950408889634
NEG_INF = -1e30
NUM_BUCKETS = 32
MAX_EXACT = NUM_BUCKETS // 2
MAX_DISTANCE = 128
CONV_WIDTH = 31
TOP_K = 4
SWIGLU_LIMIT = 7.0
SWIGLU_ALPHA = 1.702
LN_EPS = 1e-5
PAGE_SIZE = 128

LANES = 128
SUBLANES = 8
VMEM_LIMIT = 56 * 1024 * 1024

TOKEN_TILE = 512
ATTN_TQ = 1024
ATTN_TK = 1024
ATTN_GROUP = 2
MOE_TILE = 512
COMBINE_TILE = 256
CONV_HALO = 32
CONV_CHUNK = 64


def _cparams(*sem):
    return pltpu.CompilerParams(dimension_semantics=sem, vmem_limit_bytes=VMEM_LIMIT)


def _load_rows(ref, tm):
    return jnp.concatenate([ref[pl.ds(c, tm, stride=SUBLANES), :] for c in range(SUBLANES)], axis=1)


def _store_rows(ref, x):
    tm = x.shape[0]
    for c in range(SUBLANES):
        ref[pl.ds(c, tm, stride=SUBLANES), :] = x[:, c * LANES:(c + 1) * LANES]


def _layer_norm(x, g, b):
    mu = jnp.mean(x, -1, keepdims=True)
    xc = x - mu
    var = jnp.mean(xc * xc, -1, keepdims=True)
    return xc * lax.rsqrt(var + LN_EPS) * g + b


def _in_proj_body(x_ref, w_ref, b_ref, q_ref, k_ref, v_ref, kb_ref, vb_ref, u_ref, sg_ref, *, d_att, d_conv, by_head):
    x = x_ref[...].astype(BF16)

    def seg(lo, hi):
        return jnp.dot(x, w_ref[:, lo:hi], preferred_element_type=F32) + b_ref[:, lo:hi]

    def head(a, h):
        return a[:, h * QK_DIM:(h + 1) * QK_DIM]

    o = 0
    q = seg(o, o + d_att) * (ATTN_SCALE * LOG2E)
    o += d_att
    k = seg(o, o + d_att)
    o += d_att
    v = seg(o, o + d_att)
    o += d_att
    tm = x.shape[0]
    for h in range(N_HEADS):
        k_ref[pl.ds(h, tm, stride=N_HEADS), :] = head(k, h)
        v_ref[pl.ds(h, tm, stride=N_HEADS), :] = head(v, h)
    if by_head:
        for h in range(N_HEADS):
            q_ref[h] = head(q, h).T.astype(BF16)
            kb_ref[h] = head(k, h).astype(BF16)
            vb_ref[h] = head(v, h).T.astype(BF16)
    else:
        q_ref[...] = q.astype(BF16)
        kb_ref[...] = k.astype(BF16)
        vb_ref[...] = v.astype(BF16)
    c = seg(o, o + 2 * d_conv)
    u_ref[...] = c[:, :d_conv] * jax.nn.sigmoid(c[:, d_conv:])
    o += 2 * d_conv
    sg_ref[...] = jax.nn.sigmoid(seg(o, w_ref.shape[1])).astype(BF16)


def _in_proj(x, w_bf, b, d_att, d_conv, attn_tiles=None):
    n, d = x.shape
    tm = min(TOKEN_TILE, n)
    row = lambda width: pl.BlockSpec((tm, width), lambda i: (i, 0))
    full = lambda a: pl.BlockSpec(a.shape, lambda i: (0, 0))
    f32_out = lambda width: jax.ShapeDtypeStruct((n, width), F32)
    kv_shape = jax.ShapeDtypeStruct((n * N_HEADS, QK_DIM), F32)
    kv_spec = pl.BlockSpec((tm * N_HEADS, QK_DIM), lambda i: (i, 0))
    if attn_tiles is None:
        qkv_shapes = [jax.ShapeDtypeStruct((n, d_att), BF16)] * 3
        qkv_specs = [row(d_att)] * 3
    else:
        bsz, seq, tq, tk = attn_tiles
        assert seq % tm == 0 and tq % tm == 0 and tk % tm == 0
        per_seq = seq // tm

        def spec(t, transposed):
            parts = t // tm
            blk = (None, N_HEADS, None, QK_DIM, tm) if transposed else (None, N_HEADS, None, tm, QK_DIM)

            def index(i):
                ti = i % per_seq
                tile, part = ti // parts, ti % parts
                return (i // per_seq, 0, tile, 0, part) if transposed else (i // per_seq, 0, tile, part, 0)
            return pl.BlockSpec(blk, index)

        qkv_shapes = [jax.ShapeDtypeStruct((bsz, N_HEADS, seq // tq, QK_DIM, tq), BF16),
                      jax.ShapeDtypeStruct((bsz, N_HEADS, seq // tk, tk, QK_DIM), BF16),
                      jax.ShapeDtypeStruct((bsz, N_HEADS, seq // tk, V_DIM, tk), BF16)]
        qkv_specs = [spec(tq, True), spec(tk, False), spec(tk, True)]
    out_shape = (qkv_shapes[0], kv_shape, kv_shape, qkv_shapes[1], qkv_shapes[2],
                 f32_out(d_conv), jax.ShapeDtypeStruct((n, 2 * d), BF16))
    return pl.pallas_call(
        functools.partial(_in_proj_body, d_att=d_att, d_conv=d_conv, by_head=attn_tiles is not None),
        grid=(n // tm,),
        in_specs=[row(d), full(w_bf), full(b)],
        out_specs=(qkv_specs[0], kv_spec, kv_spec, qkv_specs[1], qkv_specs[2], row(d_conv), row(2 * d)),
        out_shape=out_shape,
        compiler_params=_cparams("parallel"),
        name="in_proj",
    )(x, w_bf, b)


def _t5_bucket(dist):
    n = jnp.maximum(dist, 0)
    nf = jnp.maximum(n, 1).astype(F32)
    large = MAX_EXACT + (jnp.log(nf / MAX_EXACT) / math.log(MAX_DISTANCE / MAX_EXACT)
                         * (NUM_BUCKETS - MAX_EXACT)).astype(I32)
    large = jnp.minimum(large, NUM_BUCKETS - 1)
    return jnp.where(n < MAX_EXACT, n, large)


def _bias_of(dist, rel_bias, shift=None):
    b = rel_bias[_t5_bucket(dist)].astype(F32)
    if shift is not None:
        b = b - shift
    b = jnp.where((dist >= 0)[..., None], b * LOG2E, NEG_INF)
    return jnp.moveaxis(b, -1, 0)


def _flash_body(lam_ref, qt_ref, k_ref, vt_ref, btab_ref, g_ref, o_ref, m_s, l_s, acc_s, bias_ref, *, ratio, group,
                lam_init):
    i = pl.program_id(2)
    tk, tq = bias_ref.shape[1:]

    @pl.when(i == 0)
    def _():
        for o in range(ratio + 1):
            table = jnp.broadcast_to(btab_ref[o], (tk, btab_ref.shape[2]))
            bias_ref[o] = pltpu.roll(table, 0, 1, stride=1, stride_axis=0)[:, :tq]

    sub = lax.broadcasted_iota(I32, (QK_DIM, 1), 0)
    qs = []
    for b in range(group):
        qt = qt_ref[b]
        zero = jnp.zeros_like(qt)
        qs.append((jnp.where(sub < HEAD_DIM, qt, zero), jnp.where(sub >= HEAD_DIM, qt, zero)))
    m_s[...] = jnp.full(m_s.shape, NEG_INF, F32)
    l_s[...] = jnp.zeros(l_s.shape, F32)
    acc_s[...] = jnp.zeros(acc_s.shape, F32)

    def step(j, bias):
        for b in range(group):
            k = k_ref[b, j]
            vt = vt_ref[b, j]
            for c in range(2):
                s = jnp.dot(k, qs[b][c], preferred_element_type=F32)
                if bias is not None:
                    s = s + bias
                m_old = m_s[b, c]
                m_new = jnp.maximum(m_old, jnp.max(s, 0, keepdims=True))
                alpha = jnp.exp2(m_old - m_new)
                p = jnp.exp2(s - m_new)
                l_s[b, c] = alpha * l_s[b, c] + jnp.sum(p, 0, keepdims=True)
                acc_s[b, c] = alpha * acc_s[b, c] + jnp.dot(vt, p.astype(BF16), preferred_element_type=F32)
                m_s[b, c] = m_new

    def far_step(j, carry):
        step(j, None)
        return carry

    lax.fori_loop(0, jnp.maximum(i * ratio - 1, 0), far_step, 0)

    @pl.when(i >= 1)
    def _():
        step(i * ratio - 1, bias_ref[0])

    for o in range(ratio):
        step(i * ratio + o, bias_ref[o + 1])

    for b in range(group):
        o = acc_s[b, 0] / l_s[b, 0] - lam_ref[0] * (acc_s[b, 1] / l_s[b, 1])
        o = o * lax.rsqrt(jnp.mean(o * o, 0, keepdims=True) + LN_EPS) * g_ref[...]
        o_ref[b] = (o * (1.0 - lam_init)).T.astype(BF16)


def _attn_tiles(seq):
    tq = min(ATTN_TQ, seq)
    tk = min(ATTN_TK, tq)
    assert tk >= MAX_DISTANCE and seq % tq == 0 and tq % tk == 0
    return tq, tk


def _prompt_attention(qt, k4, vt, rel_bias, lam, subln_g, lam_init):
    bsz, _, nq, _, tq = qt.shape
    nk, tk = k4.shape[2], k4.shape[3]
    seq, ratio, width = nq * tq, tq // tk, N_HEADS * V_DIM
    group = ATTN_GROUP if bsz % ATTN_GROUP == 0 else 1
    far = rel_bias[NUM_BUCKETS - 1].astype(F32)
    period = tq + tk
    assert period % LANES == 0
    y = jnp.arange(period, dtype=I32)
    r_minus_c = jnp.where(y < tq, y, y - period)
    dist = r_minus_c[None, :] - (jnp.arange(ratio + 1, dtype=I32)[:, None] - 1) * tk
    btab = _bias_of(dist, rel_bias, far).reshape(N_HEADS, ratio + 1, 1, period)
    return pl.pallas_call(
        functools.partial(_flash_body, ratio=ratio, group=group, lam_init=lam_init),
        grid=(bsz // group, N_HEADS, nq),
        in_specs=[
            pl.BlockSpec(memory_space=pltpu.SMEM),
            pl.BlockSpec((group, None, None, QK_DIM, tq), lambda b, h, i: (b, h, i, 0, 0)),
            pl.BlockSpec((group, None, nk, tk, QK_DIM), lambda b, h, i: (b, h, 0, 0, 0)),
            pl.BlockSpec((group, None, nk, V_DIM, tk), lambda b, h, i: (b, h, 0, 0, 0)),
            pl.BlockSpec((None, ratio + 1, 1, period), lambda b, h, i: (h, 0, 0, 0)),
            pl.BlockSpec((V_DIM, 1), lambda b, h, i: (0, 0)),
        ],
        out_specs=pl.BlockSpec((group, tq, V_DIM), lambda b, h, i: (b, i, h)),
        out_shape=jax.ShapeDtypeStruct((bsz, seq, width), BF16),
        scratch_shapes=[pltpu.VMEM((group, 2, 1, tq), F32), pltpu.VMEM((group, 2, 1, tq), F32),
                        pltpu.VMEM((group, 2, V_DIM, tq), F32), pltpu.VMEM((ratio + 1, tk, tq), F32)],
        compiler_params=_cparams("parallel", "parallel", "arbitrary"),
        name="prompt_attention",
    )(lam, qt, k4, vt, btab, subln_g.reshape(V_DIM, 1))


def _paged_body(pt_ref, lam_ref, q_ref, kn_ref, vn_ref, bias_ref, g_ref, *refs, n_pages, t_new, lam_init):
    k_refs = refs[:n_pages]
    v_refs = refs[n_pages:2 * n_pages]
    o_ref = refs[2 * n_pages]
    lam = lam_ref[0]
    rows = 2 * t_new
    cols = PAGE_SIZE * N_HEADS
    nt = (((1,), (1,)), ((), ()))
    q = q_ref[0]
    parts = [lax.dot_general(q, k_refs[p][...].astype(BF16), nt, preferred_element_type=F32)
             for p in range(n_pages)]
    parts.append(lax.dot_general(q, kn_ref[0], nt, preferred_element_type=F32))
    s = jnp.concatenate(parts, axis=1) + bias_ref[...]
    m = jnp.max(s, -1, keepdims=True)
    p_un = jnp.exp2(s - m)
    pn = p_un / jnp.sum(p_un, -1, keepdims=True)
    w = (pn - lam * pltpu.roll(pn, N_HEADS * rows - t_new, axis=0)).astype(BF16)
    o = jnp.dot(w[:, n_pages * cols:], vn_ref[0], preferred_element_type=F32)
    for p in range(n_pages):
        o = o + jnp.dot(w[:, p * cols:(p + 1) * cols], v_refs[p][...].astype(BF16), preferred_element_type=F32)
    o = o * lax.rsqrt(jnp.mean(o * o, -1, keepdims=True) + LN_EPS) * g_ref[...]
    o = (o * (1.0 - lam_init)).astype(BF16)
    o_ref[0] = jnp.concatenate([o[h * rows:(h + 1) * rows] for h in range(N_HEADS)], axis=1)


def _sample_attention(q, kb, vb, cache_k, cache_v, page_table, rel_bias, lam, subln_g, lam_init):
    bs, t_new, width = q.shape
    n_pool = cache_k.shape[0]
    n_pages = page_table.shape[1]
    past = n_pages * PAGE_SIZE
    rows = 2 * t_new
    cols = PAGE_SIZE * N_HEADS
    new_pos = LANES // N_HEADS
    assert rows % SUBLANES == 0 and t_new <= new_pos
    qh = q.reshape(bs, t_new, N_HEADS, QK_DIM).transpose(0, 2, 1, 3)
    lane_map = (jnp.arange(QK_DIM) // HEAD_DIM)[None, :] == jnp.arange(2)[:, None]
    q_rows = (qh[:, :, None] * lane_map[None, None, :, None, :].astype(BF16)).reshape(bs, N_HEADS * rows, QK_DIM)
    pad = ((0, 0), (0, LANES - t_new * N_HEADS), (0, 0))
    kn = jnp.pad(kb.reshape(bs, t_new * N_HEADS, QK_DIM), pad)
    vn = jnp.pad(vb.reshape(bs, t_new * N_HEADS, V_DIM), pad)
    qpos = past + jnp.arange(t_new, dtype=I32)
    kpos = jnp.concatenate([jnp.arange(past + t_new, dtype=I32),
                            jnp.full((new_pos - t_new,), past + t_new + new_pos, I32)])
    b = _bias_of(qpos[:, None] - kpos[None, :], rel_bias)
    same = jnp.arange(N_HEADS)[:, None] == jnp.arange(N_HEADS)[None, :]
    b = jnp.where(same[:, None, None, :], b[..., None], NEG_INF)
    b = b.reshape(N_HEADS, 1, t_new, -1)
    bias = jnp.concatenate([b, b], axis=1).reshape(N_HEADS * rows, -1)

    page_spec = lambda j: pl.BlockSpec((None, cols, QK_DIM), lambda b, pt, j=j: (pt[b * n_pages + j], 0, 0))
    per_seq = lambda r, c: pl.BlockSpec((1, r, c), lambda b, pt: (b, 0, 0))
    grid_spec = pltpu.PrefetchScalarGridSpec(
        num_scalar_prefetch=1,
        grid=(bs,),
        in_specs=[
            pl.BlockSpec(memory_space=pltpu.SMEM),
            per_seq(N_HEADS * rows, QK_DIM),
            per_seq(LANES, QK_DIM),
            per_seq(LANES, V_DIM),
            pl.BlockSpec(bias.shape, lambda b, pt: (0, 0)),
            pl.BlockSpec((1, V_DIM), lambda b, pt: (0, 0)),
        ] + [page_spec(j) for j in range(n_pages)] * 2,
        out_specs=pl.BlockSpec((1, rows, width), lambda b, pt: (b, 0, 0)),
    )
    ck = cache_k.reshape(n_pool, cols, QK_DIM)
    cv = cache_v.reshape(n_pool, cols, V_DIM)
    return pl.pallas_call(
        functools.partial(_paged_body, n_pages=n_pages, t_new=t_new, lam_init=lam_init),
        grid_spec=grid_spec,
        out_shape=jax.ShapeDtypeStruct((bs, rows, width), BF16),
        compiler_params=_cparams("parallel"),
        name="sample_attention",
    )(page_table.reshape(-1), lam, q_rows, kn, vn, bias, subln_g, *([ck] * n_pages), *([cv] * n_pages))


def _conv_prompt_body(u_ref, halo_ref, w_ref, cb_ref, g_ref, b_ref, o_ref, buf, shifted, *, tc):
    i = pl.program_id(1)
    halo = halo_ref[0]
    buf[0:CONV_HALO, :] = jnp.where(i > 0, halo, jnp.zeros_like(halo))
    buf[CONV_HALO:, :] = u_ref[0]
    rows = CONV_HALO + tc - SUBLANES
    for s in range(1, SUBLANES):
        shifted[s - 1, 0:rows, :] = buf[s:s + rows, :]
    first = CONV_HALO - (CONV_WIDTH - 1)
    for r0 in range(0, tc, CONV_CHUNK):
        acc = jnp.zeros((CONV_CHUNK, u_ref.shape[2]), F32) + cb_ref[...]
        for j in range(CONV_WIDTH):
            s = (first + j) % SUBLANES
            a = first + j - s + r0
            win = buf[a:a + CONV_CHUNK, :] if s == 0 else shifted[s - 1, a:a + CONV_CHUNK, :]
            acc = acc + w_ref[j:j + 1, :] * win
        y = _layer_norm(acc, g_ref[...], b_ref[...])
        o_ref[0, r0:r0 + CONV_CHUNK, :] = (y * jax.nn.sigmoid(y)).astype(BF16)


def _conv_prompt(u, conv_w, conv_b, ln_g, ln_b):
    bsz, seq, ch = u.shape
    tc = min(TOKEN_TILE, seq)
    assert seq % tc == 0 and tc % CONV_CHUNK == 0 and tc % CONV_HALO == 0
    vec = pl.BlockSpec((1, ch), lambda b, i: (0, 0))
    return pl.pallas_call(
        functools.partial(_conv_prompt_body, tc=tc),
        grid=(bsz, seq // tc),
        in_specs=[
            pl.BlockSpec((1, tc, ch), lambda b, i: (b, i, 0)),
            pl.BlockSpec((1, CONV_HALO, ch), lambda b, i: (b, jnp.maximum(i * (tc // CONV_HALO) - 1, 0), 0)),
            pl.BlockSpec((CONV_WIDTH, ch), lambda b, i: (0, 0)),
            vec, vec, vec,
        ],
        out_specs=pl.BlockSpec((1, tc, ch), lambda b, i: (b, i, 0)),
        out_shape=jax.ShapeDtypeStruct((bsz, seq, ch), BF16),
        scratch_shapes=[pltpu.VMEM((CONV_HALO + tc, ch), F32), pltpu.VMEM((SUBLANES - 1, CONV_HALO + tc, ch), F32)],
        compiler_params=_cparams("parallel", "arbitrary"),
        name="conv_prompt",
    )(u, u, conv_w, conv_b, ln_g, ln_b)


def _conv_sample_body(buf_ref, w_ref, cb_ref, g_ref, b_ref, o_ref, *, t_new):
    for t in range(t_new):
        acc = jnp.zeros(buf_ref.shape[1:], F32) + cb_ref[...]
        for j in range(CONV_WIDTH):
            acc = acc + w_ref[j:j + 1, :] * buf_ref[t + j]
        y = _layer_norm(acc, g_ref[...], b_ref[...])
        o_ref[t] = (y * jax.nn.sigmoid(y)).astype(BF16)


def _conv_sample(buf_t, conv_w, conv_b, ln_g, ln_b):
    rows, bs, ch = buf_t.shape
    t_new = rows - (CONV_WIDTH - 1)
    gb = min(32, bs)
    assert bs % gb == 0
    vec = pl.BlockSpec((1, ch), lambda i: (0, 0))
    return pl.pallas_call(
        functools.partial(_conv_sample_body, t_new=t_new),
        grid=(bs // gb,),
        in_specs=[pl.BlockSpec((rows, gb, ch), lambda i: (0, i, 0)),
                  pl.BlockSpec((CONV_WIDTH, ch), lambda i: (0, 0)), vec, vec, vec],
        out_specs=pl.BlockSpec((t_new, gb, ch), lambda i: (0, i, 0)),
        out_shape=jax.ShapeDtypeStruct((t_new, bs, ch), BF16),
        compiler_params=_cparams("parallel"),
        name="conv_sample",
    )(buf_t, conv_w, conv_b, ln_g, ln_b)


N_MERGE_OUT = 5


def _merge_body(*refs, alpha, n_alias):
    (o_ref, y_ref, sg_ref, x_ref, wa_ref, wc_ref, bc_ref, wo_ref, g1_ref, b1_ref, rw_ref, rb_ref) = refs[:12]
    h1_ref, ti_ref, rk_ref, wcol_ref, cnt_ref = refs[12 + n_alias:]
    tm, d = x_ref.shape
    a = jnp.dot(o_ref[...], wa_ref[...], preferred_element_type=F32)
    b = jnp.dot(y_ref[...], wc_ref[...], preferred_element_type=F32) + bc_ref[...]
    mix_in = sg_ref[:, :d].astype(F32) * a + sg_ref[:, d:].astype(F32) * b
    mix = jnp.dot(mix_in.astype(BF16), wo_ref[...], preferred_element_type=F32)
    h1 = _layer_norm(alpha * x_ref[...] + mix, g1_ref[...], b1_ref[...])
    _store_rows(h1_ref, h1)
    logits = lax.dot_general(rw_ref[...], h1, (((1,), (1,)), ((), ())), preferred_element_type=F32,
                             precision=lax.Precision.HIGHEST) + rb_ref[...]
    n_e = logits.shape[0]
    eid = lax.broadcasted_iota(I32, logits.shape, 0)
    vals, idxs, hots = [], [], []
    for _ in range(TOP_K):
        m = jnp.max(logits, 0, keepdims=True)
        idx = jnp.min(jnp.where(logits == m, eid, n_e), 0, keepdims=True)
        hot = eid == idx
        vals.append(m)
        idxs.append(idx)
        hots.append(hot)
        logits = jnp.where(hot, -jnp.inf, logits)
    ex = [jnp.exp(v - vals[0]) for v in vals]
    den = ex[0]
    for e in ex[1:]:
        den = den + e
    ti_ref[...] = jnp.concatenate(idxs, 0)
    wrows = jnp.concatenate([e / den for e in ex] + [jnp.zeros((LANES - TOP_K, tm), F32)], 0)
    wcol_ref[...] = wrows.T
    sel = hots[0]
    for hot in hots[1:]:
        sel = sel | hot
    sel_f = jnp.where(sel, 1.0, 0.0)
    before = lax.broadcasted_iota(I32, (tm, tm), 0) < lax.broadcasted_iota(I32, (tm, tm), 1)
    upper = jnp.where(before, 1.0, 0.0).astype(BF16)
    ahead = jnp.dot(sel_f.astype(BF16), upper, preferred_element_type=F32)
    rk_ref[...] = jnp.concatenate(
        [jnp.sum(jnp.where(hot, ahead, 0.0), 0, keepdims=True) for hot in hots], 0).astype(I32)
    cnt = jnp.sum(sel_f, 1, keepdims=True).astype(I32)
    cnt_ref[0] = jnp.broadcast_to(cnt, (n_e, LANES))


def _merge(o, yact, sg, x, wts, alpha, n_total, row_off, tm, prev):
    n, d = x.shape
    wa, wc, bc, wo, g1, b1, rwt, rb = wts
    n_e = rwt.shape[0]
    off = row_off // tm
    last = n // tm - 1
    steps = n // tm if prev is not None else n_total // tm
    row = lambda width: pl.BlockSpec((tm, width), lambda i: (jnp.minimum(i, last), 0))
    full = lambda a: pl.BlockSpec(a.shape, lambda i: (0, 0))
    in_specs = [row(o.shape[1]), row(yact.shape[1]), row(2 * d), row(d),
                full(wa), full(wc), full(bc), full(wo), full(g1), full(b1), full(rwt), full(rb)]
    args = [o, yact, sg, x, wa, wc, bc, wo, g1, b1, rwt, rb]
    aliases = {}
    if prev is not None:
        in_specs += [pl.BlockSpec(memory_space=pl.ANY)] * N_MERGE_OUT
        aliases = {len(args) + j: j for j in range(N_MERGE_OUT)}
        args += list(prev)
    slot_major = pl.BlockSpec((TOP_K, tm), lambda i: (0, off + i))
    return pl.pallas_call(
        functools.partial(_merge_body, alpha=alpha, n_alias=0 if prev is None else N_MERGE_OUT),
        grid=(steps,),
        in_specs=in_specs,
        out_specs=(pl.BlockSpec((tm * SUBLANES, LANES), lambda i: (off + i, 0)), slot_major, slot_major,
                   pl.BlockSpec((tm, LANES), lambda i: (off + i, 0)),
                   pl.BlockSpec((1, n_e, LANES), lambda i: (off + i, 0, 0))),
        out_shape=(jax.ShapeDtypeStruct((n_total * SUBLANES, LANES), F32),
                   jax.ShapeDtypeStruct((TOP_K, n_total), I32),
                   jax.ShapeDtypeStruct((TOP_K, n_total), I32),
                   jax.ShapeDtypeStruct((n_total, LANES), F32),
                   jax.ShapeDtypeStruct((n_total // tm, n_e, LANES), I32)),
        input_output_aliases=aliases,
        compiler_params=_cparams("parallel"),
        name="merge_ln1_router",
    )(*args)


def _plan(topi, rank, cnt, tm_tok, tm_moe):
    k, n = topi.shape
    n_e = cnt.shape[1]
    before = jnp.cumsum(cnt, 0) - cnt
    total = jnp.sum(cnt, 0)
    ntile_e = (total + tm_moe - 1) // tm_moe
    tend = jnp.cumsum(ntile_e)
    tstart = tend - ntile_e
    base = (tstart * tm_moe)[None, :] + before
    base_tok = jnp.repeat(base, tm_tok, axis=0)
    hot = topi[:, :, None] == jnp.arange(n_e, dtype=I32)[None, None, :]
    pos = (jnp.sum(jnp.where(hot, base_tok[None], 0), -1).astype(I32) + rank) * SUBLANES
    n_tiles = (k * n + tm_moe - 1) // tm_moe + n_e
    n_used = tend[-1]
    tid = jnp.arange(n_tiles, dtype=I32)
    src = jnp.minimum(tid, n_used - 1)
    te = jnp.minimum(jnp.sum((src[:, None] >= tend[None, :]).astype(I32), 1), n_e - 1)
    first = (tid == tstart[te]).astype(I32)
    half = (total[te] - (tid - tstart[te]) * tm_moe <= tm_moe // 2).astype(I32)
    slot = (jnp.cumsum(first) - 1) % 2
    nxt = jnp.where(tend[te] < n_used, te[jnp.minimum(tend[te], n_tiles - 1)], -1)
    sched = jnp.stack([te, first, half, slot, nxt]).astype(I32)
    last_tile = jnp.where(ntile_e > 0, tend - 1, -1).astype(I32)
    return pos, sched, n_used.reshape(1).astype(I32), last_tile, n_tiles


def _dispatch_body(lt_ref, nu_ref, pos_ref, x_ref, xs_hbm, zbuf, zsem, sem, *, tm, tm_moe, n_e, n_tiles):
    i = pl.program_id(0)

    def zero_copy(tile):
        start = pl.multiple_of(tile * (tm_moe * SUBLANES), tm_moe * SUBLANES)
        return pltpu.make_async_copy(zbuf, xs_hbm.at[pl.ds(start, tm_moe * SUBLANES), :], zsem)

    @pl.when(i == 0)
    def _():
        zbuf[...] = jnp.zeros(zbuf.shape, F32)

        def z_start(e, c):
            @pl.when(lt_ref[e] >= 0)
            def _():
                zero_copy(lt_ref[e]).start()
            return c

        def z_wait(e, c):
            @pl.when(lt_ref[e] >= 0)
            def _():
                zero_copy(lt_ref[e]).wait()
            return c

        def t_start(t, c):
            zero_copy(t).start()
            return c

        def t_wait(t, c):
            zero_copy(t).wait()
            return c

        lax.fori_loop(0, n_e, z_start, 0)
        lax.fori_loop(nu_ref[0], n_tiles, t_start, 0)
        lax.fori_loop(0, n_e, z_wait, 0)
        lax.fori_loop(nu_ref[0], n_tiles, t_wait, 0)

    def issue(g, c):
        base = pl.multiple_of(g * SUBLANES, SUBLANES)
        for rr in range(SUBLANES):
            src = x_ref.at[pl.ds(pl.multiple_of((base + rr) * SUBLANES, SUBLANES), SUBLANES), :]
            for k in range(TOP_K):
                dst = pl.multiple_of(pos_ref[k, base + rr], SUBLANES)
                pltpu.make_async_copy(src, xs_hbm.at[pl.ds(dst, SUBLANES), :], sem).start()
        return c

    lax.fori_loop(0, tm // SUBLANES, issue, 0)
    for k in range(TOP_K):
        pltpu.make_async_copy(x_ref, xs_hbm.at[pl.ds(0, tm * SUBLANES), :], sem).wait()


def _dispatch(h1, pos, last_tile, n_used, n_tiles, tm, tm_moe):
    n = h1.shape[0] // SUBLANES
    n_e = last_tile.shape[0]
    grid_spec = pltpu.PrefetchScalarGridSpec(
        num_scalar_prefetch=2,
        grid=(n // tm,),
        in_specs=[pl.BlockSpec((TOP_K, tm), lambda i, lt, nu: (0, i), memory_space=pltpu.SMEM),
                  pl.BlockSpec((tm * SUBLANES, LANES), lambda i, lt, nu: (i, 0))],
        out_specs=pl.BlockSpec(memory_space=pl.ANY),
        scratch_shapes=[pltpu.VMEM((tm_moe * SUBLANES, LANES), F32), pltpu.SemaphoreType.DMA,
                        pltpu.SemaphoreType.DMA],
    )
    return pl.pallas_call(
        functools.partial(_dispatch_body, tm=tm, tm_moe=tm_moe, n_e=n_e, n_tiles=n_tiles),
        grid_spec=grid_spec,
        out_shape=jax.ShapeDtypeStruct((n_tiles * tm_moe * SUBLANES, LANES), F32),
        compiler_params=_cparams("arbitrary"),
        name="moe_dispatch",
    )(last_tile, n_used, pos, h1)


TE, FIRST, HALF, SLOT, NEXT = range(5)


def _experts_body(sch_ref, nu_ref, x_ref, w1_hbm, b1_ref, w2_hbm, b2_ref, y_ref, w1s, w2s, w1b, w2b, sems, *, tm):
    i = pl.program_id(0)
    d_ff = w2b.shape[0]

    def fetch(expert, slot):
        return (pltpu.make_async_copy(w1_hbm.at[expert], w1s.at[slot], sems.at[0, slot]),
                pltpu.make_async_copy(w2_hbm.at[expert], w2s.at[slot], sems.at[1, slot]))

    def run(rows):
        x = _load_rows(x_ref, rows).astype(BF16)
        hh = jnp.dot(x, w1b[...], preferred_element_type=F32) + b1_ref[0]
        g = jnp.minimum(hh[:, :d_ff], SWIGLU_LIMIT)
        u = jnp.clip(hh[:, d_ff:], -SWIGLU_LIMIT, SWIGLU_LIMIT)
        act = (u + 1.0) * g * jax.nn.sigmoid(SWIGLU_ALPHA * g)
        _store_rows(y_ref, jnp.dot(act.astype(BF16), w2b[...], preferred_element_type=F32) + b2_ref[0])

    @pl.when(i == 0)
    def _():
        for cp in fetch(sch_ref[TE, 0], 0):
            cp.start()

    @pl.when(i >= nu_ref[0])
    def _():
        y_ref[...] = jnp.zeros(y_ref.shape, F32)

    @pl.when(i < nu_ref[0])
    def _():
        @pl.when(sch_ref[FIRST, i] == 1)
        def _():
            slot = sch_ref[SLOT, i]

            @pl.when(sch_ref[NEXT, i] >= 0)
            def _():
                for cp in fetch(sch_ref[NEXT, i], 1 - slot):
                    cp.start()

            for cp in fetch(sch_ref[TE, i], slot):
                cp.wait()
            w1b[...] = w1s[slot].astype(BF16)
            w2b[...] = w2s[slot].astype(BF16)

        @pl.when(sch_ref[HALF, i] == 0)
        def _():
            run(tm)

        @pl.when(sch_ref[HALF, i] == 1)
        def _():
            run(tm // 2)
            y_ref[pl.ds(tm // 2 * SUBLANES, tm // 2 * SUBLANES), :] = jnp.zeros((tm // 2 * SUBLANES, LANES), F32)


def _experts(xs, sched, n_used, w1, b1, w2, b2, tm):
    n_e, d, f2 = w1.shape
    rows = tm * SUBLANES
    by_tile = lambda i, sch, nu: (i, 0)
    by_expert = lambda i, sch, nu: (sch[TE, i], 0, 0)
    grid_spec = pltpu.PrefetchScalarGridSpec(
        num_scalar_prefetch=2,
        grid=(xs.shape[0] // rows,),
        in_specs=[
            pl.BlockSpec((rows, LANES), by_tile),
            pl.BlockSpec(memory_space=pl.ANY),
            pl.BlockSpec((1, 1, f2), by_expert),
            pl.BlockSpec(memory_space=pl.ANY),
            pl.BlockSpec((1, 1, d), by_expert),
        ],
        out_specs=pl.BlockSpec((rows, LANES), by_tile),
        scratch_shapes=[pltpu.VMEM((2, d, f2), F32), pltpu.VMEM((2, f2 // 2, d), F32),
                        pltpu.VMEM((d, f2), BF16), pltpu.VMEM((f2 // 2, d), BF16), pltpu.SemaphoreType.DMA((2, 2))],
    )
    return pl.pallas_call(
        functools.partial(_experts_body, tm=tm),
        grid_spec=grid_spec,
        out_shape=jax.ShapeDtypeStruct(xs.shape, F32),
        compiler_params=_cparams("arbitrary"),
        name="moe_experts",
    )(sched, n_used, xs, w1, b1.reshape(n_e, 1, f2), w2, b2.reshape(n_e, 1, d))


def _combine_body(pos_ref, nxt_ref, h1_ref, wcol_ref, ys_hbm, g_ref, b_ref, o_ref, ybuf, sems, *, alpha, tm):
    i = pl.program_id(0)
    n = pl.num_programs(0)

    def fetch(p_ref, slot):
        def issue(g, c):
            base = pl.multiple_of(g * SUBLANES, SUBLANES)
            for rr in range(SUBLANES):
                dst = pl.ds(pl.multiple_of((base + rr) * SUBLANES, SUBLANES), SUBLANES)
                for k in range(TOP_K):
                    src = pl.multiple_of(p_ref[k, base + rr], SUBLANES)
                    pltpu.make_async_copy(ys_hbm.at[pl.ds(src, SUBLANES), :], ybuf.at[slot, k, dst, :],
                                          sems.at[slot]).start()
            return c
        lax.fori_loop(0, tm // SUBLANES, issue, 0)

    @pl.when(i == 0)
    def _():
        fetch(pos_ref, 0)

    @pl.when(i + 1 < n)
    def _():
        fetch(nxt_ref, (i + 1) % 2)

    slot = i % 2
    for k in range(TOP_K):
        pltpu.make_async_copy(ys_hbm.at[pl.ds(0, tm * SUBLANES), :], ybuf.at[slot, k], sems.at[slot]).wait()
    ff = wcol_ref[:, 0:1] * _load_rows(ybuf.at[slot, 0], tm)
    for k in range(1, TOP_K):
        ff = ff + wcol_ref[:, k:k + 1] * _load_rows(ybuf.at[slot, k], tm)
    o_ref[...] = _layer_norm(alpha * _load_rows(h1_ref, tm) + ff, g_ref[...], b_ref[...])


def _combine(h1, wcol, pos, ys, g, b, alpha, row_off, n):
    d = g.shape[1]
    tm = min(COMBINE_TILE, n)
    off = row_off // tm
    last = n // tm - 1
    vec = pl.BlockSpec((1, d), lambda i: (0, 0))
    return pl.pallas_call(
        functools.partial(_combine_body, alpha=alpha, tm=tm),
        grid=(n // tm,),
        in_specs=[pl.BlockSpec((TOP_K, tm), lambda i: (0, off + i), memory_space=pltpu.SMEM),
                  pl.BlockSpec((TOP_K, tm), lambda i: (0, off + jnp.minimum(i + 1, last)), memory_space=pltpu.SMEM),
                  pl.BlockSpec((tm * SUBLANES, LANES), lambda i: (off + i, 0)),
                  pl.BlockSpec((tm, LANES), lambda i: (off + i, 0)),
                  pl.BlockSpec(memory_space=pl.ANY), vec, vec],
        out_specs=pl.BlockSpec((tm, d), lambda i: (i, 0)),
        out_shape=jax.ShapeDtypeStruct((n, d), F32),
        scratch_shapes=[pltpu.VMEM((2, TOP_K, tm * SUBLANES, LANES), F32), pltpu.SemaphoreType.DMA((2,))],
        compiler_params=_cparams("arbitrary"),
        name="combine_ln2",
    )(pos, pos, h1, wcol, ys, g, b)


def _row2(v):
    return v.reshape(1, -1).astype(F32)


def kernel(x_prompt, x_sample, cache_k, cache_v, page_table, state_conv, w_in, b_in, lambda_q1, lambda_k1,
           lambda_q2, lambda_k2, subln_g, rel_bias, w_attn_proj, conv_w, conv_b, conv_ln_g, conv_ln_b,
           w_conv_proj, b_conv_proj, w_out, ln1_g, ln1_b, router_w, router_b, expert_w1, expert_b1,
           expert_w2, expert_b2, ln2_g, ln2_b):
    depth = w_in.shape[0]
    bp, seq, d = x_prompt.shape
    bs, t_new, _ = x_sample.shape
    d_att = N_HEADS * QK_DIM
    d_conv = conv_w.shape[2]
    n_p, n_s = bp * seq, bs * t_new
    n_tot = n_p + n_s
    tm = min(TOKEN_TILE, math.gcd(n_p, n_s))
    assert tm % LANES == 0
    tm_moe = min(MOE_TILE, n_tot)
    alpha = (2 * depth) ** 0.25

    hp = x_prompt.reshape(n_p, d)
    hs = x_sample.reshape(n_s, d)
    outs = [[] for _ in range(6)]
    for l in range(depth):
        lam_init = 0.8 - 0.6 * math.exp(-0.3 * l)
        lam = (jnp.exp(jnp.sum(lambda_q1[l].astype(F32) * lambda_k1[l].astype(F32)))
               - jnp.exp(jnp.sum(lambda_q2[l].astype(F32) * lambda_k2[l].astype(F32))) + lam_init).reshape(1)
        w_in_bf = w_in[l].astype(BF16)
        b_in_l = _row2(b_in[l])
        g_sub = _row2(subln_g[l])
        conv_args = (conv_w[l].astype(F32), _row2(conv_b[l]), _row2(conv_ln_g[l]), _row2(conv_ln_b[l]))
        merge_w = (w_attn_proj[l].astype(BF16), w_conv_proj[l].astype(BF16), _row2(b_conv_proj[l]),
                   w_out[l].astype(BF16), _row2(ln1_g[l]), _row2(ln1_b[l]),
                   router_w[l].astype(F32).T, router_b[l].astype(F32).reshape(-1, 1))

        qp, kp, vp, kbp, vbp, up, sgp = _in_proj(hp, w_in_bf, b_in_l, d_att, d_conv, (bp, seq) + _attn_tiles(seq))
        op = _prompt_attention(qp, kbp, vbp, rel_bias, lam, g_sub, lam_init)
        up3 = up.reshape(bp, seq, d_conv)
        yp = _conv_prompt(up3, *conv_args)
        merged = _merge(op.reshape(n_p, d_att), yp.reshape(n_p, d_conv), sgp, hp, merge_w, alpha, n_tot, 0, tm, None)

        qs, ks, vs, kbs, vbs, us, sgs = _in_proj(hs, w_in_bf, b_in_l, d_att, d_conv)
        osr = _sample_attention(qs.reshape(bs, t_new, d_att), kbs.reshape(bs, t_new, d_att),
                                vbs.reshape(bs, t_new, d_att), cache_k[l], cache_v[l], page_table,
                                rel_bias, lam, g_sub, lam_init)
        os_ = osr[:, :t_new].reshape(n_s, d_att)
        buf_s = jnp.concatenate([state_conv[l].astype(F32), us.reshape(bs, t_new, d_conv)], axis=1)
        ys = _conv_sample(buf_s.transpose(1, 0, 2), *conv_args).transpose(1, 0, 2).reshape(n_s, d_conv)
        h1, topi, rank, wcol, cnt = _merge(os_, ys, sgs, hs, merge_w, alpha, n_tot, n_p, tm, merged)

        pos, sched, n_used, last_tile, n_tiles = _plan(topi, rank, cnt[:, :, 0], tm, tm_moe)
        xs = _dispatch(h1, pos, last_tile, n_used, n_tiles, tm, tm_moe)
        ysort = _experts(xs, sched, n_used, expert_w1[l], expert_b1[l].astype(F32),
                         expert_w2[l], expert_b2[l].astype(F32), tm_moe)
        g2, b2 = _row2(ln2_g[l]), _row2(ln2_b[l])
        hp = _combine(h1, wcol, pos, ysort, g2, b2, alpha, 0, n_p)
        hs = _combine(h1, wcol, pos, ysort, g2, b2, alpha, n_p, n_s)

        w1 = CONV_WIDTH - 1
        cp = up3[:, seq - w1:] if seq >= w1 else jnp.concatenate(
            [jnp.zeros((bp, w1 - seq, d_conv), F32), up3], axis=1)
        for lst, val in zip(outs, (kp.reshape(bp, seq, N_HEADS, QK_DIM), vp.reshape(bp, seq, N_HEADS, V_DIM), cp,
                                   ks.reshape(bs, t_new, N_HEADS, QK_DIM), vs.reshape(bs, t_new, N_HEADS, V_DIM),
                                   buf_s[:, t_new:])):
            lst.append(val)
    return (hp.reshape(bp, seq, d), hs.reshape(bs, t_new, d)) + tuple(jnp.stack(o) for o in outs)
```

```python
import functools
import math

import jax
import jax.numpy as jnp
from jax import lax
from jax.experimental import pallas as pl
from jax.experimental.pallas import tpu as pltpu

F32 = jnp.float32
BF16 = jnp.bfloat16
I32 = jnp.int32

N_HEADS = 4
HEAD_DIM = 64
QK_DIM = 2 * HEAD_DIM
V_DIM = 2 * HEAD_DIM
ATTN_SCALE = HEAD_DIM ** -0.5
LOG2E = 1.4426950408889634
NEG_INF = -1e30
NUM_BUCKETS = 32
MAX_EXACT = NUM_BUCKETS // 2
MAX_DISTANCE = 128
CONV_WIDTH = 31
TOP_K = 4
SWIGLU_LIMIT = 7.0
SWIGLU_ALPHA = 1.702
LN_EPS = 1e-5
PAGE_SIZE = 128

LANES = 128
SUBLANES = 8
VMEM_LIMIT = 56 * 1024 * 1024

TOKEN_TILE = 512
ATTN_TQ = 1024
ATTN_TK = 1024
ATTN_GROUP = 2
MOE_TILE = 512
COMBINE_TILE = 256
CONV_HALO = 32
CONV_CHUNK = 64


def _cparams(*sem):
    return pltpu.CompilerParams(dimension_semantics=sem, vmem_limit_bytes=VMEM_LIMIT)


def _load_rows(ref, tm):
    return jnp.concatenate([ref[pl.ds(c, tm, stride=SUBLANES), :] for c in range(SUBLANES)], axis=1)


def _store_rows(ref, x):
    tm = x.shape[0]
    for c in range(SUBLANES):
        ref[pl.ds(c, tm, stride=SUBLANES), :] = x[:, c * LANES:(c + 1) * LANES]


def _layer_norm(x, g, b):
    mu = jnp.mean(x, -1, keepdims=True)
    xc = x - mu
    var = jnp.mean(xc * xc, -1, keepdims=True)
    return xc * lax.rsqrt(var + LN_EPS) * g + b


def _in_proj_body(x_ref, w_ref, b_ref, q_ref, k_ref, v_ref, kb_ref, vb_ref, u_ref, sg_ref, *, d_att, d_conv, by_head):
    x = x_ref[...].astype(BF16)

    def seg(lo, hi):
        return jnp.dot(x, w_ref[:, lo:hi], preferred_element_type=F32) + b_ref[:, lo:hi]

    def head(a, h):
        return a[:, h * QK_DIM:(h + 1) * QK_DIM]

    o = 0
    q = seg(o, o + d_att) * (ATTN_SCALE * LOG2E)
    o += d_att
    k = seg(o, o + d_att)
    o += d_att
    v = seg(o, o + d_att)
    o += d_att
    tm = x.shape[0]
    for h in range(N_HEADS):
        k_ref[pl.ds(h, tm, stride=N_HEADS), :] = head(k, h)
        v_ref[pl.ds(h, tm, stride=N_HEADS), :] = head(v, h)
    if by_head:
        for h in range(N_HEADS):
            q_ref[h] = head(q, h).T.astype(BF16)
            kb_ref[h] = head(k, h).astype(BF16)
            vb_ref[h] = head(v, h).T.astype(BF16)
    else:
        q_ref[...] = q.astype(BF16)
        kb_ref[...] = k.astype(BF16)
        vb_ref[...] = v.astype(BF16)
    c = seg(o, o + 2 * d_conv)
    u_ref[...] = c[:, :d_conv] * jax.nn.sigmoid(c[:, d_conv:])
    o += 2 * d_conv
    sg_ref[...] = jax.nn.sigmoid(seg(o, w_ref.shape[1])).astype(BF16)


def _in_proj(x, w_bf, b, d_att, d_conv, attn_tiles=None):
    n, d = x.shape
    tm = min(TOKEN_TILE, n)
    row = lambda width: pl.BlockSpec((tm, width), lambda i: (i, 0))
    full = lambda a: pl.BlockSpec(a.shape, lambda i: (0, 0))
    f32_out = lambda width: jax.ShapeDtypeStruct((n, width), F32)
    kv_shape = jax.ShapeDtypeStruct((n * N_HEADS, QK_DIM), F32)
    kv_spec = pl.BlockSpec((tm * N_HEADS, QK_DIM), lambda i: (i, 0))
    if attn_tiles is None:
        qkv_shapes = [jax.ShapeDtypeStruct((n, d_att), BF16)] * 3
        qkv_specs = [row(d_att)] * 3
    else:
        bsz, seq, tq, tk = attn_tiles
        assert seq % tm == 0 and tq % tm == 0 and tk % tm == 0
        per_seq = seq // tm

        def spec(t, transposed):
            parts = t // tm
            blk = (None, N_HEADS, None, QK_DIM, tm) if transposed else (None, N_HEADS, None, tm, QK_DIM)

            def index(i):
                ti = i % per_seq
                tile, part = ti // parts, ti % parts
                return (i // per_seq, 0, tile, 0, part) if transposed else (i // per_seq, 0, tile, part, 0)
            return pl.BlockSpec(blk, index)

        qkv_shapes = [jax.ShapeDtypeStruct((bsz, N_HEADS, seq // tq, QK_DIM, tq), BF16),
                      jax.ShapeDtypeStruct((bsz, N_HEADS, seq // tk, tk, QK_DIM), BF16),
                      jax.ShapeDtypeStruct((bsz, N_HEADS, seq // tk, V_DIM, tk), BF16)]
        qkv_specs = [spec(tq, True), spec(tk, False), spec(tk, True)]
    out_shape = (qkv_shapes[0], kv_shape, kv_shape, qkv_shapes[1], qkv_shapes[2],
                 f32_out(d_conv), jax.ShapeDtypeStruct((n, 2 * d), BF16))
    return pl.pallas_call(
        functools.partial(_in_proj_body, d_att=d_att, d_conv=d_conv, by_head=attn_tiles is not None),
        grid=(n // tm,),
        in_specs=[row(d), full(w_bf), full(b)],
        out_specs=(qkv_specs[0], kv_spec, kv_spec, qkv_specs[1], qkv_specs[2], row(d_conv), row(2 * d)),
        out_shape=out_shape,
        compiler_params=_cparams("parallel"),
        name="in_proj",
    )(x, w_bf, b)


def _t5_bucket(dist):
    n = jnp.maximum(dist, 0)
    nf = jnp.maximum(n, 1).astype(F32)
    large = MAX_EXACT + (jnp.log(nf / MAX_EXACT) / math.log(MAX_DISTANCE / MAX_EXACT)
                         * (NUM_BUCKETS - MAX_EXACT)).astype(I32)
    large = jnp.minimum(large, NUM_BUCKETS - 1)
    return jnp.where(n < MAX_EXACT, n, large)


def _bias_of(dist, rel_bias, shift=None):
    b = rel_bias[_t5_bucket(dist)].astype(F32)
    if shift is not None:
        b = b - shift
    b = jnp.where((dist >= 0)[..., None], b * LOG2E, NEG_INF)
    return jnp.moveaxis(b, -1, 0)


def _flash_body(lam_ref, qt_ref, k_ref, vt_ref, btab_ref, g_ref, o_ref, m_s, l_s, acc_s, bias_ref, *, ratio, group,
                lam_init):
    i = pl.program_id(2)
    tk, tq = bias_ref.shape[1:]

    @pl.when(i == 0)
    def _():
        for o in range(ratio + 1):
            table = jnp.broadcast_to(btab_ref[o], (tk, btab_ref.shape[2]))
            bias_ref[o] = pltpu.roll(table, 0, 1, stride=1, stride_axis=0)[:, :tq]

    sub = lax.broadcasted_iota(I32, (QK_DIM, 1), 0)
    qs = []
    for b in range(group):
        qt = qt_ref[b]
        zero = jnp.zeros_like(qt)
        qs.append((jnp.where(sub < HEAD_DIM, qt, zero), jnp.where(sub >= HEAD_DIM, qt, zero)))
    m_s[...] = jnp.full(m_s.shape, NEG_INF, F32)
    l_s[...] = jnp.zeros(l_s.shape, F32)
    acc_s[...] = jnp.zeros(acc_s.shape, F32)

    def step(j, bias, keys=slice(None), cols=slice(None)):
        for b in range(group):
            k = k_ref[b, j, keys, :]
            vt = vt_ref[b, j, :, keys]
            for c in range(2):
                s = jnp.dot(k, qs[b][c][:, cols], preferred_element_type=F32)
                if bias is not None:
                    s = s + bias
                m_old = m_s[b, c, :, cols]
                m_new = jnp.maximum(m_old, jnp.max(s, 0, keepdims=True))
                alpha = jnp.exp2(m_old - m_new)
                p = jnp.exp2(s - m_new)
                l_s[b, c, :, cols] = alpha * l_s[b, c, :, cols] + jnp.sum(p, 0, keepdims=True)
                acc_s[b, c, :, cols] = alpha * acc_s[b, c, :, cols] + jnp.dot(
                    vt, p.astype(BF16), preferred_element_type=F32)
                m_s[b, c, :, cols] = m_new

    def far_step(j, carry):
        step(j, None)
        return carry

    lax.fori_loop(0, jnp.maximum(i * ratio - 1, 0), far_step, 0)

    @pl.when(i >= 1)
    def _():
        step(i * ratio - 1, bias_ref[0])

    for o in range(ratio - 1):
        step(i * ratio + o, bias_ref[o + 1])
    last, half, first_col = i * ratio + ratio - 1, tk // 2, tq - tk // 2
    step(last, bias_ref[ratio, 0:half, :], keys=slice(0, half))
    step(last, bias_ref[ratio, half:tk, first_col:tq], keys=slice(half, tk), cols=slice(first_col, tq))

    for b in range(group):
        o = acc_s[b, 0] / l_s[b, 0] - lam_ref[0] * (acc_s[b, 1] / l_s[b, 1])
        o = o * lax.rsqrt(jnp.mean(o * o, 0, keepdims=True) + LN_EPS) * g_ref[...]
        o_ref[b] = (o * (1.0 - lam_init)).T.astype(BF16)


def _attn_tiles(seq):
    tq = min(ATTN_TQ, seq)
    tk = min(ATTN_TK, tq)
    assert tk >= MAX_DISTANCE and seq % tq == 0 and tq % tk == 0
    return tq, tk


def _prompt_attention(qt, k4, vt, rel_bias, lam, subln_g, lam_init):
    bsz, _, nq, _, tq = qt.shape
    nk, tk = k4.shape[2], k4.shape[3]
    seq, ratio, width = nq * tq, tq // tk, N_HEADS * V_DIM
    group = ATTN_GROUP if bsz % ATTN_GROUP == 0 else 1
    far = rel_bias[NUM_BUCKETS - 1].astype(F32)
    period = tq + tk
    assert period % LANES == 0
    y = jnp.arange(period, dtype=I32)
    r_minus_c = jnp.where(y < tq, y, y - period)
    dist = r_minus_c[None, :] - (jnp.arange(ratio + 1, dtype=I32)[:, None] - 1) * tk
    btab = _bias_of(dist, rel_bias, far).reshape(N_HEADS, ratio + 1, 1, period)
    return pl.pallas_call(
        functools.partial(_flash_body, ratio=ratio, group=group, lam_init=lam_init),
        grid=(bsz // group, N_HEADS, nq),
        in_specs=[
            pl.BlockSpec(memory_space=pltpu.SMEM),
            pl.BlockSpec((group, None, None, QK_DIM, tq), lambda b, h, i: (b, h, i, 0, 0)),
            pl.BlockSpec((group, None, nk, tk, QK_DIM), lambda b, h, i: (b, h, 0, 0, 0)),
            pl.BlockSpec((group, None, nk, V_DIM, tk), lambda b, h, i: (b, h, 0, 0, 0)),
            pl.BlockSpec((None, ratio + 1, 1, period), lambda b, h, i: (h, 0, 0, 0)),
            pl.BlockSpec((V_DIM, 1), lambda b, h, i: (0, 0)),
        ],
        out_specs=pl.BlockSpec((group, tq, V_DIM), lambda b, h, i: (b, i, h)),
        out_shape=jax.ShapeDtypeStruct((bsz, seq, width), BF16),
        scratch_shapes=[pltpu.VMEM((group, 2, 1, tq), F32), pltpu.VMEM((group, 2, 1, tq), F32),
                        pltpu.VMEM((group, 2, V_DIM, tq), F32), pltpu.VMEM((ratio + 1, tk, tq), F32)],
        compiler_params=_cparams("parallel", "parallel", "arbitrary"),
        name="prompt_attention",
    )(lam, qt, k4, vt, btab, subln_g.reshape(V_DIM, 1))


def _paged_body(pt_ref, lam_ref, q_ref, kn_ref, vn_ref, bias_ref, g_ref, *refs, n_pages, t_new, lam_init):
    k_refs = refs[:n_pages]
    v_refs = refs[n_pages:2 * n_pages]
    o_ref = refs[2 * n_pages]
    lam = lam_ref[0]
    rows = 2 * t_new
    cols = PAGE_SIZE * N_HEADS
    nt = (((1,), (1,)), ((), ()))
    q = q_ref[0]
    parts = [lax.dot_general(q, k_refs[p][...].astype(BF16), nt, preferred_element_type=F32)
             for p in range(n_pages)]
    parts.append(lax.dot_general(q, kn_ref[0], nt, preferred_element_type=F32))
    s = jnp.concatenate(parts, axis=1) + bias_ref[...]
    m = jnp.max(s, -1, keepdims=True)
    p_un = jnp.exp2(s - m)
    pn = p_un / jnp.sum(p_un, -1, keepdims=True)
    w = (pn - lam * pltpu.roll(pn, N_HEADS * rows - t_new, axis=0)).astype(BF16)
    o = jnp.dot(w[:, n_pages * cols:], vn_ref[0], preferred_element_type=F32)
    for p in range(n_pages):
        o = o + jnp.dot(w[:, p * cols:(p + 1) * cols], v_refs[p][...].astype(BF16), preferred_element_type=F32)
    o = o * lax.rsqrt(jnp.mean(o * o, -1, keepdims=True) + LN_EPS) * g_ref[...]
    o = (o * (1.0 - lam_init)).astype(BF16)
    o_ref[0] = jnp.concatenate([o[h * rows:(h + 1) * rows] for h in range(N_HEADS)], axis=1)


def _sample_attention(q, kb, vb, cache_k, cache_v, page_table, rel_bias, lam, subln_g, lam_init):
    bs, t_new, width = q.shape
    n_pool = cache_k.shape[0]
    n_pages = page_table.shape[1]
    past = n_pages * PAGE_SIZE
    rows = 2 * t_new
    cols = PAGE_SIZE * N_HEADS
    new_pos = LANES // N_HEADS
    assert rows % SUBLANES == 0 and t_new <= new_pos
    qh = q.reshape(bs, t_new, N_HEADS, QK_DIM).transpose(0, 2, 1, 3)
    lane_map = (jnp.arange(QK_DIM) // HEAD_DIM)[None, :] == jnp.arange(2)[:, None]
    q_rows = (qh[:, :, None] * lane_map[None, None, :, None, :].astype(BF16)).reshape(bs, N_HEADS * rows, QK_DIM)
    pad = ((0, 0), (0, LANES - t_new * N_HEADS), (0, 0))
    kn = jnp.pad(kb.reshape(bs, t_new * N_HEADS, QK_DIM), pad)
    vn = jnp.pad(vb.reshape(bs, t_new * N_HEADS, V_DIM), pad)
    qpos = past + jnp.arange(t_new, dtype=I32)
    kpos = jnp.concatenate([jnp.arange(past + t_new, dtype=I32),
                            jnp.full((new_pos - t_new,), past + t_new + new_pos, I32)])
    b = _bias_of(qpos[:, None] - kpos[None, :], rel_bias)
    same = jnp.arange(N_HEADS)[:, None] == jnp.arange(N_HEADS)[None, :]
    b = jnp.where(same[:, None, None, :], b[..., None], NEG_INF)
    b = b.reshape(N_HEADS, 1, t_new, -1)
    bias = jnp.concatenate([b, b], axis=1).reshape(N_HEADS * rows, -1)

    page_spec = lambda j: pl.BlockSpec((None, cols, QK_DIM), lambda b, pt, j=j: (pt[b * n_pages + j], 0, 0))
    per_seq = lambda r, c: pl.BlockSpec((1, r, c), lambda b, pt: (b, 0, 0))
    grid_spec = pltpu.PrefetchScalarGridSpec(
        num_scalar_prefetch=1,
        grid=(bs,),
        in_specs=[
            pl.BlockSpec(memory_space=pltpu.SMEM),
            per_seq(N_HEADS * rows, QK_DIM),
            per_seq(LANES, QK_DIM),
            per_seq(LANES, V_DIM),
            pl.BlockSpec(bias.shape, lambda b, pt: (0, 0)),
            pl.BlockSpec((1, V_DIM), lambda b, pt: (0, 0)),
        ] + [page_spec(j) for j in range(n_pages)] * 2,
        out_specs=pl.BlockSpec((1, rows, width), lambda b, pt: (b, 0, 0)),
    )
    ck = cache_k.reshape(n_pool, cols, QK_DIM)
    cv = cache_v.reshape(n_pool, cols, V_DIM)
    return pl.pallas_call(
        functools.partial(_paged_body, n_pages=n_pages, t_new=t_new, lam_init=lam_init),
        grid_spec=grid_spec,
        out_shape=jax.ShapeDtypeStruct((bs, rows, width), BF16),
        compiler_params=_cparams("parallel"),
        name="sample_attention",
    )(page_table.reshape(-1), lam, q_rows, kn, vn, bias, subln_g, *([ck] * n_pages), *([cv] * n_pages))


def _conv_prompt_body(u_ref, halo_ref, w_ref, cb_ref, g_ref, b_ref, o_ref, buf, shifted, *, tc):
    i = pl.program_id(1)
    halo = halo_ref[0]
    buf[0:CONV_HALO, :] = jnp.where(i > 0, halo, jnp.zeros_like(halo))
    buf[CONV_HALO:, :] = u_ref[0]
    rows = CONV_HALO + tc - SUBLANES
    for s in range(1, SUBLANES):
        shifted[s - 1, 0:rows, :] = buf[s:s + rows, :]
    first = CONV_HALO - (CONV_WIDTH - 1)
    for r0 in range(0, tc, CONV_CHUNK):
        acc = jnp.zeros((CONV_CHUNK, u_ref.shape[2]), F32) + cb_ref[...]
        for j in range(CONV_WIDTH):
            s = (first + j) % SUBLANES
            a = first + j - s + r0
            win = buf[a:a + CONV_CHUNK, :] if s == 0 else shifted[s - 1, a:a + CONV_CHUNK, :]
            acc = acc + w_ref[j:j + 1, :] * win
        y = _layer_norm(acc, g_ref[...], b_ref[...])
        o_ref[0, r0:r0 + CONV_CHUNK, :] = (y * jax.nn.sigmoid(y)).astype(BF16)


def _conv_prompt(u, conv_w, conv_b, ln_g, ln_b):
    bsz, seq, ch = u.shape
    tc = min(TOKEN_TILE, seq)
    assert seq % tc == 0 and tc % CONV_CHUNK == 0 and tc % CONV_HALO == 0
    vec = pl.BlockSpec((1, ch), lambda b, i: (0, 0))
    return pl.pallas_call(
        functools.partial(_conv_prompt_body, tc=tc),
        grid=(bsz, seq // tc),
        in_specs=[
            pl.BlockSpec((1, tc, ch), lambda b, i: (b, i, 0)),
            pl.BlockSpec((1, CONV_HALO, ch), lambda b, i: (b, jnp.maximum(i * (tc // CONV_HALO) - 1, 0), 0)),
            pl.BlockSpec((CONV_WIDTH, ch), lambda b, i: (0, 0)),
            vec, vec, vec,
        ],
        out_specs=pl.BlockSpec((1, tc, ch), lambda b, i: (b, i, 0)),
        out_shape=jax.ShapeDtypeStruct((bsz, seq, ch), BF16),
        scratch_shapes=[pltpu.VMEM((CONV_HALO + tc, ch), F32), pltpu.VMEM((SUBLANES - 1, CONV_HALO + tc, ch), F32)],
        compiler_params=_cparams("parallel", "arbitrary"),
        name="conv_prompt",
    )(u, u, conv_w, conv_b, ln_g, ln_b)


def _conv_sample_body(buf_ref, w_ref, cb_ref, g_ref, b_ref, o_ref, *, t_new):
    for t in range(t_new):
        acc = jnp.zeros(buf_ref.shape[1:], F32) + cb_ref[...]
        for j in range(CONV_WIDTH):
            acc = acc + w_ref[j:j + 1, :] * buf_ref[t + j]
        y = _layer_norm(acc, g_ref[...], b_ref[...])
        o_ref[t] = (y * jax.nn.sigmoid(y)).astype(BF16)


def _conv_sample(buf_t, conv_w, conv_b, ln_g, ln_b):
    rows, bs, ch = buf_t.shape
    t_new = rows - (CONV_WIDTH - 1)
    gb = min(32, bs)
    assert bs % gb == 0
    vec = pl.BlockSpec((1, ch), lambda i: (0, 0))
    return pl.pallas_call(
        functools.partial(_conv_sample_body, t_new=t_new),
        grid=(bs // gb,),
        in_specs=[pl.BlockSpec((rows, gb, ch), lambda i: (0, i, 0)),
                  pl.BlockSpec((CONV_WIDTH, ch), lambda i: (0, 0)), vec, vec, vec],
        out_specs=pl.BlockSpec((t_new, gb, ch), lambda i: (0, i, 0)),
        out_shape=jax.ShapeDtypeStruct((t_new, bs, ch), BF16),
        compiler_params=_cparams("parallel"),
        name="conv_sample",
    )(buf_t, conv_w, conv_b, ln_g, ln_b)


N_MERGE_OUT = 5


def _merge_body(*refs, alpha, n_alias):
    (o_ref, y_ref, sg_ref, x_ref, wa_ref, wc_ref, bc_ref, wo_ref, g1_ref, b1_ref, rw_ref, rb_ref) = refs[:12]
    h1_ref, ti_ref, rk_ref, wcol_ref, cnt_ref = refs[12 + n_alias:]
    tm, d = x_ref.shape
    a = jnp.dot(o_ref[...], wa_ref[...], preferred_element_type=F32)
    b = jnp.dot(y_ref[...], wc_ref[...], preferred_element_type=F32) + bc_ref[...]
    mix_in = sg_ref[:, :d].astype(F32) * a + sg_ref[:, d:].astype(F32) * b
    mix = jnp.dot(mix_in.astype(BF16), wo_ref[...], preferred_element_type=F32)
    h1 = _layer_norm(alpha * x_ref[...] + mix, g1_ref[...], b1_ref[...])
    _store_rows(h1_ref, h1)
    logits = lax.dot_general(rw_ref[...], h1, (((1,), (1,)), ((), ())), preferred_element_type=F32,
                             precision=lax.Precision.HIGHEST) + rb_ref[...]
    n_e = logits.shape[0]
    eid = lax.broadcasted_iota(I32, logits.shape, 0)
    vals, idxs, hots = [], [], []
    for _ in range(TOP_K):
        m = jnp.max(logits, 0, keepdims=True)
        idx = jnp.min(jnp.where(logits == m, eid, n_e), 0, keepdims=True)
        hot = eid == idx
        vals.append(m)
        idxs.append(idx)
        hots.append(hot)
        logits = jnp.where(hot, -jnp.inf, logits)
    ex = [jnp.exp(v - vals[0]) for v in vals]
    den = ex[0]
    for e in ex[1:]:
        den = den + e
    ti_ref[...] = jnp.concatenate(idxs, 0)
    wrows = jnp.concatenate([e / den for e in ex] + [jnp.zeros((LANES - TOP_K, tm), F32)], 0)
    wcol_ref[...] = wrows.T
    sel = hots[0]
    for hot in hots[1:]:
        sel = sel | hot
    sel_f = jnp.where(sel, 1.0, 0.0)
    before = lax.broadcasted_iota(I32, (tm, tm), 0) < lax.broadcasted_iota(I32, (tm, tm), 1)
    upper = jnp.where(before, 1.0, 0.0).astype(BF16)
    ahead = jnp.dot(sel_f.astype(BF16), upper, preferred_element_type=F32)
    rk_ref[...] = jnp.concatenate(
        [jnp.sum(jnp.where(hot, ahead, 0.0), 0, keepdims=True) for hot in hots], 0).astype(I32)
    cnt = jnp.sum(sel_f, 1, keepdims=True).astype(I32)
    cnt_ref[0] = jnp.broadcast_to(cnt, (n_e, LANES))


def _merge(o, yact, sg, x, wts, alpha, n_total, row_off, tm, prev):
    n, d = x.shape
    wa, wc, bc, wo, g1, b1, rwt, rb = wts
    n_e = rwt.shape[0]
    off = row_off // tm
    last = n // tm - 1
    steps = n // tm if prev is not None else n_total // tm
    row = lambda width: pl.BlockSpec((tm, width), lambda i: (jnp.minimum(i, last), 0))
    full = lambda a: pl.BlockSpec(a.shape, lambda i: (0, 0))
    in_specs = [row(o.shape[1]), row(yact.shape[1]), row(2 * d), row(d),
                full(wa), full(wc), full(bc), full(wo), full(g1), full(b1), full(rwt), full(rb)]
    args = [o, yact, sg, x, wa, wc, bc, wo, g1, b1, rwt, rb]
    aliases = {}
    if prev is not None:
        in_specs += [pl.BlockSpec(memory_space=pl.ANY)] * N_MERGE_OUT
        aliases = {len(args) + j: j for j in range(N_MERGE_OUT)}
        args += list(prev)
    slot_major = pl.BlockSpec((TOP_K, tm), lambda i: (0, off + i))
    return pl.pallas_call(
        functools.partial(_merge_body, alpha=alpha, n_alias=0 if prev is None else N_MERGE_OUT),
        grid=(steps,),
        in_specs=in_specs,
        out_specs=(pl.BlockSpec((tm * SUBLANES, LANES), lambda i: (off + i, 0)), slot_major, slot_major,
                   pl.BlockSpec((tm, LANES), lambda i: (off + i, 0)),
                   pl.BlockSpec((1, n_e, LANES), lambda i: (off + i, 0, 0))),
        out_shape=(jax.ShapeDtypeStruct((n_total * SUBLANES, LANES), F32),
                   jax.ShapeDtypeStruct((TOP_K, n_total), I32),
                   jax.ShapeDtypeStruct((TOP_K, n_total), I32),
                   jax.ShapeDtypeStruct((n_total, LANES), F32),
                   jax.ShapeDtypeStruct((n_total // tm, n_e, LANES), I32)),
        input_output_aliases=aliases,
        compiler_params=_cparams("parallel"),
        name="merge_ln1_router",
    )(*args)


def _plan(topi, rank, cnt, tm_tok, tm_moe):
    k, n = topi.shape
    n_e = cnt.shape[1]
    before = jnp.cumsum(cnt, 0) - cnt
    total = jnp.sum(cnt, 0)
    ntile_e = (total + tm_moe - 1) // tm_moe
    tend = jnp.cumsum(ntile_e)
    tstart = tend - ntile_e
    base = (tstart * tm_moe)[None, :] + before
    base_tok = jnp.repeat(base, tm_tok, axis=0)
    hot = topi[:, :, None] == jnp.arange(n_e, dtype=I32)[None, None, :]
    pos = (jnp.sum(jnp.where(hot, base_tok[None], 0), -1).astype(I32) + rank) * SUBLANES
    n_tiles = (k * n + tm_moe - 1) // tm_moe + n_e
    n_used = tend[-1]
    tid = jnp.arange(n_tiles, dtype=I32)
    src = jnp.minimum(tid, n_used - 1)
    te = jnp.minimum(jnp.sum((src[:, None] >= tend[None, :]).astype(I32), 1), n_e - 1)
    first = (tid == tstart[te]).astype(I32)
    half = (total[te] - (tid - tstart[te]) * tm_moe <= tm_moe // 2).astype(I32)
    slot = (jnp.cumsum(first) - 1) % 2
    nxt = jnp.where(tend[te] < n_used, te[jnp.minimum(tend[te], n_tiles - 1)], -1)
    sched = jnp.stack([te, first, half, slot, nxt]).astype(I32)
    last_tile = jnp.where(ntile_e > 0, tend - 1, -1).astype(I32)
    return pos, sched, n_used.reshape(1).astype(I32), last_tile, n_tiles


def _dispatch_body(lt_ref, nu_ref, pos_ref, x_ref, xs_hbm, zbuf, zsem, sem, *, tm, tm_moe, n_e, n_tiles):
    i = pl.program_id(0)

    def zero_copy(tile):
        start = pl.multiple_of(tile * (tm_moe * SUBLANES), tm_moe * SUBLANES)
        return pltpu.make_async_copy(zbuf, xs_hbm.at[pl.ds(start, tm_moe * SUBLANES), :], zsem)

    @pl.when(i == 0)
    def _():
        zbuf[...] = jnp.zeros(zbuf.shape, F32)

        def z_start(e, c):
            @pl.when(lt_ref[e] >= 0)
            def _():
                zero_copy(lt_ref[e]).start()
            return c

        def z_wait(e, c):
            @pl.when(lt_ref[e] >= 0)
            def _():
                zero_copy(lt_ref[e]).wait()
            return c

        def t_start(t, c):
            zero_copy(t).start()
            return c

        def t_wait(t, c):
            zero_copy(t).wait()
            return c

        lax.fori_loop(0, n_e, z_start, 0)
        lax.fori_loop(nu_ref[0], n_tiles, t_start, 0)
        lax.fori_loop(0, n_e, z_wait, 0)
        lax.fori_loop(nu_ref[0], n_tiles, t_wait, 0)

    def issue(g, c):
        base = pl.multiple_of(g * SUBLANES, SUBLANES)
        for rr in range(SUBLANES):
            src = x_ref.at[pl.ds(pl.multiple_of((base + rr) * SUBLANES, SUBLANES), SUBLANES), :]
            for k in range(TOP_K):
                dst = pl.multiple_of(pos_ref[k, base + rr], SUBLANES)
                pltpu.make_async_copy(src, xs_hbm.at[pl.ds(dst, SUBLANES), :], sem).start()
        return c

    lax.fori_loop(0, tm // SUBLANES, issue, 0)
    for k in range(TOP_K):
        pltpu.make_async_copy(x_ref, xs_hbm.at[pl.ds(0, tm * SUBLANES), :], sem).wait()


def _dispatch(h1, pos, last_tile, n_used, n_tiles, tm, tm_moe):
    n = h1.shape[0] // SUBLANES
    n_e = last_tile.shape[0]
    grid_spec = pltpu.PrefetchScalarGridSpec(
        num_scalar_prefetch=2,
        grid=(n // tm,),
        in_specs=[pl.BlockSpec((TOP_K, tm), lambda i, lt, nu: (0, i), memory_space=pltpu.SMEM),
                  pl.BlockSpec((tm * SUBLANES, LANES), lambda i, lt, nu: (i, 0))],
        out_specs=pl.BlockSpec(memory_space=pl.ANY),
        scratch_shapes=[pltpu.VMEM((tm_moe * SUBLANES, LANES), F32), pltpu.SemaphoreType.DMA,
                        pltpu.SemaphoreType.DMA],
    )
    return pl.pallas_call(
        functools.partial(_dispatch_body, tm=tm, tm_moe=tm_moe, n_e=n_e, n_tiles=n_tiles),
        grid_spec=grid_spec,
        out_shape=jax.ShapeDtypeStruct((n_tiles * tm_moe * SUBLANES, LANES), F32),
        compiler_params=_cparams("arbitrary"),
        name="moe_dispatch",
    )(last_tile, n_used, pos, h1)


TE, FIRST, HALF, SLOT, NEXT = range(5)


def _experts_body(sch_ref, nu_ref, x_ref, w1_hbm, b1_ref, w2_hbm, b2_ref, y_ref, w1s, w2s, w1b, w2b, sems, *, tm):
    i = pl.program_id(0)
    d_ff = w2b.shape[0]

    def fetch(expert, slot):
        return (pltpu.make_async_copy(w1_hbm.at[expert], w1s.at[slot], sems.at[0, slot]),
                pltpu.make_async_copy(w2_hbm.at[expert], w2s.at[slot], sems.at[1, slot]))

    def run(rows):
        x = _load_rows(x_ref, rows).astype(BF16)
        hh = jnp.dot(x, w1b[...], preferred_element_type=F32) + b1_ref[0]
        g = jnp.minimum(hh[:, :d_ff], SWIGLU_LIMIT)
        u = jnp.clip(hh[:, d_ff:], -SWIGLU_LIMIT, SWIGLU_LIMIT)
        act = (u + 1.0) * g * jax.nn.sigmoid(SWIGLU_ALPHA * g)
        _store_rows(y_ref, jnp.dot(act.astype(BF16), w2b[...], preferred_element_type=F32) + b2_ref[0])

    @pl.when(i == 0)
    def _():
        for cp in fetch(sch_ref[TE, 0], 0):
            cp.start()

    @pl.when(i >= nu_ref[0])
    def _():
        y_ref[...] = jnp.zeros(y_ref.shape, F32)

    @pl.when(i < nu_ref[0])
    def _():
        @pl.when(sch_ref[FIRST, i] == 1)
        def _():
            slot = sch_ref[SLOT, i]

            @pl.when(sch_ref[NEXT, i] >= 0)
            def _():
                for cp in fetch(sch_ref[NEXT, i], 1 - slot):
                    cp.start()

            for cp in fetch(sch_ref[TE, i], slot):
                cp.wait()
            w1b[...] = w1s[slot].astype(BF16)
            w2b[...] = w2s[slot].astype(BF16)

        @pl.when(sch_ref[HALF, i] == 0)
        def _():
            run(tm)

        @pl.when(sch_ref[HALF, i] == 1)
        def _():
            run(tm // 2)
            y_ref[pl.ds(tm // 2 * SUBLANES, tm // 2 * SUBLANES), :] = jnp.zeros((tm // 2 * SUBLANES, LANES), F32)


def _experts(xs, sched, n_used, w1, b1, w2, b2, tm):
    n_e, d, f2 = w1.shape
    rows = tm * SUBLANES
    by_tile = lambda i, sch, nu: (i, 0)
    by_expert = lambda i, sch, nu: (sch[TE, i], 0, 0)
    grid_spec = pltpu.PrefetchScalarGridSpec(
        num_scalar_prefetch=2,
        grid=(xs.shape[0] // rows,),
        in_specs=[
            pl.BlockSpec((rows, LANES), by_tile),
            pl.BlockSpec(memory_space=pl.ANY),
            pl.BlockSpec((1, 1, f2), by_expert),
            pl.BlockSpec(memory_space=pl.ANY),
            pl.BlockSpec((1, 1, d), by_expert),
        ],
        out_specs=pl.BlockSpec((rows, LANES), by_tile),
        scratch_shapes=[pltpu.VMEM((2, d, f2), F32), pltpu.VMEM((2, f2 // 2, d), F32),
                        pltpu.VMEM((d, f2), BF16), pltpu.VMEM((f2 // 2, d), BF16), pltpu.SemaphoreType.DMA((2, 2))],
    )
    return pl.pallas_call(
        functools.partial(_experts_body, tm=tm),
        grid_spec=grid_spec,
        out_shape=jax.ShapeDtypeStruct(xs.shape, F32),
        compiler_params=_cparams("arbitrary"),
        name="moe_experts",
    )(sched, n_used, xs, w1, b1.reshape(n_e, 1, f2), w2, b2.reshape(n_e, 1, d))


def _combine_body(pos_ref, nxt_ref, h1_ref, wcol_ref, ys_hbm, g_ref, b_ref, o_ref, ybuf, sems, *, alpha, tm):
    i = pl.program_id(0)
    n = pl.num_programs(0)

    def fetch(p_ref, slot):
        def issue(g, c):
            base = pl.multiple_of(g * SUBLANES, SUBLANES)
            for rr in range(SUBLANES):
                dst = pl.ds(pl.multiple_of((base + rr) * SUBLANES, SUBLANES), SUBLANES)
                for k in range(TOP_K):
                    src = pl.multiple_of(p_ref[k, base + rr], SUBLANES)
                    pltpu.make_async_copy(ys_hbm.at[pl.ds(src, SUBLANES), :], ybuf.at[slot, k, dst, :],
                                          sems.at[slot]).start()
            return c
        lax.fori_loop(0, tm // SUBLANES, issue, 0)

    @pl.when(i == 0)
    def _():
        fetch(pos_ref, 0)

    @pl.when(i + 1 < n)
    def _():
        fetch(nxt_ref, (i + 1) % 2)

    slot = i % 2
    for k in range(TOP_K):
        pltpu.make_async_copy(ys_hbm.at[pl.ds(0, tm * SUBLANES), :], ybuf.at[slot, k], sems.at[slot]).wait()
    ff = wcol_ref[:, 0:1] * _load_rows(ybuf.at[slot, 0], tm)
    for k in range(1, TOP_K):
        ff = ff + wcol_ref[:, k:k + 1] * _load_rows(ybuf.at[slot, k], tm)
    o_ref[...] = _layer_norm(alpha * _load_rows(h1_ref, tm) + ff, g_ref[...], b_ref[...])


def _combine(h1, wcol, pos, ys, g, b, alpha, row_off, n):
    d = g.shape[1]
    tm = min(COMBINE_TILE, n)
    off = row_off // tm
    last = n // tm - 1
    vec = pl.BlockSpec((1, d), lambda i: (0, 0))
    return pl.pallas_call(
        functools.partial(_combine_body, alpha=alpha, tm=tm),
        grid=(n // tm,),
        in_specs=[pl.BlockSpec((TOP_K, tm), lambda i: (0, off + i), memory_space=pltpu.SMEM),
                  pl.BlockSpec((TOP_K, tm), lambda i: (0, off + jnp.minimum(i + 1, last)), memory_space=pltpu.SMEM),
                  pl.BlockSpec((tm * SUBLANES, LANES), lambda i: (off + i, 0)),
                  pl.BlockSpec((tm, LANES), lambda i: (off + i, 0)),
                  pl.BlockSpec(memory_space=pl.ANY), vec, vec],
        out_specs=pl.BlockSpec((tm, d), lambda i: (i, 0)),
        out_shape=jax.ShapeDtypeStruct((n, d), F32),
        scratch_shapes=[pltpu.VMEM((2, TOP_K, tm * SUBLANES, LANES), F32), pltpu.SemaphoreType.DMA((2,))],
        compiler_params=_cparams("arbitrary"),
        name="combine_ln2",
    )(pos, pos, h1, wcol, ys, g, b)


def _row2(v):
    return v.reshape(1, -1).astype(F32)


def kernel(x_prompt, x_sample, cache_k, cache_v, page_table, state_conv, w_in, b_in, lambda_q1, lambda_k1,
           lambda_q2, lambda_k2, subln_g, rel_bias, w_attn_proj, conv_w, conv_b, conv_ln_g, conv_ln_b,
           w_conv_proj, b_conv_proj, w_out, ln1_g, ln1_b, router_w, router_b, expert_w1, expert_b1,
           expert_w2, expert_b2, ln2_g, ln2_b):
    depth = w_in.shape[0]
    bp, seq, d = x_prompt.shape
    bs, t_new, _ = x_sample.shape
    d_att = N_HEADS * QK_DIM
    d_conv = conv_w.shape[2]
    n_p, n_s = bp * seq, bs * t_new
    n_tot = n_p + n_s
    tm = min(TOKEN_TILE, math.gcd(n_p, n_s))
    assert tm % LANES == 0
    tm_moe = min(MOE_TILE, n_tot)
    alpha = (2 * depth) ** 0.25

    hp = x_prompt.reshape(n_p, d)
    hs = x_sample.reshape(n_s, d)
    outs = [[] for _ in range(6)]
    for l in range(depth):
        lam_init = 0.8 - 0.6 * math.exp(-0.3 * l)
        lam = (jnp.exp(jnp.sum(lambda_q1[l].astype(F32) * lambda_k1[l].astype(F32)))
               - jnp.exp(jnp.sum(lambda_q2[l].astype(F32) * lambda_k2[l].astype(F32))) + lam_init).reshape(1)
        w_in_bf = w_in[l].astype(BF16)
        b_in_l = _row2(b_in[l])
        g_sub = _row2(subln_g[l])
        conv_args = (conv_w[l].astype(F32), _row2(conv_b[l]), _row2(conv_ln_g[l]), _row2(conv_ln_b[l]))
        merge_w = (w_attn_proj[l].astype(BF16), w_conv_proj[l].astype(BF16), _row2(b_conv_proj[l]),
                   w_out[l].astype(BF16), _row2(ln1_g[l]), _row2(ln1_b[l]),
                   router_w[l].astype(F32).T, router_b[l].astype(F32).reshape(-1, 1))

        qp, kp, vp, kbp, vbp, up, sgp = _in_proj(hp, w_in_bf, b_in_l, d_att, d_conv, (bp, seq) + _attn_tiles(seq))
        op = _prompt_attention(qp, kbp, vbp, rel_bias, lam, g_sub, lam_init)
        up3 = up.reshape(bp, seq, d_conv)
        yp = _conv_prompt(up3, *conv_args)
        merged = _merge(op.reshape(n_p, d_att), yp.reshape(n_p, d_conv), sgp, hp, merge_w, alpha, n_tot, 0, tm, None)

        qs, ks, vs, kbs, vbs, us, sgs = _in_proj(hs, w_in_bf, b_in_l, d_att, d_conv)
        osr = _sample_attention(qs.reshape(bs, t_new, d_att), kbs.reshape(bs, t_new, d_att),
                                vbs.reshape(bs, t_new, d_att), cache_k[l], cache_v[l], page_table,
                                rel_bias, lam, g_sub, lam_init)
        os_ = osr[:, :t_new].reshape(n_s, d_att)
        buf_s = jnp.concatenate([state_conv[l].astype(F32), us.reshape(bs, t_new, d_conv)], axis=1)
        ys = _conv_sample(buf_s.transpose(1, 0, 2), *conv_args).transpose(1, 0, 2).reshape(n_s, d_conv)
        h1, topi, rank, wcol, cnt = _merge(os_, ys, sgs, hs, merge_w, alpha, n_tot, n_p, tm, merged)

        pos, sched, n_used, last_tile, n_tiles = _plan(topi, rank, cnt[:, :, 0], tm, tm_moe)
        xs = _dispatch(h1, pos, last_tile, n_used, n_tiles, tm, tm_moe)
        ysort = _experts(xs, sched, n_used, expert_w1[l], expert_b1[l].astype(F32),
                         expert_w2[l], expert_b2[l].astype(F32), tm_moe)
        g2, b2 = _row2(ln2_g[l]), _row2(ln2_b[l])
        hp = _combine(h1, wcol, pos, ysort, g2, b2, alpha, 0, n_p)
        hs = _combine(h1, wcol, pos, ysort, g2, b2, alpha, n_p, n_s)

        w1 = CONV_WIDTH - 1
        cp = up3[:, seq - w1:] if seq >= w1 else jnp.concatenate(
            [jnp.zeros((bp, w1 - seq, d_conv), F32), up3], axis=1)
        for lst, val in zip(outs, (kp.reshape(bp, seq, N_HEADS, QK_DIM), vp.reshape(bp, seq, N_HEADS, V_DIM), cp,
                                   ks.reshape(bs, t_new, N_HEADS, QK_DIM), vs.reshape(bs, t_new, N_HEADS, V_DIM),
                                   buf_s[:, t_new:])):
            lst.append(val)
    return (hp.reshape(bp, seq, d), hs.reshape(bs, t_new, d)) + tuple(jnp.stack(o) for o in outs)
```

```python
import functools
import math

import jax
import jax.numpy as jnp
from jax import lax
from jax.experimental import pallas as pl
from jax.experimental.pallas import tpu as pltpu

F32 = jnp.float32
BF16 = jnp.bfloat16
I32 = jnp.int32

N_HEADS = 4
HEAD_DIM = 64
QK_DIM = 2 * HEAD_DIM
V_DIM = 2 * HEAD_DIM
ATTN_SCALE = HEAD_DIM ** -0.5
LOG2E = 1.4426950408889634
NEG_INF = -1e30
NUM_BUCKETS = 32
MAX_EXACT = NUM_BUCKETS // 2
MAX_DISTANCE = 128
CONV_WIDTH = 31
TOP_K = 4
SWIGLU_LIMIT = 7.0
SWIGLU_ALPHA = 1.702
LN_EPS = 1e-5
PAGE_SIZE = 128

LANES = 128
SUBLANES = 8
VMEM_LIMIT = 56 * 1024 * 1024

TOKEN_TILE = 512
ATTN_TQ = 1024
ATTN_TK = 1024
ATTN_GROUP = 2
MOE_TILE = 512
COMBINE_TILE = 256
CONV_HALO = 32
CONV_CHUNK = 64


def _cparams(*sem):
    return pltpu.CompilerParams(dimension_semantics=sem, vmem_limit_bytes=VMEM_LIMIT)


def _load_rows(ref, tm):
    return jnp.concatenate([ref[pl.ds(c, tm, stride=SUBLANES), :] for c in range(SUBLANES)], axis=1)


def _store_rows(ref, x):
    tm = x.shape[0]
    for c in range(SUBLANES):
        ref[pl.ds(c, tm, stride=SUBLANES), :] = x[:, c * LANES:(c + 1) * LANES]


def _layer_norm(x, g, b):
    mu = jnp.mean(x, -1, keepdims=True)
    xc = x - mu
    var = jnp.mean(xc * xc, -1, keepdims=True)
    return xc * lax.rsqrt(var + LN_EPS) * g + b


def _in_proj_body(x_ref, w_ref, b_ref, q_ref, k_ref, v_ref, kb_ref, vb_ref, u_ref, sg_ref, *, d_att, d_conv, by_head):
    x = x_ref[...].astype(BF16)

    def seg(lo, hi):
        return jnp.dot(x, w_ref[:, lo:hi], preferred_element_type=F32) + b_ref[:, lo:hi]

    def head(a, h):
        return a[:, h * QK_DIM:(h + 1) * QK_DIM]

    o = 0
    q = seg(o, o + d_att) * (ATTN_SCALE * LOG2E)
    o += d_att
    k = seg(o, o + d_att)
    o += d_att
    v = seg(o, o + d_att)
    o += d_att
    tm = x.shape[0]
    for h in range(N_HEADS):
        k_ref[pl.ds(h, tm, stride=N_HEADS), :] = head(k, h)
        v_ref[pl.ds(h, tm, stride=N_HEADS), :] = head(v, h)
    if by_head:
        for h in range(N_HEADS):
            q_ref[h] = head(q, h).T.astype(BF16)
            kb_ref[h] = head(k, h).astype(BF16)
            vb_ref[h] = head(v, h).T.astype(BF16)
    else:
        q_ref[...] = q.astype(BF16)
        kb_ref[...] = k.astype(BF16)
        vb_ref[...] = v.astype(BF16)
    c = seg(o, o + 2 * d_conv)
    u_ref[...] = c[:, :d_conv] * jax.nn.sigmoid(c[:, d_conv:])
    o += 2 * d_conv
    sg_ref[...] = jax.nn.sigmoid(seg(o, w_ref.shape[1])).astype(BF16)


def _in_proj(x, w_bf, b, d_att, d_conv, attn_tiles=None):
    n, d = x.shape
    tm = min(TOKEN_TILE, n)
    row = lambda width: pl.BlockSpec((tm, width), lambda i: (i, 0))
    full = lambda a: pl.BlockSpec(a.shape, lambda i: (0, 0))
    f32_out = lambda width: jax.ShapeDtypeStruct((n, width), F32)
    kv_shape = jax.ShapeDtypeStruct((n * N_HEADS, QK_DIM), F32)
    kv_spec = pl.BlockSpec((tm * N_HEADS, QK_DIM), lambda i: (i, 0))
    if attn_tiles is None:
        qkv_shapes = [jax.ShapeDtypeStruct((n, d_att), BF16)] * 3
        qkv_specs = [row(d_att)] * 3
    else:
        bsz, seq, tq, tk = attn_tiles
        assert seq % tm == 0 and tq % tm == 0 and tk % tm == 0
        per_seq = seq // tm

        def spec(t, transposed):
            parts = t // tm
            blk = (None, N_HEADS, None, QK_DIM, tm) if transposed else (None, N_HEADS, None, tm, QK_DIM)

            def index(i):
                ti = i % per_seq
                tile, part = ti // parts, ti % parts
                return (i // per_seq, 0, tile, 0, part) if transposed else (i // per_seq, 0, tile, part, 0)
            return pl.BlockSpec(blk, index)

        qkv_shapes = [jax.ShapeDtypeStruct((bsz, N_HEADS, seq // tq, QK_DIM, tq), BF16),
                      jax.ShapeDtypeStruct((bsz, N_HEADS, seq // tk, tk, QK_DIM), BF16),
                      jax.ShapeDtypeStruct((bsz, N_HEADS, seq // tk, V_DIM, tk), BF16)]
        qkv_specs = [spec(tq, True), spec(tk, False), spec(tk, True)]
    out_shape = (qkv_shapes[0], kv_shape, kv_shape, qkv_shapes[1], qkv_shapes[2],
                 f32_out(d_conv), jax.ShapeDtypeStruct((n, 2 * d), BF16))
    return pl.pallas_call(
        functools.partial(_in_proj_body, d_att=d_att, d_conv=d_conv, by_head=attn_tiles is not None),
        grid=(n // tm,),
        in_specs=[row(d), full(w_bf), full(b)],
        out_specs=(qkv_specs[0], kv_spec, kv_spec, qkv_specs[1], qkv_specs[2], row(d_conv), row(2 * d)),
        out_shape=out_shape,
        compiler_params=_cparams("parallel"),
        name="in_proj",
    )(x, w_bf, b)


def _t5_bucket(dist):
    n = jnp.maximum(dist, 0)
    nf = jnp.maximum(n, 1).astype(F32)
    large = MAX_EXACT + (jnp.log(nf / MAX_EXACT) / math.log(MAX_DISTANCE / MAX_EXACT)
                         * (NUM_BUCKETS - MAX_EXACT)).astype(I32)
    large = jnp.minimum(large, NUM_BUCKETS - 1)
    return jnp.where(n < MAX_EXACT, n, large)


def _bias_of(dist, rel_bias, shift=None):
    hot = _t5_bucket(dist)[..., None, None] == jnp.arange(NUM_BUCKETS, dtype=I32)[:, None]
    b = jnp.sum(jnp.where(hot, rel_bias.astype(F32), 0.0), -2)
    if shift is not None:
        b = b - shift
    b = jnp.where((dist >= 0)[..., None], b * LOG2E, NEG_INF)
    return jnp.moveaxis(b, -1, 0)


def _flash_body(lam_ref, qt_ref, k_ref, vt_ref, btab_ref, g_ref, o_ref, m_s, l_s, acc_s, bias_ref, *, ratio, group,
                lam_init):
    i = pl.program_id(2)
    tk, tq = bias_ref.shape[1:]

    @pl.when(i == 0)
    def _():
        for o in range(ratio + 1):
            table = jnp.broadcast_to(btab_ref[o], (tk, btab_ref.shape[2]))
            bias_ref[o] = pltpu.roll(table, 0, 1, stride=1, stride_axis=0)[:, :tq]

    sub = lax.broadcasted_iota(I32, (QK_DIM, 1), 0)
    qs = []
    for b in range(group):
        qt = qt_ref[b]
        zero = jnp.zeros_like(qt)
        qs.append((jnp.where(sub < HEAD_DIM, qt, zero), jnp.where(sub >= HEAD_DIM, qt, zero)))
    m_s[...] = jnp.full(m_s.shape, NEG_INF, F32)
    l_s[...] = jnp.zeros(l_s.shape, F32)
    acc_s[...] = jnp.zeros(acc_s.shape, F32)

    def step(j, bias, keys=slice(None), cols=slice(None)):
        for b in range(group):
            k = k_ref[b, j, keys, :]
            vt = vt_ref[b, j, :, keys]
            for c in range(2):
                s = jnp.dot(k, qs[b][c][:, cols], preferred_element_type=F32)
                if bias is not None:
                    s = s + bias
                m_old = m_s[b, c, :, cols]
                m_new = jnp.maximum(m_old, jnp.max(s, 0, keepdims=True))
                alpha = jnp.exp2(m_old - m_new)
                p = jnp.exp2(s - m_new)
                l_s[b, c, :, cols] = alpha * l_s[b, c, :, cols] + jnp.sum(p, 0, keepdims=True)
                acc_s[b, c, :, cols] = alpha * acc_s[b, c, :, cols] + jnp.dot(
                    vt, p.astype(BF16), preferred_element_type=F32)
                m_s[b, c, :, cols] = m_new

    def far_step(j, carry):
        step(j, None)
        return carry

    lax.fori_loop(0, jnp.maximum(i * ratio - 1, 0), far_step, 0)

    @pl.when(i >= 1)
    def _():
        step(i * ratio - 1, bias_ref[0])

    for o in range(ratio - 1):
        step(i * ratio + o, bias_ref[o + 1])
    last, half, first_col = i * ratio + ratio - 1, tk // 2, tq - tk // 2
    step(last, bias_ref[ratio, 0:half, :], keys=slice(0, half))
    step(last, bias_ref[ratio, half:tk, first_col:tq], keys=slice(half, tk), cols=slice(first_col, tq))

    for b in range(group):
        o = acc_s[b, 0] / l_s[b, 0] - lam_ref[0] * (acc_s[b, 1] / l_s[b, 1])
        o = o * lax.rsqrt(jnp.mean(o * o, 0, keepdims=True) + LN_EPS) * g_ref[...]
        o_ref[b] = (o * (1.0 - lam_init)).T.astype(BF16)


def _attn_tiles(seq):
    tq = min(ATTN_TQ, seq)
    tk = min(ATTN_TK, tq)
    assert tk >= MAX_DISTANCE and seq % tq == 0 and tq % tk == 0
    return tq, tk


def _prompt_attention(qt, k4, vt, rel_bias, lam, subln_g, lam_init):
    bsz, _, nq, _, tq = qt.shape
    nk, tk = k4.shape[2], k4.shape[3]
    seq, ratio, width = nq * tq, tq // tk, N_HEADS * V_DIM
    group = ATTN_GROUP if bsz % ATTN_GROUP == 0 else 1
    far = rel_bias[NUM_BUCKETS - 1].astype(F32)
    period = tq + tk
    assert period % LANES == 0
    y = jnp.arange(period, dtype=I32)
    r_minus_c = jnp.where(y < tq, y, y - period)
    dist = r_minus_c[None, :] - (jnp.arange(ratio + 1, dtype=I32)[:, None] - 1) * tk
    btab = _bias_of(dist, rel_bias, far).reshape(N_HEADS, ratio + 1, 1, period)
    return pl.pallas_call(
        functools.partial(_flash_body, ratio=ratio, group=group, lam_init=lam_init),
        grid=(bsz // group, N_HEADS, nq),
        in_specs=[
            pl.BlockSpec(memory_space=pltpu.SMEM),
            pl.BlockSpec((group, None, None, QK_DIM, tq), lambda b, h, i: (b, h, i, 0, 0)),
            pl.BlockSpec((group, None, nk, tk, QK_DIM), lambda b, h, i: (b, h, 0, 0, 0)),
            pl.BlockSpec((group, None, nk, V_DIM, tk), lambda b, h, i: (b, h, 0, 0, 0)),
            pl.BlockSpec((None, ratio + 1, 1, period), lambda b, h, i: (h, 0, 0, 0)),
            pl.BlockSpec((V_DIM, 1), lambda b, h, i: (0, 0)),
        ],
        out_specs=pl.BlockSpec((group, tq, V_DIM), lambda b, h, i: (b, i, h)),
        out_shape=jax.ShapeDtypeStruct((bsz, seq, width), BF16),
        scratch_shapes=[pltpu.VMEM((group, 2, 1, tq), F32), pltpu.VMEM((group, 2, 1, tq), F32),
                        pltpu.VMEM((group, 2, V_DIM, tq), F32), pltpu.VMEM((ratio + 1, tk, tq), F32)],
        compiler_params=_cparams("parallel", "parallel", "arbitrary"),
        name="prompt_attention",
    )(lam, qt, k4, vt, btab, subln_g.reshape(V_DIM, 1))


def _paged_body(pt_ref, lam_ref, q_ref, kn_ref, vn_ref, bias_ref, g_ref, *refs, n_pages, t_new, lam_init):
    k_refs = refs[:n_pages]
    v_refs = refs[n_pages:2 * n_pages]
    o_ref = refs[2 * n_pages]
    lam = lam_ref[0]
    rows = 2 * t_new
    cols = PAGE_SIZE * N_HEADS
    nt = (((1,), (1,)), ((), ()))
    q = q_ref[0]
    parts = [lax.dot_general(q, k_refs[p][...].astype(BF16), nt, preferred_element_type=F32)
             for p in range(n_pages)]
    parts.append(lax.dot_general(q, kn_ref[0], nt, preferred_element_type=F32))
    s = jnp.concatenate(parts, axis=1) + bias_ref[...]
    m = jnp.max(s, -1, keepdims=True)
    p_un = jnp.exp2(s - m)
    pn = p_un / jnp.sum(p_un, -1, keepdims=True)
    w = (pn - lam * pltpu.roll(pn, N_HEADS * rows - t_new, axis=0)).astype(BF16)
    o = jnp.dot(w[:, n_pages * cols:], vn_ref[0], preferred_element_type=F32)
    for p in range(n_pages):
        o = o + jnp.dot(w[:, p * cols:(p + 1) * cols], v_refs[p][...].astype(BF16), preferred_element_type=F32)
    o = o * lax.rsqrt(jnp.mean(o * o, -1, keepdims=True) + LN_EPS) * g_ref[...]
    o = (o * (1.0 - lam_init)).astype(BF16)
    o_ref[0] = jnp.concatenate([o[h * rows:(h + 1) * rows] for h in range(N_HEADS)], axis=1)


def _sample_attention(q, kb, vb, cache_k, cache_v, page_table, rel_bias, lam, subln_g, lam_init):
    bs, t_new, width = q.shape
    n_pool = cache_k.shape[0]
    n_pages = page_table.shape[1]
    past = n_pages * PAGE_SIZE
    rows = 2 * t_new
    cols = PAGE_SIZE * N_HEADS
    new_pos = LANES // N_HEADS
    assert rows % SUBLANES == 0 and t_new <= new_pos
    qh = q.reshape(bs, t_new, N_HEADS, QK_DIM).transpose(0, 2, 1, 3)
    lane_map = (jnp.arange(QK_DIM) // HEAD_DIM)[None, :] == jnp.arange(2)[:, None]
    q_rows = (qh[:, :, None] * lane_map[None, None, :, None, :].astype(BF16)).reshape(bs, N_HEADS * rows, QK_DIM)
    pad = ((0, 0), (0, LANES - t_new * N_HEADS), (0, 0))
    kn = jnp.pad(kb.reshape(bs, t_new * N_HEADS, QK_DIM), pad)
    vn = jnp.pad(vb.reshape(bs, t_new * N_HEADS, V_DIM), pad)
    qpos = past + jnp.arange(t_new, dtype=I32)
    kpos = jnp.concatenate([jnp.arange(past + t_new, dtype=I32),
                            jnp.full((new_pos - t_new,), past + t_new + new_pos, I32)])
    b = _bias_of(qpos[:, None] - kpos[None, :], rel_bias)
    same = jnp.arange(N_HEADS)[:, None] == jnp.arange(N_HEADS)[None, :]
    b = jnp.where(same[:, None, None, :], b[..., None], NEG_INF)
    b = b.reshape(N_HEADS, 1, t_new, -1)
    bias = jnp.concatenate([b, b], axis=1).reshape(N_HEADS * rows, -1)

    page_spec = lambda j: pl.BlockSpec((None, cols, QK_DIM), lambda b, pt, j=j: (pt[b * n_pages + j], 0, 0))
    per_seq = lambda r, c: pl.BlockSpec((1, r, c), lambda b, pt: (b, 0, 0))
    grid_spec = pltpu.PrefetchScalarGridSpec(
        num_scalar_prefetch=1,
        grid=(bs,),
        in_specs=[
            pl.BlockSpec(memory_space=pltpu.SMEM),
            per_seq(N_HEADS * rows, QK_DIM),
            per_seq(LANES, QK_DIM),
            per_seq(LANES, V_DIM),
            pl.BlockSpec(bias.shape, lambda b, pt: (0, 0)),
            pl.BlockSpec((1, V_DIM), lambda b, pt: (0, 0)),
        ] + [page_spec(j) for j in range(n_pages)] * 2,
        out_specs=pl.BlockSpec((1, rows, width), lambda b, pt: (b, 0, 0)),
    )
    ck = cache_k.reshape(n_pool, cols, QK_DIM)
    cv = cache_v.reshape(n_pool, cols, V_DIM)
    return pl.pallas_call(
        functools.partial(_paged_body, n_pages=n_pages, t_new=t_new, lam_init=lam_init),
        grid_spec=grid_spec,
        out_shape=jax.ShapeDtypeStruct((bs, rows, width), BF16),
        compiler_params=_cparams("parallel"),
        name="sample_attention",
    )(page_table.reshape(-1), lam, q_rows, kn, vn, bias, subln_g, *([ck] * n_pages), *([cv] * n_pages))


def _conv_prompt_body(u_ref, halo_ref, w_ref, cb_ref, g_ref, b_ref, o_ref, buf, shifted, *, tc):
    i = pl.program_id(1)
    halo = halo_ref[0]
    buf[0:CONV_HALO, :] = jnp.where(i > 0, halo, jnp.zeros_like(halo))
    buf[CONV_HALO:, :] = u_ref[0]
    rows = CONV_HALO + tc - SUBLANES
    for s in range(1, SUBLANES):
        shifted[s - 1, 0:rows, :] = buf[s:s + rows, :]
    first = CONV_HALO - (CONV_WIDTH - 1)
    for r0 in range(0, tc, CONV_CHUNK):
        acc = jnp.zeros((CONV_CHUNK, u_ref.shape[2]), F32) + cb_ref[...]
        for j in range(CONV_WIDTH):
            s = (first + j) % SUBLANES
            a = first + j - s + r0
            win = buf[a:a + CONV_CHUNK, :] if s == 0 else shifted[s - 1, a:a + CONV_CHUNK, :]
            acc = acc + w_ref[j:j + 1, :] * win
        y = _layer_norm(acc, g_ref[...], b_ref[...])
        o_ref[0, r0:r0 + CONV_CHUNK, :] = (y * jax.nn.sigmoid(y)).astype(BF16)


def _conv_prompt(u, conv_w, conv_b, ln_g, ln_b):
    bsz, seq, ch = u.shape
    tc = min(TOKEN_TILE, seq)
    assert seq % tc == 0 and tc % CONV_CHUNK == 0 and tc % CONV_HALO == 0
    vec = pl.BlockSpec((1, ch), lambda b, i: (0, 0))
    return pl.pallas_call(
        functools.partial(_conv_prompt_body, tc=tc),
        grid=(bsz, seq // tc),
        in_specs=[
            pl.BlockSpec((1, tc, ch), lambda b, i: (b, i, 0)),
            pl.BlockSpec((1, CONV_HALO, ch), lambda b, i: (b, jnp.maximum(i * (tc // CONV_HALO) - 1, 0), 0)),
            pl.BlockSpec((CONV_WIDTH, ch), lambda b, i: (0, 0)),
            vec, vec, vec,
        ],
        out_specs=pl.BlockSpec((1, tc, ch), lambda b, i: (b, i, 0)),
        out_shape=jax.ShapeDtypeStruct((bsz, seq, ch), BF16),
        scratch_shapes=[pltpu.VMEM((CONV_HALO + tc, ch), F32), pltpu.VMEM((SUBLANES - 1, CONV_HALO + tc, ch), F32)],
        compiler_params=_cparams("parallel", "arbitrary"),
        name="conv_prompt",
    )(u, u, conv_w, conv_b, ln_g, ln_b)


def _conv_sample_body(buf_ref, w_ref, cb_ref, g_ref, b_ref, o_ref, *, t_new):
    for t in range(t_new):
        acc = jnp.zeros(buf_ref.shape[1:], F32) + cb_ref[...]
        for j in range(CONV_WIDTH):
            acc = acc + w_ref[j:j + 1, :] * buf_ref[t + j]
        y = _layer_norm(acc, g_ref[...], b_ref[...])
        o_ref[t] = (y * jax.nn.sigmoid(y)).astype(BF16)


def _conv_sample(buf_t, conv_w, conv_b, ln_g, ln_b):
    rows, bs, ch = buf_t.shape
    t_new = rows - (CONV_WIDTH - 1)
    gb = min(32, bs)
    assert bs % gb == 0
    vec = pl.BlockSpec((1, ch), lambda i: (0, 0))
    return pl.pallas_call(
        functools.partial(_conv_sample_body, t_new=t_new),
        grid=(bs // gb,),
        in_specs=[pl.BlockSpec((rows, gb, ch), lambda i: (0, i, 0)),
                  pl.BlockSpec((CONV_WIDTH, ch), lambda i: (0, 0)), vec, vec, vec],
        out_specs=pl.BlockSpec((t_new, gb, ch), lambda i: (0, i, 0)),
        out_shape=jax.ShapeDtypeStruct((t_new, bs, ch), BF16),
        compiler_params=_cparams("parallel"),
        name="conv_sample",
    )(buf_t, conv_w, conv_b, ln_g, ln_b)


N_MERGE_OUT = 5


def _merge_body(*refs, alpha, n_alias):
    (o_ref, y_ref, sg_ref, x_ref, wa_ref, wc_ref, bc_ref, wo_ref, g1_ref, b1_ref, rw_ref, rb_ref) = refs[:12]
    h1_ref, ti_ref, rk_ref, wcol_ref, cnt_ref = refs[12 + n_alias:]
    tm, d = x_ref.shape
    a = jnp.dot(o_ref[...], wa_ref[...], preferred_element_type=F32)
    b = jnp.dot(y_ref[...], wc_ref[...], preferred_element_type=F32) + bc_ref[...]
    mix_in = sg_ref[:, :d].astype(F32) * a + sg_ref[:, d:].astype(F32) * b
    mix = jnp.dot(mix_in.astype(BF16), wo_ref[...], preferred_element_type=F32)
    h1 = _layer_norm(alpha * x_ref[...] + mix, g1_ref[...], b1_ref[...])
    _store_rows(h1_ref, h1)
    logits = lax.dot_general(rw_ref[...], h1, (((1,), (1,)), ((), ())), preferred_element_type=F32,
                             precision=lax.Precision.HIGHEST) + rb_ref[...]
    n_e = logits.shape[0]
    eid = lax.broadcasted_iota(I32, logits.shape, 0)
    vals, idxs, hots = [], [], []
    for _ in range(TOP_K):
        m = jnp.max(logits, 0, keepdims=True)
        idx = jnp.min(jnp.where(logits == m, eid, n_e), 0, keepdims=True)
        hot = eid == idx
        vals.append(m)
        idxs.append(idx)
        hots.append(hot)
        logits = jnp.where(hot, -jnp.inf, logits)
    ex = [jnp.exp(v - vals[0]) for v in vals]
    den = ex[0]
    for e in ex[1:]:
        den = den + e
    ti_ref[...] = jnp.concatenate(idxs, 0)
    wrows = jnp.concatenate([e / den for e in ex] + [jnp.zeros((LANES - TOP_K, tm), F32)], 0)
    wcol_ref[...] = wrows.T
    sel = hots[0]
    for hot in hots[1:]:
        sel = sel | hot
    sel_f = jnp.where(sel, 1.0, 0.0)
    before = lax.broadcasted_iota(I32, (tm, tm), 0) < lax.broadcasted_iota(I32, (tm, tm), 1)
    upper = jnp.where(before, 1.0, 0.0).astype(BF16)
    ahead = jnp.dot(sel_f.astype(BF16), upper, preferred_element_type=F32)
    rk_ref[...] = jnp.concatenate(
        [jnp.sum(jnp.where(hot, ahead, 0.0), 0, keepdims=True) for hot in hots], 0).astype(I32)
    cnt = jnp.sum(sel_f, 1, keepdims=True).astype(I32)
    cnt_ref[0] = jnp.broadcast_to(cnt, (n_e, LANES))


def _merge(o, yact, sg, x, wts, alpha, n_total, row_off, tm, prev):
    n, d = x.shape
    wa, wc, bc, wo, g1, b1, rwt, rb = wts
    n_e = rwt.shape[0]
    off = row_off // tm
    last = n // tm - 1
    steps = n // tm if prev is not None else n_total // tm
    row = lambda width: pl.BlockSpec((tm, width), lambda i: (jnp.minimum(i, last), 0))
    full = lambda a: pl.BlockSpec(a.shape, lambda i: (0, 0))
    in_specs = [row(o.shape[1]), row(yact.shape[1]), row(2 * d), row(d),
                full(wa), full(wc), full(bc), full(wo), full(g1), full(b1), full(rwt), full(rb)]
    args = [o, yact, sg, x, wa, wc, bc, wo, g1, b1, rwt, rb]
    aliases = {}
    if prev is not None:
        in_specs += [pl.BlockSpec(memory_space=pl.ANY)] * N_MERGE_OUT
        aliases = {len(args) + j: j for j in range(N_MERGE_OUT)}
        args += list(prev)
    slot_major = pl.BlockSpec((TOP_K, tm), lambda i: (0, off + i))
    return pl.pallas_call(
        functools.partial(_merge_body, alpha=alpha, n_alias=0 if prev is None else N_MERGE_OUT),
        grid=(steps,),
        in_specs=in_specs,
        out_specs=(pl.BlockSpec((tm * SUBLANES, LANES), lambda i: (off + i, 0)), slot_major, slot_major,
                   pl.BlockSpec((tm, LANES), lambda i: (off + i, 0)),
                   pl.BlockSpec((1, n_e, LANES), lambda i: (off + i, 0, 0))),
        out_shape=(jax.ShapeDtypeStruct((n_total * SUBLANES, LANES), F32),
                   jax.ShapeDtypeStruct((TOP_K, n_total), I32),
                   jax.ShapeDtypeStruct((TOP_K, n_total), I32),
                   jax.ShapeDtypeStruct((n_total, LANES), F32),
                   jax.ShapeDtypeStruct((n_total // tm, n_e, LANES), I32)),
        input_output_aliases=aliases,
        compiler_params=_cparams("parallel"),
        name="merge_ln1_router",
    )(*args)


def _plan(topi, rank, cnt, tm_tok, tm_moe):
    k, n = topi.shape
    n_e = cnt.shape[1]
    before = jnp.cumsum(cnt, 0) - cnt
    total = jnp.sum(cnt, 0)
    ntile_e = (total + tm_moe - 1) // tm_moe
    tend = jnp.cumsum(ntile_e)
    tstart = tend - ntile_e
    base = (tstart * tm_moe)[None, :] + before
    base_tok = jnp.repeat(base.T, tm_tok, axis=1)
    hot = topi[:, None, :] == jnp.arange(n_e, dtype=I32)[None, :, None]
    pos = (jnp.sum(jnp.where(hot, base_tok[None], 0), 1).astype(I32) + rank) * SUBLANES
    n_tiles = (k * n + tm_moe - 1) // tm_moe + n_e
    n_used = tend[-1]
    tid = jnp.arange(n_tiles, dtype=I32)
    src = jnp.minimum(tid, n_used - 1)
    te = jnp.minimum(jnp.sum((src[:, None] >= tend[None, :]).astype(I32), 1), n_e - 1)
    first = (tid == tstart[te]).astype(I32)
    half = (total[te] - (tid - tstart[te]) * tm_moe <= tm_moe // 2).astype(I32)
    slot = (jnp.cumsum(first) - 1) % 2
    nxt = jnp.where(tend[te] < n_used, te[jnp.minimum(tend[te], n_tiles - 1)], -1)
    sched = jnp.stack([te, first, half, slot, nxt]).astype(I32)
    last_tile = jnp.where(ntile_e > 0, tend - 1, -1).astype(I32)
    return pos, sched, n_used.reshape(1).astype(I32), last_tile, n_tiles


def _dispatch_body(lt_ref, nu_ref, pos_ref, x_ref, xs_hbm, zbuf, zsem, sem, *, tm, tm_moe, n_e, n_tiles):
    i = pl.program_id(0)

    def zero_copy(tile):
        start = pl.multiple_of(tile * (tm_moe * SUBLANES), tm_moe * SUBLANES)
        return pltpu.make_async_copy(zbuf, xs_hbm.at[pl.ds(start, tm_moe * SUBLANES), :], zsem)

    @pl.when(i == 0)
    def _():
        zbuf[...] = jnp.zeros(zbuf.shape, F32)

        def z_start(e, c):
            @pl.when(lt_ref[e] >= 0)
            def _():
                zero_copy(lt_ref[e]).start()
            return c

        def z_wait(e, c):
            @pl.when(lt_ref[e] >= 0)
            def _():
                zero_copy(lt_ref[e]).wait()
            return c

        def t_start(t, c):
            zero_copy(t).start()
            return c

        def t_wait(t, c):
            zero_copy(t).wait()
            return c

        lax.fori_loop(0, n_e, z_start, 0)
        lax.fori_loop(nu_ref[0], n_tiles, t_start, 0)
        lax.fori_loop(0, n_e, z_wait, 0)
        lax.fori_loop(nu_ref[0], n_tiles, t_wait, 0)

    def issue(g, c):
        base = pl.multiple_of(g * SUBLANES, SUBLANES)
        for rr in range(SUBLANES):
            src = x_ref.at[pl.ds(pl.multiple_of((base + rr) * SUBLANES, SUBLANES), SUBLANES), :]
            for k in range(TOP_K):
                dst = pl.multiple_of(pos_ref[k, base + rr], SUBLANES)
                pltpu.make_async_copy(src, xs_hbm.at[pl.ds(dst, SUBLANES), :], sem).start()
        return c

    lax.fori_loop(0, tm // SUBLANES, issue, 0)
    for k in range(TOP_K):
        pltpu.make_async_copy(x_ref, xs_hbm.at[pl.ds(0, tm * SUBLANES), :], sem).wait()


def _dispatch(h1, pos, last_tile, n_used, n_tiles, tm, tm_moe):
    n = h1.shape[0] // SUBLANES
    n_e = last_tile.shape[0]
    grid_spec = pltpu.PrefetchScalarGridSpec(
        num_scalar_prefetch=2,
        grid=(n // tm,),
        in_specs=[pl.BlockSpec((TOP_K, tm), lambda i, lt, nu: (0, i), memory_space=pltpu.SMEM),
                  pl.BlockSpec((tm * SUBLANES, LANES), lambda i, lt, nu: (i, 0))],
        out_specs=pl.BlockSpec(memory_space=pl.ANY),
        scratch_shapes=[pltpu.VMEM((tm_moe * SUBLANES, LANES), F32), pltpu.SemaphoreType.DMA,
                        pltpu.SemaphoreType.DMA],
    )
    return pl.pallas_call(
        functools.partial(_dispatch_body, tm=tm, tm_moe=tm_moe, n_e=n_e, n_tiles=n_tiles),
        grid_spec=grid_spec,
        out_shape=jax.ShapeDtypeStruct((n_tiles * tm_moe * SUBLANES, LANES), F32),
        compiler_params=_cparams("arbitrary"),
        name="moe_dispatch",
    )(last_tile, n_used, pos, h1)


TE, FIRST, HALF, SLOT, NEXT = range(5)


def _experts_body(sch_ref, nu_ref, x_ref, w1_hbm, b1_ref, w2_hbm, b2_ref, y_ref, w1s, w2s, w1b, w2b, sems, *, tm):
    i = pl.program_id(0)
    d_ff = w2b.shape[0]

    def fetch(expert, slot):
        return (pltpu.make_async_copy(w1_hbm.at[expert], w1s.at[slot], sems.at[0, slot]),
                pltpu.make_async_copy(w2_hbm.at[expert], w2s.at[slot], sems.at[1, slot]))

    def run(rows):
        x = _load_rows(x_ref, rows).astype(BF16)
        hh = jnp.dot(x, w1b[...], preferred_element_type=F32) + b1_ref[0]
        g = jnp.minimum(hh[:, :d_ff], SWIGLU_LIMIT)
        u = jnp.clip(hh[:, d_ff:], -SWIGLU_LIMIT, SWIGLU_LIMIT)
        act = (u + 1.0) * g * jax.nn.sigmoid(SWIGLU_ALPHA * g)
        _store_rows(y_ref, jnp.dot(act.astype(BF16), w2b[...], preferred_element_type=F32) + b2_ref[0])

    @pl.when(i == 0)
    def _():
        for cp in fetch(sch_ref[TE, 0], 0):
            cp.start()

    @pl.when(i >= nu_ref[0])
    def _():
        y_ref[...] = jnp.zeros(y_ref.shape, F32)

    @pl.when(i < nu_ref[0])
    def _():
        @pl.when(sch_ref[FIRST, i] == 1)
        def _():
            slot = sch_ref[SLOT, i]

            @pl.when(sch_ref[NEXT, i] >= 0)
            def _():
                for cp in fetch(sch_ref[NEXT, i], 1 - slot):
                    cp.start()

            for cp in fetch(sch_ref[TE, i], slot):
                cp.wait()
            w1b[...] = w1s[slot].astype(BF16)
            w2b[...] = w2s[slot].astype(BF16)

        @pl.when(sch_ref[HALF, i] == 0)
        def _():
            run(tm)

        @pl.when(sch_ref[HALF, i] == 1)
        def _():
            run(tm // 2)
            y_ref[pl.ds(tm // 2 * SUBLANES, tm // 2 * SUBLANES), :] = jnp.zeros((tm // 2 * SUBLANES, LANES), F32)


def _experts(xs, sched, n_used, w1, b1, w2, b2, tm):
    n_e, d, f2 = w1.shape
    rows = tm * SUBLANES
    by_tile = lambda i, sch, nu: (i, 0)
    by_expert = lambda i, sch, nu: (sch[TE, i], 0, 0)
    grid_spec = pltpu.PrefetchScalarGridSpec(
        num_scalar_prefetch=2,
        grid=(xs.shape[0] // rows,),
        in_specs=[
            pl.BlockSpec((rows, LANES), by_tile),
            pl.BlockSpec(memory_space=pl.ANY),
            pl.BlockSpec((1, 1, f2), by_expert),
            pl.BlockSpec(memory_space=pl.ANY),
            pl.BlockSpec((1, 1, d), by_expert),
        ],
        out_specs=pl.BlockSpec((rows, LANES), by_tile),
        scratch_shapes=[pltpu.VMEM((2, d, f2), F32), pltpu.VMEM((2, f2 // 2, d), F32),
                        pltpu.VMEM((d, f2), BF16), pltpu.VMEM((f2 // 2, d), BF16), pltpu.SemaphoreType.DMA((2, 2))],
    )
    return pl.pallas_call(
        functools.partial(_experts_body, tm=tm),
        grid_spec=grid_spec,
        out_shape=jax.ShapeDtypeStruct(xs.shape, F32),
        compiler_params=_cparams("arbitrary"),
        name="moe_experts",
    )(sched, n_used, xs, w1, b1.reshape(n_e, 1, f2), w2, b2.reshape(n_e, 1, d))


def _combine_body(pos_ref, nxt_ref, h1_ref, wcol_ref, ys_hbm, g_ref, b_ref, o_ref, ybuf, sems, *, alpha, tm):
    i = pl.program_id(0)
    n = pl.num_programs(0)

    def fetch(p_ref, slot):
        def issue(g, c):
            base = pl.multiple_of(g * SUBLANES, SUBLANES)
            for rr in range(SUBLANES):
                dst = pl.ds(pl.multiple_of((base + rr) * SUBLANES, SUBLANES), SUBLANES)
                for k in range(TOP_K):
                    src = pl.multiple_of(p_ref[k, base + rr], SUBLANES)
                    pltpu.make_async_copy(ys_hbm.at[pl.ds(src, SUBLANES), :], ybuf.at[slot, k, dst, :],
                                          sems.at[slot]).start()
            return c
        lax.fori_loop(0, tm // SUBLANES, issue, 0)

    @pl.when(i == 0)
    def _():
        fetch(pos_ref, 0)

    @pl.when(i + 1 < n)
    def _():
        fetch(nxt_ref, (i + 1) % 2)

    slot = i % 2
    for k in range(TOP_K):
        pltpu.make_async_copy(ys_hbm.at[pl.ds(0, tm * SUBLANES), :], ybuf.at[slot, k], sems.at[slot]).wait()
    ff = wcol_ref[:, 0:1] * _load_rows(ybuf.at[slot, 0], tm)
    for k in range(1, TOP_K):
        ff = ff + wcol_ref[:, k:k + 1] * _load_rows(ybuf.at[slot, k], tm)
    o_ref[...] = _layer_norm(alpha * _load_rows(h1_ref, tm) + ff, g_ref[...], b_ref[...])


def _combine(h1, wcol, pos, ys, g, b, alpha, row_off, n):
    d = g.shape[1]
    tm = min(COMBINE_TILE, n)
    off = row_off // tm
    last = n // tm - 1
    vec = pl.BlockSpec((1, d), lambda i: (0, 0))
    return pl.pallas_call(
        functools.partial(_combine_body, alpha=alpha, tm=tm),
        grid=(n // tm,),
        in_specs=[pl.BlockSpec((TOP_K, tm), lambda i: (0, off + i), memory_space=pltpu.SMEM),
                  pl.BlockSpec((TOP_K, tm), lambda i: (0, off + jnp.minimum(i + 1, last)), memory_space=pltpu.SMEM),
                  pl.BlockSpec((tm * SUBLANES, LANES), lambda i: (off + i, 0)),
                  pl.BlockSpec((tm, LANES), lambda i: (off + i, 0)),
                  pl.BlockSpec(memory_space=pl.ANY), vec, vec],
        out_specs=pl.BlockSpec((tm, d), lambda i: (i, 0)),
        out_shape=jax.ShapeDtypeStruct((n, d), F32),
        scratch_shapes=[pltpu.VMEM((2, TOP_K, tm * SUBLANES, LANES), F32), pltpu.SemaphoreType.DMA((2,))],
        compiler_params=_cparams("arbitrary"),
        name="combine_ln2",
    )(pos, pos, h1, wcol, ys, g, b)


def _row2(v):
    return v.reshape(1, -1).astype(F32)


def kernel(x_prompt, x_sample, cache_k, cache_v, page_table, state_conv, w_in, b_in, lambda_q1, lambda_k1,
           lambda_q2, lambda_k2, subln_g, rel_bias, w_attn_proj, conv_w, conv_b, conv_ln_g, conv_ln_b,
           w_conv_proj, b_conv_proj, w_out, ln1_g, ln1_b, router_w, router_b, expert_w1, expert_b1,
           expert_w2, expert_b2, ln2_g, ln2_b):
    depth = w_in.shape[0]
    bp, seq, d = x_prompt.shape
    bs, t_new, _ = x_sample.shape
    d_att = N_HEADS * QK_DIM
    d_conv = conv_w.shape[2]
    n_p, n_s = bp * seq, bs * t_new
    n_tot = n_p + n_s
    tm = min(TOKEN_TILE, math.gcd(n_p, n_s))
    assert tm % LANES == 0
    tm_moe = min(MOE_TILE, n_tot)
    alpha = (2 * depth) ** 0.25

    hp = x_prompt.reshape(n_p, d)
    hs = x_sample.reshape(n_s, d)
    outs = [[] for _ in range(6)]
    for l in range(depth):
        lam_init = 0.8 - 0.6 * math.exp(-0.3 * l)
        lam = (jnp.exp(jnp.sum(lambda_q1[l].astype(F32) * lambda_k1[l].astype(F32)))
               - jnp.exp(jnp.sum(lambda_q2[l].astype(F32) * lambda_k2[l].astype(F32))) + lam_init).reshape(1)
        w_in_bf = w_in[l].astype(BF16)
        b_in_l = _row2(b_in[l])
        g_sub = _row2(subln_g[l])
        conv_args = (conv_w[l].astype(F32), _row2(conv_b[l]), _row2(conv_ln_g[l]), _row2(conv_ln_b[l]))
        merge_w = (w_attn_proj[l].astype(BF16), w_conv_proj[l].astype(BF16), _row2(b_conv_proj[l]),
                   w_out[l].astype(BF16), _row2(ln1_g[l]), _row2(ln1_b[l]),
                   router_w[l].astype(F32).T, router_b[l].astype(F32).reshape(-1, 1))

        qp, kp, vp, kbp, vbp, up, sgp = _in_proj(hp, w_in_bf, b_in_l, d_att, d_conv, (bp, seq) + _attn_tiles(seq))
        op = _prompt_attention(qp, kbp, vbp, rel_bias, lam, g_sub, lam_init)
        up3 = up.reshape(bp, seq, d_conv)
        yp = _conv_prompt(up3, *conv_args)
        merged = _merge(op.reshape(n_p, d_att), yp.reshape(n_p, d_conv), sgp, hp, merge_w, alpha, n_tot, 0, tm, None)

        qs, ks, vs, kbs, vbs, us, sgs = _in_proj(hs, w_in_bf, b_in_l, d_att, d_conv)
        osr = _sample_attention(qs.reshape(bs, t_new, d_att), kbs.reshape(bs, t_new, d_att),
                                vbs.reshape(bs, t_new, d_att), cache_k[l], cache_v[l], page_table,
                                rel_bias, lam, g_sub, lam_init)
        os_ = osr[:, :t_new].reshape(n_s, d_att)
        buf_s = jnp.concatenate([state_conv[l].astype(F32), us.reshape(bs, t_new, d_conv)], axis=1)
        ys = _conv_sample(buf_s.transpose(1, 0, 2), *conv_args).transpose(1, 0, 2).reshape(n_s, d_conv)
        h1, topi, rank, wcol, cnt = _merge(os_, ys, sgs, hs, merge_w, alpha, n_tot, n_p, tm, merged)

        pos, sched, n_used, last_tile, n_tiles = _plan(topi, rank, cnt[:, :, 0], tm, tm_moe)
        xs = _dispatch(h1, pos, last_tile, n_used, n_tiles, tm, tm_moe)
        ysort = _experts(xs, sched, n_used, expert_w1[l], expert_b1[l].astype(F32),
                         expert_w2[l], expert_b2[l].astype(F32), tm_moe)
        g2, b2 = _row2(ln2_g[l]), _row2(ln2_b[l])
        hp = _combine(h1, wcol, pos, ysort, g2, b2, alpha, 0, n_p)
        hs = _combine(h1, wcol, pos, ysort, g2, b2, alpha, n_p, n_s)

        w1 = CONV_WIDTH - 1
        cp = up3[:, seq - w1:] if seq >= w1 else jnp.concatenate(
            [jnp.zeros((bp, w1 - seq, d_conv), F32), up3], axis=1)
        for lst, val in zip(outs, (kp.reshape(bp, seq, N_HEADS, QK_DIM), vp.reshape(bp, seq, N_HEADS, V_DIM), cp,
                                   ks.reshape(bs, t_new, N_HEADS, QK_DIM), vs.reshape(bs, t_new, N_HEADS, V_DIM),
                                   buf_s[:, t_new:])):
            lst.append(val)
    return (hp.reshape(bp, seq, d), hs.reshape(bs, t_new, d)) + tuple(jnp.stack(o) for o in outs)
```

```python
import functools
import math

import jax
import jax.numpy as jnp
from jax import lax
from jax.experimental import pallas as pl
from jax.experimental.pallas import tpu as pltpu

F32 = jnp.float32
BF16 = jnp.bfloat16
I32 = jnp.int32

N_HEADS = 4
HEAD_DIM = 64
QK_DIM = 2 * HEAD_DIM
V_DIM = 2 * HEAD_DIM
ATTN_SCALE = HEAD_DIM ** -0.5
LOG2E = 1.4426950408889634
NEG_INF = -1e30
NUM_BUCKETS = 32
MAX_EXACT = NUM_BUCKETS // 2
MAX_DISTANCE = 128
CONV_WIDTH = 31
TOP_K = 4
SWIGLU_LIMIT = 7.0
SWIGLU_ALPHA = 1.702
LN_EPS = 1e-5
PAGE_SIZE = 128

LANES = 128
SUBLANES = 8
VMEM_LIMIT = 56 * 1024 * 1024

TOKEN_TILE = 512
ATTN_TQ = 1024
ATTN_TK = 1024
ATTN_GROUP = 2
MOE_TILE = 512
COMBINE_TILE = 256
CONV_HALO = 32
CONV_CHUNK = 64


def _cparams(*sem):
    return pltpu.CompilerParams(dimension_semantics=sem, vmem_limit_bytes=VMEM_LIMIT)


def _load_rows(ref, tm):
    return jnp.concatenate([ref[pl.ds(c, tm, stride=SUBLANES), :] for c in range(SUBLANES)], axis=1)


def _store_rows(ref, x):
    tm = x.shape[0]
    for c in range(SUBLANES):
        ref[pl.ds(c, tm, stride=SUBLANES), :] = x[:, c * LANES:(c + 1) * LANES]


def _layer_norm(x, g, b):
    mu = jnp.mean(x, -1, keepdims=True)
    xc = x - mu
    var = jnp.mean(xc * xc, -1, keepdims=True)
    return xc * lax.rsqrt(var + LN_EPS) * g + b


def _in_proj_body(x_ref, w_ref, b_ref, q_ref, k_ref, v_ref, kb_ref, vb_ref, u_ref, sg_ref, *, d_att, d_conv, by_head):
    x = x_ref[...].astype(BF16)

    def seg(lo, hi):
        return jnp.dot(x, w_ref[:, lo:hi], preferred_element_type=F32) + b_ref[:, lo:hi]

    def head(a, h):
        return a[:, h * QK_DIM:(h + 1) * QK_DIM]

    o = 0
    q = seg(o, o + d_att) * (ATTN_SCALE * LOG2E)
    o += d_att
    k = seg(o, o + d_att)
    o += d_att
    v = seg(o, o + d_att)
    o += d_att
    tm = x.shape[0]
    for h in range(N_HEADS):
        k_ref[pl.ds(h, tm, stride=N_HEADS), :] = head(k, h)
        v_ref[pl.ds(h, tm, stride=N_HEADS), :] = head(v, h)
    if by_head:
        for h in range(N_HEADS):
            q_ref[h] = head(q, h).T.astype(BF16)
            kb_ref[h] = head(k, h).astype(BF16)
            vb_ref[h] = head(v, h).T.astype(BF16)
    else:
        q_ref[...] = q.astype(BF16)
        kb_ref[...] = k.astype(BF16)
        vb_ref[...] = v.astype(BF16)
    c = seg(o, o + 2 * d_conv)
    u_ref[...] = c[:, :d_conv] * jax.nn.sigmoid(c[:, d_conv:])
    o += 2 * d_conv
    sg_ref[...] = jax.nn.sigmoid(seg(o, w_ref.shape[1])).astype(BF16)


def _in_proj(x, w_bf, b, d_att, d_conv, attn_tiles=None):
    n, d = x.shape
    tm = min(TOKEN_TILE, n)
    row = lambda width: pl.BlockSpec((tm, width), lambda i: (i, 0))
    full = lambda a: pl.BlockSpec(a.shape, lambda i: (0, 0))
    f32_out = lambda width: jax.ShapeDtypeStruct((n, width), F32)
    kv_shape = jax.ShapeDtypeStruct((n * N_HEADS, QK_DIM), F32)
    kv_spec = pl.BlockSpec((tm * N_HEADS, QK_DIM), lambda i: (i, 0))
    if attn_tiles is None:
        qkv_shapes = [jax.ShapeDtypeStruct((n, d_att), BF16)] * 3
        qkv_specs = [row(d_att)] * 3
    else:
        bsz, seq, tq, tk = attn_tiles
        assert seq % tm == 0 and tq % tm == 0 and tk % tm == 0
        per_seq = seq // tm

        def spec(t, transposed):
            parts = t // tm
            blk = (None, N_HEADS, None, QK_DIM, tm) if transposed else (None, N_HEADS, None, tm, QK_DIM)

            def index(i):
                ti = i % per_seq
                tile, part = ti // parts, ti % parts
                return (i // per_seq, 0, tile, 0, part) if transposed else (i // per_seq, 0, tile, part, 0)
            return pl.BlockSpec(blk, index)

        qkv_shapes = [jax.ShapeDtypeStruct((bsz, N_HEADS, seq // tq, QK_DIM, tq), BF16),
                      jax.ShapeDtypeStruct((bsz, N_HEADS, seq // tk, tk, QK_DIM), BF16),
                      jax.ShapeDtypeStruct((bsz, N_HEADS, seq // tk, V_DIM, tk), BF16)]
        qkv_specs = [spec(tq, True), spec(tk, False), spec(tk, True)]
    out_shape = (qkv_shapes[0], kv_shape, kv_shape, qkv_shapes[1], qkv_shapes[2],
                 f32_out(d_conv), jax.ShapeDtypeStruct((n, 2 * d), BF16))
    return pl.pallas_call(
        functools.partial(_in_proj_body, d_att=d_att, d_conv=d_conv, by_head=attn_tiles is not None),
        grid=(n // tm,),
        in_specs=[row(d), full(w_bf), full(b)],
        out_specs=(qkv_specs[0], kv_spec, kv_spec, qkv_specs[1], qkv_specs[2], row(d_conv), row(2 * d)),
        out_shape=out_shape,
        compiler_params=_cparams("parallel"),
        name="in_proj",
    )(x, w_bf, b)


def _t5_bucket(dist):
    n = jnp.maximum(dist, 0)
    nf = jnp.maximum(n, 1).astype(F32)
    large = MAX_EXACT + (jnp.log(nf / MAX_EXACT) / math.log(MAX_DISTANCE / MAX_EXACT)
                         * (NUM_BUCKETS - MAX_EXACT)).astype(I32)
    large = jnp.minimum(large, NUM_BUCKETS - 1)
    return jnp.where(n < MAX_EXACT, n, large)


def _bias_of(dist, rel_bias, shift=None):
    hot = _t5_bucket(dist)[..., None] == jnp.arange(NUM_BUCKETS, dtype=I32)
    table = rel_bias.astype(F32) if shift is None else rel_bias.astype(F32) - shift
    heads = [jnp.sum(jnp.where(hot, table[:, h], 0.0), -1) for h in range(table.shape[1])]
    return jnp.where(dist >= 0, jnp.stack(heads) * LOG2E, NEG_INF)


def _flash_body(lam_ref, qt_ref, k_ref, vt_ref, btab_ref, g_ref, o_ref, m_s, l_s, acc_s, bias_ref, *, ratio, group,
                lam_init):
    i = pl.program_id(2)
    tk, tq = bias_ref.shape[1:]

    @pl.when(i == 0)
    def _():
        for o in range(ratio + 1):
            table = jnp.broadcast_to(btab_ref[o], (tk, btab_ref.shape[2]))
            bias_ref[o] = pltpu.roll(table, 0, 1, stride=1, stride_axis=0)[:, :tq]

    sub = lax.broadcasted_iota(I32, (QK_DIM, 1), 0)
    qs = []
    for b in range(group):
        qt = qt_ref[b]
        zero = jnp.zeros_like(qt)
        qs.append((jnp.where(sub < HEAD_DIM, qt, zero), jnp.where(sub >= HEAD_DIM, qt, zero)))
    m_s[...] = jnp.full(m_s.shape, NEG_INF, F32)
    l_s[...] = jnp.zeros(l_s.shape, F32)
    acc_s[...] = jnp.zeros(acc_s.shape, F32)

    def step(j, bias, keys=slice(None), cols=slice(None)):
        for b in range(group):
            k = k_ref[b, j, keys, :]
            vt = vt_ref[b, j, :, keys]
            for c in range(2):
                s = jnp.dot(k, qs[b][c][:, cols], preferred_element_type=F32)
                if bias is not None:
                    s = s + bias
                m_old = m_s[b, c, :, cols]
                m_new = jnp.maximum(m_old, jnp.max(s, 0, keepdims=True))
                alpha = jnp.exp2(m_old - m_new)
                p = jnp.exp2(s - m_new)
                l_s[b, c, :, cols] = alpha * l_s[b, c, :, cols] + jnp.sum(p, 0, keepdims=True)
                acc_s[b, c, :, cols] = alpha * acc_s[b, c, :, cols] + jnp.dot(
                    vt, p.astype(BF16), preferred_element_type=F32)
                m_s[b, c, :, cols] = m_new

    def far_step(j, carry):
        step(j, None)
        return carry

    lax.fori_loop(0, jnp.maximum(i * ratio - 1, 0), far_step, 0)

    @pl.when(i >= 1)
    def _():
        step(i * ratio - 1, bias_ref[0])

    for o in range(ratio - 1):
        step(i * ratio + o, bias_ref[o + 1])
    last, half, first_col = i * ratio + ratio - 1, tk // 2, tq - tk // 2
    step(last, bias_ref[ratio, 0:half, :], keys=slice(0, half))
    step(last, bias_ref[ratio, half:tk, first_col:tq], keys=slice(half, tk), cols=slice(first_col, tq))

    for b in range(group):
        o = acc_s[b, 0] / l_s[b, 0] - lam_ref[0] * (acc_s[b, 1] / l_s[b, 1])
        o = o * lax.rsqrt(jnp.mean(o * o, 0, keepdims=True) + LN_EPS) * g_ref[...]
        o_ref[b] = (o * (1.0 - lam_init)).T.astype(BF16)


def _attn_tiles(seq):
    tq = min(ATTN_TQ, seq)
    tk = min(ATTN_TK, tq)
    assert tk >= MAX_DISTANCE and seq % tq == 0 and tq % tk == 0
    return tq, tk


def _prompt_attention(qt, k4, vt, rel_bias, lam, subln_g, lam_init):
    bsz, _, nq, _, tq = qt.shape
    nk, tk = k4.shape[2], k4.shape[3]
    seq, ratio, width = nq * tq, tq // tk, N_HEADS * V_DIM
    group = ATTN_GROUP if bsz % ATTN_GROUP == 0 else 1
    far = rel_bias[NUM_BUCKETS - 1].astype(F32)
    period = tq + tk
    assert period % LANES == 0
    y = jnp.arange(period, dtype=I32)
    r_minus_c = jnp.where(y < tq, y, y - period)
    dist = r_minus_c[None, :] - (jnp.arange(ratio + 1, dtype=I32)[:, None] - 1) * tk
    btab = _bias_of(dist, rel_bias, far).reshape(N_HEADS, ratio + 1, 1, period)
    return pl.pallas_call(
        functools.partial(_flash_body, ratio=ratio, group=group, lam_init=lam_init),
        grid=(bsz // group, N_HEADS, nq),
        in_specs=[
            pl.BlockSpec(memory_space=pltpu.SMEM),
            pl.BlockSpec((group, None, None, QK_DIM, tq), lambda b, h, i: (b, h, i, 0, 0)),
            pl.BlockSpec((group, None, nk, tk, QK_DIM), lambda b, h, i: (b, h, 0, 0, 0)),
            pl.BlockSpec((group, None, nk, V_DIM, tk), lambda b, h, i: (b, h, 0, 0, 0)),
            pl.BlockSpec((None, ratio + 1, 1, period), lambda b, h, i: (h, 0, 0, 0)),
            pl.BlockSpec((V_DIM, 1), lambda b, h, i: (0, 0)),
        ],
        out_specs=pl.BlockSpec((group, tq, V_DIM), lambda b, h, i: (b, i, h)),
        out_shape=jax.ShapeDtypeStruct((bsz, seq, width), BF16),
        scratch_shapes=[pltpu.VMEM((group, 2, 1, tq), F32), pltpu.VMEM((group, 2, 1, tq), F32),
                        pltpu.VMEM((group, 2, V_DIM, tq), F32), pltpu.VMEM((ratio + 1, tk, tq), F32)],
        compiler_params=_cparams("parallel", "parallel", "arbitrary"),
        name="prompt_attention",
    )(lam, qt, k4, vt, btab, subln_g.reshape(V_DIM, 1))


def _paged_body(pt_ref, lam_ref, q_ref, kn_ref, vn_ref, bias_ref, g_ref, *refs, n_pages, t_new, lam_init):
    k_refs = refs[:n_pages]
    v_refs = refs[n_pages:2 * n_pages]
    o_ref = refs[2 * n_pages]
    lam = lam_ref[0]
    rows = 2 * t_new
    cols = PAGE_SIZE * N_HEADS
    nt = (((1,), (1,)), ((), ()))
    q = q_ref[0]
    parts = [lax.dot_general(q, k_refs[p][...].astype(BF16), nt, preferred_element_type=F32)
             for p in range(n_pages)]
    parts.append(lax.dot_general(q, kn_ref[0], nt, preferred_element_type=F32))
    s = jnp.concatenate(parts, axis=1) + bias_ref[...]
    m = jnp.max(s, -1, keepdims=True)
    p_un = jnp.exp2(s - m)
    pn = p_un / jnp.sum(p_un, -1, keepdims=True)
    w = (pn - lam * pltpu.roll(pn, N_HEADS * rows - t_new, axis=0)).astype(BF16)
    o = jnp.dot(w[:, n_pages * cols:], vn_ref[0], preferred_element_type=F32)
    for p in range(n_pages):
        o = o + jnp.dot(w[:, p * cols:(p + 1) * cols], v_refs[p][...].astype(BF16), preferred_element_type=F32)
    o = o * lax.rsqrt(jnp.mean(o * o, -1, keepdims=True) + LN_EPS) * g_ref[...]
    o = (o * (1.0 - lam_init)).astype(BF16)
    o_ref[0] = jnp.concatenate([o[h * rows:(h + 1) * rows] for h in range(N_HEADS)], axis=1)


def _sample_attention(q, kb, vb, cache_k, cache_v, page_table, rel_bias, lam, subln_g, lam_init):
    bs, t_new, width = q.shape
    n_pool = cache_k.shape[0]
    n_pages = page_table.shape[1]
    past = n_pages * PAGE_SIZE
    rows = 2 * t_new
    cols = PAGE_SIZE * N_HEADS
    new_pos = LANES // N_HEADS
    assert rows % SUBLANES == 0 and t_new <= new_pos
    qh = q.reshape(bs, t_new, N_HEADS, QK_DIM).transpose(0, 2, 1, 3)
    lane_map = (jnp.arange(QK_DIM) // HEAD_DIM)[None, :] == jnp.arange(2)[:, None]
    q_rows = (qh[:, :, None] * lane_map[None, None, :, None, :].astype(BF16)).reshape(bs, N_HEADS * rows, QK_DIM)
    pad = ((0, 0), (0, LANES - t_new * N_HEADS), (0, 0))
    kn = jnp.pad(kb.reshape(bs, t_new * N_HEADS, QK_DIM), pad)
    vn = jnp.pad(vb.reshape(bs, t_new * N_HEADS, V_DIM), pad)
    qpos = past + jnp.arange(t_new, dtype=I32)
    kpos = jnp.concatenate([jnp.arange(past + t_new, dtype=I32),
                            jnp.full((new_pos - t_new,), past + t_new + new_pos, I32)])
    col_pos = jnp.repeat(kpos, N_HEADS)
    col_head = jnp.tile(jnp.arange(N_HEADS, dtype=I32), kpos.shape[0])
    b = _bias_of(qpos[:, None] - col_pos[None, :], rel_bias)
    b = jnp.where(col_head[None, None, :] == jnp.arange(N_HEADS, dtype=I32)[:, None, None], b, NEG_INF)
    bias = jnp.stack([b, b], axis=1).reshape(N_HEADS * rows, -1)

    page_spec = lambda j: pl.BlockSpec((None, cols, QK_DIM), lambda b, pt, j=j: (pt[b * n_pages + j], 0, 0))
    per_seq = lambda r, c: pl.BlockSpec((1, r, c), lambda b, pt: (b, 0, 0))
    grid_spec = pltpu.PrefetchScalarGridSpec(
        num_scalar_prefetch=1,
        grid=(bs,),
        in_specs=[
            pl.BlockSpec(memory_space=pltpu.SMEM),
            per_seq(N_HEADS * rows, QK_DIM),
            per_seq(LANES, QK_DIM),
            per_seq(LANES, V_DIM),
            pl.BlockSpec(bias.shape, lambda b, pt: (0, 0)),
            pl.BlockSpec((1, V_DIM), lambda b, pt: (0, 0)),
        ] + [page_spec(j) for j in range(n_pages)] * 2,
        out_specs=pl.BlockSpec((1, rows, width), lambda b, pt: (b, 0, 0)),
    )
    ck = cache_k.reshape(n_pool, cols, QK_DIM)
    cv = cache_v.reshape(n_pool, cols, V_DIM)
    return pl.pallas_call(
        functools.partial(_paged_body, n_pages=n_pages, t_new=t_new, lam_init=lam_init),
        grid_spec=grid_spec,
        out_shape=jax.ShapeDtypeStruct((bs, rows, width), BF16),
        compiler_params=_cparams("parallel"),
        name="sample_attention",
    )(page_table.reshape(-1), lam, q_rows, kn, vn, bias, subln_g, *([ck] * n_pages), *([cv] * n_pages))


def _conv_prompt_body(u_ref, halo_ref, w_ref, cb_ref, g_ref, b_ref, o_ref, buf, shifted, *, tc):
    i = pl.program_id(1)
    halo = halo_ref[0]
    buf[0:CONV_HALO, :] = jnp.where(i > 0, halo, jnp.zeros_like(halo))
    buf[CONV_HALO:, :] = u_ref[0]
    rows = CONV_HALO + tc - SUBLANES
    for s in range(1, SUBLANES):
        shifted[s - 1, 0:rows, :] = buf[s:s + rows, :]
    first = CONV_HALO - (CONV_WIDTH - 1)
    for r0 in range(0, tc, CONV_CHUNK):
        acc = jnp.zeros((CONV_CHUNK, u_ref.shape[2]), F32) + cb_ref[...]
        for j in range(CONV_WIDTH):
            s = (first + j) % SUBLANES
            a = first + j - s + r0
            win = buf[a:a + CONV_CHUNK, :] if s == 0 else shifted[s - 1, a:a + CONV_CHUNK, :]
            acc = acc + w_ref[j:j + 1, :] * win
        y = _layer_norm(acc, g_ref[...], b_ref[...])
        o_ref[0, r0:r0 + CONV_CHUNK, :] = (y * jax.nn.sigmoid(y)).astype(BF16)


def _conv_prompt(u, conv_w, conv_b, ln_g, ln_b):
    bsz, seq, ch = u.shape
    tc = min(TOKEN_TILE, seq)
    assert seq % tc == 0 and tc % CONV_CHUNK == 0 and tc % CONV_HALO == 0
    vec = pl.BlockSpec((1, ch), lambda b, i: (0, 0))
    return pl.pallas_call(
        functools.partial(_conv_prompt_body, tc=tc),
        grid=(bsz, seq // tc),
        in_specs=[
            pl.BlockSpec((1, tc, ch), lambda b, i: (b, i, 0)),
            pl.BlockSpec((1, CONV_HALO, ch), lambda b, i: (b, jnp.maximum(i * (tc // CONV_HALO) - 1, 0), 0)),
            pl.BlockSpec((CONV_WIDTH, ch), lambda b, i: (0, 0)),
            vec, vec, vec,
        ],
        out_specs=pl.BlockSpec((1, tc, ch), lambda b, i: (b, i, 0)),
        out_shape=jax.ShapeDtypeStruct((bsz, seq, ch), BF16),
        scratch_shapes=[pltpu.VMEM((CONV_HALO + tc, ch), F32), pltpu.VMEM((SUBLANES - 1, CONV_HALO + tc, ch), F32)],
        compiler_params=_cparams("parallel", "arbitrary"),
        name="conv_prompt",
    )(u, u, conv_w, conv_b, ln_g, ln_b)


def _conv_sample_body(buf_ref, w_ref, cb_ref, g_ref, b_ref, o_ref, *, t_new):
    for t in range(t_new):
        acc = jnp.zeros(buf_ref.shape[1:], F32) + cb_ref[...]
        for j in range(CONV_WIDTH):
            acc = acc + w_ref[j:j + 1, :] * buf_ref[t + j]
        y = _layer_norm(acc, g_ref[...], b_ref[...])
        o_ref[t] = (y * jax.nn.sigmoid(y)).astype(BF16)


def _conv_sample(buf_t, conv_w, conv_b, ln_g, ln_b):
    rows, bs, ch = buf_t.shape
    t_new = rows - (CONV_WIDTH - 1)
    gb = min(32, bs)
    assert bs % gb == 0
    vec = pl.BlockSpec((1, ch), lambda i: (0, 0))
    return pl.pallas_call(
        functools.partial(_conv_sample_body, t_new=t_new),
        grid=(bs // gb,),
        in_specs=[pl.BlockSpec((rows, gb, ch), lambda i: (0, i, 0)),
                  pl.BlockSpec((CONV_WIDTH, ch), lambda i: (0, 0)), vec, vec, vec],
        out_specs=pl.BlockSpec((t_new, gb, ch), lambda i: (0, i, 0)),
        out_shape=jax.ShapeDtypeStruct((t_new, bs, ch), BF16),
        compiler_params=_cparams("parallel"),
        name="conv_sample",
    )(buf_t, conv_w, conv_b, ln_g, ln_b)


N_MERGE_OUT = 5


def _merge_body(*refs, alpha, n_alias):
    (o_ref, y_ref, sg_ref, x_ref, wa_ref, wc_ref, bc_ref, wo_ref, g1_ref, b1_ref, rw_ref, rb_ref) = refs[:12]
    h1_ref, ti_ref, rk_ref, wcol_ref, cnt_ref = refs[12 + n_alias:]
    tm, d = x_ref.shape
    a = jnp.dot(o_ref[...], wa_ref[...], preferred_element_type=F32)
    b = jnp.dot(y_ref[...], wc_ref[...], preferred_element_type=F32) + bc_ref[...]
    mix_in = sg_ref[:, :d].astype(F32) * a + sg_ref[:, d:].astype(F32) * b
    mix = jnp.dot(mix_in.astype(BF16), wo_ref[...], preferred_element_type=F32)
    h1 = _layer_norm(alpha * x_ref[...] + mix, g1_ref[...], b1_ref[...])
    _store_rows(h1_ref, h1)
    logits = lax.dot_general(rw_ref[...], h1, (((1,), (1,)), ((), ())), preferred_element_type=F32,
                             precision=lax.Precision.HIGHEST) + rb_ref[...]
    n_e = logits.shape[0]
    eid = lax.broadcasted_iota(I32, logits.shape, 0)
    vals, idxs, hots = [], [], []
    for _ in range(TOP_K):
        m = jnp.max(logits, 0, keepdims=True)
        idx = jnp.min(jnp.where(logits == m, eid, n_e), 0, keepdims=True)
        hot = eid == idx
        vals.append(m)
        idxs.append(idx)
        hots.append(hot)
        logits = jnp.where(hot, -jnp.inf, logits)
    ex = [jnp.exp(v - vals[0]) for v in vals]
    den = ex[0]
    for e in ex[1:]:
        den = den + e
    ti_ref[...] = jnp.concatenate(idxs, 0)
    wrows = jnp.concatenate([e / den for e in ex] + [jnp.zeros((LANES - TOP_K, tm), F32)], 0)
    wcol_ref[...] = wrows.T
    sel = hots[0]
    for hot in hots[1:]:
        sel = sel | hot
    sel_f = jnp.where(sel, 1.0, 0.0)
    before = lax.broadcasted_iota(I32, (tm, tm), 0) < lax.broadcasted_iota(I32, (tm, tm), 1)
    upper = jnp.where(before, 1.0, 0.0).astype(BF16)
    ahead = jnp.dot(sel_f.astype(BF16), upper, preferred_element_type=F32)
    rk_ref[...] = jnp.concatenate(
        [jnp.sum(jnp.where(hot, ahead, 0.0), 0, keepdims=True) for hot in hots], 0).astype(I32)
    cnt = jnp.sum(sel_f, 1, keepdims=True).astype(I32)
    cnt_ref[0] = jnp.broadcast_to(cnt, (n_e, LANES))


def _merge(o, yact, sg, x, wts, alpha, n_total, row_off, tm, prev):
    n, d = x.shape
    wa, wc, bc, wo, g1, b1, rwt, rb = wts
    n_e = rwt.shape[0]
    off = row_off // tm
    last = n // tm - 1
    steps = n // tm if prev is not None else n_total // tm
    row = lambda width: pl.BlockSpec((tm, width), lambda i: (jnp.minimum(i, last), 0))
    full = lambda a: pl.BlockSpec(a.shape, lambda i: (0, 0))
    in_specs = [row(o.shape[1]), row(yact.shape[1]), row(2 * d), row(d),
                full(wa), full(wc), full(bc), full(wo), full(g1), full(b1), full(rwt), full(rb)]
    args = [o, yact, sg, x, wa, wc, bc, wo, g1, b1, rwt, rb]
    aliases = {}
    if prev is not None:
        in_specs += [pl.BlockSpec(memory_space=pl.ANY)] * N_MERGE_OUT
        aliases = {len(args) + j: j for j in range(N_MERGE_OUT)}
        args += list(prev)
    slot_major = pl.BlockSpec((TOP_K, tm), lambda i: (0, off + i))
    return pl.pallas_call(
        functools.partial(_merge_body, alpha=alpha, n_alias=0 if prev is None else N_MERGE_OUT),
        grid=(steps,),
        in_specs=in_specs,
        out_specs=(pl.BlockSpec((tm * SUBLANES, LANES), lambda i: (off + i, 0)), slot_major, slot_major,
                   pl.BlockSpec((tm, LANES), lambda i: (off + i, 0)),
                   pl.BlockSpec((1, n_e, LANES), lambda i: (off + i, 0, 0))),
        out_shape=(jax.ShapeDtypeStruct((n_total * SUBLANES, LANES), F32),
                   jax.ShapeDtypeStruct((TOP_K, n_total), I32),
                   jax.ShapeDtypeStruct((TOP_K, n_total), I32),
                   jax.ShapeDtypeStruct((n_total, LANES), F32),
                   jax.ShapeDtypeStruct((n_total // tm, n_e, LANES), I32)),
        input_output_aliases=aliases,
        compiler_params=_cparams("parallel"),
        name="merge_ln1_router",
    )(*args)


def _plan(topi, rank, cnt, tm_tok, tm_moe):
    k, n = topi.shape
    n_e = cnt.shape[1]
    before = jnp.cumsum(cnt, 0) - cnt
    total = jnp.sum(cnt, 0)
    ntile_e = (total + tm_moe - 1) // tm_moe
    tend = jnp.cumsum(ntile_e)
    tstart = tend - ntile_e
    base = (tstart * tm_moe)[None, :] + before
    base_tok = jnp.repeat(base.T, tm_tok, axis=1)
    hot = topi[:, None, :] == jnp.arange(n_e, dtype=I32)[None, :, None]
    pos = (jnp.sum(jnp.where(hot, base_tok[None], 0), 1).astype(I32) + rank) * SUBLANES
    n_tiles = (k * n + tm_moe - 1) // tm_moe + n_e
    n_used = tend[-1]
    tid = jnp.arange(n_tiles, dtype=I32)
    src = jnp.minimum(tid, n_used - 1)
    te = jnp.minimum(jnp.sum((src[:, None] >= tend[None, :]).astype(I32), 1), n_e - 1)
    first = (tid == tstart[te]).astype(I32)
    half = (total[te] - (tid - tstart[te]) * tm_moe <= tm_moe // 2).astype(I32)
    slot = (jnp.cumsum(first) - 1) % 2
    nxt = jnp.where(tend[te] < n_used, te[jnp.minimum(tend[te], n_tiles - 1)], -1)
    sched = jnp.stack([te, first, half, slot, nxt]).astype(I32)
    last_tile = jnp.where(ntile_e > 0, tend - 1, -1).astype(I32)
    return pos, sched, n_used.reshape(1).astype(I32), last_tile, n_tiles


def _dispatch_body(lt_ref, nu_ref, pos_ref, x_ref, xs_hbm, zbuf, zsem, sem, *, tm, tm_moe, n_e, n_tiles):
    i = pl.program_id(0)

    def zero_copy(tile):
        start = pl.multiple_of(tile * (tm_moe * SUBLANES), tm_moe * SUBLANES)
        return pltpu.make_async_copy(zbuf, xs_hbm.at[pl.ds(start, tm_moe * SUBLANES), :], zsem)

    @pl.when(i == 0)
    def _():
        zbuf[...] = jnp.zeros(zbuf.shape, F32)

        def z_start(e, c):
            @pl.when(lt_ref[e] >= 0)
            def _():
                zero_copy(lt_ref[e]).start()
            return c

        def z_wait(e, c):
            @pl.when(lt_ref[e] >= 0)
            def _():
                zero_copy(lt_ref[e]).wait()
            return c

        def t_start(t, c):
            zero_copy(t).start()
            return c

        def t_wait(t, c):
            zero_copy(t).wait()
            return c

        lax.fori_loop(0, n_e, z_start, 0)
        lax.fori_loop(nu_ref[0], n_tiles, t_start, 0)
        lax.fori_loop(0, n_e, z_wait, 0)
        lax.fori_loop(nu_ref[0], n_tiles, t_wait, 0)

    def issue(g, c):
        base = pl.multiple_of(g * SUBLANES, SUBLANES)
        for rr in range(SUBLANES):
            src = x_ref.at[pl.ds(pl.multiple_of((base + rr) * SUBLANES, SUBLANES), SUBLANES), :]
            for k in range(TOP_K):
                dst = pl.multiple_of(pos_ref[k, base + rr], SUBLANES)
                pltpu.make_async_copy(src, xs_hbm.at[pl.ds(dst, SUBLANES), :], sem).start()
        return c

    lax.fori_loop(0, tm // SUBLANES, issue, 0)
    for k in range(TOP_K):
        pltpu.make_async_copy(x_ref, xs_hbm.at[pl.ds(0, tm * SUBLANES), :], sem).wait()


def _dispatch(h1, pos, last_tile, n_used, n_tiles, tm, tm_moe):
    n = h1.shape[0] // SUBLANES
    n_e = last_tile.shape[0]
    grid_spec = pltpu.PrefetchScalarGridSpec(
        num_scalar_prefetch=2,
        grid=(n // tm,),
        in_specs=[pl.BlockSpec((TOP_K, tm), lambda i, lt, nu: (0, i), memory_space=pltpu.SMEM),
                  pl.BlockSpec((tm * SUBLANES, LANES), lambda i, lt, nu: (i, 0))],
        out_specs=pl.BlockSpec(memory_space=pl.ANY),
        scratch_shapes=[pltpu.VMEM((tm_moe * SUBLANES, LANES), F32), pltpu.SemaphoreType.DMA,
                        pltpu.SemaphoreType.DMA],
    )
    return pl.pallas_call(
        functools.partial(_dispatch_body, tm=tm, tm_moe=tm_moe, n_e=n_e, n_tiles=n_tiles),
        grid_spec=grid_spec,
        out_shape=jax.ShapeDtypeStruct((n_tiles * tm_moe * SUBLANES, LANES), F32),
        compiler_params=_cparams("arbitrary"),
        name="moe_dispatch",
    )(last_tile, n_used, pos, h1)


TE, FIRST, HALF, SLOT, NEXT = range(5)


def _experts_body(sch_ref, nu_ref, x_ref, w1_hbm, b1_ref, w2_hbm, b2_ref, y_ref, w1s, w2s, w1b, w2b, sems, *, tm):
    i = pl.program_id(0)
    d_ff = w2b.shape[0]

    def fetch(expert, slot):
        return (pltpu.make_async_copy(w1_hbm.at[expert], w1s.at[slot], sems.at[0, slot]),
                pltpu.make_async_copy(w2_hbm.at[expert], w2s.at[slot], sems.at[1, slot]))

    def run(rows):
        x = _load_rows(x_ref, rows).astype(BF16)
        hh = jnp.dot(x, w1b[...], preferred_element_type=F32) + b1_ref[0]
        g = jnp.minimum(hh[:, :d_ff], SWIGLU_LIMIT)
        u = jnp.clip(hh[:, d_ff:], -SWIGLU_LIMIT, SWIGLU_LIMIT)
        act = (u + 1.0) * g * jax.nn.sigmoid(SWIGLU_ALPHA * g)
        _store_rows(y_ref, jnp.dot(act.astype(BF16), w2b[...], preferred_element_type=F32) + b2_ref[0])

    @pl.when(i == 0)
    def _():
        for cp in fetch(sch_ref[TE, 0], 0):
            cp.start()

    @pl.when(i >= nu_ref[0])
    def _():
        y_ref[...] = jnp.zeros(y_ref.shape, F32)

    @pl.when(i < nu_ref[0])
    def _():
        @pl.when(sch_ref[FIRST, i] == 1)
        def _():
            slot = sch_ref[SLOT, i]

            @pl.when(sch_ref[NEXT, i] >= 0)
            def _():
                for cp in fetch(sch_ref[NEXT, i], 1 - slot):
                    cp.start()

            for cp in fetch(sch_ref[TE, i], slot):
                cp.wait()
            w1b[...] = w1s[slot].astype(BF16)
            w2b[...] = w2s[slot].astype(BF16)

        @pl.when(sch_ref[HALF, i] == 0)
        def _():
            run(tm)

        @pl.when(sch_ref[HALF, i] == 1)
        def _():
            run(tm // 2)
            y_ref[pl.ds(tm // 2 * SUBLANES, tm // 2 * SUBLANES), :] = jnp.zeros((tm // 2 * SUBLANES, LANES), F32)


def _experts(xs, sched, n_used, w1, b1, w2, b2, tm):
    n_e, d, f2 = w1.shape
    rows = tm * SUBLANES
    by_tile = lambda i, sch, nu: (i, 0)
    by_expert = lambda i, sch, nu: (sch[TE, i], 0, 0)
    grid_spec = pltpu.PrefetchScalarGridSpec(
        num_scalar_prefetch=2,
        grid=(xs.shape[0] // rows,),
        in_specs=[
            pl.BlockSpec((rows, LANES), by_tile),
            pl.BlockSpec(memory_space=pl.ANY),
            pl.BlockSpec((1, 1, f2), by_expert),
            pl.BlockSpec(memory_space=pl.ANY),
            pl.BlockSpec((1, 1, d), by_expert),
        ],
        out_specs=pl.BlockSpec((rows, LANES), by_tile),
        scratch_shapes=[pltpu.VMEM((2, d, f2), F32), pltpu.VMEM((2, f2 // 2, d), F32),
                        pltpu.VMEM((d, f2), BF16), pltpu.VMEM((f2 // 2, d), BF16), pltpu.SemaphoreType.DMA((2, 2))],
    )
    return pl.pallas_call(
        functools.partial(_experts_body, tm=tm),
        grid_spec=grid_spec,
        out_shape=jax.ShapeDtypeStruct(xs.shape, F32),
        compiler_params=_cparams("arbitrary"),
        name="moe_experts",
    )(sched, n_used, xs, w1, b1.reshape(n_e, 1, f2), w2, b2.reshape(n_e, 1, d))


def _combine_body(pos_ref, nxt_ref, h1_ref, wcol_ref, ys_hbm, g_ref, b_ref, o_ref, ybuf, sems, *, alpha, tm):
    i = pl.program_id(0)
    n = pl.num_programs(0)

    def fetch(p_ref, slot):
        def issue(g, c):
            base = pl.multiple_of(g * SUBLANES, SUBLANES)
            for rr in range(SUBLANES):
                dst = pl.ds(pl.multiple_of((base + rr) * SUBLANES, SUBLANES), SUBLANES)
                for k in range(TOP_K):
                    src = pl.multiple_of(p_ref[k, base + rr], SUBLANES)
                    pltpu.make_async_copy(ys_hbm.at[pl.ds(src, SUBLANES), :], ybuf.at[slot, k, dst, :],
                                          sems.at[slot]).start()
            return c
        lax.fori_loop(0, tm // SUBLANES, issue, 0)

    @pl.when(i == 0)
    def _():
        fetch(pos_ref, 0)

    @pl.when(i + 1 < n)
    def _():
        fetch(nxt_ref, (i + 1) % 2)

    slot = i % 2
    for k in range(TOP_K):
        pltpu.make_async_copy(ys_hbm.at[pl.ds(0, tm * SUBLANES), :], ybuf.at[slot, k], sems.at[slot]).wait()
    ff = wcol_ref[:, 0:1] * _load_rows(ybuf.at[slot, 0], tm)
    for k in range(1, TOP_K):
        ff = ff + wcol_ref[:, k:k + 1] * _load_rows(ybuf.at[slot, k], tm)
    o_ref[...] = _layer_norm(alpha * _load_rows(h1_ref, tm) + ff, g_ref[...], b_ref[...])


def _combine(h1, wcol, pos, ys, g, b, alpha, row_off, n):
    d = g.shape[1]
    tm = min(COMBINE_TILE, n)
    off = row_off // tm
    last = n // tm - 1
    vec = pl.BlockSpec((1, d), lambda i: (0, 0))
    return pl.pallas_call(
        functools.partial(_combine_body, alpha=alpha, tm=tm),
        grid=(n // tm,),
        in_specs=[pl.BlockSpec((TOP_K, tm), lambda i: (0, off + i), memory_space=pltpu.SMEM),
                  pl.BlockSpec((TOP_K, tm), lambda i: (0, off + jnp.minimum(i + 1, last)), memory_space=pltpu.SMEM),
                  pl.BlockSpec((tm * SUBLANES, LANES), lambda i: (off + i, 0)),
                  pl.BlockSpec((tm, LANES), lambda i: (off + i, 0)),
                  pl.BlockSpec(memory_space=pl.ANY), vec, vec],
        out_specs=pl.BlockSpec((tm, d), lambda i: (i, 0)),
        out_shape=jax.ShapeDtypeStruct((n, d), F32),
        scratch_shapes=[pltpu.VMEM((2, TOP_K, tm * SUBLANES, LANES), F32), pltpu.SemaphoreType.DMA((2,))],
        compiler_params=_cparams("arbitrary"),
        name="combine_ln2",
    )(pos, pos, h1, wcol, ys, g, b)


def _row2(v):
    return v.reshape(1, -1).astype(F32)


def kernel(x_prompt, x_sample, cache_k, cache_v, page_table, state_conv, w_in, b_in, lambda_q1, lambda_k1,
           lambda_q2, lambda_k2, subln_g, rel_bias, w_attn_proj, conv_w, conv_b, conv_ln_g, conv_ln_b,
           w_conv_proj, b_conv_proj, w_out, ln1_g, ln1_b, router_w, router_b, expert_w1, expert_b1,
           expert_w2, expert_b2, ln2_g, ln2_b):
    depth = w_in.shape[0]
    bp, seq, d = x_prompt.shape
    bs, t_new, _ = x_sample.shape
    d_att = N_HEADS * QK_DIM
    d_conv = conv_w.shape[2]
    n_p, n_s = bp * seq, bs * t_new
    n_tot = n_p + n_s
    tm = min(TOKEN_TILE, math.gcd(n_p, n_s))
    assert tm % LANES == 0
    tm_moe = min(MOE_TILE, n_tot)
    alpha = (2 * depth) ** 0.25

    hp = x_prompt.reshape(n_p, d)
    hs = x_sample.reshape(n_s, d)
    outs = [[] for _ in range(6)]
    for l in range(depth):
        lam_init = 0.8 - 0.6 * math.exp(-0.3 * l)
        lam = (jnp.exp(jnp.sum(lambda_q1[l].astype(F32) * lambda_k1[l].astype(F32)))
               - jnp.exp(jnp.sum(lambda_q2[l].astype(F32) * lambda_k2[l].astype(F32))) + lam_init).reshape(1)
        w_in_bf = w_in[l].astype(BF16)
        b_in_l = _row2(b_in[l])
        g_sub = _row2(subln_g[l])
        conv_args = (conv_w[l].astype(F32), _row2(conv_b[l]), _row2(conv_ln_g[l]), _row2(conv_ln_b[l]))
        merge_w = (w_attn_proj[l].astype(BF16), w_conv_proj[l].astype(BF16), _row2(b_conv_proj[l]),
                   w_out[l].astype(BF16), _row2(ln1_g[l]), _row2(ln1_b[l]),
                   router_w[l].astype(F32).T, router_b[l].astype(F32).reshape(-1, 1))

        qp, kp, vp, kbp, vbp, up, sgp = _in_proj(hp, w_in_bf, b_in_l, d_att, d_conv, (bp, seq) + _attn_tiles(seq))
        op = _prompt_attention(qp, kbp, vbp, rel_bias, lam, g_sub, lam_init)
        up3 = up.reshape(bp, seq, d_conv)
        yp = _conv_prompt(up3, *conv_args)
        merged = _merge(op.reshape(n_p, d_att), yp.reshape(n_p, d_conv), sgp, hp, merge_w, alpha, n_tot, 0, tm, None)

        qs, ks, vs, kbs, vbs, us, sgs = _in_proj(hs, w_in_bf, b_in_l, d_att, d_conv)
        osr = _sample_attention(qs.reshape(bs, t_new, d_att), kbs.reshape(bs, t_new, d_att),
                                vbs.reshape(bs, t_new, d_att), cache_k[l], cache_v[l], page_table,
                                rel_bias, lam, g_sub, lam_init)
        os_ = osr[:, :t_new].reshape(n_s, d_att)
        buf_s = jnp.concatenate([state_conv[l].astype(F32), us.reshape(bs, t_new, d_conv)], axis=1)
        ys = _conv_sample(buf_s.transpose(1, 0, 2), *conv_args).transpose(1, 0, 2).reshape(n_s, d_conv)
        h1, topi, rank, wcol, cnt = _merge(os_, ys, sgs, hs, merge_w, alpha, n_tot, n_p, tm, merged)

        pos, sched, n_used, last_tile, n_tiles = _plan(topi, rank, cnt[:, :, 0], tm, tm_moe)
        xs = _dispatch(h1, pos, last_tile, n_used, n_tiles, tm, tm_moe)
        ysort = _experts(xs, sched, n_used, expert_w1[l], expert_b1[l].astype(F32),
                         expert_w2[l], expert_b2[l].astype(F32), tm_moe)
        g2, b2 = _row2(ln2_g[l]), _row2(ln2_b[l])
        hp = _combine(h1, wcol, pos, ysort, g2, b2, alpha, 0, n_p)
        hs = _combine(h1, wcol, pos, ysort, g2, b2, alpha, n_p, n_s)

        w1 = CONV_WIDTH - 1
        cp = up3[:, seq - w1:] if seq >= w1 else jnp.concatenate(
            [jnp.zeros((bp, w1 - seq, d_conv), F32), up3], axis=1)
        for lst, val in zip(outs, (kp.reshape(bp, seq, N_HEADS, QK_DIM), vp.reshape(bp, seq, N_HEADS, V_DIM), cp,
                                   ks.reshape(bs, t_new, N_HEADS, QK_DIM), vs.reshape(bs, t_new, N_HEADS, V_DIM),
                                   buf_s[:, t_new:])):
            lst.append(val)
    return (hp.reshape(bp, seq, d), hs.reshape(bs, t_new, d)) + tuple(jnp.stack(o) for o in outs)
```

```python
import functools
import math

import jax
import jax.numpy as jnp
from jax import lax
from jax.experimental import pallas as pl
from jax.experimental.pallas import tpu as pltpu

F32 = jnp.float32
BF16 = jnp.bfloat16
I32 = jnp.int32

N_HEADS = 4
HEAD_DIM = 64
QK_DIM = 2 * HEAD_DIM
V_DIM = 2 * HEAD_DIM
ATTN_SCALE = HEAD_DIM ** -0.5
LOG2E = 1.4426950408889634
NEG_INF = -1e30
NUM_BUCKETS = 32
MAX_EXACT = NUM_BUCKETS // 2
MAX_DISTANCE = 128
CONV_WIDTH = 31
TOP_K = 4
SWIGLU_LIMIT = 7.0
SWIGLU_ALPHA = 1.702
LN_EPS = 1e-5
PAGE_SIZE = 128

LANES = 128
SUBLANES = 8
VMEM_LIMIT = 56 * 1024 * 1024

TOKEN_TILE = 512
ATTN_TQ = 1024
ATTN_TK = 1024
ATTN_GROUP = 2
SAMPLE_GROUP = 2
MOE_TILE = 512
COMBINE_TILE = 256
CONV_HALO = 32
CONV_CHUNK = 64


def _cparams(*sem):
    return pltpu.CompilerParams(dimension_semantics=sem, vmem_limit_bytes=VMEM_LIMIT)


def _load_rows(ref, tm):
    return jnp.concatenate([ref[pl.ds(c, tm, stride=SUBLANES), :] for c in range(SUBLANES)], axis=1)


def _store_rows(ref, x):
    tm = x.shape[0]
    for c in range(SUBLANES):
        ref[pl.ds(c, tm, stride=SUBLANES), :] = x[:, c * LANES:(c + 1) * LANES]


def _layer_norm(x, g, b):
    mu = jnp.mean(x, -1, keepdims=True)
    xc = x - mu
    var = jnp.mean(xc * xc, -1, keepdims=True)
    return xc * lax.rsqrt(var + LN_EPS) * g + b


def _in_proj_body(x_ref, w_ref, b_ref, q_ref, k_ref, v_ref, kb_ref, vb_ref, u_ref, sg_ref, *, d_att, d_conv, by_head):
    x = x_ref[...].astype(BF16)

    def seg(lo, hi):
        return jnp.dot(x, w_ref[:, lo:hi], preferred_element_type=F32) + b_ref[:, lo:hi]

    def head(a, h):
        return a[:, h * QK_DIM:(h + 1) * QK_DIM]

    o = 0
    q = seg(o, o + d_att) * (ATTN_SCALE * LOG2E)
    o += d_att
    k = seg(o, o + d_att)
    o += d_att
    v = seg(o, o + d_att)
    o += d_att
    tm = x.shape[0]
    for h in range(N_HEADS):
        k_ref[pl.ds(h, tm, stride=N_HEADS), :] = head(k, h)
        v_ref[pl.ds(h, tm, stride=N_HEADS), :] = head(v, h)
    if by_head:
        for h in range(N_HEADS):
            q_ref[h] = head(q, h).T.astype(BF16)
            kb_ref[h] = head(k, h).astype(BF16)
            vb_ref[h] = head(v, h).T.astype(BF16)
    else:
        q_ref[...] = q.astype(BF16)
        kb_ref[...] = k.astype(BF16)
        vb_ref[...] = v.astype(BF16)
    c = seg(o, o + 2 * d_conv)
    u_ref[...] = c[:, :d_conv] * jax.nn.sigmoid(c[:, d_conv:])
    o += 2 * d_conv
    sg_ref[...] = jax.nn.sigmoid(seg(o, w_ref.shape[1])).astype(BF16)


def _in_proj(x, w_bf, b, d_att, d_conv, attn_tiles=None):
    n, d = x.shape
    tm = min(TOKEN_TILE, n)
    row = lambda width: pl.BlockSpec((tm, width), lambda i: (i, 0))
    full = lambda a: pl.BlockSpec(a.shape, lambda i: (0, 0))
    f32_out = lambda width: jax.ShapeDtypeStruct((n, width), F32)
    kv_shape = jax.ShapeDtypeStruct((n * N_HEADS, QK_DIM), F32)
    kv_spec = pl.BlockSpec((tm * N_HEADS, QK_DIM), lambda i: (i, 0))
    if attn_tiles is None:
        qkv_shapes = [jax.ShapeDtypeStruct((n, d_att), BF16)] * 3
        qkv_specs = [row(d_att)] * 3
    else:
        bsz, seq, tq, tk = attn_tiles
        assert seq % tm == 0 and tq % tm == 0 and tk % tm == 0
        per_seq = seq // tm

        def spec(t, transposed):
            parts = t // tm
            blk = (None, N_HEADS, None, QK_DIM, tm) if transposed else (None, N_HEADS, None, tm, QK_DIM)

            def index(i):
                ti = i % per_seq
                tile, part = ti // parts, ti % parts
                return (i // per_seq, 0, tile, 0, part) if transposed else (i // per_seq, 0, tile, part, 0)
            return pl.BlockSpec(blk, index)

        qkv_shapes = [jax.ShapeDtypeStruct((bsz, N_HEADS, seq // tq, QK_DIM, tq), BF16),
                      jax.ShapeDtypeStruct((bsz, N_HEADS, seq // tk, tk, QK_DIM), BF16),
                      jax.ShapeDtypeStruct((bsz, N_HEADS, seq // tk, V_DIM, tk), BF16)]
        qkv_specs = [spec(tq, True), spec(tk, False), spec(tk, True)]
    out_shape = (qkv_shapes[0], kv_shape, kv_shape, qkv_shapes[1], qkv_shapes[2],
                 f32_out(d_conv), jax.ShapeDtypeStruct((n, 2 * d), BF16))
    return pl.pallas_call(
        functools.partial(_in_proj_body, d_att=d_att, d_conv=d_conv, by_head=attn_tiles is not None),
        grid=(n // tm,),
        in_specs=[row(d), full(w_bf), full(b)],
        out_specs=(qkv_specs[0], kv_spec, kv_spec, qkv_specs[1], qkv_specs[2], row(d_conv), row(2 * d)),
        out_shape=out_shape,
        compiler_params=_cparams("parallel"),
        name="in_proj",
    )(x, w_bf, b)


def _t5_bucket(dist):
    n = jnp.maximum(dist, 0)
    nf = jnp.maximum(n, 1).astype(F32)
    large = MAX_EXACT + (jnp.log(nf / MAX_EXACT) / math.log(MAX_DISTANCE / MAX_EXACT)
                         * (NUM_BUCKETS - MAX_EXACT)).astype(I32)
    large = jnp.minimum(large, NUM_BUCKETS - 1)
    return jnp.where(n < MAX_EXACT, n, large)


def _bias_of(dist, rel_bias, shift=None):
    hot = _t5_bucket(dist)[..., None] == jnp.arange(NUM_BUCKETS, dtype=I32)
    table = rel_bias.astype(F32) if shift is None else rel_bias.astype(F32) - shift
    heads = [jnp.sum(jnp.where(hot, table[:, h], 0.0), -1) for h in range(table.shape[1])]
    return jnp.where(dist >= 0, jnp.stack(heads) * LOG2E, NEG_INF)


def _flash_body(lam_ref, qt_ref, k_ref, vt_ref, btab_ref, g_ref, o_ref, m_s, l_s, acc_s, bias_ref, *, ratio, group,
                lam_init):
    i = pl.program_id(2)
    tk, tq = bias_ref.shape[1:]

    @pl.when(i == 0)
    def _():
        for o in range(ratio + 1):
            table = jnp.broadcast_to(btab_ref[o], (tk, btab_ref.shape[2]))
            bias_ref[o] = pltpu.roll(table, 0, 1, stride=1, stride_axis=0)[:, :tq]

    sub = lax.broadcasted_iota(I32, (QK_DIM, 1), 0)
    qs = []
    for b in range(group):
        qt = qt_ref[b]
        zero = jnp.zeros_like(qt)
        qs.append((jnp.where(sub < HEAD_DIM, qt, zero), jnp.where(sub >= HEAD_DIM, qt, zero)))
    m_s[...] = jnp.full(m_s.shape, NEG_INF, F32)
    l_s[...] = jnp.zeros(l_s.shape, F32)
    acc_s[...] = jnp.zeros(acc_s.shape, F32)

    def step(j, bias, keys=slice(None), cols=slice(None)):
        for b in range(group):
            k = k_ref[b, j, keys, :]
            vt = vt_ref[b, j, :, keys]
            for c in range(2):
                s = jnp.dot(k, qs[b][c][:, cols], preferred_element_type=F32)
                if bias is not None:
                    s = s + bias
                m_old = m_s[b, c, :, cols]
                m_new = jnp.maximum(m_old, jnp.max(s, 0, keepdims=True))
                alpha = jnp.exp2(m_old - m_new)
                p = jnp.exp2(s - m_new)
                l_s[b, c, :, cols] = alpha * l_s[b, c, :, cols] + jnp.sum(p, 0, keepdims=True)
                acc_s[b, c, :, cols] = alpha * acc_s[b, c, :, cols] + jnp.dot(
                    vt, p.astype(BF16), preferred_element_type=F32)
                m_s[b, c, :, cols] = m_new

    def far_step(j, carry):
        step(j, None)
        return carry

    lax.fori_loop(0, jnp.maximum(i * ratio - 1, 0), far_step, 0)

    @pl.when(i >= 1)
    def _():
        step(i * ratio - 1, bias_ref[0])

    for o in range(ratio - 1):
        step(i * ratio + o, bias_ref[o + 1])
    last, half, first_col = i * ratio + ratio - 1, tk // 2, tq - tk // 2
    step(last, bias_ref[ratio, 0:half, :], keys=slice(0, half))
    step(last, bias_ref[ratio, half:tk, first_col:tq], keys=slice(half, tk), cols=slice(first_col, tq))

    for b in range(group):
        o = acc_s[b, 0] / l_s[b, 0] - lam_ref[0] * (acc_s[b, 1] / l_s[b, 1])
        o = o * lax.rsqrt(jnp.mean(o * o, 0, keepdims=True) + LN_EPS) * g_ref[...]
        o_ref[b] = (o * (1.0 - lam_init)).T.astype(BF16)


def _attn_tiles(seq):
    tq = min(ATTN_TQ, seq)
    tk = min(ATTN_TK, tq)
    assert tk >= MAX_DISTANCE and seq % tq == 0 and tq % tk == 0
    return tq, tk


def _prompt_attention(qt, k4, vt, rel_bias, lam, subln_g, lam_init):
    bsz, _, nq, _, tq = qt.shape
    nk, tk = k4.shape[2], k4.shape[3]
    seq, ratio, width = nq * tq, tq // tk, N_HEADS * V_DIM
    group = ATTN_GROUP if bsz % ATTN_GROUP == 0 else 1
    far = rel_bias[NUM_BUCKETS - 1].astype(F32)
    period = tq + tk
    assert period % LANES == 0
    y = jnp.arange(period, dtype=I32)
    r_minus_c = jnp.where(y < tq, y, y - period)
    dist = r_minus_c[None, :] - (jnp.arange(ratio + 1, dtype=I32)[:, None] - 1) * tk
    btab = _bias_of(dist, rel_bias, far).reshape(N_HEADS, ratio + 1, 1, period)
    return pl.pallas_call(
        functools.partial(_flash_body, ratio=ratio, group=group, lam_init=lam_init),
        grid=(bsz // group, N_HEADS, nq),
        in_specs=[
            pl.BlockSpec(memory_space=pltpu.SMEM),
            pl.BlockSpec((group, None, None, QK_DIM, tq), lambda b, h, i: (b, h, i, 0, 0)),
            pl.BlockSpec((group, None, nk, tk, QK_DIM), lambda b, h, i: (b, h, 0, 0, 0)),
            pl.BlockSpec((group, None, nk, V_DIM, tk), lambda b, h, i: (b, h, 0, 0, 0)),
            pl.BlockSpec((None, ratio + 1, 1, period), lambda b, h, i: (h, 0, 0, 0)),
            pl.BlockSpec((V_DIM, 1), lambda b, h, i: (0, 0)),
        ],
        out_specs=pl.BlockSpec((group, tq, V_DIM), lambda b, h, i: (b, i, h)),
        out_shape=jax.ShapeDtypeStruct((bsz, seq, width), BF16),
        scratch_shapes=[pltpu.VMEM((group, 2, 1, tq), F32), pltpu.VMEM((group, 2, 1, tq), F32),
                        pltpu.VMEM((group, 2, V_DIM, tq), F32), pltpu.VMEM((ratio + 1, tk, tq), F32)],
        compiler_params=_cparams("parallel", "parallel", "arbitrary"),
        name="prompt_attention",
    )(lam, qt, k4, vt, btab, subln_g.reshape(V_DIM, 1))


def _paged_body(pt_ref, lam_ref, q_ref, kn_ref, vn_ref, bias_ref, g_ref, *refs, n_pages, t_new, group, lam_init):
    o_ref = refs[2 * group * n_pages]
    lam = lam_ref[0]
    rows = 2 * t_new
    cols = PAGE_SIZE * N_HEADS
    nt = (((1,), (1,)), ((), ()))
    for sq in range(group):
        k_refs = refs[sq * n_pages:(sq + 1) * n_pages]
        v_refs = refs[(group + sq) * n_pages:(group + sq + 1) * n_pages]
        q = q_ref[sq]
        parts = [lax.dot_general(q, k_refs[p][...].astype(BF16), nt, preferred_element_type=F32)
                 for p in range(n_pages)]
        parts.append(lax.dot_general(q, kn_ref[sq], nt, preferred_element_type=F32))
        s = jnp.concatenate(parts, axis=1) + bias_ref[...]
        m = jnp.max(s, -1, keepdims=True)
        p_un = jnp.exp2(s - m)
        pn = p_un / jnp.sum(p_un, -1, keepdims=True)
        w = (pn - lam * pltpu.roll(pn, N_HEADS * rows - t_new, axis=0)).astype(BF16)
        o = jnp.dot(w[:, n_pages * cols:], vn_ref[sq], preferred_element_type=F32)
        for p in range(n_pages):
            o = o + jnp.dot(w[:, p * cols:(p + 1) * cols], v_refs[p][...].astype(BF16), preferred_element_type=F32)
        o = o * lax.rsqrt(jnp.mean(o * o, -1, keepdims=True) + LN_EPS) * g_ref[...]
        o = (o * (1.0 - lam_init)).astype(BF16)
        o_ref[sq] = jnp.concatenate([o[h * rows:(h + 1) * rows] for h in range(N_HEADS)], axis=1)


def _sample_attention(q, kb, vb, cache_k, cache_v, page_table, rel_bias, lam, subln_g, lam_init):
    bs, t_new, width = q.shape
    n_pool = cache_k.shape[0]
    n_pages = page_table.shape[1]
    past = n_pages * PAGE_SIZE
    rows = 2 * t_new
    cols = PAGE_SIZE * N_HEADS
    new_pos = LANES // N_HEADS
    assert rows % SUBLANES == 0 and t_new <= new_pos
    qh = q.reshape(bs, t_new, N_HEADS, QK_DIM).transpose(0, 2, 1, 3)
    lane_map = (jnp.arange(QK_DIM) // HEAD_DIM)[None, :] == jnp.arange(2)[:, None]
    q_rows = (qh[:, :, None] * lane_map[None, None, :, None, :].astype(BF16)).reshape(bs, N_HEADS * rows, QK_DIM)
    pad = ((0, 0), (0, LANES - t_new * N_HEADS), (0, 0))
    kn = jnp.pad(kb.reshape(bs, t_new * N_HEADS, QK_DIM), pad)
    vn = jnp.pad(vb.reshape(bs, t_new * N_HEADS, V_DIM), pad)
    qpos = past + jnp.arange(t_new, dtype=I32)
    kpos = jnp.concatenate([jnp.arange(past + t_new, dtype=I32),
                            jnp.full((new_pos - t_new,), past + t_new + new_pos, I32)])
    col_pos = jnp.repeat(kpos, N_HEADS)
    col_head = jnp.tile(jnp.arange(N_HEADS, dtype=I32), kpos.shape[0])
    b = _bias_of(qpos[:, None] - col_pos[None, :], rel_bias)
    b = jnp.where(col_head[None, None, :] == jnp.arange(N_HEADS, dtype=I32)[:, None, None], b, NEG_INF)
    bias = jnp.stack([b, b], axis=1).reshape(N_HEADS * rows, -1)

    group = SAMPLE_GROUP if bs % SAMPLE_GROUP == 0 else 1
    page_spec = lambda sq, j: pl.BlockSpec(
        (None, cols, QK_DIM), lambda b, pt, sq=sq, j=j: (pt[(b * group + sq) * n_pages + j], 0, 0))
    page_specs = [page_spec(sq, j) for sq in range(group) for j in range(n_pages)]
    per_seq = lambda r, c: pl.BlockSpec((group, r, c), lambda b, pt: (b, 0, 0))
    grid_spec = pltpu.PrefetchScalarGridSpec(
        num_scalar_prefetch=1,
        grid=(bs // group,),
        in_specs=[
            pl.BlockSpec(memory_space=pltpu.SMEM),
            per_seq(N_HEADS * rows, QK_DIM),
            per_seq(LANES, QK_DIM),
            per_seq(LANES, V_DIM),
            pl.BlockSpec(bias.shape, lambda b, pt: (0, 0)),
            pl.BlockSpec((1, V_DIM), lambda b, pt: (0, 0)),
        ] + page_specs * 2,
        out_specs=pl.BlockSpec((group, rows, width), lambda b, pt: (b, 0, 0)),
    )
    ck = cache_k.reshape(n_pool, cols, QK_DIM)
    cv = cache_v.reshape(n_pool, cols, V_DIM)
    return pl.pallas_call(
        functools.partial(_paged_body, n_pages=n_pages, t_new=t_new, group=group, lam_init=lam_init),
        grid_spec=grid_spec,
        out_shape=jax.ShapeDtypeStruct((bs, rows, width), BF16),
        compiler_params=_cparams("parallel"),
        name="sample_attention",
    )(page_table.reshape(-1), lam, q_rows, kn, vn, bias, subln_g, *([ck] * (group * n_pages)),
      *([cv] * (group * n_pages)))


def _conv_prompt_body(u_ref, halo_ref, w_ref, cb_ref, g_ref, b_ref, o_ref, buf, shifted, *, tc):
    i = pl.program_id(1)
    halo = halo_ref[0]
    buf[0:CONV_HALO, :] = jnp.where(i > 0, halo, jnp.zeros_like(halo))
    buf[CONV_HALO:, :] = u_ref[0]
    rows = CONV_HALO + tc - SUBLANES
    for s in range(1, SUBLANES):
        shifted[s - 1, 0:rows, :] = buf[s:s + rows, :]
    first = CONV_HALO - (CONV_WIDTH - 1)
    for r0 in range(0, tc, CONV_CHUNK):
        acc = jnp.zeros((CONV_CHUNK, u_ref.shape[2]), F32) + cb_ref[...]
        for j in range(CONV_WIDTH):
            s = (first + j) % SUBLANES
            a = first + j - s + r0
            win = buf[a:a + CONV_CHUNK, :] if s == 0 else shifted[s - 1, a:a + CONV_CHUNK, :]
            acc = acc + w_ref[j:j + 1, :] * win
        y = _layer_norm(acc, g_ref[...], b_ref[...])
        o_ref[0, r0:r0 + CONV_CHUNK, :] = (y * jax.nn.sigmoid(y)).astype(BF16)


def _conv_prompt(u, conv_w, conv_b, ln_g, ln_b):
    bsz, seq, ch = u.shape
    tc = min(TOKEN_TILE, seq)
    assert seq % tc == 0 and tc % CONV_CHUNK == 0 and tc % CONV_HALO == 0
    vec = pl.BlockSpec((1, ch), lambda b, i: (0, 0))
    return pl.pallas_call(
        functools.partial(_conv_prompt_body, tc=tc),
        grid=(bsz, seq // tc),
        in_specs=[
            pl.BlockSpec((1, tc, ch), lambda b, i: (b, i, 0)),
            pl.BlockSpec((1, CONV_HALO, ch), lambda b, i: (b, jnp.maximum(i * (tc // CONV_HALO) - 1, 0), 0)),
            pl.BlockSpec((CONV_WIDTH, ch), lambda b, i: (0, 0)),
            vec, vec, vec,
        ],
        out_specs=pl.BlockSpec((1, tc, ch), lambda b, i: (b, i, 0)),
        out_shape=jax.ShapeDtypeStruct((bsz, seq, ch), BF16),
        scratch_shapes=[pltpu.VMEM((CONV_HALO + tc, ch), F32), pltpu.VMEM((SUBLANES - 1, CONV_HALO + tc, ch), F32)],
        compiler_params=_cparams("parallel", "arbitrary"),
        name="conv_prompt",
    )(u, u, conv_w, conv_b, ln_g, ln_b)


def _conv_sample_body(buf_ref, w_ref, cb_ref, g_ref, b_ref, o_ref, *, t_new):
    for t in range(t_new):
        acc = jnp.zeros(buf_ref.shape[1:], F32) + cb_ref[...]
        for j in range(CONV_WIDTH):
            acc = acc + w_ref[j:j + 1, :] * buf_ref[t + j]
        y = _layer_norm(acc, g_ref[...], b_ref[...])
        o_ref[t] = (y * jax.nn.sigmoid(y)).astype(BF16)


def _conv_sample(buf_t, conv_w, conv_b, ln_g, ln_b):
    rows, bs, ch = buf_t.shape
    t_new = rows - (CONV_WIDTH - 1)
    gb = min(32, bs)
    assert bs % gb == 0
    vec = pl.BlockSpec((1, ch), lambda i: (0, 0))
    return pl.pallas_call(
        functools.partial(_conv_sample_body, t_new=t_new),
        grid=(bs // gb,),
        in_specs=[pl.BlockSpec((rows, gb, ch), lambda i: (0, i, 0)),
                  pl.BlockSpec((CONV_WIDTH, ch), lambda i: (0, 0)), vec, vec, vec],
        out_specs=pl.BlockSpec((t_new, gb, ch), lambda i: (0, i, 0)),
        out_shape=jax.ShapeDtypeStruct((t_new, bs, ch), BF16),
        compiler_params=_cparams("parallel"),
        name="conv_sample",
    )(buf_t, conv_w, conv_b, ln_g, ln_b)


N_MERGE_OUT = 5


def _merge_body(*refs, alpha, n_alias):
    (o_ref, y_ref, sg_ref, x_ref, wa_ref, wc_ref, bc_ref, wo_ref, g1_ref, b1_ref, rw_ref, rb_ref) = refs[:12]
    h1_ref, ti_ref, rk_ref, wcol_ref, cnt_ref = refs[12 + n_alias:]
    tm, d = x_ref.shape
    a = jnp.dot(o_ref[...], wa_ref[...], preferred_element_type=F32)
    b = jnp.dot(y_ref[...], wc_ref[...], preferred_element_type=F32) + bc_ref[...]
    mix_in = sg_ref[:, :d].astype(F32) * a + sg_ref[:, d:].astype(F32) * b
    mix = jnp.dot(mix_in.astype(BF16), wo_ref[...], preferred_element_type=F32)
    h1 = _layer_norm(alpha * x_ref[...] + mix, g1_ref[...], b1_ref[...])
    _store_rows(h1_ref, h1)
    logits = lax.dot_general(rw_ref[...], h1, (((1,), (1,)), ((), ())), preferred_element_type=F32,
                             precision=lax.Precision.HIGHEST) + rb_ref[...]
    n_e = logits.shape[0]
    eid = lax.broadcasted_iota(I32, logits.shape, 0)
    vals, idxs, hots = [], [], []
    for _ in range(TOP_K):
        m = jnp.max(logits, 0, keepdims=True)
        idx = jnp.min(jnp.where(logits == m, eid, n_e), 0, keepdims=True)
        hot = eid == idx
        vals.append(m)
        idxs.append(idx)
        hots.append(hot)
        logits = jnp.where(hot, -jnp.inf, logits)
    ex = [jnp.exp(v - vals[0]) for v in vals]
    den = ex[0]
    for e in ex[1:]:
        den = den + e
    ti_ref[...] = jnp.concatenate(idxs, 0)
    wrows = jnp.concatenate([e / den for e in ex] + [jnp.zeros((LANES - TOP_K, tm), F32)], 0)
    wcol_ref[...] = wrows.T
    sel = hots[0]
    for hot in hots[1:]:
        sel = sel | hot
    sel_f = jnp.where(sel, 1.0, 0.0)
    before = lax.broadcasted_iota(I32, (tm, tm), 0) < lax.broadcasted_iota(I32, (tm, tm), 1)
    upper = jnp.where(before, 1.0, 0.0).astype(BF16)
    ahead = jnp.dot(sel_f.astype(BF16), upper, preferred_element_type=F32)
    rk_ref[...] = jnp.concatenate(
        [jnp.sum(jnp.where(hot, ahead, 0.0), 0, keepdims=True) for hot in hots], 0).astype(I32)
    cnt = jnp.sum(sel_f, 1, keepdims=True).astype(I32)
    cnt_ref[0] = jnp.broadcast_to(cnt, (n_e, LANES))


def _merge(o, yact, sg, x, wts, alpha, n_total, row_off, tm, prev):
    n, d = x.shape
    wa, wc, bc, wo, g1, b1, rwt, rb = wts
    n_e = rwt.shape[0]
    off = row_off // tm
    last = n // tm - 1
    steps = n // tm if prev is not None else n_total // tm
    row = lambda width: pl.BlockSpec((tm, width), lambda i: (jnp.minimum(i, last), 0))
    full = lambda a: pl.BlockSpec(a.shape, lambda i: (0, 0))
    in_specs = [row(o.shape[1]), row(yact.shape[1]), row(2 * d), row(d),
                full(wa), full(wc), full(bc), full(wo), full(g1), full(b1), full(rwt), full(rb)]
    args = [o, yact, sg, x, wa, wc, bc, wo, g1, b1, rwt, rb]
    aliases = {}
    if prev is not None:
        in_specs += [pl.BlockSpec(memory_space=pl.ANY)] * N_MERGE_OUT
        aliases = {len(args) + j: j for j in range(N_MERGE_OUT)}
        args += list(prev)
    slot_major = pl.BlockSpec((TOP_K, tm), lambda i: (0, off + i))
    return pl.pallas_call(
        functools.partial(_merge_body, alpha=alpha, n_alias=0 if prev is None else N_MERGE_OUT),
        grid=(steps,),
        in_specs=in_specs,
        out_specs=(pl.BlockSpec((tm * SUBLANES, LANES), lambda i: (off + i, 0)), slot_major, slot_major,
                   pl.BlockSpec((tm, LANES), lambda i: (off + i, 0)),
                   pl.BlockSpec((1, n_e, LANES), lambda i: (off + i, 0, 0))),
        out_shape=(jax.ShapeDtypeStruct((n_total * SUBLANES, LANES), F32),
                   jax.ShapeDtypeStruct((TOP_K, n_total), I32),
                   jax.ShapeDtypeStruct((TOP_K, n_total), I32),
                   jax.ShapeDtypeStruct((n_total, LANES), F32),
                   jax.ShapeDtypeStruct((n_total // tm, n_e, LANES), I32)),
        input_output_aliases=aliases,
        compiler_params=_cparams("parallel"),
        name="merge_ln1_router",
    )(*args)


def _plan(topi, rank, cnt, tm_tok, tm_moe):
    k, n = topi.shape
    n_e = cnt.shape[1]
    before = jnp.cumsum(cnt, 0) - cnt
    total = jnp.sum(cnt, 0)
    ntile_e = (total + tm_moe - 1) // tm_moe
    tend = jnp.cumsum(ntile_e)
    tstart = tend - ntile_e
    base = (tstart * tm_moe)[None, :] + before
    base_tok = jnp.repeat(base.T, tm_tok, axis=1)
    hot = topi[:, None, :] == jnp.arange(n_e, dtype=I32)[None, :, None]
    pos = (jnp.sum(jnp.where(hot, base_tok[None], 0), 1).astype(I32) + rank) * SUBLANES
    n_tiles = (k * n + tm_moe - 1) // tm_moe + n_e
    n_used = tend[-1]
    tid = jnp.arange(n_tiles, dtype=I32)
    src = jnp.minimum(tid, n_used - 1)
    te = jnp.minimum(jnp.sum((src[:, None] >= tend[None, :]).astype(I32), 1), n_e - 1)
    first = (tid == tstart[te]).astype(I32)
    half = (total[te] - (tid - tstart[te]) * tm_moe <= tm_moe // 2).astype(I32)
    slot = (jnp.cumsum(first) - 1) % 2
    nxt = jnp.where(tend[te] < n_used, te[jnp.minimum(tend[te], n_tiles - 1)], -1)
    sched = jnp.stack([te, first, half, slot, nxt]).astype(I32)
    last_tile = jnp.where(ntile_e > 0, tend - 1, -1).astype(I32)
    return pos, sched, n_used.reshape(1).astype(I32), last_tile, n_tiles


def _dispatch_body(lt_ref, nu_ref, pos_ref, x_ref, xs_hbm, zbuf, zsem, sem, *, tm, tm_moe, n_e, n_tiles):
    i = pl.program_id(0)

    def zero_copy(tile):
        start = pl.multiple_of(tile * (tm_moe * SUBLANES), tm_moe * SUBLANES)
        return pltpu.make_async_copy(zbuf, xs_hbm.at[pl.ds(start, tm_moe * SUBLANES), :], zsem)

    @pl.when(i == 0)
    def _():
        zbuf[...] = jnp.zeros(zbuf.shape, F32)

        def z_start(e, c):
            @pl.when(lt_ref[e] >= 0)
            def _():
                zero_copy(lt_ref[e]).start()
            return c

        def z_wait(e, c):
            @pl.when(lt_ref[e] >= 0)
            def _():
                zero_copy(lt_ref[e]).wait()
            return c

        def t_start(t, c):
            zero_copy(t).start()
            return c

        def t_wait(t, c):
            zero_copy(t).wait()
            return c

        lax.fori_loop(0, n_e, z_start, 0)
        lax.fori_loop(nu_ref[0], n_tiles, t_start, 0)
        lax.fori_loop(0, n_e, z_wait, 0)
        lax.fori_loop(nu_ref[0], n_tiles, t_wait, 0)

    def issue(g, c):
        base = pl.multiple_of(g * SUBLANES, SUBLANES)
        for rr in range(SUBLANES):
            src = x_ref.at[pl.ds(pl.multiple_of((base + rr) * SUBLANES, SUBLANES), SUBLANES), :]
            for k in range(TOP_K):
                dst = pl.multiple_of(pos_ref[k, base + rr], SUBLANES)
                pltpu.make_async_copy(src, xs_hbm.at[pl.ds(dst, SUBLANES), :], sem).start()
        return c

    lax.fori_loop(0, tm // SUBLANES, issue, 0)
    for k in range(TOP_K):
        pltpu.make_async_copy(x_ref, xs_hbm.at[pl.ds(0, tm * SUBLANES), :], sem).wait()


def _dispatch(h1, pos, last_tile, n_used, n_tiles, tm, tm_moe):
    n = h1.shape[0] // SUBLANES
    n_e = last_tile.shape[0]
    grid_spec = pltpu.PrefetchScalarGridSpec(
        num_scalar_prefetch=2,
        grid=(n // tm,),
        in_specs=[pl.BlockSpec((TOP_K, tm), lambda i, lt, nu: (0, i), memory_space=pltpu.SMEM),
                  pl.BlockSpec((tm * SUBLANES, LANES), lambda i, lt, nu: (i, 0))],
        out_specs=pl.BlockSpec(memory_space=pl.ANY),
        scratch_shapes=[pltpu.VMEM((tm_moe * SUBLANES, LANES), F32), pltpu.SemaphoreType.DMA,
                        pltpu.SemaphoreType.DMA],
    )
    return pl.pallas_call(
        functools.partial(_dispatch_body, tm=tm, tm_moe=tm_moe, n_e=n_e, n_tiles=n_tiles),
        grid_spec=grid_spec,
        out_shape=jax.ShapeDtypeStruct((n_tiles * tm_moe * SUBLANES, LANES), F32),
        compiler_params=_cparams("arbitrary"),
        name="moe_dispatch",
    )(last_tile, n_used, pos, h1)


TE, FIRST, HALF, SLOT, NEXT = range(5)


def _experts_body(sch_ref, nu_ref, x_ref, w1_hbm, b1_ref, w2_hbm, b2_ref, y_ref, w1s, w2s, w1b, w2b, sems, *, tm):
    i = pl.program_id(0)
    d_ff = w2b.shape[0]

    def fetch(expert, slot):
        return (pltpu.make_async_copy(w1_hbm.at[expert], w1s.at[slot], sems.at[0, slot]),
                pltpu.make_async_copy(w2_hbm.at[expert], w2s.at[slot], sems.at[1, slot]))

    def run(rows):
        x = _load_rows(x_ref, rows).astype(BF16)
        hh = jnp.dot(x, w1b[...], preferred_element_type=F32) + b1_ref[0]
        g = jnp.minimum(hh[:, :d_ff], SWIGLU_LIMIT)
        u = jnp.clip(hh[:, d_ff:], -SWIGLU_LIMIT, SWIGLU_LIMIT)
        act = (u + 1.0) * g * jax.nn.sigmoid(SWIGLU_ALPHA * g)
        _store_rows(y_ref, jnp.dot(act.astype(BF16), w2b[...], preferred_element_type=F32) + b2_ref[0])

    @pl.when(i == 0)
    def _():
        for cp in fetch(sch_ref[TE, 0], 0):
            cp.start()

    @pl.when(i >= nu_ref[0])
    def _():
        y_ref[...] = jnp.zeros(y_ref.shape, F32)

    @pl.when(i < nu_ref[0])
    def _():
        @pl.when(sch_ref[FIRST, i] == 1)
        def _():
            slot = sch_ref[SLOT, i]

            @pl.when(sch_ref[NEXT, i] >= 0)
            def _():
                for cp in fetch(sch_ref[NEXT, i], 1 - slot):
                    cp.start()

            for cp in fetch(sch_ref[TE, i], slot):
                cp.wait()
            w1b[...] = w1s[slot].astype(BF16)
            w2b[...] = w2s[slot].astype(BF16)

        @pl.when(sch_ref[HALF, i] == 0)
        def _():
            run(tm)

        @pl.when(sch_ref[HALF, i] == 1)
        def _():
            run(tm // 2)
            y_ref[pl.ds(tm // 2 * SUBLANES, tm // 2 * SUBLANES), :] = jnp.zeros((tm // 2 * SUBLANES, LANES), F32)


def _experts(xs, sched, n_used, w1, b1, w2, b2, tm):
    n_e, d, f2 = w1.shape
    rows = tm * SUBLANES
    by_tile = lambda i, sch, nu: (i, 0)
    by_expert = lambda i, sch, nu: (sch[TE, i], 0, 0)
    grid_spec = pltpu.PrefetchScalarGridSpec(
        num_scalar_prefetch=2,
        grid=(xs.shape[0] // rows,),
        in_specs=[
            pl.BlockSpec((rows, LANES), by_tile),
            pl.BlockSpec(memory_space=pl.ANY),
            pl.BlockSpec((1, 1, f2), by_expert),
            pl.BlockSpec(memory_space=pl.ANY),
            pl.BlockSpec((1, 1, d), by_expert),
        ],
        out_specs=pl.BlockSpec((rows, LANES), by_tile),
        scratch_shapes=[pltpu.VMEM((2, d, f2), F32), pltpu.VMEM((2, f2 // 2, d), F32),
                        pltpu.VMEM((d, f2), BF16), pltpu.VMEM((f2 // 2, d), BF16), pltpu.SemaphoreType.DMA((2, 2))],
    )
    return pl.pallas_call(
        functools.partial(_experts_body, tm=tm),
        grid_spec=grid_spec,
        out_shape=jax.ShapeDtypeStruct(xs.shape, F32),
        compiler_params=_cparams("arbitrary"),
        name="moe_experts",
    )(sched, n_used, xs, w1, b1.reshape(n_e, 1, f2), w2, b2.reshape(n_e, 1, d))


def _combine_body(pos_ref, nxt_ref, h1_ref, wcol_ref, ys_hbm, g_ref, b_ref, o_ref, ybuf, sems, *, alpha, tm):
    i = pl.program_id(0)
    n = pl.num_programs(0)

    def fetch(p_ref, slot):
        def issue(g, c):
            base = pl.multiple_of(g * SUBLANES, SUBLANES)
            for rr in range(SUBLANES):
                dst = pl.ds(pl.multiple_of((base + rr) * SUBLANES, SUBLANES), SUBLANES)
                for k in range(TOP_K):
                    src = pl.multiple_of(p_ref[k, base + rr], SUBLANES)
                    pltpu.make_async_copy(ys_hbm.at[pl.ds(src, SUBLANES), :], ybuf.at[slot, k, dst, :],
                                          sems.at[slot]).start()
            return c
        lax.fori_loop(0, tm // SUBLANES, issue, 0)

    @pl.when(i == 0)
    def _():
        fetch(pos_ref, 0)

    @pl.when(i + 1 < n)
    def _():
        fetch(nxt_ref, (i + 1) % 2)

    slot = i % 2
    for k in range(TOP_K):
        pltpu.make_async_copy(ys_hbm.at[pl.ds(0, tm * SUBLANES), :], ybuf.at[slot, k], sems.at[slot]).wait()
    ff = wcol_ref[:, 0:1] * _load_rows(ybuf.at[slot, 0], tm)
    for k in range(1, TOP_K):
        ff = ff + wcol_ref[:, k:k + 1] * _load_rows(ybuf.at[slot, k], tm)
    o_ref[...] = _layer_norm(alpha * _load_rows(h1_ref, tm) + ff, g_ref[...], b_ref[...])


def _combine(h1, wcol, pos, ys, g, b, alpha, row_off, n):
    d = g.shape[1]
    tm = min(COMBINE_TILE, n)
    off = row_off // tm
    last = n // tm - 1
    vec = pl.BlockSpec((1, d), lambda i: (0, 0))
    return pl.pallas_call(
        functools.partial(_combine_body, alpha=alpha, tm=tm),
        grid=(n // tm,),
        in_specs=[pl.BlockSpec((TOP_K, tm), lambda i: (0, off + i), memory_space=pltpu.SMEM),
                  pl.BlockSpec((TOP_K, tm), lambda i: (0, off + jnp.minimum(i + 1, last)), memory_space=pltpu.SMEM),
                  pl.BlockSpec((tm * SUBLANES, LANES), lambda i: (off + i, 0)),
                  pl.BlockSpec((tm, LANES), lambda i: (off + i, 0)),
                  pl.BlockSpec(memory_space=pl.ANY), vec, vec],
        out_specs=pl.BlockSpec((tm, d), lambda i: (i, 0)),
        out_shape=jax.ShapeDtypeStruct((n, d), F32),
        scratch_shapes=[pltpu.VMEM((2, TOP_K, tm * SUBLANES, LANES), F32), pltpu.SemaphoreType.DMA((2,))],
        compiler_params=_cparams("arbitrary"),
        name="combine_ln2",
    )(pos, pos, h1, wcol, ys, g, b)


def _row2(v):
    return v.reshape(1, -1).astype(F32)


def kernel(x_prompt, x_sample, cache_k, cache_v, page_table, state_conv, w_in, b_in, lambda_q1, lambda_k1,
           lambda_q2, lambda_k2, subln_g, rel_bias, w_attn_proj, conv_w, conv_b, conv_ln_g, conv_ln_b,
           w_conv_proj, b_conv_proj, w_out, ln1_g, ln1_b, router_w, router_b, expert_w1, expert_b1,
           expert_w2, expert_b2, ln2_g, ln2_b):
    depth = w_in.shape[0]
    bp, seq, d = x_prompt.shape
    bs, t_new, _ = x_sample.shape
    d_att = N_HEADS * QK_DIM
    d_conv = conv_w.shape[2]
    n_p, n_s = bp * seq, bs * t_new
    n_tot = n_p + n_s
    tm = min(TOKEN_TILE, math.gcd(n_p, n_s))
    assert tm % LANES == 0
    tm_moe = min(MOE_TILE, n_tot)
    alpha = (2 * depth) ** 0.25

    hp = x_prompt.reshape(n_p, d)
    hs = x_sample.reshape(n_s, d)
    outs = [[] for _ in range(6)]
    for l in range(depth):
        lam_init = 0.8 - 0.6 * math.exp(-0.3 * l)
        lam = (jnp.exp(jnp.sum(lambda_q1[l].astype(F32) * lambda_k1[l].astype(F32)))
               - jnp.exp(jnp.sum(lambda_q2[l].astype(F32) * lambda_k2[l].astype(F32))) + lam_init).reshape(1)
        w_in_bf = w_in[l].astype(BF16)
        b_in_l = _row2(b_in[l])
        g_sub = _row2(subln_g[l])
        conv_args = (conv_w[l].astype(F32), _row2(conv_b[l]), _row2(conv_ln_g[l]), _row2(conv_ln_b[l]))
        merge_w = (w_attn_proj[l].astype(BF16), w_conv_proj[l].astype(BF16), _row2(b_conv_proj[l]),
                   w_out[l].astype(BF16), _row2(ln1_g[l]), _row2(ln1_b[l]),
                   router_w[l].astype(F32).T, router_b[l].astype(F32).reshape(-1, 1))

        qp, kp, vp, kbp, vbp, up, sgp = _in_proj(hp, w_in_bf, b_in_l, d_att, d_conv, (bp, seq) + _attn_tiles(seq))
        op = _prompt_attention(qp, kbp, vbp, rel_bias, lam, g_sub, lam_init)
        up3 = up.reshape(bp, seq, d_conv)
        yp = _conv_prompt(up3, *conv_args)
        merged = _merge(op.reshape(n_p, d_att), yp.reshape(n_p, d_conv), sgp, hp, merge_w, alpha, n_tot, 0, tm, None)

        qs, ks, vs, kbs, vbs, us, sgs = _in_proj(hs, w_in_bf, b_in_l, d_att, d_conv)
        osr = _sample_attention(qs.reshape(bs, t_new, d_att), kbs.reshape(bs, t_new, d_att),
                                vbs.reshape(bs, t_new, d_att), cache_k[l], cache_v[l], page_table,
                                rel_bias, lam, g_sub, lam_init)
        os_ = osr[:, :t_new].reshape(n_s, d_att)
        buf_s = jnp.concatenate([state_conv[l].astype(F32), us.reshape(bs, t_new, d_conv)], axis=1)
        ys = _conv_sample(buf_s.transpose(1, 0, 2), *conv_args).transpose(1, 0, 2).reshape(n_s, d_conv)
        h1, topi, rank, wcol, cnt = _merge(os_, ys, sgs, hs, merge_w, alpha, n_tot, n_p, tm, merged)

        pos, sched, n_used, last_tile, n_tiles = _plan(topi, rank, cnt[:, :, 0], tm, tm_moe)
        xs = _dispatch(h1, pos, last_tile, n_used, n_tiles, tm, tm_moe)
        ysort = _experts(xs, sched, n_used, expert_w1[l], expert_b1[l].astype(F32),
                         expert_w2[l], expert_b2[l].astype(F32), tm_moe)
        g2, b2 = _row2(ln2_g[l]), _row2(ln2_b[l])
        hp = _combine(h1, wcol, pos, ysort, g2, b2, alpha, 0, n_p)
        hs = _combine(h1, wcol, pos, ysort, g2, b2, alpha, n_p, n_s)

        w1 = CONV_WIDTH - 1
        cp = up3[:, seq - w1:] if seq >= w1 else jnp.concatenate(
            [jnp.zeros((bp, w1 - seq, d_conv), F32), up3], axis=1)
        for lst, val in zip(outs, (kp.reshape(bp, seq, N_HEADS, QK_DIM), vp.reshape(bp, seq, N_HEADS, V_DIM), cp,
                                   ks.reshape(bs, t_new, N_HEADS, QK_DIM), vs.reshape(bs, t_new, N_HEADS, V_DIM),
                                   buf_s[:, t_new:])):
            lst.append(val)
    return (hp.reshape(bp, seq, d), hs.reshape(bs, t_new, d)) + tuple(jnp.stack(o) for o in outs)
```

```python
import functools
import math

import jax
import jax.numpy as jnp
from jax import lax
from jax.experimental import pallas as pl
from jax.experimental.pallas import tpu as pltpu

F32 = jnp.float32
BF16 = jnp.bfloat16
I32 = jnp.int32

N_HEADS = 4
HEAD_DIM = 64
QK_DIM = 2 * HEAD_DIM
V_DIM = 2 * HEAD_DIM
ATTN_SCALE = HEAD_DIM ** -0.5
LOG2E = 1.4426950408889634
NEG_INF = -1e30
NUM_BUCKETS = 32
MAX_EXACT = NUM_BUCKETS // 2
MAX_DISTANCE = 128
CONV_WIDTH = 31
TOP_K = 4
SWIGLU_LIMIT = 7.0
SWIGLU_ALPHA = 1.702
LN_EPS = 1e-5
PAGE_SIZE = 128

LANES = 128
SUBLANES = 8
VMEM_LIMIT = 56 * 1024 * 1024

TOKEN_TILE = 512
ATTN_TQ = 1024
ATTN_TK = 1024
ATTN_GROUP = 2
SAMPLE_GROUP = 2
MOE_TILE = 512
COMBINE_TILE = 256
CONV_HALO = 32
CONV_CHUNK = 64


def _cparams(*sem):
    return pltpu.CompilerParams(dimension_semantics=sem, vmem_limit_bytes=VMEM_LIMIT)


def _load_rows(ref, tm):
    return jnp.concatenate([ref[pl.ds(c, tm, stride=SUBLANES), :] for c in range(SUBLANES)], axis=1)


def _store_rows(ref, x):
    tm = x.shape[0]
    for c in range(SUBLANES):
        ref[pl.ds(c, tm, stride=SUBLANES), :] = x[:, c * LANES:(c + 1) * LANES]


def _layer_norm(x, g, b):
    mu = jnp.mean(x, -1, keepdims=True)
    xc = x - mu
    var = jnp.mean(xc * xc, -1, keepdims=True)
    return xc * lax.rsqrt(var + LN_EPS) * g + b


def _in_proj_body(x_ref, w_ref, b_ref, q_ref, k_ref, v_ref, kb_ref, vb_ref, u_ref, sg_ref, *, d_att, d_conv, by_head):
    x = x_ref[...].astype(BF16)

    def seg(lo, hi):
        return jnp.dot(x, w_ref[:, lo:hi], preferred_element_type=F32) + b_ref[:, lo:hi]

    def head(a, h):
        return a[:, h * QK_DIM:(h + 1) * QK_DIM]

    o = 0
    q = seg(o, o + d_att) * (ATTN_SCALE * LOG2E)
    o += d_att
    k = seg(o, o + d_att)
    o += d_att
    v = seg(o, o + d_att)
    o += d_att
    tm = x.shape[0]
    for h in range(N_HEADS):
        k_ref[pl.ds(h, tm, stride=N_HEADS), :] = head(k, h)
        v_ref[pl.ds(h, tm, stride=N_HEADS), :] = head(v, h)
    if by_head:
        for h in range(N_HEADS):
            q_ref[h] = head(q, h).T.astype(BF16)
            kb_ref[h] = head(k, h).astype(BF16)
            vb_ref[h] = head(v, h).T.astype(BF16)
    else:
        q_ref[...] = q.astype(BF16)
        kb_ref[...] = k.astype(BF16)
        vb_ref[...] = v.astype(BF16)
    c = seg(o, o + 2 * d_conv)
    u_ref[...] = c[:, :d_conv] * jax.nn.sigmoid(c[:, d_conv:])
    o += 2 * d_conv
    sg_ref[...] = jax.nn.sigmoid(seg(o, w_ref.shape[1])).astype(BF16)


def _in_proj(x, w_bf, b, d_att, d_conv, attn_tiles=None):
    n, d = x.shape
    tm = min(TOKEN_TILE, n)
    row = lambda width: pl.BlockSpec((tm, width), lambda i: (i, 0))
    full = lambda a: pl.BlockSpec(a.shape, lambda i: (0, 0))
    f32_out = lambda width: jax.ShapeDtypeStruct((n, width), F32)
    kv_shape = jax.ShapeDtypeStruct((n * N_HEADS, QK_DIM), F32)
    kv_spec = pl.BlockSpec((tm * N_HEADS, QK_DIM), lambda i: (i, 0))
    if attn_tiles is None:
        qkv_shapes = [jax.ShapeDtypeStruct((n, d_att), BF16)] * 3
        qkv_specs = [row(d_att)] * 3
    else:
        bsz, seq, tq, tk = attn_tiles
        assert seq % tm == 0 and tq % tm == 0 and tk % tm == 0
        per_seq = seq // tm

        def spec(t, transposed):
            parts = t // tm
            blk = (None, N_HEADS, None, QK_DIM, tm) if transposed else (None, N_HEADS, None, tm, QK_DIM)

            def index(i):
                ti = i % per_seq
                tile, part = ti // parts, ti % parts
                return (i // per_seq, 0, tile, 0, part) if transposed else (i // per_seq, 0, tile, part, 0)
            return pl.BlockSpec(blk, index)

        qkv_shapes = [jax.ShapeDtypeStruct((bsz, N_HEADS, seq // tq, QK_DIM, tq), BF16),
                      jax.ShapeDtypeStruct((bsz, N_HEADS, seq // tk, tk, QK_DIM), BF16),
                      jax.ShapeDtypeStruct((bsz, N_HEADS, seq // tk, V_DIM, tk), BF16)]
        qkv_specs = [spec(tq, True), spec(tk, False), spec(tk, True)]
    out_shape = (qkv_shapes[0], kv_shape, kv_shape, qkv_shapes[1], qkv_shapes[2],
                 f32_out(d_conv), jax.ShapeDtypeStruct((n, 2 * d), BF16))
    return pl.pallas_call(
        functools.partial(_in_proj_body, d_att=d_att, d_conv=d_conv, by_head=attn_tiles is not None),
        grid=(n // tm,),
        in_specs=[row(d), full(w_bf), full(b)],
        out_specs=(qkv_specs[0], kv_spec, kv_spec, qkv_specs[1], qkv_specs[2], row(d_conv), row(2 * d)),
        out_shape=out_shape,
        compiler_params=_cparams("parallel"),
        name="in_proj",
    )(x, w_bf, b)


def _t5_bucket(dist):
    n = jnp.maximum(dist, 0)
    nf = jnp.maximum(n, 1).astype(F32)
    large = MAX_EXACT + (jnp.log(nf / MAX_EXACT) / math.log(MAX_DISTANCE / MAX_EXACT)
                         * (NUM_BUCKETS - MAX_EXACT)).astype(I32)
    large = jnp.minimum(large, NUM_BUCKETS - 1)
    return jnp.where(n < MAX_EXACT, n, large)


def _bias_of(dist, rel_bias, shift=None):
    hot = _t5_bucket(dist)[..., None] == jnp.arange(NUM_BUCKETS, dtype=I32)
    table = rel_bias.astype(F32) if shift is None else rel_bias.astype(F32) - shift
    heads = [jnp.sum(jnp.where(hot, table[:, h], 0.0), -1) for h in range(table.shape[1])]
    return jnp.where(dist >= 0, jnp.stack(heads) * LOG2E, NEG_INF)


def _flash_body(lam_ref, qt_ref, k_ref, vt_ref, btab_ref, g_ref, o_ref, m_s, l_s, acc_s, bias_ref, *, ratio, group,
                lam_init):
    i = pl.program_id(2)
    tk, tq = bias_ref.shape[1:]

    @pl.when(i == 0)
    def _():
        for o in range(ratio + 1):
            table = jnp.broadcast_to(btab_ref[o], (tk, btab_ref.shape[2]))
            bias_ref[o] = pltpu.roll(table, 0, 1, stride=1, stride_axis=0)[:, :tq]

    sub = lax.broadcasted_iota(I32, (QK_DIM, 1), 0)
    qs = []
    for b in range(group):
        qt = qt_ref[b]
        zero = jnp.zeros_like(qt)
        qs.append((jnp.where(sub < HEAD_DIM, qt, zero), jnp.where(sub >= HEAD_DIM, qt, zero)))
    m_s[...] = jnp.full(m_s.shape, NEG_INF, F32)
    l_s[...] = jnp.zeros(l_s.shape, F32)
    acc_s[...] = jnp.zeros(acc_s.shape, F32)

    def step(j, bias, keys=slice(None), cols=slice(None)):
        for b in range(group):
            k = k_ref[b, j, keys, :]
            vt = vt_ref[b, j, :, keys]
            for c in range(2):
                s = jnp.dot(k, qs[b][c][:, cols], preferred_element_type=F32)
                if bias is not None:
                    s = s + bias
                m_old = m_s[b, c, :, cols]
                m_new = jnp.maximum(m_old, jnp.max(s, 0, keepdims=True))
                alpha = jnp.exp2(m_old - m_new)
                p = jnp.exp2(s - m_new)
                l_s[b, c, :, cols] = alpha * l_s[b, c, :, cols] + jnp.sum(p, 0, keepdims=True)
                acc_s[b, c, :, cols] = alpha * acc_s[b, c, :, cols] + jnp.dot(
                    vt, p.astype(BF16), preferred_element_type=F32)
                m_s[b, c, :, cols] = m_new

    def far_step(j, carry):
        step(j, None)
        return carry

    lax.fori_loop(0, jnp.maximum(i * ratio - 1, 0), far_step, 0)

    @pl.when(i >= 1)
    def _():
        step(i * ratio - 1, bias_ref[0])

    for o in range(ratio - 1):
        step(i * ratio + o, bias_ref[o + 1])
    last, half, first_col = i * ratio + ratio - 1, tk // 2, tq - tk // 2
    step(last, bias_ref[ratio, 0:half, :], keys=slice(0, half))
    step(last, bias_ref[ratio, half:tk, first_col:tq], keys=slice(half, tk), cols=slice(first_col, tq))

    for b in range(group):
        o = acc_s[b, 0] / l_s[b, 0] - lam_ref[0] * (acc_s[b, 1] / l_s[b, 1])
        o = o * lax.rsqrt(jnp.mean(o * o, 0, keepdims=True) + LN_EPS) * g_ref[...]
        o_ref[b] = (o * (1.0 - lam_init)).T.astype(BF16)


def _attn_tiles(seq):
    tq = min(ATTN_TQ, seq)
    tk = min(ATTN_TK, tq)
    assert tk >= MAX_DISTANCE and seq % tq == 0 and tq % tk == 0
    return tq, tk


def _prompt_attention(qt, k4, vt, rel_bias, lam, subln_g, lam_init):
    bsz, _, nq, _, tq = qt.shape
    nk, tk = k4.shape[2], k4.shape[3]
    seq, ratio, width = nq * tq, tq // tk, N_HEADS * V_DIM
    group = ATTN_GROUP if bsz % ATTN_GROUP == 0 else 1
    far = rel_bias[NUM_BUCKETS - 1].astype(F32)
    period = tq + tk
    assert period % LANES == 0
    y = jnp.arange(period, dtype=I32)
    r_minus_c = jnp.where(y < tq, y, y - period)
    dist = r_minus_c[None, :] - (jnp.arange(ratio + 1, dtype=I32)[:, None] - 1) * tk
    btab = _bias_of(dist, rel_bias, far).reshape(N_HEADS, ratio + 1, 1, period)
    return pl.pallas_call(
        functools.partial(_flash_body, ratio=ratio, group=group, lam_init=lam_init),
        grid=(bsz // group, N_HEADS, nq),
        in_specs=[
            pl.BlockSpec(memory_space=pltpu.SMEM),
            pl.BlockSpec((group, None, None, QK_DIM, tq), lambda b, h, i: (b, h, i, 0, 0)),
            pl.BlockSpec((group, None, nk, tk, QK_DIM), lambda b, h, i: (b, h, 0, 0, 0)),
            pl.BlockSpec((group, None, nk, V_DIM, tk), lambda b, h, i: (b, h, 0, 0, 0)),
            pl.BlockSpec((None, ratio + 1, 1, period), lambda b, h, i: (h, 0, 0, 0)),
            pl.BlockSpec((V_DIM, 1), lambda b, h, i: (0, 0)),
        ],
        out_specs=pl.BlockSpec((group, tq, V_DIM), lambda b, h, i: (b, i, h)),
        out_shape=jax.ShapeDtypeStruct((bsz, seq, width), BF16),
        scratch_shapes=[pltpu.VMEM((group, 2, 1, tq), F32), pltpu.VMEM((group, 2, 1, tq), F32),
                        pltpu.VMEM((group, 2, V_DIM, tq), F32), pltpu.VMEM((ratio + 1, tk, tq), F32)],
        compiler_params=_cparams("parallel", "parallel", "arbitrary"),
        name="prompt_attention",
    )(lam, qt, k4, vt, btab, subln_g.reshape(V_DIM, 1))


def _paged_body(pt_ref, lam_ref, q_ref, kn_ref, vn_ref, bias_ref, g_ref, *refs, n_pages, t_new, group, lam_init):
    o_ref = refs[2 * group * n_pages]
    lam = lam_ref[0]
    rows = 2 * t_new
    cols = PAGE_SIZE * N_HEADS
    nt = (((1,), (1,)), ((), ()))
    for sq in range(group):
        k_refs = refs[sq * n_pages:(sq + 1) * n_pages]
        v_refs = refs[(group + sq) * n_pages:(group + sq + 1) * n_pages]
        q = q_ref[sq]
        parts = [lax.dot_general(q, k_refs[p][...].astype(BF16), nt, preferred_element_type=F32)
                 for p in range(n_pages)]
        parts.append(lax.dot_general(q, kn_ref[sq], nt, preferred_element_type=F32))
        s = jnp.concatenate(parts, axis=1) + bias_ref[...]
        m = jnp.max(s, -1, keepdims=True)
        p_un = jnp.exp2(s - m)
        pn = p_un / jnp.sum(p_un, -1, keepdims=True)
        w = (pn - lam * pltpu.roll(pn, N_HEADS * rows - t_new, axis=0)).astype(BF16)
        o = jnp.dot(w[:, n_pages * cols:], vn_ref[sq], preferred_element_type=F32)
        for p in range(n_pages):
            o = o + jnp.dot(w[:, p * cols:(p + 1) * cols], v_refs[p][...].astype(BF16), preferred_element_type=F32)
        o = o * lax.rsqrt(jnp.mean(o * o, -1, keepdims=True) + LN_EPS) * g_ref[...]
        o = (o * (1.0 - lam_init)).astype(BF16)
        o_ref[sq] = jnp.concatenate([o[h * rows:(h + 1) * rows] for h in range(N_HEADS)], axis=1)


def _sample_attention(q, kb, vb, cache_k, cache_v, page_table, rel_bias, lam, subln_g, lam_init):
    bs, t_new, width = q.shape
    n_pool = cache_k.shape[0]
    n_pages = page_table.shape[1]
    past = n_pages * PAGE_SIZE
    rows = 2 * t_new
    cols = PAGE_SIZE * N_HEADS
    new_pos = LANES // N_HEADS
    assert rows % SUBLANES == 0 and t_new <= new_pos
    qh = q.reshape(bs, t_new, N_HEADS, QK_DIM).transpose(0, 2, 1, 3)
    lane_map = (jnp.arange(QK_DIM) // HEAD_DIM)[None, :] == jnp.arange(2)[:, None]
    q_rows = (qh[:, :, None] * lane_map[None, None, :, None, :].astype(BF16)).reshape(bs, N_HEADS * rows, QK_DIM)
    pad = ((0, 0), (0, LANES - t_new * N_HEADS), (0, 0))
    kn = jnp.pad(kb.reshape(bs, t_new * N_HEADS, QK_DIM), pad)
    vn = jnp.pad(vb.reshape(bs, t_new * N_HEADS, V_DIM), pad)
    qpos = past + jnp.arange(t_new, dtype=I32)
    kpos = jnp.concatenate([jnp.arange(past + t_new, dtype=I32),
                            jnp.full((new_pos - t_new,), past + t_new + new_pos, I32)])
    col_pos = jnp.repeat(kpos, N_HEADS)
    col_head = jnp.tile(jnp.arange(N_HEADS, dtype=I32), kpos.shape[0])
    b = _bias_of(qpos[:, None] - col_pos[None, :], rel_bias)
    b = jnp.where(col_head[None, None, :] == jnp.arange(N_HEADS, dtype=I32)[:, None, None], b, NEG_INF)
    bias = jnp.stack([b, b], axis=1).reshape(N_HEADS * rows, -1)

    group = SAMPLE_GROUP if bs % SAMPLE_GROUP == 0 else 1
    page_spec = lambda sq, j: pl.BlockSpec(
        (None, cols, QK_DIM), lambda b, pt, sq=sq, j=j: (pt[(b * group + sq) * n_pages + j], 0, 0))
    page_specs = [page_spec(sq, j) for sq in range(group) for j in range(n_pages)]
    per_seq = lambda r, c: pl.BlockSpec((group, r, c), lambda b, pt: (b, 0, 0))
    grid_spec = pltpu.PrefetchScalarGridSpec(
        num_scalar_prefetch=1,
        grid=(bs // group,),
        in_specs=[
            pl.BlockSpec(memory_space=pltpu.SMEM),
            per_seq(N_HEADS * rows, QK_DIM),
            per_seq(LANES, QK_DIM),
            per_seq(LANES, V_DIM),
            pl.BlockSpec(bias.shape, lambda b, pt: (0, 0)),
            pl.BlockSpec((1, V_DIM), lambda b, pt: (0, 0)),
        ] + page_specs * 2,
        out_specs=pl.BlockSpec((group, rows, width), lambda b, pt: (b, 0, 0)),
    )
    ck = cache_k.reshape(n_pool, cols, QK_DIM)
    cv = cache_v.reshape(n_pool, cols, V_DIM)
    return pl.pallas_call(
        functools.partial(_paged_body, n_pages=n_pages, t_new=t_new, group=group, lam_init=lam_init),
        grid_spec=grid_spec,
        out_shape=jax.ShapeDtypeStruct((bs, rows, width), BF16),
        compiler_params=_cparams("parallel"),
        name="sample_attention",
    )(page_table.reshape(-1), lam, q_rows, kn, vn, bias, subln_g, *([ck] * (group * n_pages)),
      *([cv] * (group * n_pages)))


def _conv_prompt_body(u_ref, halo_ref, w_ref, cb_ref, g_ref, b_ref, o_ref, buf, shifted, *, tc):
    i = pl.program_id(1)
    halo = halo_ref[0]
    buf[0:CONV_HALO, :] = jnp.where(i > 0, halo, jnp.zeros_like(halo))
    buf[CONV_HALO:, :] = u_ref[0]
    rows = CONV_HALO + tc - SUBLANES
    for s in range(1, SUBLANES):
        shifted[s - 1, 0:rows, :] = buf[s:s + rows, :]
    first = CONV_HALO - (CONV_WIDTH - 1)
    for r0 in range(0, tc, CONV_CHUNK):
        acc = jnp.zeros((CONV_CHUNK, u_ref.shape[2]), F32) + cb_ref[...]
        for j in range(CONV_WIDTH):
            s = (first + j) % SUBLANES
            a = first + j - s + r0
            win = buf[a:a + CONV_CHUNK, :] if s == 0 else shifted[s - 1, a:a + CONV_CHUNK, :]
            acc = acc + w_ref[j:j + 1, :] * win
        y = _layer_norm(acc, g_ref[...], b_ref[...])
        o_ref[0, r0:r0 + CONV_CHUNK, :] = (y * jax.nn.sigmoid(y)).astype(BF16)


def _conv_prompt(u, conv_w, conv_b, ln_g, ln_b):
    bsz, seq, ch = u.shape
    tc = min(TOKEN_TILE, seq)
    assert seq % tc == 0 and tc % CONV_CHUNK == 0 and tc % CONV_HALO == 0
    vec = pl.BlockSpec((1, ch), lambda b, i: (0, 0))
    return pl.pallas_call(
        functools.partial(_conv_prompt_body, tc=tc),
        grid=(bsz, seq // tc),
        in_specs=[
            pl.BlockSpec((1, tc, ch), lambda b, i: (b, i, 0)),
            pl.BlockSpec((1, CONV_HALO, ch), lambda b, i: (b, jnp.maximum(i * (tc // CONV_HALO) - 1, 0), 0)),
            pl.BlockSpec((CONV_WIDTH, ch), lambda b, i: (0, 0)),
            vec, vec, vec,
        ],
        out_specs=pl.BlockSpec((1, tc, ch), lambda b, i: (b, i, 0)),
        out_shape=jax.ShapeDtypeStruct((bsz, seq, ch), BF16),
        scratch_shapes=[pltpu.VMEM((CONV_HALO + tc, ch), F32), pltpu.VMEM((SUBLANES - 1, CONV_HALO + tc, ch), F32)],
        compiler_params=_cparams("parallel", "arbitrary"),
        name="conv_prompt",
    )(u, u, conv_w, conv_b, ln_g, ln_b)


def _conv_sample_body(buf_ref, w_ref, cb_ref, g_ref, b_ref, o_ref, *, t_new):
    for t in range(t_new):
        acc = jnp.zeros(buf_ref.shape[1:], F32) + cb_ref[...]
        for j in range(CONV_WIDTH):
            acc = acc + w_ref[j:j + 1, :] * buf_ref[t + j]
        y = _layer_norm(acc, g_ref[...], b_ref[...])
        o_ref[t] = (y * jax.nn.sigmoid(y)).astype(BF16)


def _conv_sample(buf_t, conv_w, conv_b, ln_g, ln_b):
    rows, bs, ch = buf_t.shape
    t_new = rows - (CONV_WIDTH - 1)
    gb = min(32, bs)
    assert bs % gb == 0
    vec = pl.BlockSpec((1, ch), lambda i: (0, 0))
    return pl.pallas_call(
        functools.partial(_conv_sample_body, t_new=t_new),
        grid=(bs // gb,),
        in_specs=[pl.BlockSpec((rows, gb, ch), lambda i: (0, i, 0)),
                  pl.BlockSpec((CONV_WIDTH, ch), lambda i: (0, 0)), vec, vec, vec],
        out_specs=pl.BlockSpec((t_new, gb, ch), lambda i: (0, i, 0)),
        out_shape=jax.ShapeDtypeStruct((t_new, bs, ch), BF16),
        compiler_params=_cparams("parallel"),
        name="conv_sample",
    )(buf_t, conv_w, conv_b, ln_g, ln_b)


N_MERGE_OUT = 5


def _merge_body(*refs, alpha, n_alias):
    (o_ref, y_ref, sg_ref, x_ref, wa_ref, wc_ref, bc_ref, wo_ref, g1_ref, b1_ref, rw_ref, rb_ref) = refs[:12]
    h1_ref, ti_ref, rk_ref, wcol_ref, cnt_ref = refs[12 + n_alias:]
    tm, d = x_ref.shape
    a = jnp.dot(o_ref[...], wa_ref[...], preferred_element_type=F32)
    b = jnp.dot(y_ref[...], wc_ref[...], preferred_element_type=F32) + bc_ref[...]
    mix_in = sg_ref[:, :d].astype(F32) * a + sg_ref[:, d:].astype(F32) * b
    mix = jnp.dot(mix_in.astype(BF16), wo_ref[...], preferred_element_type=F32)
    h1 = _layer_norm(alpha * x_ref[...] + mix, g1_ref[...], b1_ref[...])
    _store_rows(h1_ref, h1)
    logits = lax.dot_general(rw_ref[...], h1, (((1,), (1,)), ((), ())), preferred_element_type=F32,
                             precision=lax.Precision.HIGHEST) + rb_ref[...]
    n_e = logits.shape[0]
    eid = lax.broadcasted_iota(I32, logits.shape, 0)
    vals, idxs, hots = [], [], []
    for _ in range(TOP_K):
        m = jnp.max(logits, 0, keepdims=True)
        idx = jnp.min(jnp.where(logits == m, eid, n_e), 0, keepdims=True)
        hot = eid == idx
        vals.append(m)
        idxs.append(idx)
        hots.append(hot)
        logits = jnp.where(hot, -jnp.inf, logits)
    ex = [jnp.exp(v - vals[0]) for v in vals]
    den = ex[0]
    for e in ex[1:]:
        den = den + e
    ti_ref[...] = jnp.concatenate(idxs, 0)
    wrows = jnp.concatenate([e / den for e in ex] + [jnp.zeros((LANES - TOP_K, tm), F32)], 0)
    wcol_ref[...] = wrows.T
    sel = hots[0]
    for hot in hots[1:]:
        sel = sel | hot
    sel_f = jnp.where(sel, 1.0, 0.0)
    before = lax.broadcasted_iota(I32, (tm, tm), 0) < lax.broadcasted_iota(I32, (tm, tm), 1)
    upper = jnp.where(before, 1.0, 0.0).astype(BF16)
    ahead = jnp.dot(sel_f.astype(BF16), upper, preferred_element_type=F32)
    rk_ref[...] = jnp.concatenate(
        [jnp.sum(jnp.where(hot, ahead, 0.0), 0, keepdims=True) for hot in hots], 0).astype(I32)
    cnt = jnp.sum(sel_f, 1, keepdims=True).astype(I32)
    cnt_ref[0] = jnp.broadcast_to(cnt, (n_e, LANES))


def _merge(o, yact, sg, x, wts, alpha, n_total, row_off, tm, prev):
    n, d = x.shape
    wa, wc, bc, wo, g1, b1, rwt, rb = wts
    n_e = rwt.shape[0]
    off = row_off // tm
    last = n // tm - 1
    steps = n // tm if prev is not None else n_total // tm
    row = lambda width: pl.BlockSpec((tm, width), lambda i: (jnp.minimum(i, last), 0))
    full = lambda a: pl.BlockSpec(a.shape, lambda i: (0, 0))
    in_specs = [row(o.shape[1]), row(yact.shape[1]), row(2 * d), row(d),
                full(wa), full(wc), full(bc), full(wo), full(g1), full(b1), full(rwt), full(rb)]
    args = [o, yact, sg, x, wa, wc, bc, wo, g1, b1, rwt, rb]
    aliases = {}
    if prev is not None:
        in_specs += [pl.BlockSpec(memory_space=pl.ANY)] * N_MERGE_OUT
        aliases = {len(args) + j: j for j in range(N_MERGE_OUT)}
        args += list(prev)
    slot_major = pl.BlockSpec((TOP_K, tm), lambda i: (0, off + i))
    return pl.pallas_call(
        functools.partial(_merge_body, alpha=alpha, n_alias=0 if prev is None else N_MERGE_OUT),
        grid=(steps,),
        in_specs=in_specs,
        out_specs=(pl.BlockSpec((tm * SUBLANES, LANES), lambda i: (off + i, 0)), slot_major, slot_major,
                   pl.BlockSpec((tm, LANES), lambda i: (off + i, 0)),
                   pl.BlockSpec((1, n_e, LANES), lambda i: (off + i, 0, 0))),
        out_shape=(jax.ShapeDtypeStruct((n_total * SUBLANES, LANES), F32),
                   jax.ShapeDtypeStruct((TOP_K, n_total), I32),
                   jax.ShapeDtypeStruct((TOP_K, n_total), I32),
                   jax.ShapeDtypeStruct((n_total, LANES), F32),
                   jax.ShapeDtypeStruct((n_total // tm, n_e, LANES), I32)),
        input_output_aliases=aliases,
        compiler_params=_cparams("parallel"),
        name="merge_ln1_router",
    )(*args)


def _plan(topi, rank, cnt, tm_tok, tm_moe):
    k, n = topi.shape
    n_e = cnt.shape[1]
    before = jnp.cumsum(cnt, 0) - cnt
    total = jnp.sum(cnt, 0)
    ntile_e = (total + tm_moe - 1) // tm_moe
    tend = jnp.cumsum(ntile_e)
    tstart = tend - ntile_e
    base = (tstart * tm_moe)[None, :] + before
    base_tok = jnp.repeat(base.T, tm_tok, axis=1)
    hot = topi[:, None, :] == jnp.arange(n_e, dtype=I32)[None, :, None]
    pos = (jnp.sum(jnp.where(hot, base_tok[None], 0), 1).astype(I32) + rank) * SUBLANES
    n_tiles = (k * n + tm_moe - 1) // tm_moe + n_e
    n_used = tend[-1]
    tid = jnp.arange(n_tiles, dtype=I32)
    src = jnp.minimum(tid, n_used - 1)
    te = jnp.minimum(jnp.sum((src[:, None] >= tend[None, :]).astype(I32), 1), n_e - 1)
    first = (tid == tstart[te]).astype(I32)
    half = (total[te] - (tid - tstart[te]) * tm_moe <= tm_moe // 2).astype(I32)
    slot = (jnp.cumsum(first) - 1) % 2
    nxt = jnp.where(tend[te] < n_used, te[jnp.minimum(tend[te], n_tiles - 1)], -1)
    sched = jnp.stack([te, first, half, slot, nxt]).astype(I32)
    last_tile = jnp.where(ntile_e > 0, tend - 1, -1).astype(I32)
    return pos, sched, n_used.reshape(1).astype(I32), last_tile, n_tiles


def _dispatch_body(lt_ref, nu_ref, pos_ref, x_ref, xs_hbm, zbuf, xbuf, zsem, sems, *, tm, tm_moe, n_e, n_tiles,
                   n_steps):
    i = pl.program_id(0)

    def zero_copy(tile):
        start = pl.multiple_of(tile * (tm_moe * SUBLANES), tm_moe * SUBLANES)
        return pltpu.make_async_copy(zbuf, xs_hbm.at[pl.ds(start, tm_moe * SUBLANES), :], zsem)

    @pl.when(i == 0)
    def _():
        zbuf[...] = jnp.zeros(zbuf.shape, F32)

        def z_start(e, c):
            @pl.when(lt_ref[e] >= 0)
            def _():
                zero_copy(lt_ref[e]).start()
            return c

        def z_wait(e, c):
            @pl.when(lt_ref[e] >= 0)
            def _():
                zero_copy(lt_ref[e]).wait()
            return c

        def t_start(t, c):
            zero_copy(t).start()
            return c

        def t_wait(t, c):
            zero_copy(t).wait()
            return c

        lax.fori_loop(0, n_e, z_start, 0)
        lax.fori_loop(nu_ref[0], n_tiles, t_start, 0)
        lax.fori_loop(0, n_e, z_wait, 0)
        lax.fori_loop(nu_ref[0], n_tiles, t_wait, 0)

    slot = i % 2

    def drain(s):
        for k in range(TOP_K):
            pltpu.make_async_copy(xbuf.at[s], xs_hbm.at[pl.ds(0, tm * SUBLANES), :], sems.at[s]).wait()

    @pl.when(i >= 2)
    def _():
        drain(slot)

    xbuf[slot] = x_ref[...]

    def issue(g, c):
        base = pl.multiple_of(g * SUBLANES, SUBLANES)
        for rr in range(SUBLANES):
            src = xbuf.at[slot, pl.ds(pl.multiple_of((base + rr) * SUBLANES, SUBLANES), SUBLANES), :]
            for k in range(TOP_K):
                dst = pl.multiple_of(pos_ref[k, base + rr], SUBLANES)
                pltpu.make_async_copy(src, xs_hbm.at[pl.ds(dst, SUBLANES), :], sems.at[slot]).start()
        return c

    lax.fori_loop(0, tm // SUBLANES, issue, 0)

    @pl.when(i == n_steps - 1)
    def _():
        drain(slot)
        if n_steps >= 2:
            drain(1 - slot)


def _dispatch(h1, pos, last_tile, n_used, n_tiles, tm, tm_moe):
    n = h1.shape[0] // SUBLANES
    n_e = last_tile.shape[0]
    grid_spec = pltpu.PrefetchScalarGridSpec(
        num_scalar_prefetch=2,
        grid=(n // tm,),
        in_specs=[pl.BlockSpec((TOP_K, tm), lambda i, lt, nu: (0, i), memory_space=pltpu.SMEM),
                  pl.BlockSpec((tm * SUBLANES, LANES), lambda i, lt, nu: (i, 0))],
        out_specs=pl.BlockSpec(memory_space=pl.ANY),
        scratch_shapes=[pltpu.VMEM((tm_moe * SUBLANES, LANES), F32), pltpu.VMEM((2, tm * SUBLANES, LANES), F32),
                        pltpu.SemaphoreType.DMA, pltpu.SemaphoreType.DMA((2,))],
    )
    return pl.pallas_call(
        functools.partial(_dispatch_body, tm=tm, tm_moe=tm_moe, n_e=n_e, n_tiles=n_tiles, n_steps=n // tm),
        grid_spec=grid_spec,
        out_shape=jax.ShapeDtypeStruct((n_tiles * tm_moe * SUBLANES, LANES), F32),
        compiler_params=_cparams("arbitrary"),
        name="moe_dispatch",
    )(last_tile, n_used, pos, h1)


TE, FIRST, HALF, SLOT, NEXT = range(5)


def _experts_body(sch_ref, nu_ref, x_ref, w1_hbm, b1_ref, w2_hbm, b2_ref, y_ref, w1s, w2s, w1b, w2b, sems, *, tm):
    i = pl.program_id(0)
    d_ff = w2b.shape[0]

    def fetch(expert, slot):
        return (pltpu.make_async_copy(w1_hbm.at[expert], w1s.at[slot], sems.at[0, slot]),
                pltpu.make_async_copy(w2_hbm.at[expert], w2s.at[slot], sems.at[1, slot]))

    def run(rows):
        x = _load_rows(x_ref, rows).astype(BF16)
        hh = jnp.dot(x, w1b[...], preferred_element_type=F32) + b1_ref[0]
        g = jnp.minimum(hh[:, :d_ff], SWIGLU_LIMIT)
        u = jnp.clip(hh[:, d_ff:], -SWIGLU_LIMIT, SWIGLU_LIMIT)
        act = (u + 1.0) * g * jax.nn.sigmoid(SWIGLU_ALPHA * g)
        _store_rows(y_ref, jnp.dot(act.astype(BF16), w2b[...], preferred_element_type=F32) + b2_ref[0])

    @pl.when(i == 0)
    def _():
        for cp in fetch(sch_ref[TE, 0], 0):
            cp.start()

    @pl.when(i >= nu_ref[0])
    def _():
        y_ref[...] = jnp.zeros(y_ref.shape, F32)

    @pl.when(i < nu_ref[0])
    def _():
        @pl.when(sch_ref[FIRST, i] == 1)
        def _():
            slot = sch_ref[SLOT, i]

            @pl.when(sch_ref[NEXT, i] >= 0)
            def _():
                for cp in fetch(sch_ref[NEXT, i], 1 - slot):
                    cp.start()

            for cp in fetch(sch_ref[TE, i], slot):
                cp.wait()
            w1b[...] = w1s[slot].astype(BF16)
            w2b[...] = w2s[slot].astype(BF16)

        @pl.when(sch_ref[HALF, i] == 0)
        def _():
            run(tm)

        @pl.when(sch_ref[HALF, i] == 1)
        def _():
            run(tm // 2)
            y_ref[pl.ds(tm // 2 * SUBLANES, tm // 2 * SUBLANES), :] = jnp.zeros((tm // 2 * SUBLANES, LANES), F32)


def _experts(xs, sched, n_used, w1, b1, w2, b2, tm):
    n_e, d, f2 = w1.shape
    rows = tm * SUBLANES
    by_tile = lambda i, sch, nu: (i, 0)
    by_expert = lambda i, sch, nu: (sch[TE, i], 0, 0)
    grid_spec = pltpu.PrefetchScalarGridSpec(
        num_scalar_prefetch=2,
        grid=(xs.shape[0] // rows,),
        in_specs=[
            pl.BlockSpec((rows, LANES), by_tile),
            pl.BlockSpec(memory_space=pl.ANY),
            pl.BlockSpec((1, 1, f2), by_expert),
            pl.BlockSpec(memory_space=pl.ANY),
            pl.BlockSpec((1, 1, d), by_expert),
        ],
        out_specs=pl.BlockSpec((rows, LANES), by_tile),
        scratch_shapes=[pltpu.VMEM((2, d, f2), F32), pltpu.VMEM((2, f2 // 2, d), F32),
                        pltpu.VMEM((d, f2), BF16), pltpu.VMEM((f2 // 2, d), BF16), pltpu.SemaphoreType.DMA((2, 2))],
    )
    return pl.pallas_call(
        functools.partial(_experts_body, tm=tm),
        grid_spec=grid_spec,
        out_shape=jax.ShapeDtypeStruct(xs.shape, F32),
        compiler_params=_cparams("arbitrary"),
        name="moe_experts",
    )(sched, n_used, xs, w1, b1.reshape(n_e, 1, f2), w2, b2.reshape(n_e, 1, d))


def _combine_body(pos_ref, nxt_ref, h1_ref, wcol_ref, ys_hbm, g_ref, b_ref, o_ref, ybuf, sems, *, alpha, tm):
    i = pl.program_id(0)
    n = pl.num_programs(0)

    def fetch(p_ref, slot):
        def issue(g, c):
            base = pl.multiple_of(g * SUBLANES, SUBLANES)
            for rr in range(SUBLANES):
                dst = pl.ds(pl.multiple_of((base + rr) * SUBLANES, SUBLANES), SUBLANES)
                for k in range(TOP_K):
                    src = pl.multiple_of(p_ref[k, base + rr], SUBLANES)
                    pltpu.make_async_copy(ys_hbm.at[pl.ds(src, SUBLANES), :], ybuf.at[slot, k, dst, :],
                                          sems.at[slot]).start()
            return c
        lax.fori_loop(0, tm // SUBLANES, issue, 0)

    @pl.when(i == 0)
    def _():
        fetch(pos_ref, 0)

    @pl.when(i + 1 < n)
    def _():
        fetch(nxt_ref, (i + 1) % 2)

    slot = i % 2
    for k in range(TOP_K):
        pltpu.make_async_copy(ys_hbm.at[pl.ds(0, tm * SUBLANES), :], ybuf.at[slot, k], sems.at[slot]).wait()
    ff = wcol_ref[:, 0:1] * _load_rows(ybuf.at[slot, 0], tm)
    for k in range(1, TOP_K):
        ff = ff + wcol_ref[:, k:k + 1] * _load_rows(ybuf.at[slot, k], tm)
    o_ref[...] = _layer_norm(alpha * _load_rows(h1_ref, tm) + ff, g_ref[...], b_ref[...])


def _combine(h1, wcol, pos, ys, g, b, alpha, row_off, n):
    d = g.shape[1]
    tm = min(COMBINE_TILE, n)
    off = row_off // tm
    last = n // tm - 1
    vec = pl.BlockSpec((1, d), lambda i: (0, 0))
    return pl.pallas_call(
        functools.partial(_combine_body, alpha=alpha, tm=tm),
        grid=(n // tm,),
        in_specs=[pl.BlockSpec((TOP_K, tm), lambda i: (0, off + i), memory_space=pltpu.SMEM),
                  pl.BlockSpec((TOP_K, tm), lambda i: (0, off + jnp.minimum(i + 1, last)), memory_space=pltpu.SMEM),
                  pl.BlockSpec((tm * SUBLANES, LANES), lambda i: (off + i, 0)),
                  pl.BlockSpec((tm, LANES), lambda i: (off + i, 0)),
                  pl.BlockSpec(memory_space=pl.ANY), vec, vec],
        out_specs=pl.BlockSpec((tm, d), lambda i: (i, 0)),
        out_shape=jax.ShapeDtypeStruct((n, d), F32),
        scratch_shapes=[pltpu.VMEM((2, TOP_K, tm * SUBLANES, LANES), F32), pltpu.SemaphoreType.DMA((2,))],
        compiler_params=_cparams("arbitrary"),
        name="combine_ln2",
    )(pos, pos, h1, wcol, ys, g, b)


def _row2(v):
    return v.reshape(1, -1).astype(F32)


def kernel(x_prompt, x_sample, cache_k, cache_v, page_table, state_conv, w_in, b_in, lambda_q1, lambda_k1,
           lambda_q2, lambda_k2, subln_g, rel_bias, w_attn_proj, conv_w, conv_b, conv_ln_g, conv_ln_b,
           w_conv_proj, b_conv_proj, w_out, ln1_g, ln1_b, router_w, router_b, expert_w1, expert_b1,
           expert_w2, expert_b2, ln2_g, ln2_b):
    depth = w_in.shape[0]
    bp, seq, d = x_prompt.shape
    bs, t_new, _ = x_sample.shape
    d_att = N_HEADS * QK_DIM
    d_conv = conv_w.shape[2]
    n_p, n_s = bp * seq, bs * t_new
    n_tot = n_p + n_s
    tm = min(TOKEN_TILE, math.gcd(n_p, n_s))
    assert tm % LANES == 0
    tm_moe = min(MOE_TILE, n_tot)
    alpha = (2 * depth) ** 0.25

    hp = x_prompt.reshape(n_p, d)
    hs = x_sample.reshape(n_s, d)
    outs = [[] for _ in range(6)]
    for l in range(depth):
        lam_init = 0.8 - 0.6 * math.exp(-0.3 * l)
        lam = (jnp.exp(jnp.sum(lambda_q1[l].astype(F32) * lambda_k1[l].astype(F32)))
               - jnp.exp(jnp.sum(lambda_q2[l].astype(F32) * lambda_k2[l].astype(F32))) + lam_init).reshape(1)
        w_in_bf = w_in[l].astype(BF16)
        b_in_l = _row2(b_in[l])
        g_sub = _row2(subln_g[l])
        conv_args = (conv_w[l].astype(F32), _row2(conv_b[l]), _row2(conv_ln_g[l]), _row2(conv_ln_b[l]))
        merge_w = (w_attn_proj[l].astype(BF16), w_conv_proj[l].astype(BF16), _row2(b_conv_proj[l]),
                   w_out[l].astype(BF16), _row2(ln1_g[l]), _row2(ln1_b[l]),
                   router_w[l].astype(F32).T, router_b[l].astype(F32).reshape(-1, 1))

        qp, kp, vp, kbp, vbp, up, sgp = _in_proj(hp, w_in_bf, b_in_l, d_att, d_conv, (bp, seq) + _attn_tiles(seq))
        op = _prompt_attention(qp, kbp, vbp, rel_bias, lam, g_sub, lam_init)
        up3 = up.reshape(bp, seq, d_conv)
        yp = _conv_prompt(up3, *conv_args)
        merged = _merge(op.reshape(n_p, d_att), yp.reshape(n_p, d_conv), sgp, hp, merge_w, alpha, n_tot, 0, tm, None)

        qs, ks, vs, kbs, vbs, us, sgs = _in_proj(hs, w_in_bf, b_in_l, d_att, d_conv)
        osr = _sample_attention(qs.reshape(bs, t_new, d_att), kbs.reshape(bs, t_new, d_att),
                                vbs.reshape(bs, t_new, d_att), cache_k[l], cache_v[l], page_table,
                                rel_bias, lam, g_sub, lam_init)
        os_ = osr[:, :t_new].reshape(n_s, d_att)
        buf_s = jnp.concatenate([state_conv[l].astype(F32), us.reshape(bs, t_new, d_conv)], axis=1)
        ys = _conv_sample(buf_s.transpose(1, 0, 2), *conv_args).transpose(1, 0, 2).reshape(n_s, d_conv)
        h1, topi, rank, wcol, cnt = _merge(os_, ys, sgs, hs, merge_w, alpha, n_tot, n_p, tm, merged)

        pos, sched, n_used, last_tile, n_tiles = _plan(topi, rank, cnt[:, :, 0], tm, tm_moe)
        xs = _dispatch(h1, pos, last_tile, n_used, n_tiles, tm, tm_moe)
        ysort = _experts(xs, sched, n_used, expert_w1[l], expert_b1[l].astype(F32),
                         expert_w2[l], expert_b2[l].astype(F32), tm_moe)
        g2, b2 = _row2(ln2_g[l]), _row2(ln2_b[l])
        hp = _combine(h1, wcol, pos, ysort, g2, b2, alpha, 0, n_p)
        hs = _combine(h1, wcol, pos, ysort, g2, b2, alpha, n_p, n_s)

        w1 = CONV_WIDTH - 1
        cp = up3[:, seq - w1:] if seq >= w1 else jnp.concatenate(
            [jnp.zeros((bp, w1 - seq, d_conv), F32), up3], axis=1)
        for lst, val in zip(outs, (kp.reshape(bp, seq, N_HEADS, QK_DIM), vp.reshape(bp, seq, N_HEADS, V_DIM), cp,
                                   ks.reshape(bs, t_new, N_HEADS, QK_DIM), vs.reshape(bs, t_new, N_HEADS, V_DIM),
                                   buf_s[:, t_new:])):
            lst.append(val)
    return (hp.reshape(bp, seq, d), hs.reshape(bs, t_new, d)) + tuple(jnp.stack(o) for o in outs)
```

```python
import functools
import math

import jax
import jax.numpy as jnp
from jax import lax
from jax.experimental import pallas as pl
from jax.experimental.pallas import tpu as pltpu

F32 = jnp.float32
BF16 = jnp.bfloat16
I32 = jnp.int32

N_HEADS = 4
HEAD_DIM = 64
QK_DIM = 2 * HEAD_DIM
V_DIM = 2 * HEAD_DIM
ATTN_SCALE = HEAD_DIM ** -0.5
LOG2E = 1.4426950408889634
NEG_INF = -1e30
NUM_BUCKETS = 32
MAX_EXACT = NUM_BUCKETS // 2
MAX_DISTANCE = 128
CONV_WIDTH = 31
TOP_K = 4
SWIGLU_LIMIT = 7.0
SWIGLU_ALPHA = 1.702
LN_EPS = 1e-5
PAGE_SIZE = 128

LANES = 128
SUBLANES = 8
VMEM_LIMIT = 56 * 1024 * 1024

TOKEN_TILE = 512
ATTN_TQ = 1024
ATTN_TK = 1024
ATTN_GROUP = 2
SAMPLE_GROUP = 2
MOE_TILE = 512
COMBINE_TILE = 512
CONV_HALO = 32
CONV_CHUNK = 64


def _cparams(*sem):
    return pltpu.CompilerParams(dimension_semantics=sem, vmem_limit_bytes=VMEM_LIMIT)


def _load_rows(ref, tm):
    return jnp.concatenate([ref[pl.ds(c, tm, stride=SUBLANES), :] for c in range(SUBLANES)], axis=1)


def _store_rows(ref, x):
    tm = x.shape[0]
    for c in range(SUBLANES):
        ref[pl.ds(c, tm, stride=SUBLANES), :] = x[:, c * LANES:(c + 1) * LANES]


def _layer_norm(x, g, b):
    mu = jnp.mean(x, -1, keepdims=True)
    xc = x - mu
    var = jnp.mean(xc * xc, -1, keepdims=True)
    return xc * lax.rsqrt(var + LN_EPS) * g + b


def _in_proj_body(x_ref, w_ref, b_ref, q_ref, k_ref, v_ref, kb_ref, vb_ref, u_ref, sg_ref, *, d_att, d_conv, by_head):
    x = x_ref[...].astype(BF16)

    def seg(lo, hi):
        return jnp.dot(x, w_ref[:, lo:hi], preferred_element_type=F32) + b_ref[:, lo:hi]

    def head(a, h):
        return a[:, h * QK_DIM:(h + 1) * QK_DIM]

    o = 0
    q = seg(o, o + d_att) * (ATTN_SCALE * LOG2E)
    o += d_att
    k = seg(o, o + d_att)
    o += d_att
    v = seg(o, o + d_att)
    o += d_att
    tm = x.shape[0]
    for h in range(N_HEADS):
        k_ref[pl.ds(h, tm, stride=N_HEADS), :] = head(k, h)
        v_ref[pl.ds(h, tm, stride=N_HEADS), :] = head(v, h)
    if by_head:
        for h in range(N_HEADS):
            q_ref[h] = head(q, h).T.astype(BF16)
            kb_ref[h] = head(k, h).astype(BF16)
            vb_ref[h] = head(v, h).T.astype(BF16)
    else:
        q_ref[...] = q.astype(BF16)
        kb_ref[...] = k.astype(BF16)
        vb_ref[...] = v.astype(BF16)
    c = seg(o, o + 2 * d_conv)
    u_ref[...] = c[:, :d_conv] * jax.nn.sigmoid(c[:, d_conv:])
    o += 2 * d_conv
    sg_ref[...] = jax.nn.sigmoid(seg(o, w_ref.shape[1])).astype(BF16)


def _in_proj(x, w_bf, b, d_att, d_conv, attn_tiles=None):
    n, d = x.shape
    tm = min(TOKEN_TILE, n)
    row = lambda width: pl.BlockSpec((tm, width), lambda i: (i, 0))
    full = lambda a: pl.BlockSpec(a.shape, lambda i: (0, 0))
    f32_out = lambda width: jax.ShapeDtypeStruct((n, width), F32)
    kv_shape = jax.ShapeDtypeStruct((n * N_HEADS, QK_DIM), F32)
    kv_spec = pl.BlockSpec((tm * N_HEADS, QK_DIM), lambda i: (i, 0))
    if attn_tiles is None:
        qkv_shapes = [jax.ShapeDtypeStruct((n, d_att), BF16)] * 3
        qkv_specs = [row(d_att)] * 3
    else:
        bsz, seq, tq, tk = attn_tiles
        assert seq % tm == 0 and tq % tm == 0 and tk % tm == 0
        per_seq = seq // tm

        def spec(t, transposed):
            parts = t // tm
            blk = (None, N_HEADS, None, QK_DIM, tm) if transposed else (None, N_HEADS, None, tm, QK_DIM)

            def index(i):
                ti = i % per_seq
                tile, part = ti // parts, ti % parts
                return (i // per_seq, 0, tile, 0, part) if transposed else (i // per_seq, 0, tile, part, 0)
            return pl.BlockSpec(blk, index)

        qkv_shapes = [jax.ShapeDtypeStruct((bsz, N_HEADS, seq // tq, QK_DIM, tq), BF16),
                      jax.ShapeDtypeStruct((bsz, N_HEADS, seq // tk, tk, QK_DIM), BF16),
                      jax.ShapeDtypeStruct((bsz, N_HEADS, seq // tk, V_DIM, tk), BF16)]
        qkv_specs = [spec(tq, True), spec(tk, False), spec(tk, True)]
    out_shape = (qkv_shapes[0], kv_shape, kv_shape, qkv_shapes[1], qkv_shapes[2],
                 f32_out(d_conv), jax.ShapeDtypeStruct((n, 2 * d), BF16))
    return pl.pallas_call(
        functools.partial(_in_proj_body, d_att=d_att, d_conv=d_conv, by_head=attn_tiles is not None),
        grid=(n // tm,),
        in_specs=[row(d), full(w_bf), full(b)],
        out_specs=(qkv_specs[0], kv_spec, kv_spec, qkv_specs[1], qkv_specs[2], row(d_conv), row(2 * d)),
        out_shape=out_shape,
        compiler_params=_cparams("parallel"),
        name="in_proj",
    )(x, w_bf, b)


def _t5_bucket(dist):
    n = jnp.maximum(dist, 0)
    nf = jnp.maximum(n, 1).astype(F32)
    large = MAX_EXACT + (jnp.log(nf / MAX_EXACT) / math.log(MAX_DISTANCE / MAX_EXACT)
                         * (NUM_BUCKETS - MAX_EXACT)).astype(I32)
    large = jnp.minimum(large, NUM_BUCKETS - 1)
    return jnp.where(n < MAX_EXACT, n, large)


def _bias_of(dist, rel_bias, shift=None):
    hot = _t5_bucket(dist)[..., None] == jnp.arange(NUM_BUCKETS, dtype=I32)
    table = rel_bias.astype(F32) if shift is None else rel_bias.astype(F32) - shift
    heads = [jnp.sum(jnp.where(hot, table[:, h], 0.0), -1) for h in range(table.shape[1])]
    return jnp.where(dist >= 0, jnp.stack(heads) * LOG2E, NEG_INF)


def _flash_body(lam_ref, qt_ref, k_ref, vt_ref, btab_ref, g_ref, o_ref, m_s, l_s, acc_s, bias_ref, *, ratio, group,
                lam_init):
    i = pl.program_id(2)
    tk, tq = bias_ref.shape[1:]

    @pl.when(i == 0)
    def _():
        for o in range(ratio + 1):
            table = jnp.broadcast_to(btab_ref[o], (tk, btab_ref.shape[2]))
            bias_ref[o] = pltpu.roll(table, 0, 1, stride=1, stride_axis=0)[:, :tq]

    sub = lax.broadcasted_iota(I32, (QK_DIM, 1), 0)
    qs = []
    for b in range(group):
        qt = qt_ref[b]
        zero = jnp.zeros_like(qt)
        qs.append((jnp.where(sub < HEAD_DIM, qt, zero), jnp.where(sub >= HEAD_DIM, qt, zero)))
    m_s[...] = jnp.full(m_s.shape, NEG_INF, F32)
    l_s[...] = jnp.zeros(l_s.shape, F32)
    acc_s[...] = jnp.zeros(acc_s.shape, F32)

    def step(j, bias, keys=slice(None), cols=slice(None)):
        for b in range(group):
            k = k_ref[b, j, keys, :]
            vt = vt_ref[b, j, :, keys]
            for c in range(2):
                s = jnp.dot(k, qs[b][c][:, cols], preferred_element_type=F32)
                if bias is not None:
                    s = s + bias
                m_old = m_s[b, c, :, cols]
                m_new = jnp.maximum(m_old, jnp.max(s, 0, keepdims=True))
                alpha = jnp.exp2(m_old - m_new)
                p = jnp.exp2(s - m_new)
                l_s[b, c, :, cols] = alpha * l_s[b, c, :, cols] + jnp.sum(p, 0, keepdims=True)
                acc_s[b, c, :, cols] = alpha * acc_s[b, c, :, cols] + jnp.dot(
                    vt, p.astype(BF16), preferred_element_type=F32)
                m_s[b, c, :, cols] = m_new

    def far_step(j, carry):
        step(j, None)
        return carry

    lax.fori_loop(0, jnp.maximum(i * ratio - 1, 0), far_step, 0)

    @pl.when(i >= 1)
    def _():
        step(i * ratio - 1, bias_ref[0])

    for o in range(ratio - 1):
        step(i * ratio + o, bias_ref[o + 1])
    last, half, first_col = i * ratio + ratio - 1, tk // 2, tq - tk // 2
    step(last, bias_ref[ratio, 0:half, :], keys=slice(0, half))
    step(last, bias_ref[ratio, half:tk, first_col:tq], keys=slice(half, tk), cols=slice(first_col, tq))

    for b in range(group):
        o = acc_s[b, 0] / l_s[b, 0] - lam_ref[0] * (acc_s[b, 1] / l_s[b, 1])
        o = o * lax.rsqrt(jnp.mean(o * o, 0, keepdims=True) + LN_EPS) * g_ref[...]
        o_ref[b] = (o * (1.0 - lam_init)).T.astype(BF16)


def _attn_tiles(seq):
    tq = min(ATTN_TQ, seq)
    tk = min(ATTN_TK, tq)
    assert tk >= MAX_DISTANCE and seq % tq == 0 and tq % tk == 0
    return tq, tk


def _prompt_attention(qt, k4, vt, rel_bias, lam, subln_g, lam_init):
    bsz, _, nq, _, tq = qt.shape
    nk, tk = k4.shape[2], k4.shape[3]
    seq, ratio, width = nq * tq, tq // tk, N_HEADS * V_DIM
    group = ATTN_GROUP if bsz % ATTN_GROUP == 0 else 1
    far = rel_bias[NUM_BUCKETS - 1].astype(F32)
    period = tq + tk
    assert period % LANES == 0
    y = jnp.arange(period, dtype=I32)
    r_minus_c = jnp.where(y < tq, y, y - period)
    dist = r_minus_c[None, :] - (jnp.arange(ratio + 1, dtype=I32)[:, None] - 1) * tk
    btab = _bias_of(dist, rel_bias, far).reshape(N_HEADS, ratio + 1, 1, period)
    return pl.pallas_call(
        functools.partial(_flash_body, ratio=ratio, group=group, lam_init=lam_init),
        grid=(bsz // group, N_HEADS, nq),
        in_specs=[
            pl.BlockSpec(memory_space=pltpu.SMEM),
            pl.BlockSpec((group, None, None, QK_DIM, tq), lambda b, h, i: (b, h, i, 0, 0)),
            pl.BlockSpec((group, None, nk, tk, QK_DIM), lambda b, h, i: (b, h, 0, 0, 0)),
            pl.BlockSpec((group, None, nk, V_DIM, tk), lambda b, h, i: (b, h, 0, 0, 0)),
            pl.BlockSpec((None, ratio + 1, 1, period), lambda b, h, i: (h, 0, 0, 0)),
            pl.BlockSpec((V_DIM, 1), lambda b, h, i: (0, 0)),
        ],
        out_specs=pl.BlockSpec((group, tq, V_DIM), lambda b, h, i: (b, i, h)),
        out_shape=jax.ShapeDtypeStruct((bsz, seq, width), BF16),
        scratch_shapes=[pltpu.VMEM((group, 2, 1, tq), F32), pltpu.VMEM((group, 2, 1, tq), F32),
                        pltpu.VMEM((group, 2, V_DIM, tq), F32), pltpu.VMEM((ratio + 1, tk, tq), F32)],
        compiler_params=_cparams("parallel", "parallel", "arbitrary"),
        name="prompt_attention",
    )(lam, qt, k4, vt, btab, subln_g.reshape(V_DIM, 1))


def _paged_body(pt_ref, lam_ref, q_ref, kn_ref, vn_ref, bias_ref, g_ref, *refs, n_pages, t_new, group, lam_init):
    o_ref = refs[2 * group * n_pages]
    lam = lam_ref[0]
    rows = 2 * t_new
    cols = PAGE_SIZE * N_HEADS
    nt = (((1,), (1,)), ((), ()))
    for sq in range(group):
        k_refs = refs[sq * n_pages:(sq + 1) * n_pages]
        v_refs = refs[(group + sq) * n_pages:(group + sq + 1) * n_pages]
        q = q_ref[sq]
        parts = [lax.dot_general(q, k_refs[p][...].astype(BF16), nt, preferred_element_type=F32)
                 for p in range(n_pages)]
        parts.append(lax.dot_general(q, kn_ref[sq], nt, preferred_element_type=F32))
        s = jnp.concatenate(parts, axis=1) + bias_ref[...]
        m = jnp.max(s, -1, keepdims=True)
        p_un = jnp.exp2(s - m)
        pn = p_un / jnp.sum(p_un, -1, keepdims=True)
        w = (pn - lam * pltpu.roll(pn, N_HEADS * rows - t_new, axis=0)).astype(BF16)
        o = jnp.dot(w[:, n_pages * cols:], vn_ref[sq], preferred_element_type=F32)
        for p in range(n_pages):
            o = o + jnp.dot(w[:, p * cols:(p + 1) * cols], v_refs[p][...].astype(BF16), preferred_element_type=F32)
        o = o * lax.rsqrt(jnp.mean(o * o, -1, keepdims=True) + LN_EPS) * g_ref[...]
        o = (o * (1.0 - lam_init)).astype(BF16)
        o_ref[sq] = jnp.concatenate([o[h * rows:(h + 1) * rows] for h in range(N_HEADS)], axis=1)


def _sample_attention(q, kb, vb, cache_k, cache_v, page_table, rel_bias, lam, subln_g, lam_init):
    bs, t_new, width = q.shape
    n_pool = cache_k.shape[0]
    n_pages = page_table.shape[1]
    past = n_pages * PAGE_SIZE
    rows = 2 * t_new
    cols = PAGE_SIZE * N_HEADS
    new_pos = LANES // N_HEADS
    assert rows % SUBLANES == 0 and t_new <= new_pos
    qh = q.reshape(bs, t_new, N_HEADS, QK_DIM).transpose(0, 2, 1, 3)
    lane_map = (jnp.arange(QK_DIM) // HEAD_DIM)[None, :] == jnp.arange(2)[:, None]
    q_rows = (qh[:, :, None] * lane_map[None, None, :, None, :].astype(BF16)).reshape(bs, N_HEADS * rows, QK_DIM)
    pad = ((0, 0), (0, LANES - t_new * N_HEADS), (0, 0))
    kn = jnp.pad(kb.reshape(bs, t_new * N_HEADS, QK_DIM), pad)
    vn = jnp.pad(vb.reshape(bs, t_new * N_HEADS, V_DIM), pad)
    qpos = past + jnp.arange(t_new, dtype=I32)
    kpos = jnp.concatenate([jnp.arange(past + t_new, dtype=I32),
                            jnp.full((new_pos - t_new,), past + t_new + new_pos, I32)])
    col_pos = jnp.repeat(kpos, N_HEADS)
    col_head = jnp.tile(jnp.arange(N_HEADS, dtype=I32), kpos.shape[0])
    b = _bias_of(qpos[:, None] - col_pos[None, :], rel_bias)
    b = jnp.where(col_head[None, None, :] == jnp.arange(N_HEADS, dtype=I32)[:, None, None], b, NEG_INF)
    bias = jnp.stack([b, b], axis=1).reshape(N_HEADS * rows, -1)

    group = SAMPLE_GROUP if bs % SAMPLE_GROUP == 0 else 1
    page_spec = lambda sq, j: pl.BlockSpec(
        (None, cols, QK_DIM), lambda b, pt, sq=sq, j=j: (pt[(b * group + sq) * n_pages + j], 0, 0))
    page_specs = [page_spec(sq, j) for sq in range(group) for j in range(n_pages)]
    per_seq = lambda r, c: pl.BlockSpec((group, r, c), lambda b, pt: (b, 0, 0))
    grid_spec = pltpu.PrefetchScalarGridSpec(
        num_scalar_prefetch=1,
        grid=(bs // group,),
        in_specs=[
            pl.BlockSpec(memory_space=pltpu.SMEM),
            per_seq(N_HEADS * rows, QK_DIM),
            per_seq(LANES, QK_DIM),
            per_seq(LANES, V_DIM),
            pl.BlockSpec(bias.shape, lambda b, pt: (0, 0)),
            pl.BlockSpec((1, V_DIM), lambda b, pt: (0, 0)),
        ] + page_specs * 2,
        out_specs=pl.BlockSpec((group, rows, width), lambda b, pt: (b, 0, 0)),
    )
    ck = cache_k.reshape(n_pool, cols, QK_DIM)
    cv = cache_v.reshape(n_pool, cols, V_DIM)
    return pl.pallas_call(
        functools.partial(_paged_body, n_pages=n_pages, t_new=t_new, group=group, lam_init=lam_init),
        grid_spec=grid_spec,
        out_shape=jax.ShapeDtypeStruct((bs, rows, width), BF16),
        compiler_params=_cparams("parallel"),
        name="sample_attention",
    )(page_table.reshape(-1), lam, q_rows, kn, vn, bias, subln_g, *([ck] * (group * n_pages)),
      *([cv] * (group * n_pages)))


def _conv_prompt_body(u_ref, halo_ref, w_ref, cb_ref, g_ref, b_ref, o_ref, buf, shifted, *, tc):
    i = pl.program_id(1)
    halo = halo_ref[0]
    buf[0:CONV_HALO, :] = jnp.where(i > 0, halo, jnp.zeros_like(halo))
    buf[CONV_HALO:, :] = u_ref[0]
    rows = CONV_HALO + tc - SUBLANES
    for s in range(1, SUBLANES):
        shifted[s - 1, 0:rows, :] = buf[s:s + rows, :]
    first = CONV_HALO - (CONV_WIDTH - 1)
    for r0 in range(0, tc, CONV_CHUNK):
        acc = jnp.zeros((CONV_CHUNK, u_ref.shape[2]), F32) + cb_ref[...]
        for j in range(CONV_WIDTH):
            s = (first + j) % SUBLANES
            a = first + j - s + r0
            win = buf[a:a + CONV_CHUNK, :] if s == 0 else shifted[s - 1, a:a + CONV_CHUNK, :]
            acc = acc + w_ref[j:j + 1, :] * win
        y = _layer_norm(acc, g_ref[...], b_ref[...])
        o_ref[0, r0:r0 + CONV_CHUNK, :] = (y * jax.nn.sigmoid(y)).astype(BF16)


def _conv_prompt(u, conv_w, conv_b, ln_g, ln_b):
    bsz, seq, ch = u.shape
    tc = min(TOKEN_TILE, seq)
    assert seq % tc == 0 and tc % CONV_CHUNK == 0 and tc % CONV_HALO == 0
    vec = pl.BlockSpec((1, ch), lambda b, i: (0, 0))
    return pl.pallas_call(
        functools.partial(_conv_prompt_body, tc=tc),
        grid=(bsz, seq // tc),
        in_specs=[
            pl.BlockSpec((1, tc, ch), lambda b, i: (b, i, 0)),
            pl.BlockSpec((1, CONV_HALO, ch), lambda b, i: (b, jnp.maximum(i * (tc // CONV_HALO) - 1, 0), 0)),
            pl.BlockSpec((CONV_WIDTH, ch), lambda b, i: (0, 0)),
            vec, vec, vec,
        ],
        out_specs=pl.BlockSpec((1, tc, ch), lambda b, i: (b, i, 0)),
        out_shape=jax.ShapeDtypeStruct((bsz, seq, ch), BF16),
        scratch_shapes=[pltpu.VMEM((CONV_HALO + tc, ch), F32), pltpu.VMEM((SUBLANES - 1, CONV_HALO + tc, ch), F32)],
        compiler_params=_cparams("parallel", "arbitrary"),
        name="conv_prompt",
    )(u, u, conv_w, conv_b, ln_g, ln_b)


def _conv_sample_body(buf_ref, w_ref, cb_ref, g_ref, b_ref, o_ref, *, t_new):
    for t in range(t_new):
        acc = jnp.zeros(buf_ref.shape[1:], F32) + cb_ref[...]
        for j in range(CONV_WIDTH):
            acc = acc + w_ref[j:j + 1, :] * buf_ref[t + j]
        y = _layer_norm(acc, g_ref[...], b_ref[...])
        o_ref[t] = (y * jax.nn.sigmoid(y)).astype(BF16)


def _conv_sample(buf_t, conv_w, conv_b, ln_g, ln_b):
    rows, bs, ch = buf_t.shape
    t_new = rows - (CONV_WIDTH - 1)
    gb = min(32, bs)
    assert bs % gb == 0
    vec = pl.BlockSpec((1, ch), lambda i: (0, 0))
    return pl.pallas_call(
        functools.partial(_conv_sample_body, t_new=t_new),
        grid=(bs // gb,),
        in_specs=[pl.BlockSpec((rows, gb, ch), lambda i: (0, i, 0)),
                  pl.BlockSpec((CONV_WIDTH, ch), lambda i: (0, 0)), vec, vec, vec],
        out_specs=pl.BlockSpec((t_new, gb, ch), lambda i: (0, i, 0)),
        out_shape=jax.ShapeDtypeStruct((t_new, bs, ch), BF16),
        compiler_params=_cparams("parallel"),
        name="conv_sample",
    )(buf_t, conv_w, conv_b, ln_g, ln_b)


N_MERGE_OUT = 5


def _merge_body(*refs, alpha, n_alias):
    (o_ref, y_ref, sg_ref, x_ref, wa_ref, wc_ref, bc_ref, wo_ref, g1_ref, b1_ref, rw_ref, rb_ref) = refs[:12]
    h1_ref, ti_ref, rk_ref, wcol_ref, cnt_ref = refs[12 + n_alias:]
    tm, d = x_ref.shape
    a = jnp.dot(o_ref[...], wa_ref[...], preferred_element_type=F32)
    b = jnp.dot(y_ref[...], wc_ref[...], preferred_element_type=F32) + bc_ref[...]
    mix_in = sg_ref[:, :d].astype(F32) * a + sg_ref[:, d:].astype(F32) * b
    mix = jnp.dot(mix_in.astype(BF16), wo_ref[...], preferred_element_type=F32)
    h1 = _layer_norm(alpha * x_ref[...] + mix, g1_ref[...], b1_ref[...])
    _store_rows(h1_ref, h1)
    logits = lax.dot_general(rw_ref[...], h1, (((1,), (1,)), ((), ())), preferred_element_type=F32,
                             precision=lax.Precision.HIGHEST) + rb_ref[...]
    n_e = logits.shape[0]
    eid = lax.broadcasted_iota(I32, logits.shape, 0)
    vals, idxs, hots = [], [], []
    for _ in range(TOP_K):
        m = jnp.max(logits, 0, keepdims=True)
        idx = jnp.min(jnp.where(logits == m, eid, n_e), 0, keepdims=True)
        hot = eid == idx
        vals.append(m)
        idxs.append(idx)
        hots.append(hot)
        logits = jnp.where(hot, -jnp.inf, logits)
    ex = [jnp.exp(v - vals[0]) for v in vals]
    den = ex[0]
    for e in ex[1:]:
        den = den + e
    ti_ref[...] = jnp.concatenate(idxs, 0)
    wrows = jnp.concatenate([e / den for e in ex] + [jnp.zeros((LANES - TOP_K, tm), F32)], 0)
    wcol_ref[...] = wrows.T
    sel = hots[0]
    for hot in hots[1:]:
        sel = sel | hot
    sel_f = jnp.where(sel, 1.0, 0.0)
    before = lax.broadcasted_iota(I32, (tm, tm), 0) < lax.broadcasted_iota(I32, (tm, tm), 1)
    upper = jnp.where(before, 1.0, 0.0).astype(BF16)
    ahead = jnp.dot(sel_f.astype(BF16), upper, preferred_element_type=F32)
    rk_ref[...] = jnp.concatenate(
        [jnp.sum(jnp.where(hot, ahead, 0.0), 0, keepdims=True) for hot in hots], 0).astype(I32)
    cnt = jnp.sum(sel_f, 1, keepdims=True).astype(I32)
    cnt_ref[0] = jnp.broadcast_to(cnt, (n_e, LANES))


def _merge(o, yact, sg, x, wts, alpha, n_total, row_off, tm, prev):
    n, d = x.shape
    wa, wc, bc, wo, g1, b1, rwt, rb = wts
    n_e = rwt.shape[0]
    off = row_off // tm
    last = n // tm - 1
    steps = n // tm if prev is not None else n_total // tm
    row = lambda width: pl.BlockSpec((tm, width), lambda i: (jnp.minimum(i, last), 0))
    full = lambda a: pl.BlockSpec(a.shape, lambda i: (0, 0))
    in_specs = [row(o.shape[1]), row(yact.shape[1]), row(2 * d), row(d),
                full(wa), full(wc), full(bc), full(wo), full(g1), full(b1), full(rwt), full(rb)]
    args = [o, yact, sg, x, wa, wc, bc, wo, g1, b1, rwt, rb]
    aliases = {}
    if prev is not None:
        in_specs += [pl.BlockSpec(memory_space=pl.ANY)] * N_MERGE_OUT
        aliases = {len(args) + j: j for j in range(N_MERGE_OUT)}
        args += list(prev)
    slot_major = pl.BlockSpec((TOP_K, tm), lambda i: (0, off + i))
    return pl.pallas_call(
        functools.partial(_merge_body, alpha=alpha, n_alias=0 if prev is None else N_MERGE_OUT),
        grid=(steps,),
        in_specs=in_specs,
        out_specs=(pl.BlockSpec((tm * SUBLANES, LANES), lambda i: (off + i, 0)), slot_major, slot_major,
                   pl.BlockSpec((tm, LANES), lambda i: (off + i, 0)),
                   pl.BlockSpec((1, n_e, LANES), lambda i: (off + i, 0, 0))),
        out_shape=(jax.ShapeDtypeStruct((n_total * SUBLANES, LANES), F32),
                   jax.ShapeDtypeStruct((TOP_K, n_total), I32),
                   jax.ShapeDtypeStruct((TOP_K, n_total), I32),
                   jax.ShapeDtypeStruct((n_total, LANES), F32),
                   jax.ShapeDtypeStruct((n_total // tm, n_e, LANES), I32)),
        input_output_aliases=aliases,
        compiler_params=_cparams("parallel"),
        name="merge_ln1_router",
    )(*args)


def _plan(topi, rank, cnt, tm_tok, tm_moe):
    k, n = topi.shape
    n_e = cnt.shape[1]
    before = jnp.cumsum(cnt, 0) - cnt
    total = jnp.sum(cnt, 0)
    ntile_e = (total + tm_moe - 1) // tm_moe
    tend = jnp.cumsum(ntile_e)
    tstart = tend - ntile_e
    base = (tstart * tm_moe)[None, :] + before
    base_tok = jnp.repeat(base.T, tm_tok, axis=1)
    hot = topi[:, None, :] == jnp.arange(n_e, dtype=I32)[None, :, None]
    pos = (jnp.sum(jnp.where(hot, base_tok[None], 0), 1).astype(I32) + rank) * SUBLANES
    n_tiles = (k * n + tm_moe - 1) // tm_moe + n_e
    n_used = tend[-1]
    tid = jnp.arange(n_tiles, dtype=I32)
    src = jnp.minimum(tid, n_used - 1)
    te = jnp.minimum(jnp.sum((src[:, None] >= tend[None, :]).astype(I32), 1), n_e - 1)
    first = (tid == tstart[te]).astype(I32)
    half = (total[te] - (tid - tstart[te]) * tm_moe <= tm_moe // 2).astype(I32)
    slot = (jnp.cumsum(first) - 1) % 2
    nxt = jnp.where(tend[te] < n_used, te[jnp.minimum(tend[te], n_tiles - 1)], -1)
    sched = jnp.stack([te, first, half, slot, nxt]).astype(I32)
    last_tile = jnp.where(ntile_e > 0, tend - 1, -1).astype(I32)
    return pos, sched, n_used.reshape(1).astype(I32), last_tile, n_tiles


def _dispatch_body(lt_ref, nu_ref, pos_ref, x_ref, xs_hbm, zbuf, xbuf, zsem, sems, *, tm, tm_moe, n_e, n_tiles,
                   n_steps):
    i = pl.program_id(0)

    def zero_copy(tile):
        start = pl.multiple_of(tile * (tm_moe * SUBLANES), tm_moe * SUBLANES)
        return pltpu.make_async_copy(zbuf, xs_hbm.at[pl.ds(start, tm_moe * SUBLANES), :], zsem)

    @pl.when(i == 0)
    def _():
        zbuf[...] = jnp.zeros(zbuf.shape, F32)

        def z_start(e, c):
            @pl.when(lt_ref[e] >= 0)
            def _():
                zero_copy(lt_ref[e]).start()
            return c

        def z_wait(e, c):
            @pl.when(lt_ref[e] >= 0)
            def _():
                zero_copy(lt_ref[e]).wait()
            return c

        def t_start(t, c):
            zero_copy(t).start()
            return c

        def t_wait(t, c):
            zero_copy(t).wait()
            return c

        lax.fori_loop(0, n_e, z_start, 0)
        lax.fori_loop(nu_ref[0], n_tiles, t_start, 0)
        lax.fori_loop(0, n_e, z_wait, 0)
        lax.fori_loop(nu_ref[0], n_tiles, t_wait, 0)

    slot = i % 2

    def drain(s):
        for k in range(TOP_K):
            pltpu.make_async_copy(xbuf.at[s], xs_hbm.at[pl.ds(0, tm * SUBLANES), :], sems.at[s]).wait()

    @pl.when(i >= 2)
    def _():
        drain(slot)

    xbuf[slot] = x_ref[...]

    def issue(g, c):
        base = pl.multiple_of(g * SUBLANES, SUBLANES)
        for rr in range(SUBLANES):
            src = xbuf.at[slot, pl.ds(pl.multiple_of((base + rr) * SUBLANES, SUBLANES), SUBLANES), :]
            for k in range(TOP_K):
                dst = pl.multiple_of(pos_ref[k, base + rr], SUBLANES)
                pltpu.make_async_copy(src, xs_hbm.at[pl.ds(dst, SUBLANES), :], sems.at[slot]).start()
        return c

    lax.fori_loop(0, tm // SUBLANES, issue, 0)

    @pl.when(i == n_steps - 1)
    def _():
        drain(slot)
        if n_steps >= 2:
            drain(1 - slot)


def _dispatch(h1, pos, last_tile, n_used, n_tiles, tm, tm_moe):
    n = h1.shape[0] // SUBLANES
    n_e = last_tile.shape[0]
    grid_spec = pltpu.PrefetchScalarGridSpec(
        num_scalar_prefetch=2,
        grid=(n // tm,),
        in_specs=[pl.BlockSpec((TOP_K, tm), lambda i, lt, nu: (0, i), memory_space=pltpu.SMEM),
                  pl.BlockSpec((tm * SUBLANES, LANES), lambda i, lt, nu: (i, 0))],
        out_specs=pl.BlockSpec(memory_space=pl.ANY),
        scratch_shapes=[pltpu.VMEM((tm_moe * SUBLANES, LANES), F32), pltpu.VMEM((2, tm * SUBLANES, LANES), F32),
                        pltpu.SemaphoreType.DMA, pltpu.SemaphoreType.DMA((2,))],
    )
    return pl.pallas_call(
        functools.partial(_dispatch_body, tm=tm, tm_moe=tm_moe, n_e=n_e, n_tiles=n_tiles, n_steps=n // tm),
        grid_spec=grid_spec,
        out_shape=jax.ShapeDtypeStruct((n_tiles * tm_moe * SUBLANES, LANES), F32),
        compiler_params=_cparams("arbitrary"),
        name="moe_dispatch",
    )(last_tile, n_used, pos, h1)


TE, FIRST, HALF, SLOT, NEXT = range(5)


def _experts_body(sch_ref, nu_ref, x_ref, w1_hbm, b1_ref, w2_hbm, b2_ref, y_ref, w1s, w2s, w1b, w2b, sems, *, tm):
    i = pl.program_id(0)
    d_ff = w2b.shape[0]

    def fetch(expert, slot):
        return (pltpu.make_async_copy(w1_hbm.at[expert], w1s.at[slot], sems.at[0, slot]),
                pltpu.make_async_copy(w2_hbm.at[expert], w2s.at[slot], sems.at[1, slot]))

    def run(rows):
        x = _load_rows(x_ref, rows).astype(BF16)
        hh = jnp.dot(x, w1b[...], preferred_element_type=F32) + b1_ref[0]
        g = jnp.minimum(hh[:, :d_ff], SWIGLU_LIMIT)
        u = jnp.clip(hh[:, d_ff:], -SWIGLU_LIMIT, SWIGLU_LIMIT)
        act = (u + 1.0) * g * jax.nn.sigmoid(SWIGLU_ALPHA * g)
        _store_rows(y_ref, jnp.dot(act.astype(BF16), w2b[...], preferred_element_type=F32) + b2_ref[0])

    @pl.when(i == 0)
    def _():
        for cp in fetch(sch_ref[TE, 0], 0):
            cp.start()

    @pl.when(i >= nu_ref[0])
    def _():
        y_ref[...] = jnp.zeros(y_ref.shape, F32)

    @pl.when(i < nu_ref[0])
    def _():
        @pl.when(sch_ref[FIRST, i] == 1)
        def _():
            slot = sch_ref[SLOT, i]

            @pl.when(sch_ref[NEXT, i] >= 0)
            def _():
                for cp in fetch(sch_ref[NEXT, i], 1 - slot):
                    cp.start()

            for cp in fetch(sch_ref[TE, i], slot):
                cp.wait()
            w1b[...] = w1s[slot].astype(BF16)
            w2b[...] = w2s[slot].astype(BF16)

        @pl.when(sch_ref[HALF, i] == 0)
        def _():
            run(tm)

        @pl.when(sch_ref[HALF, i] == 1)
        def _():
            run(tm // 2)
            y_ref[pl.ds(tm // 2 * SUBLANES, tm // 2 * SUBLANES), :] = jnp.zeros((tm // 2 * SUBLANES, LANES), F32)


def _experts(xs, sched, n_used, w1, b1, w2, b2, tm):
    n_e, d, f2 = w1.shape
    rows = tm * SUBLANES
    by_tile = lambda i, sch, nu: (i, 0)
    by_expert = lambda i, sch, nu: (sch[TE, i], 0, 0)
    grid_spec = pltpu.PrefetchScalarGridSpec(
        num_scalar_prefetch=2,
        grid=(xs.shape[0] // rows,),
        in_specs=[
            pl.BlockSpec((rows, LANES), by_tile),
            pl.BlockSpec(memory_space=pl.ANY),
            pl.BlockSpec((1, 1, f2), by_expert),
            pl.BlockSpec(memory_space=pl.ANY),
            pl.BlockSpec((1, 1, d), by_expert),
        ],
        out_specs=pl.BlockSpec((rows, LANES), by_tile),
        scratch_shapes=[pltpu.VMEM((2, d, f2), F32), pltpu.VMEM((2, f2 // 2, d), F32),
                        pltpu.VMEM((d, f2), BF16), pltpu.VMEM((f2 // 2, d), BF16), pltpu.SemaphoreType.DMA((2, 2))],
    )
    return pl.pallas_call(
        functools.partial(_experts_body, tm=tm),
        grid_spec=grid_spec,
        out_shape=jax.ShapeDtypeStruct(xs.shape, F32),
        compiler_params=_cparams("arbitrary"),
        name="moe_experts",
    )(sched, n_used, xs, w1, b1.reshape(n_e, 1, f2), w2, b2.reshape(n_e, 1, d))


def _combine_body(pos_ref, nxt_ref, h1_ref, wcol_ref, ys_hbm, g_ref, b_ref, o_ref, ybuf, sems, *, alpha, tm):
    i = pl.program_id(0)
    n = pl.num_programs(0)

    def fetch(p_ref, slot):
        def issue(g, c):
            base = pl.multiple_of(g * SUBLANES, SUBLANES)
            for rr in range(SUBLANES):
                dst = pl.ds(pl.multiple_of((base + rr) * SUBLANES, SUBLANES), SUBLANES)
                for k in range(TOP_K):
                    src = pl.multiple_of(p_ref[k, base + rr], SUBLANES)
                    pltpu.make_async_copy(ys_hbm.at[pl.ds(src, SUBLANES), :], ybuf.at[slot, k, dst, :],
                                          sems.at[slot]).start()
            return c
        lax.fori_loop(0, tm // SUBLANES, issue, 0)

    @pl.when(i == 0)
    def _():
        fetch(pos_ref, 0)

    @pl.when(i + 1 < n)
    def _():
        fetch(nxt_ref, (i + 1) % 2)

    slot = i % 2
    for k in range(TOP_K):
        pltpu.make_async_copy(ys_hbm.at[pl.ds(0, tm * SUBLANES), :], ybuf.at[slot, k], sems.at[slot]).wait()
    ff = wcol_ref[:, 0:1] * _load_rows(ybuf.at[slot, 0], tm)
    for k in range(1, TOP_K):
        ff = ff + wcol_ref[:, k:k + 1] * _load_rows(ybuf.at[slot, k], tm)
    o_ref[...] = _layer_norm(alpha * _load_rows(h1_ref, tm) + ff, g_ref[...], b_ref[...])


def _combine(h1, wcol, pos, ys, g, b, alpha, row_off, n):
    d = g.shape[1]
    tm = min(COMBINE_TILE, n)
    off = row_off // tm
    last = n // tm - 1
    vec = pl.BlockSpec((1, d), lambda i: (0, 0))
    return pl.pallas_call(
        functools.partial(_combine_body, alpha=alpha, tm=tm),
        grid=(n // tm,),
        in_specs=[pl.BlockSpec((TOP_K, tm), lambda i: (0, off + i), memory_space=pltpu.SMEM),
                  pl.BlockSpec((TOP_K, tm), lambda i: (0, off + jnp.minimum(i + 1, last)), memory_space=pltpu.SMEM),
                  pl.BlockSpec((tm * SUBLANES, LANES), lambda i: (off + i, 0)),
                  pl.BlockSpec((tm, LANES), lambda i: (off + i, 0)),
                  pl.BlockSpec(memory_space=pl.ANY), vec, vec],
        out_specs=pl.BlockSpec((tm, d), lambda i: (i, 0)),
        out_shape=jax.ShapeDtypeStruct((n, d), F32),
        scratch_shapes=[pltpu.VMEM((2, TOP_K, tm * SUBLANES, LANES), F32), pltpu.SemaphoreType.DMA((2,))],
        compiler_params=_cparams("arbitrary"),
        name="combine_ln2",
    )(pos, pos, h1, wcol, ys, g, b)


def _row2(v):
    return v.reshape(1, -1).astype(F32)


def kernel(x_prompt, x_sample, cache_k, cache_v, page_table, state_conv, w_in, b_in, lambda_q1, lambda_k1,
           lambda_q2, lambda_k2, subln_g, rel_bias, w_attn_proj, conv_w, conv_b, conv_ln_g, conv_ln_b,
           w_conv_proj, b_conv_proj, w_out, ln1_g, ln1_b, router_w, router_b, expert_w1, expert_b1,
           expert_w2, expert_b2, ln2_g, ln2_b):
    depth = w_in.shape[0]
    bp, seq, d = x_prompt.shape
    bs, t_new, _ = x_sample.shape
    d_att = N_HEADS * QK_DIM
    d_conv = conv_w.shape[2]
    n_p, n_s = bp * seq, bs * t_new
    n_tot = n_p + n_s
    tm = min(TOKEN_TILE, math.gcd(n_p, n_s))
    assert tm % LANES == 0
    tm_moe = min(MOE_TILE, n_tot)
    alpha = (2 * depth) ** 0.25

    hp = x_prompt.reshape(n_p, d)
    hs = x_sample.reshape(n_s, d)
    outs = [[] for _ in range(6)]
    for l in range(depth):
        lam_init = 0.8 - 0.6 * math.exp(-0.3 * l)
        lam = (jnp.exp(jnp.sum(lambda_q1[l].astype(F32) * lambda_k1[l].astype(F32)))
               - jnp.exp(jnp.sum(lambda_q2[l].astype(F32) * lambda_k2[l].astype(F32))) + lam_init).reshape(1)
        w_in_bf = w_in[l].astype(BF16)
        b_in_l = _row2(b_in[l])
        g_sub = _row2(subln_g[l])
        conv_args = (conv_w[l].astype(F32), _row2(conv_b[l]), _row2(conv_ln_g[l]), _row2(conv_ln_b[l]))
        merge_w = (w_attn_proj[l].astype(BF16), w_conv_proj[l].astype(BF16), _row2(b_conv_proj[l]),
                   w_out[l].astype(BF16), _row2(ln1_g[l]), _row2(ln1_b[l]),
                   router_w[l].astype(F32).T, router_b[l].astype(F32).reshape(-1, 1))

        qp, kp, vp, kbp, vbp, up, sgp = _in_proj(hp, w_in_bf, b_in_l, d_att, d_conv, (bp, seq) + _attn_tiles(seq))
        op = _prompt_attention(qp, kbp, vbp, rel_bias, lam, g_sub, lam_init)
        up3 = up.reshape(bp, seq, d_conv)
        yp = _conv_prompt(up3, *conv_args)
        merged = _merge(op.reshape(n_p, d_att), yp.reshape(n_p, d_conv), sgp, hp, merge_w, alpha, n_tot, 0, tm, None)

        qs, ks, vs, kbs, vbs, us, sgs = _in_proj(hs, w_in_bf, b_in_l, d_att, d_conv)
        osr = _sample_attention(qs.reshape(bs, t_new, d_att), kbs.reshape(bs, t_new, d_att),
                                vbs.reshape(bs, t_new, d_att), cache_k[l], cache_v[l], page_table,
                                rel_bias, lam, g_sub, lam_init)
        os_ = osr[:, :t_new].reshape(n_s, d_att)
        buf_s = jnp.concatenate([state_conv[l].astype(F32), us.reshape(bs, t_new, d_conv)], axis=1)
        ys = _conv_sample(buf_s.transpose(1, 0, 2), *conv_args).transpose(1, 0, 2).reshape(n_s, d_conv)
        h1, topi, rank, wcol, cnt = _merge(os_, ys, sgs, hs, merge_w, alpha, n_tot, n_p, tm, merged)

        pos, sched, n_used, last_tile, n_tiles = _plan(topi, rank, cnt[:, :, 0], tm, tm_moe)
        xs = _dispatch(h1, pos, last_tile, n_used, n_tiles, tm, tm_moe)
        ysort = _experts(xs, sched, n_used, expert_w1[l], expert_b1[l].astype(F32),
                         expert_w2[l], expert_b2[l].astype(F32), tm_moe)
        g2, b2 = _row2(ln2_g[l]), _row2(ln2_b[l])
        hp = _combine(h1, wcol, pos, ysort, g2, b2, alpha, 0, n_p)
        hs = _combine(h1, wcol, pos, ysort, g2, b2, alpha, n_p, n_s)

        w1 = CONV_WIDTH - 1
        cp = up3[:, seq - w1:] if seq >= w1 else jnp.concatenate(
            [jnp.zeros((bp, w1 - seq, d_conv), F32), up3], axis=1)
        for lst, val in zip(outs, (kp.reshape(bp, seq, N_HEADS, QK_DIM), vp.reshape(bp, seq, N_HEADS, V_DIM), cp,
                                   ks.reshape(bs, t_new, N_HEADS, QK_DIM), vs.reshape(bs, t_new, N_HEADS, V_DIM),
                                   buf_s[:, t_new:])):
            lst.append(val)
    return (hp.reshape(bp, seq, d), hs.reshape(bs, t_new, d)) + tuple(jnp.stack(o) for o in outs)
```

```python
import functools
import math

import jax
import jax.numpy as jnp
from jax import lax
from jax.experimental import pallas as pl
from jax.experimental.pallas import tpu as pltpu

F32 = jnp.float32
BF16 = jnp.bfloat16
I32 = jnp.int32

N_HEADS = 4
HEAD_DIM = 64
QK_DIM = 2 * HEAD_DIM
V_DIM = 2 * HEAD_DIM
ATTN_SCALE = HEAD_DIM ** -0.5
LOG2E = 1.4426950408889634
NEG_INF = -1e30
NUM_BUCKETS = 32
MAX_EXACT = NUM_BUCKETS // 2
MAX_DISTANCE = 128
CONV_WIDTH = 31
TOP_K = 4
SWIGLU_LIMIT = 7.0
SWIGLU_ALPHA = 1.702
LN_EPS = 1e-5
PAGE_SIZE = 128

LANES = 128
SUBLANES = 8
VMEM_LIMIT = 56 * 1024 * 1024

TOKEN_TILE = 512
ATTN_TQ = 1024
ATTN_TK = 1024
ATTN_GROUP = 2
SAMPLE_GROUP = 2
MOE_TILE = 512
COMBINE_TILE = 256
CONV_HALO = 32
CONV_CHUNK = 64


def _cparams(*sem):
    return pltpu.CompilerParams(dimension_semantics=sem, vmem_limit_bytes=VMEM_LIMIT)


def _load_rows(ref, tm):
    return jnp.concatenate([ref[pl.ds(c, tm, stride=SUBLANES), :] for c in range(SUBLANES)], axis=1)


def _store_rows(ref, x):
    tm = x.shape[0]
    for c in range(SUBLANES):
        ref[pl.ds(c, tm, stride=SUBLANES), :] = x[:, c * LANES:(c + 1) * LANES]


def _layer_norm(x, g, b):
    mu = jnp.mean(x, -1, keepdims=True)
    xc = x - mu
    var = jnp.mean(xc * xc, -1, keepdims=True)
    return xc * lax.rsqrt(var + LN_EPS) * g + b


def _in_proj_body(x_ref, w_ref, b_ref, q_ref, k_ref, v_ref, kb_ref, vb_ref, u_ref, sg_ref, *, d_att, d_conv, by_head):
    x = x_ref[...].astype(BF16)

    def seg(lo, hi):
        return jnp.dot(x, w_ref[:, lo:hi], preferred_element_type=F32) + b_ref[:, lo:hi]

    def head(a, h):
        return a[:, h * QK_DIM:(h + 1) * QK_DIM]

    o = 0
    q = seg(o, o + d_att) * (ATTN_SCALE * LOG2E)
    o += d_att
    k = seg(o, o + d_att)
    o += d_att
    v = seg(o, o + d_att)
    o += d_att
    tm = x.shape[0]
    for h in range(N_HEADS):
        k_ref[pl.ds(h, tm, stride=N_HEADS), :] = head(k, h)
        v_ref[pl.ds(h, tm, stride=N_HEADS), :] = head(v, h)
    if by_head:
        for h in range(N_HEADS):
            q_ref[h] = head(q, h).T.astype(BF16)
            kb_ref[h] = head(k, h).astype(BF16)
            vb_ref[h] = head(v, h).T.astype(BF16)
    else:
        q_ref[...] = q.astype(BF16)
        kb_ref[...] = k.astype(BF16)
        vb_ref[...] = v.astype(BF16)
    c = seg(o, o + 2 * d_conv)
    u_ref[...] = c[:, :d_conv] * jax.nn.sigmoid(c[:, d_conv:])
    o += 2 * d_conv
    sg_ref[...] = jax.nn.sigmoid(seg(o, w_ref.shape[1])).astype(BF16)


def _in_proj(x, w_bf, b, d_att, d_conv, attn_tiles=None):
    n, d = x.shape
    tm = min(TOKEN_TILE, n)
    row = lambda width: pl.BlockSpec((tm, width), lambda i: (i, 0))
    full = lambda a: pl.BlockSpec(a.shape, lambda i: (0, 0))
    f32_out = lambda width: jax.ShapeDtypeStruct((n, width), F32)
    kv_shape = jax.ShapeDtypeStruct((n * N_HEADS, QK_DIM), F32)
    kv_spec = pl.BlockSpec((tm * N_HEADS, QK_DIM), lambda i: (i, 0))
    if attn_tiles is None:
        qkv_shapes = [jax.ShapeDtypeStruct((n, d_att), BF16)] * 3
        qkv_specs = [row(d_att)] * 3
    else:
        bsz, seq, tq, tk = attn_tiles
        assert seq % tm == 0 and tq % tm == 0 and tk % tm == 0
        per_seq = seq // tm

        def spec(t, transposed):
            parts = t // tm
            blk = (None, N_HEADS, None, QK_DIM, tm) if transposed else (None, N_HEADS, None, tm, QK_DIM)

            def index(i):
                ti = i % per_seq
                tile, part = ti // parts, ti % parts
                return (i // per_seq, 0, tile, 0, part) if transposed else (i // per_seq, 0, tile, part, 0)
            return pl.BlockSpec(blk, index)

        qkv_shapes = [jax.ShapeDtypeStruct((bsz, N_HEADS, seq // tq, QK_DIM, tq), BF16),
                      jax.ShapeDtypeStruct((bsz, N_HEADS, seq // tk, tk, QK_DIM), BF16),
                      jax.ShapeDtypeStruct((bsz, N_HEADS, seq // tk, V_DIM, tk), BF16)]
        qkv_specs = [spec(tq, True), spec(tk, False), spec(tk, True)]
    out_shape = (qkv_shapes[0], kv_shape, kv_shape, qkv_shapes[1], qkv_shapes[2],
                 f32_out(d_conv), jax.ShapeDtypeStruct((n, 2 * d), BF16))
    return pl.pallas_call(
        functools.partial(_in_proj_body, d_att=d_att, d_conv=d_conv, by_head=attn_tiles is not None),
        grid=(n // tm,),
        in_specs=[row(d), full(w_bf), full(b)],
        out_specs=(qkv_specs[0], kv_spec, kv_spec, qkv_specs[1], qkv_specs[2], row(d_conv), row(2 * d)),
        out_shape=out_shape,
        compiler_params=_cparams("parallel"),
        name="in_proj",
    )(x, w_bf, b)


def _t5_bucket(dist):
    n = jnp.maximum(dist, 0)
    nf = jnp.maximum(n, 1).astype(F32)
    large = MAX_EXACT + (jnp.log(nf / MAX_EXACT) / math.log(MAX_DISTANCE / MAX_EXACT)
                         * (NUM_BUCKETS - MAX_EXACT)).astype(I32)
    large = jnp.minimum(large, NUM_BUCKETS - 1)
    return jnp.where(n < MAX_EXACT, n, large)


def _bias_of(dist, rel_bias, shift=None):
    hot = _t5_bucket(dist)[..., None] == jnp.arange(NUM_BUCKETS, dtype=I32)
    table = rel_bias.astype(F32) if shift is None else rel_bias.astype(F32) - shift
    heads = [jnp.sum(jnp.where(hot, table[:, h], 0.0), -1) for h in range(table.shape[1])]
    return jnp.where(dist >= 0, jnp.stack(heads) * LOG2E, NEG_INF)


def _flash_body(lam_ref, qt_ref, k_ref, vt_ref, btab_ref, g_ref, o_ref, m_s, l_s, acc_s, bias_ref, *, ratio, group,
                lam_init):
    i = pl.program_id(2)
    tk, tq = bias_ref.shape[1:]

    @pl.when(i == 0)
    def _():
        for o in range(ratio + 1):
            table = jnp.broadcast_to(btab_ref[o], (tk, btab_ref.shape[2]))
            bias_ref[o] = pltpu.roll(table, 0, 1, stride=1, stride_axis=0)[:, :tq]

    sub = lax.broadcasted_iota(I32, (QK_DIM, 1), 0)
    qs = []
    for b in range(group):
        qt = qt_ref[b]
        zero = jnp.zeros_like(qt)
        qs.append((jnp.where(sub < HEAD_DIM, qt, zero), jnp.where(sub >= HEAD_DIM, qt, zero)))
    m_s[...] = jnp.full(m_s.shape, NEG_INF, F32)
    l_s[...] = jnp.zeros(l_s.shape, F32)
    acc_s[...] = jnp.zeros(acc_s.shape, F32)

    def step(j, bias, keys=slice(None), cols=slice(None)):
        for b in range(group):
            k = k_ref[b, j, keys, :]
            vt = vt_ref[b, j, :, keys]
            for c in range(2):
                s = jnp.dot(k, qs[b][c][:, cols], preferred_element_type=F32)
                if bias is not None:
                    s = s + bias
                m_old = m_s[b, c, :, cols]
                m_new = jnp.maximum(m_old, jnp.max(s, 0, keepdims=True))
                alpha = jnp.exp2(m_old - m_new)
                p = jnp.exp2(s - m_new)
                l_s[b, c, :, cols] = alpha * l_s[b, c, :, cols] + jnp.sum(p, 0, keepdims=True)
                acc_s[b, c, :, cols] = alpha * acc_s[b, c, :, cols] + jnp.dot(
                    vt, p.astype(BF16), preferred_element_type=F32)
                m_s[b, c, :, cols] = m_new

    def far_step(j, carry):
        step(j, None)
        return carry

    lax.fori_loop(0, jnp.maximum(i * ratio - 1, 0), far_step, 0)

    @pl.when(i >= 1)
    def _():
        step(i * ratio - 1, bias_ref[0])

    for o in range(ratio - 1):
        step(i * ratio + o, bias_ref[o + 1])
    last, half, first_col = i * ratio + ratio - 1, tk // 2, tq - tk // 2
    step(last, bias_ref[ratio, 0:half, :], keys=slice(0, half))
    step(last, bias_ref[ratio, half:tk, first_col:tq], keys=slice(half, tk), cols=slice(first_col, tq))

    for b in range(group):
        o = acc_s[b, 0] / l_s[b, 0] - lam_ref[0] * (acc_s[b, 1] / l_s[b, 1])
        o = o * lax.rsqrt(jnp.mean(o * o, 0, keepdims=True) + LN_EPS) * g_ref[...]
        o_ref[b] = (o * (1.0 - lam_init)).T.astype(BF16)


def _attn_tiles(seq):
    tq = min(ATTN_TQ, seq)
    tk = min(ATTN_TK, tq)
    assert tk >= MAX_DISTANCE and seq % tq == 0 and tq % tk == 0
    return tq, tk


def _prompt_attention(qt, k4, vt, rel_bias, lam, subln_g, lam_init):
    bsz, _, nq, _, tq = qt.shape
    nk, tk = k4.shape[2], k4.shape[3]
    seq, ratio, width = nq * tq, tq // tk, N_HEADS * V_DIM
    group = ATTN_GROUP if bsz % ATTN_GROUP == 0 else 1
    far = rel_bias[NUM_BUCKETS - 1].astype(F32)
    period = tq + tk
    assert period % LANES == 0
    y = jnp.arange(period, dtype=I32)
    r_minus_c = jnp.where(y < tq, y, y - period)
    dist = r_minus_c[None, :] - (jnp.arange(ratio + 1, dtype=I32)[:, None] - 1) * tk
    btab = _bias_of(dist, rel_bias, far).reshape(N_HEADS, ratio + 1, 1, period)
    return pl.pallas_call(
        functools.partial(_flash_body, ratio=ratio, group=group, lam_init=lam_init),
        grid=(bsz // group, N_HEADS, nq),
        in_specs=[
            pl.BlockSpec(memory_space=pltpu.SMEM),
            pl.BlockSpec((group, None, None, QK_DIM, tq), lambda b, h, i: (b, h, i, 0, 0)),
            pl.BlockSpec((group, None, nk, tk, QK_DIM), lambda b, h, i: (b, h, 0, 0, 0)),
            pl.BlockSpec((group, None, nk, V_DIM, tk), lambda b, h, i: (b, h, 0, 0, 0)),
            pl.BlockSpec((None, ratio + 1, 1, period), lambda b, h, i: (h, 0, 0, 0)),
            pl.BlockSpec((V_DIM, 1), lambda b, h, i: (0, 0)),
        ],
        out_specs=pl.BlockSpec((group, tq, V_DIM), lambda b, h, i: (b, i, h)),
        out_shape=jax.ShapeDtypeStruct((bsz, seq, width), BF16),
        scratch_shapes=[pltpu.VMEM((group, 2, 1, tq), F32), pltpu.VMEM((group, 2, 1, tq), F32),
                        pltpu.VMEM((group, 2, V_DIM, tq), F32), pltpu.VMEM((ratio + 1, tk, tq), F32)],
        compiler_params=_cparams("parallel", "parallel", "arbitrary"),
        name="prompt_attention",
    )(lam, qt, k4, vt, btab, subln_g.reshape(V_DIM, 1))


def _paged_body(pt_ref, lam_ref, q_ref, kn_ref, vn_ref, bias_ref, g_ref, *refs, n_pages, t_new, group, lam_init):
    o_ref = refs[2 * group * n_pages]
    lam = lam_ref[0]
    rows = 2 * t_new
    cols = PAGE_SIZE * N_HEADS
    nt = (((1,), (1,)), ((), ()))
    for sq in range(group):
        k_refs = refs[sq * n_pages:(sq + 1) * n_pages]
        v_refs = refs[(group + sq) * n_pages:(group + sq + 1) * n_pages]
        q = q_ref[sq]
        parts = [lax.dot_general(q, k_refs[p][...].astype(BF16), nt, preferred_element_type=F32)
                 for p in range(n_pages)]
        parts.append(lax.dot_general(q, kn_ref[sq], nt, preferred_element_type=F32))
        s = jnp.concatenate(parts, axis=1) + bias_ref[...]
        m = jnp.max(s, -1, keepdims=True)
        p_un = jnp.exp2(s - m)
        pn = p_un / jnp.sum(p_un, -1, keepdims=True)
        w = (pn - lam * pltpu.roll(pn, N_HEADS * rows - t_new, axis=0)).astype(BF16)
        o = jnp.dot(w[:, n_pages * cols:], vn_ref[sq], preferred_element_type=F32)
        for p in range(n_pages):
            o = o + jnp.dot(w[:, p * cols:(p + 1) * cols], v_refs[p][...].astype(BF16), preferred_element_type=F32)
        o = o * lax.rsqrt(jnp.mean(o * o, -1, keepdims=True) + LN_EPS) * g_ref[...]
        o = (o * (1.0 - lam_init)).astype(BF16)
        o_ref[sq] = jnp.concatenate([o[h * rows:(h + 1) * rows] for h in range(N_HEADS)], axis=1)


def _sample_attention(q, kb, vb, cache_k, cache_v, page_table, rel_bias, lam, subln_g, lam_init):
    bs, t_new, width = q.shape
    n_pool = cache_k.shape[0]
    n_pages = page_table.shape[1]
    past = n_pages * PAGE_SIZE
    rows = 2 * t_new
    cols = PAGE_SIZE * N_HEADS
    new_pos = LANES // N_HEADS
    assert rows % SUBLANES == 0 and t_new <= new_pos
    qh = q.reshape(bs, t_new, N_HEADS, QK_DIM).transpose(0, 2, 1, 3)
    lane_map = (jnp.arange(QK_DIM) // HEAD_DIM)[None, :] == jnp.arange(2)[:, None]
    q_rows = (qh[:, :, None] * lane_map[None, None, :, None, :].astype(BF16)).reshape(bs, N_HEADS * rows, QK_DIM)
    pad = ((0, 0), (0, LANES - t_new * N_HEADS), (0, 0))
    kn = jnp.pad(kb.reshape(bs, t_new * N_HEADS, QK_DIM), pad)
    vn = jnp.pad(vb.reshape(bs, t_new * N_HEADS, V_DIM), pad)
    qpos = past + jnp.arange(t_new, dtype=I32)
    kpos = jnp.concatenate([jnp.arange(past + t_new, dtype=I32),
                            jnp.full((new_pos - t_new,), past + t_new + new_pos, I32)])
    col_pos = jnp.repeat(kpos, N_HEADS)
    col_head = jnp.tile(jnp.arange(N_HEADS, dtype=I32), kpos.shape[0])
    b = _bias_of(qpos[:, None] - col_pos[None, :], rel_bias)
    b = jnp.where(col_head[None, None, :] == jnp.arange(N_HEADS, dtype=I32)[:, None, None], b, NEG_INF)
    bias = jnp.stack([b, b], axis=1).reshape(N_HEADS * rows, -1)

    group = SAMPLE_GROUP if bs % SAMPLE_GROUP == 0 else 1
    page_spec = lambda sq, j: pl.BlockSpec(
        (None, cols, QK_DIM), lambda b, pt, sq=sq, j=j: (pt[(b * group + sq) * n_pages + j], 0, 0))
    page_specs = [page_spec(sq, j) for sq in range(group) for j in range(n_pages)]
    per_seq = lambda r, c: pl.BlockSpec((group, r, c), lambda b, pt: (b, 0, 0))
    grid_spec = pltpu.PrefetchScalarGridSpec(
        num_scalar_prefetch=1,
        grid=(bs // group,),
        in_specs=[
            pl.BlockSpec(memory_space=pltpu.SMEM),
            per_seq(N_HEADS * rows, QK_DIM),
            per_seq(LANES, QK_DIM),
            per_seq(LANES, V_DIM),
            pl.BlockSpec(bias.shape, lambda b, pt: (0, 0)),
            pl.BlockSpec((1, V_DIM), lambda b, pt: (0, 0)),
        ] + page_specs * 2,
        out_specs=pl.BlockSpec((group, rows, width), lambda b, pt: (b, 0, 0)),
    )
    ck = cache_k.reshape(n_pool, cols, QK_DIM)
    cv = cache_v.reshape(n_pool, cols, V_DIM)
    return pl.pallas_call(
        functools.partial(_paged_body, n_pages=n_pages, t_new=t_new, group=group, lam_init=lam_init),
        grid_spec=grid_spec,
        out_shape=jax.ShapeDtypeStruct((bs, rows, width), BF16),
        compiler_params=_cparams("parallel"),
        name="sample_attention",
    )(page_table.reshape(-1), lam, q_rows, kn, vn, bias, subln_g, *([ck] * (group * n_pages)),
      *([cv] * (group * n_pages)))


def _conv_prompt_body(u_ref, halo_ref, w_ref, cb_ref, g_ref, b_ref, o_ref, buf, shifted, *, tc):
    i = pl.program_id(1)
    halo = halo_ref[0]
    buf[0:CONV_HALO, :] = jnp.where(i > 0, halo, jnp.zeros_like(halo))
    buf[CONV_HALO:, :] = u_ref[0]
    rows = CONV_HALO + tc - SUBLANES
    for s in range(1, SUBLANES):
        shifted[s - 1, 0:rows, :] = buf[s:s + rows, :]
    first = CONV_HALO - (CONV_WIDTH - 1)
    for r0 in range(0, tc, CONV_CHUNK):
        acc = jnp.zeros((CONV_CHUNK, u_ref.shape[2]), F32) + cb_ref[...]
        for j in range(CONV_WIDTH):
            s = (first + j) % SUBLANES
            a = first + j - s + r0
            win = buf[a:a + CONV_CHUNK, :] if s == 0 else shifted[s - 1, a:a + CONV_CHUNK, :]
            acc = acc + w_ref[j:j + 1, :] * win
        y = _layer_norm(acc, g_ref[...], b_ref[...])
        o_ref[0, r0:r0 + CONV_CHUNK, :] = (y * jax.nn.sigmoid(y)).astype(BF16)


def _conv_prompt(u, conv_w, conv_b, ln_g, ln_b):
    bsz, seq, ch = u.shape
    tc = min(TOKEN_TILE, seq)
    assert seq % tc == 0 and tc % CONV_CHUNK == 0 and tc % CONV_HALO == 0
    vec = pl.BlockSpec((1, ch), lambda b, i: (0, 0))
    return pl.pallas_call(
        functools.partial(_conv_prompt_body, tc=tc),
        grid=(bsz, seq // tc),
        in_specs=[
            pl.BlockSpec((1, tc, ch), lambda b, i: (b, i, 0)),
            pl.BlockSpec((1, CONV_HALO, ch), lambda b, i: (b, jnp.maximum(i * (tc // CONV_HALO) - 1, 0), 0)),
            pl.BlockSpec((CONV_WIDTH, ch), lambda b, i: (0, 0)),
            vec, vec, vec,
        ],
        out_specs=pl.BlockSpec((1, tc, ch), lambda b, i: (b, i, 0)),
        out_shape=jax.ShapeDtypeStruct((bsz, seq, ch), BF16),
        scratch_shapes=[pltpu.VMEM((CONV_HALO + tc, ch), F32), pltpu.VMEM((SUBLANES - 1, CONV_HALO + tc, ch), F32)],
        compiler_params=_cparams("parallel", "arbitrary"),
        name="conv_prompt",
    )(u, u, conv_w, conv_b, ln_g, ln_b)


def _conv_sample_body(buf_ref, w_ref, cb_ref, g_ref, b_ref, o_ref, *, t_new):
    for t in range(t_new):
        acc = jnp.zeros(buf_ref.shape[1:], F32) + cb_ref[...]
        for j in range(CONV_WIDTH):
            acc = acc + w_ref[j:j + 1, :] * buf_ref[t + j]
        y = _layer_norm(acc, g_ref[...], b_ref[...])
        o_ref[t] = (y * jax.nn.sigmoid(y)).astype(BF16)


def _conv_sample(buf_t, conv_w, conv_b, ln_g, ln_b):
    rows, bs, ch = buf_t.shape
    t_new = rows - (CONV_WIDTH - 1)
    gb = min(32, bs)
    assert bs % gb == 0
    vec = pl.BlockSpec((1, ch), lambda i: (0, 0))
    return pl.pallas_call(
        functools.partial(_conv_sample_body, t_new=t_new),
        grid=(bs // gb,),
        in_specs=[pl.BlockSpec((rows, gb, ch), lambda i: (0, i, 0)),
                  pl.BlockSpec((CONV_WIDTH, ch), lambda i: (0, 0)), vec, vec, vec],
        out_specs=pl.BlockSpec((t_new, gb, ch), lambda i: (0, i, 0)),
        out_shape=jax.ShapeDtypeStruct((t_new, bs, ch), BF16),
        compiler_params=_cparams("parallel"),
        name="conv_sample",
    )(buf_t, conv_w, conv_b, ln_g, ln_b)


N_MERGE_OUT = 5


def _merge_body(*refs, alpha, n_alias):
    (o_ref, y_ref, sg_ref, x_ref, wa_ref, wc_ref, bc_ref, wo_ref, g1_ref, b1_ref, rw_ref, rb_ref) = refs[:12]
    h1_ref, ti_ref, rk_ref, wcol_ref, cnt_ref = refs[12 + n_alias:]
    tm, d = x_ref.shape
    a = jnp.dot(o_ref[...], wa_ref[...], preferred_element_type=F32)
    b = jnp.dot(y_ref[...], wc_ref[...], preferred_element_type=F32) + bc_ref[...]
    mix_in = sg_ref[:, :d].astype(F32) * a + sg_ref[:, d:].astype(F32) * b
    mix = jnp.dot(mix_in.astype(BF16), wo_ref[...], preferred_element_type=F32)
    h1 = _layer_norm(alpha * x_ref[...] + mix, g1_ref[...], b1_ref[...])
    _store_rows(h1_ref, h1)
    logits = lax.dot_general(rw_ref[...], h1, (((1,), (1,)), ((), ())), preferred_element_type=F32,
                             precision=lax.Precision.HIGHEST) + rb_ref[...]
    n_e = logits.shape[0]
    eid = lax.broadcasted_iota(I32, logits.shape, 0)
    vals, idxs, hots = [], [], []
    for _ in range(TOP_K):
        m = jnp.max(logits, 0, keepdims=True)
        idx = jnp.min(jnp.where(logits == m, eid, n_e), 0, keepdims=True)
        hot = eid == idx
        vals.append(m)
        idxs.append(idx)
        hots.append(hot)
        logits = jnp.where(hot, -jnp.inf, logits)
    ex = [jnp.exp(v - vals[0]) for v in vals]
    den = ex[0]
    for e in ex[1:]:
        den = den + e
    ti_ref[...] = jnp.concatenate(idxs, 0)
    wrows = jnp.concatenate([e / den for e in ex] + [jnp.zeros((LANES - TOP_K, tm), F32)], 0)
    wcol_ref[...] = wrows.T
    sel = hots[0]
    for hot in hots[1:]:
        sel = sel | hot
    sel_f = jnp.where(sel, 1.0, 0.0)
    before = lax.broadcasted_iota(I32, (tm, tm), 0) < lax.broadcasted_iota(I32, (tm, tm), 1)
    upper = jnp.where(before, 1.0, 0.0).astype(BF16)
    ahead = jnp.dot(sel_f.astype(BF16), upper, preferred_element_type=F32)
    rk_ref[...] = jnp.concatenate(
        [jnp.sum(jnp.where(hot, ahead, 0.0), 0, keepdims=True) for hot in hots], 0).astype(I32)
    cnt = jnp.sum(sel_f, 1, keepdims=True).astype(I32)
    cnt_ref[0] = jnp.broadcast_to(cnt, (n_e, LANES))


def _merge(o, yact, sg, x, wts, alpha, n_total, row_off, tm, prev):
    n, d = x.shape
    wa, wc, bc, wo, g1, b1, rwt, rb = wts
    n_e = rwt.shape[0]
    off = row_off // tm
    last = n // tm - 1
    steps = n // tm if prev is not None else n_total // tm
    row = lambda width: pl.BlockSpec((tm, width), lambda i: (jnp.minimum(i, last), 0))
    full = lambda a: pl.BlockSpec(a.shape, lambda i: (0, 0))
    in_specs = [row(o.shape[1]), row(yact.shape[1]), row(2 * d), row(d),
                full(wa), full(wc), full(bc), full(wo), full(g1), full(b1), full(rwt), full(rb)]
    args = [o, yact, sg, x, wa, wc, bc, wo, g1, b1, rwt, rb]
    aliases = {}
    if prev is not None:
        in_specs += [pl.BlockSpec(memory_space=pl.ANY)] * N_MERGE_OUT
        aliases = {len(args) + j: j for j in range(N_MERGE_OUT)}
        args += list(prev)
    slot_major = pl.BlockSpec((TOP_K, tm), lambda i: (0, off + i))
    return pl.pallas_call(
        functools.partial(_merge_body, alpha=alpha, n_alias=0 if prev is None else N_MERGE_OUT),
        grid=(steps,),
        in_specs=in_specs,
        out_specs=(pl.BlockSpec((tm * SUBLANES, LANES), lambda i: (off + i, 0)), slot_major, slot_major,
                   pl.BlockSpec((tm, LANES), lambda i: (off + i, 0)),
                   pl.BlockSpec((1, n_e, LANES), lambda i: (off + i, 0, 0))),
        out_shape=(jax.ShapeDtypeStruct((n_total * SUBLANES, LANES), F32),
                   jax.ShapeDtypeStruct((TOP_K, n_total), I32),
                   jax.ShapeDtypeStruct((TOP_K, n_total), I32),
                   jax.ShapeDtypeStruct((n_total, LANES), F32),
                   jax.ShapeDtypeStruct((n_total // tm, n_e, LANES), I32)),
        input_output_aliases=aliases,
        compiler_params=_cparams("parallel"),
        name="merge_ln1_router",
    )(*args)


def _plan(topi, rank, cnt, tm_tok, tm_moe):
    k, n = topi.shape
    n_e = cnt.shape[1]
    before = jnp.cumsum(cnt, 0) - cnt
    total = jnp.sum(cnt, 0)
    ntile_e = (total + tm_moe - 1) // tm_moe
    tend = jnp.cumsum(ntile_e)
    tstart = tend - ntile_e
    base = (tstart * tm_moe)[None, :] + before
    base_tok = jnp.repeat(base.T, tm_tok, axis=1)
    hot = topi[:, None, :] == jnp.arange(n_e, dtype=I32)[None, :, None]
    pos = (jnp.sum(jnp.where(hot, base_tok[None], 0), 1).astype(I32) + rank) * SUBLANES
    n_tiles = (k * n + tm_moe - 1) // tm_moe + n_e
    n_used = tend[-1]
    tid = jnp.arange(n_tiles, dtype=I32)
    src = jnp.minimum(tid, n_used - 1)
    te = jnp.minimum(jnp.sum((src[:, None] >= tend[None, :]).astype(I32), 1), n_e - 1)
    first = (tid == tstart[te]).astype(I32)
    half = (total[te] - (tid - tstart[te]) * tm_moe <= tm_moe // 2).astype(I32)
    slot = (jnp.cumsum(first) - 1) % 2
    nxt = jnp.where(tend[te] < n_used, te[jnp.minimum(tend[te], n_tiles - 1)], -1)
    sched = jnp.stack([te, first, half, slot, nxt]).astype(I32)
    last_tile = jnp.where(ntile_e > 0, tend - 1, -1).astype(I32)
    return pos, sched, n_used.reshape(1).astype(I32), last_tile, n_tiles


def _dispatch_body(lt_ref, nu_ref, pos_ref, x_ref, xs_hbm, zbuf, xbuf, zsem, sems, *, tm, tm_moe, n_e, n_tiles,
                   n_steps):
    i = pl.program_id(0)

    def zero_copy(tile):
        start = pl.multiple_of(tile * (tm_moe * SUBLANES), tm_moe * SUBLANES)
        return pltpu.make_async_copy(zbuf, xs_hbm.at[pl.ds(start, tm_moe * SUBLANES), :], zsem)

    @pl.when(i == 0)
    def _():
        zbuf[...] = jnp.zeros(zbuf.shape, F32)

        def z_start(e, c):
            @pl.when(lt_ref[e] >= 0)
            def _():
                zero_copy(lt_ref[e]).start()
            return c

        def z_wait(e, c):
            @pl.when(lt_ref[e] >= 0)
            def _():
                zero_copy(lt_ref[e]).wait()
            return c

        def t_start(t, c):
            zero_copy(t).start()
            return c

        def t_wait(t, c):
            zero_copy(t).wait()
            return c

        lax.fori_loop(0, n_e, z_start, 0)
        lax.fori_loop(nu_ref[0], n_tiles, t_start, 0)
        lax.fori_loop(0, n_e, z_wait, 0)
        lax.fori_loop(nu_ref[0], n_tiles, t_wait, 0)

    slot = i % 2

    def drain(s):
        for k in range(TOP_K):
            pltpu.make_async_copy(xbuf.at[s], xs_hbm.at[pl.ds(0, tm * SUBLANES), :], sems.at[s]).wait()

    @pl.when(i >= 2)
    def _():
        drain(slot)

    xbuf[slot] = x_ref[...]

    def issue(g, c):
        base = pl.multiple_of(g * SUBLANES, SUBLANES)
        for rr in range(SUBLANES):
            src = xbuf.at[slot, pl.ds(pl.multiple_of((base + rr) * SUBLANES, SUBLANES), SUBLANES), :]
            for k in range(TOP_K):
                dst = pl.multiple_of(pos_ref[k, base + rr], SUBLANES)
                pltpu.make_async_copy(src, xs_hbm.at[pl.ds(dst, SUBLANES), :], sems.at[slot]).start(priority=k % 2)
        return c

    lax.fori_loop(0, tm // SUBLANES, issue, 0)

    @pl.when(i == n_steps - 1)
    def _():
        drain(slot)
        if n_steps >= 2:
            drain(1 - slot)


def _dispatch(h1, pos, last_tile, n_used, n_tiles, tm, tm_moe):
    n = h1.shape[0] // SUBLANES
    n_e = last_tile.shape[0]
    grid_spec = pltpu.PrefetchScalarGridSpec(
        num_scalar_prefetch=2,
        grid=(n // tm,),
        in_specs=[pl.BlockSpec((TOP_K, tm), lambda i, lt, nu: (0, i), memory_space=pltpu.SMEM),
                  pl.BlockSpec((tm * SUBLANES, LANES), lambda i, lt, nu: (i, 0))],
        out_specs=pl.BlockSpec(memory_space=pl.ANY),
        scratch_shapes=[pltpu.VMEM((tm_moe * SUBLANES, LANES), F32), pltpu.VMEM((2, tm * SUBLANES, LANES), F32),
                        pltpu.SemaphoreType.DMA, pltpu.SemaphoreType.DMA((2,))],
    )
    return pl.pallas_call(
        functools.partial(_dispatch_body, tm=tm, tm_moe=tm_moe, n_e=n_e, n_tiles=n_tiles, n_steps=n // tm),
        grid_spec=grid_spec,
        out_shape=jax.ShapeDtypeStruct((n_tiles * tm_moe * SUBLANES, LANES), F32),
        compiler_params=_cparams("arbitrary"),
        name="moe_dispatch",
    )(last_tile, n_used, pos, h1)


TE, FIRST, HALF, SLOT, NEXT = range(5)


def _experts_body(sch_ref, nu_ref, x_ref, w1_hbm, b1_ref, w2_hbm, b2_ref, y_ref, w1s, w2s, w1b, w2b, sems, *, tm):
    i = pl.program_id(0)
    d_ff = w2b.shape[0]

    def fetch(expert, slot):
        return (pltpu.make_async_copy(w1_hbm.at[expert], w1s.at[slot], sems.at[0, slot]),
                pltpu.make_async_copy(w2_hbm.at[expert], w2s.at[slot], sems.at[1, slot]))

    def run(rows):
        x = _load_rows(x_ref, rows).astype(BF16)
        hh = jnp.dot(x, w1b[...], preferred_element_type=F32) + b1_ref[0]
        g = jnp.minimum(hh[:, :d_ff], SWIGLU_LIMIT)
        u = jnp.clip(hh[:, d_ff:], -SWIGLU_LIMIT, SWIGLU_LIMIT)
        act = (u + 1.0) * g * jax.nn.sigmoid(SWIGLU_ALPHA * g)
        _store_rows(y_ref, jnp.dot(act.astype(BF16), w2b[...], preferred_element_type=F32) + b2_ref[0])

    @pl.when(i == 0)
    def _():
        for cp in fetch(sch_ref[TE, 0], 0):
            cp.start()

    @pl.when(i >= nu_ref[0])
    def _():
        y_ref[...] = jnp.zeros(y_ref.shape, F32)

    @pl.when(i < nu_ref[0])
    def _():
        @pl.when(sch_ref[FIRST, i] == 1)
        def _():
            slot = sch_ref[SLOT, i]

            @pl.when(sch_ref[NEXT, i] >= 0)
            def _():
                for cp in fetch(sch_ref[NEXT, i], 1 - slot):
                    cp.start()

            for cp in fetch(sch_ref[TE, i], slot):
                cp.wait()
            w1b[...] = w1s[slot].astype(BF16)
            w2b[...] = w2s[slot].astype(BF16)

        @pl.when(sch_ref[HALF, i] == 0)
        def _():
            run(tm)

        @pl.when(sch_ref[HALF, i] == 1)
        def _():
            run(tm // 2)
            y_ref[pl.ds(tm // 2 * SUBLANES, tm // 2 * SUBLANES), :] = jnp.zeros((tm // 2 * SUBLANES, LANES), F32)


def _experts(xs, sched, n_used, w1, b1, w2, b2, tm):
    n_e, d, f2 = w1.shape
    rows = tm * SUBLANES
    by_tile = lambda i, sch, nu: (i, 0)
    by_expert = lambda i, sch, nu: (sch[TE, i], 0, 0)
    grid_spec = pltpu.PrefetchScalarGridSpec(
        num_scalar_prefetch=2,
        grid=(xs.shape[0] // rows,),
        in_specs=[
            pl.BlockSpec((rows, LANES), by_tile),
            pl.BlockSpec(memory_space=pl.ANY),
            pl.BlockSpec((1, 1, f2), by_expert),
            pl.BlockSpec(memory_space=pl.ANY),
            pl.BlockSpec((1, 1, d), by_expert),
        ],
        out_specs=pl.BlockSpec((rows, LANES), by_tile),
        scratch_shapes=[pltpu.VMEM((2, d, f2), F32), pltpu.VMEM((2, f2 // 2, d), F32),
                        pltpu.VMEM((d, f2), BF16), pltpu.VMEM((f2 // 2, d), BF16), pltpu.SemaphoreType.DMA((2, 2))],
    )
    return pl.pallas_call(
        functools.partial(_experts_body, tm=tm),
        grid_spec=grid_spec,
        out_shape=jax.ShapeDtypeStruct(xs.shape, F32),
        compiler_params=_cparams("arbitrary"),
        name="moe_experts",
    )(sched, n_used, xs, w1, b1.reshape(n_e, 1, f2), w2, b2.reshape(n_e, 1, d))


def _combine_body(pos_ref, nxt_ref, h1_ref, wcol_ref, ys_hbm, g_ref, b_ref, o_ref, ybuf, sems, *, alpha, tm):
    i = pl.program_id(0)
    n = pl.num_programs(0)

    def fetch(p_ref, slot):
        def issue(g, c):
            base = pl.multiple_of(g * SUBLANES, SUBLANES)
            for rr in range(SUBLANES):
                dst = pl.ds(pl.multiple_of((base + rr) * SUBLANES, SUBLANES), SUBLANES)
                for k in range(TOP_K):
                    src = pl.multiple_of(p_ref[k, base + rr], SUBLANES)
                    pltpu.make_async_copy(ys_hbm.at[pl.ds(src, SUBLANES), :], ybuf.at[slot, k, dst, :],
                                          sems.at[slot]).start(priority=k % 2)
            return c
        lax.fori_loop(0, tm // SUBLANES, issue, 0)

    @pl.when(i == 0)
    def _():
        fetch(pos_ref, 0)

    @pl.when(i + 1 < n)
    def _():
        fetch(nxt_ref, (i + 1) % 2)

    slot = i % 2
    for k in range(TOP_K):
        pltpu.make_async_copy(ys_hbm.at[pl.ds(0, tm * SUBLANES), :], ybuf.at[slot, k], sems.at[slot]).wait()
    ff = wcol_ref[:, 0:1] * _load_rows(ybuf.at[slot, 0], tm)
    for k in range(1, TOP_K):
        ff = ff + wcol_ref[:, k:k + 1] * _load_rows(ybuf.at[slot, k], tm)
    o_ref[...] = _layer_norm(alpha * _load_rows(h1_ref, tm) + ff, g_ref[...], b_ref[...])


def _combine(h1, wcol, pos, ys, g, b, alpha, row_off, n):
    d = g.shape[1]
    tm = min(COMBINE_TILE, n)
    off = row_off // tm
    last = n // tm - 1
    vec = pl.BlockSpec((1, d), lambda i: (0, 0))
    return pl.pallas_call(
        functools.partial(_combine_body, alpha=alpha, tm=tm),
        grid=(n // tm,),
        in_specs=[pl.BlockSpec((TOP_K, tm), lambda i: (0, off + i), memory_space=pltpu.SMEM),
                  pl.BlockSpec((TOP_K, tm), lambda i: (0, off + jnp.minimum(i + 1, last)), memory_space=pltpu.SMEM),
                  pl.BlockSpec((tm * SUBLANES, LANES), lambda i: (off + i, 0)),
                  pl.BlockSpec((tm, LANES), lambda i: (off + i, 0)),
                  pl.BlockSpec(memory_space=pl.ANY), vec, vec],
        out_specs=pl.BlockSpec((tm, d), lambda i: (i, 0)),
        out_shape=jax.ShapeDtypeStruct((n, d), F32),
        scratch_shapes=[pltpu.VMEM((2, TOP_K, tm * SUBLANES, LANES), F32), pltpu.SemaphoreType.DMA((2,))],
        compiler_params=_cparams("arbitrary"),
        name="combine_ln2",
    )(pos, pos, h1, wcol, ys, g, b)


def _row2(v):
    return v.reshape(1, -1).astype(F32)


def kernel(x_prompt, x_sample, cache_k, cache_v, page_table, state_conv, w_in, b_in, lambda_q1, lambda_k1,
           lambda_q2, lambda_k2, subln_g, rel_bias, w_attn_proj, conv_w, conv_b, conv_ln_g, conv_ln_b,
           w_conv_proj, b_conv_proj, w_out, ln1_g, ln1_b, router_w, router_b, expert_w1, expert_b1,
           expert_w2, expert_b2, ln2_g, ln2_b):
    depth = w_in.shape[0]
    bp, seq, d = x_prompt.shape
    bs, t_new, _ = x_sample.shape
    d_att = N_HEADS * QK_DIM
    d_conv = conv_w.shape[2]
    n_p, n_s = bp * seq, bs * t_new
    n_tot = n_p + n_s
    tm = min(TOKEN_TILE, math.gcd(n_p, n_s))
    assert tm % LANES == 0
    tm_moe = min(MOE_TILE, n_tot)
    alpha = (2 * depth) ** 0.25

    hp = x_prompt.reshape(n_p, d)
    hs = x_sample.reshape(n_s, d)
    outs = [[] for _ in range(6)]
    for l in range(depth):
        lam_init = 0.8 - 0.6 * math.exp(-0.3 * l)
        lam = (jnp.exp(jnp.sum(lambda_q1[l].astype(F32) * lambda_k1[l].astype(F32)))
               - jnp.exp(jnp.sum(lambda_q2[l].astype(F32) * lambda_k2[l].astype(F32))) + lam_init).reshape(1)
        w_in_bf = w_in[l].astype(BF16)
        b_in_l = _row2(b_in[l])
        g_sub = _row2(subln_g[l])
        conv_args = (conv_w[l].astype(F32), _row2(conv_b[l]), _row2(conv_ln_g[l]), _row2(conv_ln_b[l]))
        merge_w = (w_attn_proj[l].astype(BF16), w_conv_proj[l].astype(BF16), _row2(b_conv_proj[l]),
                   w_out[l].astype(BF16), _row2(ln1_g[l]), _row2(ln1_b[l]),
                   router_w[l].astype(F32).T, router_b[l].astype(F32).reshape(-1, 1))

        qp, kp, vp, kbp, vbp, up, sgp = _in_proj(hp, w_in_bf, b_in_l, d_att, d_conv, (bp, seq) + _attn_tiles(seq))
        op = _prompt_attention(qp, kbp, vbp, rel_bias, lam, g_sub, lam_init)
        up3 = up.reshape(bp, seq, d_conv)
        yp = _conv_prompt(up3, *conv_args)
        merged = _merge(op.reshape(n_p, d_att), yp.reshape(n_p, d_conv), sgp, hp, merge_w, alpha, n_tot, 0, tm, None)

        qs, ks, vs, kbs, vbs, us, sgs = _in_proj(hs, w_in_bf, b_in_l, d_att, d_conv)
        osr = _sample_attention(qs.reshape(bs, t_new, d_att), kbs.reshape(bs, t_new, d_att),
                                vbs.reshape(bs, t_new, d_att), cache_k[l], cache_v[l], page_table,
                                rel_bias, lam, g_sub, lam_init)
        os_ = osr[:, :t_new].reshape(n_s, d_att)
        buf_s = jnp.concatenate([state_conv[l].astype(F32), us.reshape(bs, t_new, d_conv)], axis=1)
        ys = _conv_sample(buf_s.transpose(1, 0, 2), *conv_args).transpose(1, 0, 2).reshape(n_s, d_conv)
        h1, topi, rank, wcol, cnt = _merge(os_, ys, sgs, hs, merge_w, alpha, n_tot, n_p, tm, merged)

        pos, sched, n_used, last_tile, n_tiles = _plan(topi, rank, cnt[:, :, 0], tm, tm_moe)
        xs = _dispatch(h1, pos, last_tile, n_used, n_tiles, tm, tm_moe)
        ysort = _experts(xs, sched, n_used, expert_w1[l], expert_b1[l].astype(F32),
                         expert_w2[l], expert_b2[l].astype(F32), tm_moe)
        g2, b2 = _row2(ln2_g[l]), _row2(ln2_b[l])
        hp = _combine(h1, wcol, pos, ysort, g2, b2, alpha, 0, n_p)
        hs = _combine(h1, wcol, pos, ysort, g2, b2, alpha, n_p, n_s)

        w1 = CONV_WIDTH - 1
        cp = up3[:, seq - w1:] if seq >= w1 else jnp.concatenate(
            [jnp.zeros((bp, w1 - seq, d_conv), F32), up3], axis=1)
        for lst, val in zip(outs, (kp.reshape(bp, seq, N_HEADS, QK_DIM), vp.reshape(bp, seq, N_HEADS, V_DIM), cp,
                                   ks.reshape(bs, t_new, N_HEADS, QK_DIM), vs.reshape(bs, t_new, N_HEADS, V_DIM),
                                   buf_s[:, t_new:])):
            lst.append(val)
    return (hp.reshape(bp, seq, d), hs.reshape(bs, t_new, d)) + tuple(jnp.stack(o) for o in outs)
```
